```python
import math
import jax, jax.numpy as jnp
from jax import lax
import numpy as np

D_MODEL = 1024
BATCH = 2
SEQ = 16384
DEPTH = 1
DEC_BATCH = 8
DEC_SEQ = 32
PAST_LEN = 1024

CHUNK = 64
Q_BLOCK = 128
HA = 4
DA = 64
DVA = 2 * DA
HR = 4
DKR = 128
DVR = 128
N_BUCKETS = 32
MAX_DIST = 128
PEER_HEADS = 8
N_KEYS = 128
N_EXPERTS = N_KEYS * N_KEYS
KEY_DIM = 128
PEER_TOPK = 16
PEER_BLOCK = 128
EPS = 1e-6

WA = HA * DVA
WR = HR * DVR
IN_SIZES = (HA * 2 * DA, HA * 2 * DA, HA * DVA, HR * DKR, HR * DKR, HR * DVR, HR * DVR, D_MODEL, D_MODEL)
D_IN = 2 * HA * 2 * DA + HA * DVA + 2 * HR * DKR + 2 * HR * DVR + 2 * D_MODEL

kernel_name = "diffattn_retnet_peer_adaln_stream"

F32 = jnp.float32


def rms(x):
    xf = x.astype(F32)
    return (xf * lax.rsqrt(jnp.mean(xf * xf, axis=-1, keepdims=True) + EPS)).astype(x.dtype)


def ada_mod(c, w_ada, b_ada):
    mod = jax.nn.silu(c) @ w_ada + b_ada
    return jnp.split(mod, 6, axis=-1)


def modulate(x, gain, shift, scale):
    return rms(x) * gain * (1 + scale[:, None, :]) + shift[:, None, :]


def rotate(x, pos):
    inv = 1.0 / (10000.0 ** jnp.linspace(0.0, 1.0, DKR // 2, dtype=F32))
    ang = pos[:, None].astype(F32) * inv[None, :]
    cos = jnp.cos(ang)[None, :, None, :]
    sin = jnp.sin(ang)[None, :, None, :]
    x1, x2 = x[..., : DKR // 2], x[..., DKR // 2:]
    return jnp.concatenate([x1 * cos - x2 * sin, x1 * sin + x2 * cos], axis=-1).astype(x.dtype)


def in_proj(h, w_in, pos):
    B, L = h.shape[:2]
    z = h @ w_in
    qa, ka, va, qr, kr, vr, gr, ga, gb = jnp.split(z, np.cumsum(IN_SIZES)[:-1].tolist(), axis=-1)
    qa = qa.reshape(B, L, HA, 2, DA)
    ka = ka.reshape(B, L, HA, 2, DA)
    va = va.reshape(B, L, HA, DVA)
    qr = rotate(qr.reshape(B, L, HR, DKR), pos)
    kr = rotate(kr.reshape(B, L, HR, DKR), pos) * (DKR ** -0.5)
    vr = vr.reshape(B, L, HR, DVR)
    return qa, ka, va, qr, kr, vr, gr, ga, gb


def t5_bucket(rel):
    nb = N_BUCKETS // 2
    ret = jnp.where(rel > 0, nb, 0)
    n = jnp.abs(rel)
    max_exact = nb // 2
    nf = jnp.maximum(n, max_exact).astype(F32)
    large = max_exact + (jnp.log(nf / max_exact) / math.log(MAX_DIST / max_exact) * (nb - max_exact)).astype(jnp.int32)
    large = jnp.minimum(large, nb - 1)
    return ret + jnp.where(n < max_exact, n, large)


def diff_attn(q, k, v, q_pos, k_pos, rel_bias, lam):
    s = jnp.einsum('bqhmd,bkhmd->bhmqk', q, k).astype(F32) * (DA ** -0.5)
    bias = rel_bias.astype(F32)[t5_bucket(k_pos[None, :] - q_pos[:, None])]
    bias = jnp.transpose(bias, (2, 0, 1))[None, :, None]
    visible = (k_pos[None, :] // CHUNK) <= (q_pos[:, None] // CHUNK)
    s = jnp.where(visible, s + bias, -1e30)
    p = jax.nn.softmax(s, axis=-1)
    attn = p[:, :, 0] - lam * p[:, :, 1]
    return jnp.einsum('bhqk,bkhv->bqhv', attn.astype(v.dtype), v)


def diff_attn_prompt(q, k, v, rel_bias, lam):
    B, S = q.shape[:2]
    nb = S // Q_BLOCK
    k_pos = jnp.arange(S, dtype=jnp.int32)
    qb = q.reshape(B, nb, Q_BLOCK, HA, 2, DA).transpose(1, 0, 2, 3, 4, 5)

    def blk(args):
        i, qi = args
        q_pos = i * Q_BLOCK + jnp.arange(Q_BLOCK, dtype=jnp.int32)
        return diff_attn(qi, k, v, q_pos, k_pos, rel_bias, lam)

    out = lax.map(blk, (jnp.arange(nb, dtype=jnp.int32), qb))
    return out.transpose(1, 0, 2, 3, 4).reshape(B, S, HA, DVA)


def log_gammas():
    return jnp.log(1.0 - 2.0 ** (-5.0 - jnp.arange(HR, dtype=F32)))


def retention_chunk(state, q, k, v):
    L = q.shape[1]
    lg = log_gammas()
    n = jnp.arange(L, dtype=F32)
    diff = n[:, None] - n[None, :]
    decay = jnp.where(diff >= 0, jnp.exp(jnp.maximum(diff, 0.0)[None] * lg[:, None, None]), 0.0)
    scores = jnp.einsum('blhk,bmhk->bhlm', q, k) * decay[None].astype(q.dtype)
    intra = jnp.einsum('bhlm,bmhv->blhv', scores, v)
    xi = jnp.exp((n + 1.0)[:, None] * lg[None, :]).astype(q.dtype)
    cross = jnp.einsum('blhk,bhkv->blhv', q, state) * xi[None, :, :, None]
    zeta = jnp.exp((L - 1.0 - n)[:, None] * lg[None, :]).astype(k.dtype)
    new_state = jnp.exp(L * lg).astype(state.dtype)[None, :, None, None] * state + \
        jnp.einsum('blhk,blhv->bhkv', k * zeta[None, :, :, None], v)
    return intra + cross, new_state


def retention_prompt(q, k, v):
    B, S = q.shape[:2]
    nc = S // CHUNK

    def to_chunks(t):
        return t.reshape(B, nc, CHUNK, HR, t.shape[-1]).swapaxes(0, 1)

    def step(state, xs):
        qc, kc, vc = xs
        o, state = retention_chunk(state, qc, kc, vc)
        return state, o

    state0 = jnp.zeros((B, HR, DKR, DVR), v.dtype)
    state, o = lax.scan(step, state0, (to_chunks(q), to_chunks(k), to_chunks(v)))
    return o.swapaxes(0, 1).reshape(B, S, HR, DVR), state


def out_mix(oa, orr, gr, ga, gb, lam_init, subln_a, subln_r, w_ba, w_br, w_o):
    B, L = oa.shape[:2]
    ya = (rms(oa) * subln_a * (1.0 - lam_init)).reshape(B, L, WA)
    yr = jax.nn.silu(gr) * (rms(orr) * subln_r).reshape(B, L, WR)
    y = jax.nn.sigmoid(ga) * (ya @ w_ba) + jax.nn.sigmoid(gb) * (yr @ w_br)
    return y @ w_o


def peer_ffn(h, w_pq, peer_keys, peer_u, peer_v):
    B, L, D = h.shape
    T = B * L
    pad = (-T) % PEER_BLOCK
    xt = jnp.pad(h.reshape(T, D), ((0, pad), (0, 0)))
    nb = xt.shape[0] // PEER_BLOCK

    def blk(xb):
        q = (xb @ w_pq).reshape(PEER_BLOCK, PEER_HEADS, 2, KEY_DIM)
        s = jnp.einsum('thpd,hpnd->thpn', q, peer_keys).astype(F32)
        sv, si = lax.top_k(s, PEER_TOPK)
        cand = (sv[:, :, 0, :, None] + sv[:, :, 1, None, :]).reshape(PEER_BLOCK, PEER_HEADS, PEER_TOPK * PEER_TOPK)
        cidx = (si[:, :, 0, :, None] * N_KEYS + si[:, :, 1, None, :]).reshape(PEER_BLOCK, PEER_HEADS, PEER_TOPK * PEER_TOPK)
        top_s, top_p = lax.top_k(cand, PEER_TOPK)
        idx = jnp.take_along_axis(cidx, top_p, axis=-1)
        g = jax.nn.softmax(top_s, axis=-1)
        a = jax.nn.gelu(jnp.einsum('td,thkd->thk', xb, peer_u[idx]).astype(F32), approximate=False)
        return jnp.einsum('thk,thkd->td', (g * a).astype(xb.dtype), peer_v[idx])

    out = lax.map(blk, xt.reshape(nb, PEER_BLOCK, D))
    return out.reshape(nb * PEER_BLOCK, D)[:T].reshape(B, L, D)


def trunk_layer(x, c, pos, attend, retain, lam_init, w_ada, b_ada, norm1, norm2, w_in,
                subln_a, subln_r, w_ba, w_br, w_o, w_pq, peer_keys, peer_u, peer_v):
    sh1, sc1, g1, sh2, sc2, g2 = ada_mod(c, w_ada, b_ada)
    h = modulate(x, norm1, sh1, sc1)
    qa, ka, va, qr, kr, vr, gr, ga, gb = in_proj(h, w_in, pos)
    oa = attend(qa, ka, va)
    orr, st = retain(qr, kr, vr)
    x = x + g1[:, None, :] * out_mix(oa, orr, gr, ga, gb, lam_init, subln_a, subln_r, w_ba, w_br, w_o)
    h = modulate(x, norm2, sh2, sc2)
    x = x + g2[:, None, :] * peer_ffn(h, w_pq, peer_keys, peer_u, peer_v)
    B, L = x.shape[:2]
    return x, ka.reshape(B, L, HA, 2 * DA), va, st


def setup_inputs(seed: int = 0) -> dict:
    key = jax.random.key(seed)
    ks = jax.random.split(key, 32)
    nrm = lambda k, shp, s: jax.random.normal(k, shp, F32) * s
    D = D_MODEL
    return {
        "x_prompt": nrm(ks[0], (BATCH, SEQ, D), 1.0),
        "x_sample": nrm(ks[1], (DEC_BATCH, DEC_SEQ, D), 1.0),
        "cache_k": nrm(ks[2], (DEPTH, DEC_BATCH, PAST_LEN, HA, 2 * DA), 1.0),
        "cache_v": nrm(ks[3], (DEPTH, DEC_BATCH, PAST_LEN, HA, DVA), 1.0),
        "state_ret": nrm(ks[4], (DEPTH, DEC_BATCH, HR, DKR, DVR), 0.1),
        "c_prompt": nrm(ks[5], (BATCH, D), 1.0),
        "c_sample": nrm(ks[6], (DEC_BATCH, D), 1.0),
        "w_ada": nrm(ks[7], (DEPTH, D, 6 * D), 0.02),
        "b_ada": nrm(ks[8], (DEPTH, 6 * D), 0.02),
        "norm1": 1.0 + nrm(ks[9], (DEPTH, D), 0.02),
        "norm2": 1.0 + nrm(ks[10], (DEPTH, D), 0.02),
        "w_in": nrm(ks[11], (DEPTH, D, D_IN), D ** -0.5),
        "lam_q1": nrm(ks[12], (DEPTH, DA), 0.1),
        "lam_k1": nrm(ks[13], (DEPTH, DA), 0.1),
        "lam_q2": nrm(ks[14], (DEPTH, DA), 0.1),
        "lam_k2": nrm(ks[15], (DEPTH, DA), 0.1),
        "subln_a": 1.0 + nrm(ks[16], (DEPTH, DVA), 0.02),
        "subln_r": 1.0 + nrm(ks[17], (DEPTH, DVR), 0.02),
        "w_ba": nrm(ks[18], (DEPTH, WA, D), WA ** -0.5),
        "w_br": nrm(ks[19], (DEPTH, WR, D), WR ** -0.5),
        "w_o": nrm(ks[20], (DEPTH, D, D), D ** -0.5),
        "rel_bias": nrm(ks[21], (N_BUCKETS, HA), 0.5),
        "w_pq": nrm(ks[22], (DEPTH, D, PEER_HEADS * 2 * KEY_DIM), D ** -0.5),
        "peer_keys": nrm(ks[23], (DEPTH, PEER_HEADS, 2, N_KEYS, KEY_DIM), KEY_DIM ** -0.5),
        "peer_u": nrm(ks[24], (DEPTH, N_EXPERTS, D), D ** -0.5),
        "peer_v": nrm(ks[25], (DEPTH, N_EXPERTS, D), PEER_HEADS ** -0.5),
        "final_norm": 1.0 + nrm(ks[26], (D,), 0.02),
    }


def reference(x_prompt, x_sample, cache_k, cache_v, state_ret, c_prompt, c_sample,
              w_ada, b_ada, norm1, norm2, w_in, lam_q1, lam_k1, lam_q2, lam_k2,
              subln_a, subln_r, w_ba, w_br, w_o, rel_bias, w_pq, peer_keys, peer_u, peer_v,
              final_norm):
    Bp, Lp = x_prompt.shape[:2]
    Bs, Ls = x_sample.shape[:2]
    pos_p = jnp.arange(Lp, dtype=jnp.int32)
    pos_s = PAST_LEN + jnp.arange(Ls, dtype=jnp.int32)
    k_pos_s = jnp.arange(PAST_LEN + Ls, dtype=jnp.int32)
    xp, xs = x_prompt, x_sample
    kp_l, vp_l, sp_l, ks_l, vs_l, ss_l = [], [], [], [], [], []
    for l in range(DEPTH):
        lam_init = 0.8 - 0.6 * math.exp(-0.3 * l)
        lam = (jnp.exp(jnp.sum(lam_q1[l].astype(F32) * lam_k1[l].astype(F32)))
               - jnp.exp(jnp.sum(lam_q2[l].astype(F32) * lam_k2[l].astype(F32))) + lam_init)
        params = (w_ada[l], b_ada[l], norm1[l], norm2[l], w_in[l], subln_a[l], subln_r[l],
                  w_ba[l], w_br[l], w_o[l], w_pq[l], peer_keys[l], peer_u[l], peer_v[l])

        attend_p = lambda q, k, v: diff_attn_prompt(q, k, v, rel_bias, lam)
        xp, kp, vp, sp = trunk_layer(xp, c_prompt, pos_p, attend_p, retention_prompt, lam_init, *params)

        def attend_s(q, k, v, l=l):
            kc = cache_k[l].reshape(Bs, PAST_LEN, HA, 2, DA).astype(k.dtype)
            k_all = jnp.concatenate([kc, k], axis=1)
            v_all = jnp.concatenate([cache_v[l].astype(v.dtype), v], axis=1)
            return diff_attn(q, k_all, v_all, pos_s, k_pos_s, rel_bias, lam)

        def retain_s(q, k, v, l=l):
            return retention_chunk(state_ret[l].astype(v.dtype), q, k, v)

        xs, ksn, vsn, ssn = trunk_layer(xs, c_sample, pos_s, attend_s, retain_s, lam_init, *params)
        kp_l.append(kp); vp_l.append(vp); sp_l.append(sp)
        ks_l.append(ksn); vs_l.append(vsn); ss_l.append(ssn)

    y_prompt = rms(xp) * final_norm
    y_sample = rms(xs) * final_norm
    return (y_prompt, y_sample, jnp.stack(kp_l), jnp.stack(vp_l), jnp.stack(sp_l),
            jnp.stack(ks_l), jnp.stack(vs_l), jnp.stack(ss_l))
```

```python
import functools
import math

import jax
import jax.numpy as jnp
from jax import lax
from jax.experimental import pallas as pl
from jax.experimental.pallas import tpu as pltpu

F32 = jnp.float32
BF16 = jnp.bfloat16

CHUNK = 64
HA = 4
DA = 64
DVA = 2 * DA
HR = 4
DKR = 128
DVR = 128
N_BUCKETS = 32
MAX_DIST = 128
PEER_HEADS = 8
N_KEYS = 128
KEY_DIM = 128
PEER_TOPK = 16
EPS = 1e-6
WA = HA * DVA
WR = HR * DVR
NEG = -1e30
LANES = 128
VMEM_LIMIT = 56 * 1024 * 1024


def _params(sem, vmem=VMEM_LIMIT):
    return pltpu.CompilerParams(dimension_semantics=sem, vmem_limit_bytes=vmem)


def _dot(a, b):
    return jnp.dot(a, b, preferred_element_type=F32)


def _dot_nt(a, b):
    return lax.dot_general(a, b, (((1,), (1,)), ((), ())), preferred_element_type=F32)


def _dot_tn(a, b):
    return lax.dot_general(a, b, (((0,), (0,)), ((), ())), preferred_element_type=F32)


def _rms_rows(x):
    return x * lax.rsqrt(jnp.mean(x * x, axis=-1, keepdims=True) + EPS)


def _ada_kernel(c_ref, w_ref, b_ref, o_ref):
    c = c_ref[...]
    s = (c * jax.nn.sigmoid(c)).astype(BF16)
    o_ref[...] = _dot(s, w_ref[...].astype(BF16)) + b_ref[...]


def ada_mod(c, w_ada, b_ada, tn=1024):
    nb, d = c.shape
    n = w_ada.shape[1]
    return pl.pallas_call(
        _ada_kernel,
        grid=(n // tn,),
        in_specs=[pl.BlockSpec((nb, d), lambda j: (0, 0)),
                  pl.BlockSpec((d, tn), lambda j: (0, j)),
                  pl.BlockSpec((1, tn), lambda j: (0, j))],
        out_specs=pl.BlockSpec((nb, tn), lambda j: (0, j)),
        out_shape=jax.ShapeDtypeStruct((nb, n), F32),
        compiler_params=_params(("arbitrary",)),
        name="ada_mod",
    )(c, w_ada, b_ada.reshape(1, n))


def _inproj_kernel(x_ref, sh_ref, sc_ref, n1_ref, w_ref, cos_ref, sin_ref,
                   ka_ref, va_ref, qab_ref, kab_ref, vab_ref, qrb_ref, kr_ref, vrb_ref,
                   gr_ref, ga_ref, gb_ref):
    x = x_ref[...]
    h = _rms_rows(x) * n1_ref[...] * (1.0 + sc_ref[...]) + sh_ref[...]
    hb = h.astype(BF16)

    def proj(lo, n):
        return _dot(hb, w_ref[:, lo:lo + n])

    qa = proj(0, 512)
    qab_ref[...] = (qa * (DA ** -0.5)).astype(BF16)
    ka = proj(512, 512)
    ka_ref[...] = ka
    kab_ref[...] = ka.astype(BF16)
    va = proj(1024, 512)
    va_ref[...] = va
    vab_ref[...] = va.astype(BF16)
    cos2 = cos_ref[...]
    sin2 = sin_ref[...]

    def rot(z):
        parts = []
        for hh in range(HR):
            zh = z[:, hh * DKR:(hh + 1) * DKR]
            parts.append(zh * cos2 + pltpu.roll(zh, DKR // 2, 1) * sin2)
        return jnp.concatenate(parts, axis=-1)

    qrb_ref[...] = rot(proj(1536, 512)).astype(BF16)
    kr_ref[...] = rot(proj(2048, 512)) * (DKR ** -0.5)
    vrb_ref[...] = proj(2560, 512).astype(BF16)
    gr_ref[...] = proj(3072, 512)
    ga_ref[...] = proj(3584, 1024)
    gb_ref[...] = proj(4608, 1024)


def in_proj(x, sh1, sc1, norm1, w_in_b, cos2, sin2, tm):
    B, L, D = x.shape
    d_in = w_in_b.shape[1]
    row = lambda n: pl.BlockSpec((None, tm, n), lambda b, i: (b, i, 0))
    mod = pl.BlockSpec((None, 1, D), lambda b, i: (b, 0, 0))
    f = lambda n, dt: jax.ShapeDtypeStruct((B, L, n), dt)
    return pl.pallas_call(
        _inproj_kernel,
        grid=(B, L // tm),
        in_specs=[row(D), mod, mod,
                  pl.BlockSpec((1, D), lambda b, i: (0, 0)),
                  pl.BlockSpec((D, d_in), lambda b, i: (0, 0)),
                  pl.BlockSpec((tm, DKR), lambda b, i: (i, 0)),
                  pl.BlockSpec((tm, DKR), lambda b, i: (i, 0))],
        out_specs=[row(512), row(512), row(512), row(512), row(512), row(512), row(512), row(512),
                   row(512), row(1024), row(1024)],
        out_shape=[f(512, F32), f(512, F32), f(512, BF16), f(512, BF16), f(512, BF16),
                   f(512, BF16), f(512, F32), f(512, BF16), f(512, F32), f(1024, F32), f(1024, F32)],
        compiler_params=_params(("arbitrary", "arbitrary")),
        name="in_proj",
    )(x, sh1.reshape(B, 1, D), sc1.reshape(B, 1, D), norm1.reshape(1, D), w_in_b, cos2, sin2)


def _t5_bucket(rel):
    nb = N_BUCKETS // 2
    ret = jnp.where(rel > 0, nb, 0)
    n = jnp.abs(rel)
    max_exact = nb // 2
    nf = jnp.maximum(n, max_exact).astype(F32)
    large = max_exact + (jnp.log(nf / max_exact) / math.log(MAX_DIST / max_exact)
                         * (nb - max_exact)).astype(jnp.int32)
    large = jnp.minimum(large, nb - 1)
    return ret + jnp.where(n < max_exact, n, large)


def _bias_from_buckets(bkt, rb_ref, h, shift):
    val = jnp.where(bkt < 0, NEG, 0.0).astype(F32)
    for n in range(N_BUCKETS):
        val = jnp.where(bkt == n, rb_ref[n, h] - shift, val)
    return val


def _lam_value(lv_ref, lam_init):
    lv = lv_ref[...]
    a = jnp.sum(lv[0:1] * lv[1:2], axis=-1, keepdims=True)
    b = jnp.sum(lv[2:3] * lv[3:4], axis=-1, keepdims=True)
    return jnp.exp(a) - jnp.exp(b) + lam_init


def _block_diag_q(q):
    lane = lax.broadcasted_iota(jnp.int32, q.shape, 1)
    zero = jnp.zeros_like(q)
    return jnp.concatenate([jnp.where(lane < DA, q, zero), jnp.where(lane >= DA, q, zero)], axis=0)


def _attn_kernel(rb_ref, lv_ref, q_ref, k_ref, v_ref, bkt_ref, o_ref,
                 bias_sc, m_sc, l_sc, acc_sc, *, tq, lam_init):
    h = pl.program_id(1)
    i = pl.program_id(2)

    @pl.when(i == 0)
    def _():
        far = rb_ref[N_BUCKETS // 2 - 1, h]
        for t in range(2):
            b = _bias_from_buckets(bkt_ref[t], rb_ref, h, far)
            bias_sc[t, 0:tq, :] = b
            bias_sc[t, tq:2 * tq, :] = b

    qbd = _block_diag_q(q_ref[...])
    m_sc[...] = jnp.full(m_sc.shape, NEG, F32)
    l_sc[...] = jnp.zeros(l_sc.shape, F32)
    acc_sc[...] = jnp.zeros(acc_sc.shape, F32)

    def tile(j, bias):
        off = pl.multiple_of(j * tq, tq)
        s = _dot_nt(qbd, k_ref[pl.ds(off, tq), :])
        if bias is not None:
            s = s + bias
        m_prev = m_sc[...]
        m_new = jnp.maximum(m_prev, jnp.max(s, axis=-1, keepdims=True))
        alpha = jnp.exp(m_prev - m_new)
        p = jnp.exp(s - m_new)
        l_sc[...] = alpha * l_sc[...] + jnp.sum(p, axis=-1, keepdims=True)
        acc_sc[...] = alpha * acc_sc[...] + _dot(p.astype(BF16), v_ref[pl.ds(off, tq), :])
        m_sc[...] = m_new

    def far_body(j, carry):
        tile(j, None)
        return carry

    lax.fori_loop(0, i - 1, far_body, 0)

    @pl.when(i >= 1)
    def _():
        tile(i - 1, bias_sc[1])

    tile(i, bias_sc[0])

    lam = _lam_value(lv_ref, lam_init)
    o = acc_sc[...] / l_sc[...]
    o_ref[...] = o[0:tq] - lam * o[tq:2 * tq]


def attn_prompt(qab, kab, vab, rel_bias, lamv, lam_init, tq=256):
    B, S, _ = qab.shape
    r = jnp.arange(tq, dtype=jnp.int32)[:, None]
    c = jnp.arange(tq, dtype=jnp.int32)[None, :]
    diag = jnp.where((c // CHUNK) <= (r // CHUNK), _t5_bucket(c - r), -1)
    prev = _t5_bucket(c - r - tq)
    bkt = jnp.stack([diag, prev]).astype(jnp.int32)
    smem = pl.BlockSpec(memory_space=pltpu.SMEM)
    return pl.pallas_call(
        functools.partial(_attn_kernel, tq=tq, lam_init=lam_init),
        grid=(B, HA, S // tq),
        in_specs=[smem,
                  pl.BlockSpec((4, DA), lambda b, h, i: (0, 0)),
                  pl.BlockSpec((None, tq, DVA), lambda b, h, i: (b, i, h)),
                  pl.BlockSpec((None, S, DVA), lambda b, h, i: (b, 0, h)),
                  pl.BlockSpec((None, S, DVA), lambda b, h, i: (b, 0, h)),
                  pl.BlockSpec((2, tq, tq), lambda b, h, i: (0, 0, 0))],
        out_specs=pl.BlockSpec((None, tq, DVA), lambda b, h, i: (b, i, h)),
        out_shape=jax.ShapeDtypeStruct((B, S, WA), F32),
        scratch_shapes=[pltpu.VMEM((2, 2 * tq, tq), F32),
                        pltpu.VMEM((2 * tq, 1), F32),
                        pltpu.VMEM((2 * tq, 1), F32),
                        pltpu.VMEM((2 * tq, DVA), F32)],
        compiler_params=_params(("arbitrary", "arbitrary", "arbitrary")),
        name="attn_prompt",
    )(rel_bias, lamv, qab, kab, vab, bkt)


def _attn_small_kernel(rb_ref, lv_ref, q_ref, k_ref, v_ref, bkt_ref, o_ref, *, lq, lam_init):
    h = pl.program_id(1)
    bias = _bias_from_buckets(bkt_ref[...], rb_ref, h, 0.0)
    qbd = _block_diag_q(q_ref[...])
    s = _dot_nt(qbd, k_ref[...]) + jnp.concatenate([bias, bias], axis=0)
    m = jnp.max(s, axis=-1, keepdims=True)
    p = jnp.exp(s - m)
    l = jnp.sum(p, axis=-1, keepdims=True)
    o = _dot(p.astype(BF16), v_ref[...]) / l
    lam = _lam_value(lv_ref, lam_init)
    o_ref[...] = o[0:lq] - lam * o[lq:2 * lq]


def attn_small(qab, k_all, v_all, q_pos, k_pos, k_valid, rel_bias, lamv, lam_init):
    B, Lq, _ = qab.shape
    Lk = k_all.shape[1]
    visible = ((k_pos[None, :] // CHUNK) <= (q_pos[:, None] // CHUNK)) & k_valid[None, :]
    bkt = jnp.where(visible, _t5_bucket(k_pos[None, :] - q_pos[:, None]), -1).astype(jnp.int32)
    smem = pl.BlockSpec(memory_space=pltpu.SMEM)
    return pl.pallas_call(
        functools.partial(_attn_small_kernel, lq=Lq, lam_init=lam_init),
        grid=(B, HA),
        in_specs=[smem,
                  pl.BlockSpec((4, DA), lambda b, h: (0, 0)),
                  pl.BlockSpec((None, Lq, DVA), lambda b, h: (b, 0, h)),
                  pl.BlockSpec((None, Lk, DVA), lambda b, h: (b, 0, h)),
                  pl.BlockSpec((None, Lk, DVA), lambda b, h: (b, 0, h)),
                  pl.BlockSpec((Lq, Lk), lambda b, h: (0, 0))],
        out_specs=pl.BlockSpec((None, Lq, DVA), lambda b, h: (b, 0, h)),
        out_shape=jax.ShapeDtypeStruct((B, Lq, WA), F32),
        compiler_params=_params(("arbitrary", "arbitrary")),
        name="attn_sample",
    )(rel_bias, lamv, qab, k_all, v_all, bkt)


def _ret_kernel(lg_ref, q_ref, k_ref, v_ref, s0_ref, o_ref, so_ref, state_sc, decay_sc, *, C):
    h = pl.program_id(1)
    c = pl.program_id(2)
    lg = lg_ref[h]

    @pl.when(c == 0)
    def _():
        state_sc[...] = s0_ref[...]
        r = lax.broadcasted_iota(jnp.int32, (C, C), 0)
        cc = lax.broadcasted_iota(jnp.int32, (C, C), 1)
        diff = (r - cc).astype(F32)
        decay_sc[...] = jnp.where(diff >= 0, jnp.exp(jnp.maximum(diff, 0.0) * lg), 0.0)

    n = lax.broadcasted_iota(jnp.int32, (C, 1), 0).astype(F32)
    xi = jnp.exp((n + 1.0) * lg)
    zeta = jnp.exp((C - 1.0 - n) * lg)
    q = q_ref[...]
    k = k_ref[...]
    v = v_ref[...]
    state = state_sc[...]
    scores = _dot_nt(q, k.astype(BF16)) * decay_sc[...]
    intra = _dot(scores.astype(BF16), v)
    cross = _dot(q, state.astype(BF16)) * xi
    o_ref[...] = intra + cross
    kz = (k * zeta).astype(BF16)
    state_sc[...] = jnp.exp(C * lg) * state + _dot_tn(kz, v)

    @pl.when(c == pl.num_programs(2) - 1)
    def _():
        so_ref[...] = state_sc[...]


def retention(qrb, kr, vrb, state0, C):
    B, L, _ = qrb.shape
    lg = jnp.log(1.0 - 2.0 ** (-5.0 - jnp.arange(HR, dtype=F32)))
    blk = pl.BlockSpec((None, C, DKR), lambda b, h, c: (b, c, h))
    st = pl.BlockSpec((None, None, DKR, DVR), lambda b, h, c: (b, h, 0, 0))
    return pl.pallas_call(
        functools.partial(_ret_kernel, C=C),
        grid=(B, HR, L // C),
        in_specs=[pl.BlockSpec(memory_space=pltpu.SMEM), blk, blk, blk, st],
        out_specs=[blk, st],
        out_shape=[jax.ShapeDtypeStruct((B, L, WR), F32),
                   jax.ShapeDtypeStruct((B, HR, DKR, DVR), F32)],
        scratch_shapes=[pltpu.VMEM((DKR, DVR), F32), pltpu.VMEM((C, C), F32)],
        compiler_params=_params(("arbitrary", "arbitrary", "arbitrary")),
        name="retention",
    )(lg, qrb, kr, vrb, state0)


def _outmix_kernel(x_ref, oa_ref, or_ref, gr_ref, ga_ref, gb_ref, g1_ref, sh2_ref, sc2_ref, n2_ref,
                   sa_ref, sr_ref, wba_ref, wbr_ref, wo_ref, wpq_ref,
                   x1_ref, h2_ref, qp_ref, *, lam_init):
    sa = sa_ref[...] * 1.0
    sr = sr_ref[...]
    gr = gr_ref[...]
    silu_gr = gr * jax.nn.sigmoid(gr)
    ya_parts, yr_parts = [], []
    for hh in range(HA):
        sl = slice(hh * DVA, (hh + 1) * DVA)
        ya_parts.append(_rms_rows(oa_ref[:, sl]) * sa * (1.0 - lam_init))
        yr_parts.append(silu_gr[:, sl] * (_rms_rows(or_ref[:, sl]) * sr))
    ya = jnp.concatenate(ya_parts, axis=-1).astype(BF16)
    yr = jnp.concatenate(yr_parts, axis=-1).astype(BF16)
    y = (jax.nn.sigmoid(ga_ref[...]) * _dot(ya, wba_ref[...])
         + jax.nn.sigmoid(gb_ref[...]) * _dot(yr, wbr_ref[...]))
    out = _dot(y.astype(BF16), wo_ref[...])
    x1 = x_ref[...] + g1_ref[...] * out
    x1_ref[...] = x1
    h2 = (_rms_rows(x1) * n2_ref[...] * (1.0 + sc2_ref[...]) + sh2_ref[...]).astype(BF16)
    h2_ref[...] = h2
    qp_ref[...] = _dot(h2, wpq_ref[...]).astype(BF16)


def out_mix(x, oa, orr, gr, ga, gb, g1, sh2, sc2, norm2, subln_a, subln_r,
            w_ba_b, w_br_b, w_o_b, w_pq_b, lam_init, tm):
    B, L, D = x.shape
    nq = w_pq_b.shape[1]
    row = lambda n: pl.BlockSpec((None, tm, n), lambda b, i: (b, i, 0))
    mod = pl.BlockSpec((None, 1, D), lambda b, i: (b, 0, 0))
    full = lambda a: pl.BlockSpec(a.shape, lambda b, i: (0,) * a.ndim)
    n2 = norm2.reshape(1, D)
    sa = subln_a.reshape(1, DVA)
    sr = subln_r.reshape(1, DVR)
    return pl.pallas_call(
        functools.partial(_outmix_kernel, lam_init=lam_init),
        grid=(B, L // tm),
        in_specs=[row(D), row(WA), row(WR), row(WR), row(D), row(D), mod, mod, mod,
                  full(n2), full(sa), full(sr), full(w_ba_b), full(w_br_b), full(w_o_b), full(w_pq_b)],
        out_specs=[row(D), row(D), row(nq)],
        out_shape=[jax.ShapeDtypeStruct((B, L, D), F32),
                   jax.ShapeDtypeStruct((B, L, D), BF16),
                   jax.ShapeDtypeStruct((B, L, nq), BF16)],
        compiler_params=_params(("arbitrary", "arbitrary")),
        name="out_mix",
    )(x, oa, orr, gr, ga, gb, g1.reshape(B, 1, D), sh2.reshape(B, 1, D), sc2.reshape(B, 1, D),
      n2, sa, sr, w_ba_b, w_br_b, w_o_b, w_pq_b)


def _topk_rows(s, k):
    n = s.shape[0]
    iota = lax.broadcasted_iota(jnp.int32, s.shape, 0)
    work = s
    rank = jnp.full(s.shape, float(PEER_TOPK * PEER_TOPK), F32)
    vals = []
    for r in range(k):
        m = jnp.max(work, axis=0, keepdims=True)
        idx = jnp.min(jnp.where(work == m, iota, n), axis=0, keepdims=True)
        sel = iota == idx
        rank = jnp.where(sel, float(r), rank)
        work = jnp.where(sel, -jnp.inf, work)
        vals.append(m)
    return vals, rank


def _route_kernel(q_ref, keys_ref, e1_ref, e2_ref, r2_ref, c_ref):
    q = q_ref[...]
    s1 = _dot_nt(keys_ref[0], q[:, 0:KEY_DIM])
    s2 = _dot_nt(keys_ref[1], q[:, KEY_DIM:2 * KEY_DIM])
    v1, rank1 = _topk_rows(s1, PEER_TOPK)
    v2, rank2 = _topk_rows(s2, PEER_TOPK)
    tm = s1.shape[1]
    v2m = jnp.concatenate(v2, axis=0)
    cand = jnp.concatenate([v1[a] + v2m for a in range(PEER_TOPK)], axis=0)
    _, crank = _topk_rows(cand, PEER_TOPK)
    sel = crank < float(PEER_TOPK)
    cmax = v1[0] + v2[0]
    z = jnp.sum(jnp.where(sel, jnp.exp(cand - cmax), 0.0), axis=0, keepdims=True)
    cidx = jnp.zeros((N_KEYS, tm), F32)
    for a in range(PEER_TOPK):
        cnt = jnp.sum(sel[a * PEER_TOPK:(a + 1) * PEER_TOPK].astype(F32), axis=0, keepdims=True)
        cidx = jnp.where(rank1 == float(a), cnt, cidx)
    e1_ref[...] = jnp.exp(s1 - v1[0]) / z
    e2_ref[...] = jnp.exp(s2 - v2[0])
    r2_ref[...] = rank2
    c_ref[...] = cidx


def peer_route(qp, keys_b, tm):
    T = qp.shape[0]
    out = pl.BlockSpec((None, N_KEYS, tm), lambda t, h: (h, 0, t))
    shp = jax.ShapeDtypeStruct((PEER_HEADS, N_KEYS, T), F32)
    return pl.pallas_call(
        _route_kernel,
        grid=(T // tm, PEER_HEADS),
        in_specs=[pl.BlockSpec((tm, 2 * KEY_DIM), lambda t, h: (t, h)),
                  pl.BlockSpec((2, N_KEYS, KEY_DIM), lambda t, h: (h, 0, 0))],
        out_specs=[out, out, out, out],
        out_shape=[shp, shp, shp, shp],
        compiler_params=_params(("arbitrary", "arbitrary")),
        name="peer_route",
    )(qp, keys_b)


def _peer_kernel(h2_ref, u_ref, vt_ref, e1_ref, e2_ref, r2_ref, c_ref, x1_ref, g2_ref, fn_ref,
                 y_ref, acc_sc, *, ni):
    e = pl.program_id(1)

    @pl.when(e == 0)
    def _():
        acc_sc[...] = jnp.zeros(acc_sc.shape, F32)

    h2 = h2_ref[...]
    ws = []
    for ii in range(ni):
        i = e * ni + ii
        a = _dot_nt(u_ref[ii * N_KEYS:(ii + 1) * N_KEYS, :], h2)
        g = jnp.zeros(a.shape, F32)
        for hh in range(PEER_HEADS):
            c_row = c_ref[hh, pl.ds(i, 1), :]
            e1_row = e1_ref[hh, pl.ds(i, 1), :]
            g = g + jnp.where(r2_ref[hh] < c_row, e2_ref[hh] * e1_row, 0.0)
        act = 0.5 * a * (1.0 + lax.erf(a * (2.0 ** -0.5)))
        ws.append((g * act).astype(BF16))
    w = jnp.concatenate(ws, axis=0)
    acc_sc[...] += _dot(vt_ref[...], w)

    @pl.when(e == pl.num_programs(1) - 1)
    def _():
        x2 = x1_ref[...] + g2_ref[...] * acc_sc[...].T
        y_ref[...] = _rms_rows(x2) * fn_ref[...]


def peer_experts(h2, u_b, vt_b, e1, e2, r2, cc, x1, g2tok, final_norm, tm, ni=4):
    T, D = h2.shape
    ne = u_b.shape[0]
    tok = pl.BlockSpec((tm, D), lambda t, e: (t, 0))
    gate = pl.BlockSpec((PEER_HEADS, N_KEYS, tm), lambda t, e: (0, 0, t))
    return pl.pallas_call(
        functools.partial(_peer_kernel, ni=ni),
        grid=(T // tm, ne // (ni * N_KEYS)),
        in_specs=[tok,
                  pl.BlockSpec((ni * N_KEYS, D), lambda t, e: (e, 0)),
                  pl.BlockSpec((D, ni * N_KEYS), lambda t, e: (0, e)),
                  gate, gate, gate, gate, tok, tok,
                  pl.BlockSpec((1, D), lambda t, e: (0, 0))],
        out_specs=tok,
        out_shape=jax.ShapeDtypeStruct((T, D), F32),
        scratch_shapes=[pltpu.VMEM((D, tm), F32)],
        compiler_params=_params(("arbitrary", "arbitrary")),
        name="peer_experts",
    )(h2, u_b, vt_b, e1, e2, r2, cc, x1, g2tok, final_norm.reshape(1, D))


def _rot_tables(pos):
    inv = 1.0 / (10000.0 ** jnp.linspace(0.0, 1.0, DKR // 2, dtype=F32))
    ang = pos[:, None].astype(F32) * inv[None, :]
    cos, sin = jnp.cos(ang), jnp.sin(ang)
    return jnp.concatenate([cos, cos], axis=-1), jnp.concatenate([-sin, sin], axis=-1)


def _pick_tile(n, pref):
    t = min(n, pref)
    assert n % t == 0, (n, t)
    return t


def _trunk(x, mods, pos, lam_init, lamv, rel_bias, w, attend, state0, ret_chunk, final_norm):
    (norm1, norm2, w_in_b, subln_a, subln_r, w_ba_b, w_br_b, w_o_b, w_pq_b, keys_b, u_b, vt_b) = w
    sh1, sc1, g1, sh2, sc2, g2 = mods
    B, L, D = x.shape
    T = B * L
    cos2, sin2 = _rot_tables(pos)
    tm = _pick_tile(L, 256)
    ka, va, qab, kab, vab, qrb, kr, vrb, gr, ga, gb = in_proj(x, sh1, sc1, norm1, w_in_b, cos2, sin2, tm)
    oa = attend(qab, kab, vab)
    orr, st = retention(qrb, kr, vrb, state0, ret_chunk)
    x1, h2, qp = out_mix(x, oa, orr, gr, ga, gb, g1, sh2, sc2, norm2, subln_a, subln_r,
                         w_ba_b, w_br_b, w_o_b, w_pq_b, lam_init, tm)
    e1, e2, r2, cc = peer_route(qp.reshape(T, -1), keys_b, _pick_tile(T, 256))
    g2tok = jnp.broadcast_to(g2[:, None, :], (B, L, D)).reshape(T, D)
    y = peer_experts(h2.reshape(T, D), u_b, vt_b, e1, e2, r2, cc, x1.reshape(T, D), g2tok,
                     final_norm, _pick_tile(T, 512))
    return y.reshape(B, L, D), ka.reshape(B, L, HA, 2 * DA), va.reshape(B, L, HA, DVA), st


def kernel(x_prompt, x_sample, cache_k, cache_v, state_ret, c_prompt, c_sample, w_ada, b_ada, norm1,
           norm2, w_in, lam_q1, lam_k1, lam_q2, lam_k2, subln_a, subln_r, w_ba, w_br, w_o, rel_bias,
           w_pq, peer_keys, peer_u, peer_v, final_norm):
    depth = w_ada.shape[0]
    assert depth == 1, "the fused final norm assumes a single layer"
    Bp, Lp, D = x_prompt.shape
    Bs, Ls, _ = x_sample.shape
    past = cache_k.shape[2]
    pos_p = jnp.arange(Lp, dtype=jnp.int32)
    pos_s = past + jnp.arange(Ls, dtype=jnp.int32)
    lk = past + Ls
    lk_pad = -(-lk // LANES) * LANES
    k_pos_s = jnp.arange(lk_pad, dtype=jnp.int32)
    k_valid_s = k_pos_s < lk

    l = 0
    lam_init = 0.8 - 0.6 * math.exp(-0.3 * l)
    lamv = jnp.stack([lam_q1[l], lam_k1[l], lam_q2[l], lam_k2[l]]).astype(F32)
    mod = ada_mod(jnp.concatenate([c_prompt, c_sample], axis=0), w_ada[l], b_ada[l])
    mods = jnp.split(mod, 6, axis=-1)
    mods_p = [m[:Bp] for m in mods]
    mods_s = [m[Bp:] for m in mods]
    w = (norm1[l], norm2[l], w_in[l].astype(BF16), subln_a[l], subln_r[l], w_ba[l].astype(BF16),
         w_br[l].astype(BF16), w_o[l].astype(BF16), w_pq[l].astype(BF16),
         peer_keys[l].reshape(PEER_HEADS * 2, N_KEYS, KEY_DIM).astype(BF16),
         peer_u[l].astype(BF16), peer_v[l].T.astype(BF16))

    attend_p = lambda q, k, v: attn_prompt(q, k, v, rel_bias, lamv, lam_init)
    zero_state = jnp.zeros((Bp, HR, DKR, DVR), F32)
    yp, kp, vp, sp = _trunk(x_prompt, mods_p, pos_p, lam_init, lamv, rel_bias, w, attend_p,
                            zero_state, _pick_tile(Lp, 256), final_norm)

    def attend_s(q, k, v):
        padk = jnp.zeros((Bs, lk_pad - lk, WA), BF16)
        kc = cache_k[l].reshape(Bs, past, HA * 2 * DA).astype(BF16)
        vc = cache_v[l].reshape(Bs, past, WA).astype(BF16)
        k_all = jnp.concatenate([kc, k, padk], axis=1)
        v_all = jnp.concatenate([vc, v, padk], axis=1)
        return attn_small(q, k_all, v_all, pos_s, k_pos_s, k_valid_s, rel_bias, lamv, lam_init)

    ys, ks, vs, ss = _trunk(x_sample, mods_s, pos_s, lam_init, lamv, rel_bias, w, attend_s,
                            state_ret[l].astype(F32), Ls, final_norm)
    return (yp, ys, kp[None], vp[None], sp[None], ks[None], vs[None], ss[None])
```

```python
import functools
import math

import jax
import jax.numpy as jnp
from jax import lax
from jax.experimental import pallas as pl
from jax.experimental.pallas import tpu as pltpu

F32 = jnp.float32
BF16 = jnp.bfloat16

CHUNK = 64
HA = 4
DA = 64
DVA = 2 * DA
HR = 4
DKR = 128
DVR = 128
N_BUCKETS = 32
MAX_DIST = 128
PEER_HEADS = 8
N_KEYS = 128
KEY_DIM = 128
PEER_TOPK = 16
EPS = 1e-6
WA = HA * DVA
WR = HR * DVR
NEG = -1e30
LANES = 128
VMEM_LIMIT = 56 * 1024 * 1024


def _params(sem, vmem=VMEM_LIMIT):
    return pltpu.CompilerParams(dimension_semantics=sem, vmem_limit_bytes=vmem)


def _dot(a, b):
    return jnp.dot(a, b, preferred_element_type=F32)


def _dot_nt(a, b):
    return lax.dot_general(a, b, (((1,), (1,)), ((), ())), preferred_element_type=F32)


def _dot_tn(a, b):
    return lax.dot_general(a, b, (((0,), (0,)), ((), ())), preferred_element_type=F32)


def _rms_rows(x):
    return x * lax.rsqrt(jnp.mean(x * x, axis=-1, keepdims=True) + EPS)


def _ada_kernel(c_ref, w_ref, b_ref, o_ref):
    c = c_ref[...]
    s = (c * jax.nn.sigmoid(c)).astype(BF16)
    o_ref[...] = _dot(s, w_ref[...].astype(BF16)) + b_ref[...]


def ada_mod(c, w_ada, b_ada, tn=1024):
    nb, d = c.shape
    n = w_ada.shape[1]
    return pl.pallas_call(
        _ada_kernel,
        grid=(n // tn,),
        in_specs=[pl.BlockSpec((nb, d), lambda j: (0, 0)),
                  pl.BlockSpec((d, tn), lambda j: (0, j)),
                  pl.BlockSpec((1, tn), lambda j: (0, j))],
        out_specs=pl.BlockSpec((nb, tn), lambda j: (0, j)),
        out_shape=jax.ShapeDtypeStruct((nb, n), F32),
        compiler_params=_params(("arbitrary",)),
        name="ada_mod",
    )(c, w_ada, b_ada.reshape(1, n))


def _inproj_kernel(x_ref, sh_ref, sc_ref, n1_ref, w_ref, cos_ref, sin_ref,
                   ka_ref, va_ref, qab_ref, kab_ref, vab_ref, qrb_ref, kr_ref, vrb_ref,
                   gr_ref, ga_ref, gb_ref, vt_ref=None):
    x = x_ref[...]
    h = _rms_rows(x) * n1_ref[...] * (1.0 + sc_ref[...]) + sh_ref[...]
    hb = h.astype(BF16)

    def proj(lo, n):
        return _dot(hb, w_ref[:, lo:lo + n])

    qa = proj(0, 512)
    qab_ref[...] = (qa * (DA ** -0.5)).astype(BF16)
    ka = proj(512, 512)
    ka_ref[...] = ka
    kab_ref[...] = ka.astype(BF16)
    va = proj(1024, 512)
    va_ref[...] = va
    vab_ref[...] = va.astype(BF16)
    if vt_ref is not None:
        vt_ref[...] = va.T.astype(BF16)
    cos2 = cos_ref[...]
    sin2 = sin_ref[...]

    def rot(z):
        parts = []
        for hh in range(HR):
            zh = z[:, hh * DKR:(hh + 1) * DKR]
            parts.append(zh * cos2 + pltpu.roll(zh, DKR // 2, 1) * sin2)
        return jnp.concatenate(parts, axis=-1)

    qrb_ref[...] = rot(proj(1536, 512)).astype(BF16)
    kr_ref[...] = rot(proj(2048, 512)) * (DKR ** -0.5)
    vrb_ref[...] = proj(2560, 512).astype(BF16)
    gr_ref[...] = proj(3072, 512)
    ga_ref[...] = proj(3584, 1024)
    gb_ref[...] = proj(4608, 1024)


def in_proj(x, sh1, sc1, norm1, w_in_b, cos2, sin2, tm, emit_vt):
    B, L, D = x.shape
    d_in = w_in_b.shape[1]
    row = lambda n: pl.BlockSpec((None, tm, n), lambda b, i: (b, i, 0))
    mod = pl.BlockSpec((None, 1, D), lambda b, i: (b, 0, 0))
    f = lambda n, dt: jax.ShapeDtypeStruct((B, L, n), dt)
    vt_spec = [pl.BlockSpec((None, WA, tm), lambda b, i: (b, 0, i))] if emit_vt else []
    vt_shape = [jax.ShapeDtypeStruct((B, WA, L), BF16)] if emit_vt else []
    return pl.pallas_call(
        _inproj_kernel,
        grid=(B, L // tm),
        in_specs=[row(D), mod, mod,
                  pl.BlockSpec((1, D), lambda b, i: (0, 0)),
                  pl.BlockSpec((D, d_in), lambda b, i: (0, 0)),
                  pl.BlockSpec((tm, DKR), lambda b, i: (i, 0)),
                  pl.BlockSpec((tm, DKR), lambda b, i: (i, 0))],
        out_specs=[row(512), row(512), row(512), row(512), row(512), row(512), row(512), row(512),
                   row(512), row(1024), row(1024)] + vt_spec,
        out_shape=[f(512, F32), f(512, F32), f(512, BF16), f(512, BF16), f(512, BF16),
                   f(512, BF16), f(512, F32), f(512, BF16), f(512, F32), f(1024, F32), f(1024, F32)]
        + vt_shape,
        compiler_params=_params(("arbitrary", "arbitrary")),
        name="in_proj",
    )(x, sh1.reshape(B, 1, D), sc1.reshape(B, 1, D), norm1.reshape(1, D), w_in_b, cos2, sin2)


def _t5_bucket(rel):
    nb = N_BUCKETS // 2
    ret = jnp.where(rel > 0, nb, 0)
    n = jnp.abs(rel)
    max_exact = nb // 2
    nf = jnp.maximum(n, max_exact).astype(F32)
    large = max_exact + (jnp.log(nf / max_exact) / math.log(MAX_DIST / max_exact)
                         * (nb - max_exact)).astype(jnp.int32)
    large = jnp.minimum(large, nb - 1)
    return ret + jnp.where(n < max_exact, n, large)


def _bias_from_buckets(bkt, rb_ref, h, shift):
    val = jnp.where(bkt < 0, NEG, 0.0).astype(F32)
    for n in range(N_BUCKETS):
        val = jnp.where(bkt == n, rb_ref[n, h] - shift, val)
    return val


def _lam_value(lv_ref, lam_init):
    lv = lv_ref[...]
    a = jnp.sum(lv[0:1] * lv[1:2], axis=-1, keepdims=True)
    b = jnp.sum(lv[2:3] * lv[3:4], axis=-1, keepdims=True)
    return jnp.exp(a) - jnp.exp(b) + lam_init


def _block_diag_q(q):
    lane = lax.broadcasted_iota(jnp.int32, q.shape, 1)
    zero = jnp.zeros_like(q)
    return jnp.concatenate([jnp.where(lane < DA, q, zero), jnp.where(lane >= DA, q, zero)], axis=0)


def _attn_kernel(rb_ref, lv_ref, q_ref, k_ref, vt_ref, bkt_ref, o_ref,
                 bias_sc, m_sc, l_sc, acc_sc, *, tq, tb, lam_init):
    h = pl.program_id(1)
    i = pl.program_id(2)

    @pl.when(i == 0)
    def _():
        far = rb_ref[N_BUCKETS // 2 - 1, h]
        for t in range(2):
            b = _bias_from_buckets(bkt_ref[t], rb_ref, h, far)
            bias_sc[t, :, 0:tq] = b
            bias_sc[t, :, tq:2 * tq] = b

    qbd = _block_diag_q(q_ref[...])
    m_sc[...] = jnp.full(m_sc.shape, NEG, F32)
    l_sc[...] = jnp.zeros(l_sc.shape, F32)
    acc_sc[...] = jnp.zeros(acc_sc.shape, F32)

    def tile(off, tk, bias=None, nvalid=None):
        s = _dot_nt(k_ref[pl.ds(off, tk), :], qbd)
        if bias is not None:
            s = s + bias
        if nvalid is not None:
            row = lax.broadcasted_iota(jnp.int32, (tk, 1), 0)
            s = jnp.where(row < nvalid, s, NEG)
        m_prev = m_sc[...]
        m_new = jnp.maximum(m_prev, jnp.max(s, axis=0, keepdims=True))
        alpha = jnp.exp(m_prev - m_new)
        p = jnp.exp(s - m_new)
        l_sc[...] = alpha * l_sc[...] + jnp.sum(p, axis=0, keepdims=True)
        acc_sc[...] = alpha * acc_sc[...] + _dot(vt_ref[:, pl.ds(off, tk)], p.astype(BF16))
        m_sc[...] = m_new

    far_end = jnp.maximum(i - 1, 0) * tq
    nfull = far_end // tb
    rem = far_end - nfull * tb

    def far_body(t, carry):
        tile(pl.multiple_of(t * tb, tb), tb)
        return carry

    lax.fori_loop(0, nfull, far_body, 0)

    @pl.when(rem > 0)
    def _():
        tile(pl.multiple_of(nfull * tb, tb), tb, nvalid=rem)

    @pl.when(i >= 1)
    def _():
        tile(pl.multiple_of((i - 1) * tq, tq), tq, bias=bias_sc[1])

    tile(pl.multiple_of(i * tq, tq), tq, bias=bias_sc[0])

    lam = _lam_value(lv_ref, lam_init)
    o = acc_sc[...] / l_sc[...]
    o_ref[...] = (o[:, 0:tq] - lam * o[:, tq:2 * tq]).T


def attn_prompt(qab, kab, vt, rel_bias, lamv, lam_init, tq=256, tb=1024):
    B, S, _ = qab.shape
    tb = min(tb, S)
    assert S % tb == 0 and tb % tq == 0
    c = jnp.arange(tq, dtype=jnp.int32)[:, None]
    r = jnp.arange(tq, dtype=jnp.int32)[None, :]
    diag = jnp.where((c // CHUNK) <= (r // CHUNK), _t5_bucket(c - r), -1)
    prev = _t5_bucket(c - r - tq)
    bkt = jnp.stack([diag, prev]).astype(jnp.int32)
    smem = pl.BlockSpec(memory_space=pltpu.SMEM)
    return pl.pallas_call(
        functools.partial(_attn_kernel, tq=tq, tb=tb, lam_init=lam_init),
        grid=(B, HA, S // tq),
        in_specs=[smem,
                  pl.BlockSpec((4, DA), lambda b, h, i: (0, 0)),
                  pl.BlockSpec((None, tq, DVA), lambda b, h, i: (b, i, h)),
                  pl.BlockSpec((None, S, DVA), lambda b, h, i: (b, 0, h)),
                  pl.BlockSpec((None, DVA, S), lambda b, h, i: (b, h, 0)),
                  pl.BlockSpec((2, tq, tq), lambda b, h, i: (0, 0, 0))],
        out_specs=pl.BlockSpec((None, tq, DVA), lambda b, h, i: (b, i, h)),
        out_shape=jax.ShapeDtypeStruct((B, S, WA), F32),
        scratch_shapes=[pltpu.VMEM((2, tq, 2 * tq), F32),
                        pltpu.VMEM((1, 2 * tq), F32),
                        pltpu.VMEM((1, 2 * tq), F32),
                        pltpu.VMEM((DVA, 2 * tq), F32)],
        compiler_params=_params(("arbitrary", "arbitrary", "arbitrary")),
        name="attn_prompt",
    )(rel_bias, lamv, qab, kab, vt, bkt)


def _attn_small_kernel(rb_ref, lv_ref, q_ref, k_ref, v_ref, bkt_ref, o_ref, *, lq, lam_init):
    h = pl.program_id(1)
    bias = _bias_from_buckets(bkt_ref[...], rb_ref, h, 0.0)
    qbd = _block_diag_q(q_ref[...])
    s = _dot_nt(qbd, k_ref[...]) + jnp.concatenate([bias, bias], axis=0)
    m = jnp.max(s, axis=-1, keepdims=True)
    p = jnp.exp(s - m)
    l = jnp.sum(p, axis=-1, keepdims=True)
    o = _dot(p.astype(BF16), v_ref[...]) / l
    lam = _lam_value(lv_ref, lam_init)
    o_ref[...] = o[0:lq] - lam * o[lq:2 * lq]


def attn_small(qab, k_all, v_all, q_pos, k_pos, k_valid, rel_bias, lamv, lam_init):
    B, Lq, _ = qab.shape
    Lk = k_all.shape[1]
    visible = ((k_pos[None, :] // CHUNK) <= (q_pos[:, None] // CHUNK)) & k_valid[None, :]
    bkt = jnp.where(visible, _t5_bucket(k_pos[None, :] - q_pos[:, None]), -1).astype(jnp.int32)
    smem = pl.BlockSpec(memory_space=pltpu.SMEM)
    return pl.pallas_call(
        functools.partial(_attn_small_kernel, lq=Lq, lam_init=lam_init),
        grid=(B, HA),
        in_specs=[smem,
                  pl.BlockSpec((4, DA), lambda b, h: (0, 0)),
                  pl.BlockSpec((None, Lq, DVA), lambda b, h: (b, 0, h)),
                  pl.BlockSpec((None, Lk, DVA), lambda b, h: (b, 0, h)),
                  pl.BlockSpec((None, Lk, DVA), lambda b, h: (b, 0, h)),
                  pl.BlockSpec((Lq, Lk), lambda b, h: (0, 0))],
        out_specs=pl.BlockSpec((None, Lq, DVA), lambda b, h: (b, 0, h)),
        out_shape=jax.ShapeDtypeStruct((B, Lq, WA), F32),
        compiler_params=_params(("arbitrary", "arbitrary")),
        name="attn_sample",
    )(rel_bias, lamv, qab, k_all, v_all, bkt)


def _ret_kernel(lg_ref, q_ref, k_ref, v_ref, s0_ref, o_ref, so_ref, state_sc, decay_sc, *, C):
    h = pl.program_id(1)
    c = pl.program_id(2)
    lg = lg_ref[h]

    @pl.when(c == 0)
    def _():
        state_sc[...] = s0_ref[...]
        r = lax.broadcasted_iota(jnp.int32, (C, C), 0)
        cc = lax.broadcasted_iota(jnp.int32, (C, C), 1)
        diff = (r - cc).astype(F32)
        decay_sc[...] = jnp.where(diff >= 0, jnp.exp(jnp.maximum(diff, 0.0) * lg), 0.0)

    n = lax.broadcasted_iota(jnp.int32, (C, 1), 0).astype(F32)
    xi = jnp.exp((n + 1.0) * lg)
    zeta = jnp.exp((C - 1.0 - n) * lg)
    q = q_ref[...]
    k = k_ref[...]
    v = v_ref[...]
    state = state_sc[...]
    scores = _dot_nt(q, k.astype(BF16)) * decay_sc[...]
    intra = _dot(scores.astype(BF16), v)
    cross = _dot(q, state.astype(BF16)) * xi
    o_ref[...] = intra + cross
    kz = (k * zeta).astype(BF16)
    state_sc[...] = jnp.exp(C * lg) * state + _dot_tn(kz, v)

    @pl.when(c == pl.num_programs(2) - 1)
    def _():
        so_ref[...] = state_sc[...]


def retention(qrb, kr, vrb, state0, C):
    B, L, _ = qrb.shape
    lg = jnp.log(1.0 - 2.0 ** (-5.0 - jnp.arange(HR, dtype=F32)))
    blk = pl.BlockSpec((None, C, DKR), lambda b, h, c: (b, c, h))
    st = pl.BlockSpec((None, None, DKR, DVR), lambda b, h, c: (b, h, 0, 0))
    return pl.pallas_call(
        functools.partial(_ret_kernel, C=C),
        grid=(B, HR, L // C),
        in_specs=[pl.BlockSpec(memory_space=pltpu.SMEM), blk, blk, blk, st],
        out_specs=[blk, st],
        out_shape=[jax.ShapeDtypeStruct((B, L, WR), F32),
                   jax.ShapeDtypeStruct((B, HR, DKR, DVR), F32)],
        scratch_shapes=[pltpu.VMEM((DKR, DVR), F32), pltpu.VMEM((C, C), F32)],
        compiler_params=_params(("arbitrary", "arbitrary", "arbitrary")),
        name="retention",
    )(lg, qrb, kr, vrb, state0)


def _outmix_kernel(x_ref, oa_ref, or_ref, gr_ref, ga_ref, gb_ref, g1_ref, sh2_ref, sc2_ref, n2_ref,
                   sa_ref, sr_ref, wba_ref, wbr_ref, wo_ref, wpq_ref,
                   x1_ref, h2_ref, qp_ref, *, lam_init):
    sa = sa_ref[...] * 1.0
    sr = sr_ref[...]
    gr = gr_ref[...]
    silu_gr = gr * jax.nn.sigmoid(gr)
    ya_parts, yr_parts = [], []
    for hh in range(HA):
        sl = slice(hh * DVA, (hh + 1) * DVA)
        ya_parts.append(_rms_rows(oa_ref[:, sl]) * sa * (1.0 - lam_init))
        yr_parts.append(silu_gr[:, sl] * (_rms_rows(or_ref[:, sl]) * sr))
    ya = jnp.concatenate(ya_parts, axis=-1).astype(BF16)
    yr = jnp.concatenate(yr_parts, axis=-1).astype(BF16)
    y = (jax.nn.sigmoid(ga_ref[...]) * _dot(ya, wba_ref[...])
         + jax.nn.sigmoid(gb_ref[...]) * _dot(yr, wbr_ref[...]))
    out = _dot(y.astype(BF16), wo_ref[...])
    x1 = x_ref[...] + g1_ref[...] * out
    x1_ref[...] = x1
    h2 = (_rms_rows(x1) * n2_ref[...] * (1.0 + sc2_ref[...]) + sh2_ref[...]).astype(BF16)
    h2_ref[...] = h2
    qp_ref[...] = _dot(h2, wpq_ref[...]).astype(BF16)


def out_mix(x, oa, orr, gr, ga, gb, g1, sh2, sc2, norm2, subln_a, subln_r,
            w_ba_b, w_br_b, w_o_b, w_pq_b, lam_init, tm):
    B, L, D = x.shape
    nq = w_pq_b.shape[1]
    row = lambda n: pl.BlockSpec((None, tm, n), lambda b, i: (b, i, 0))
    mod = pl.BlockSpec((None, 1, D), lambda b, i: (b, 0, 0))
    full = lambda a: pl.BlockSpec(a.shape, lambda b, i: (0,) * a.ndim)
    n2 = norm2.reshape(1, D)
    sa = subln_a.reshape(1, DVA)
    sr = subln_r.reshape(1, DVR)
    return pl.pallas_call(
        functools.partial(_outmix_kernel, lam_init=lam_init),
        grid=(B, L // tm),
        in_specs=[row(D), row(WA), row(WR), row(WR), row(D), row(D), mod, mod, mod,
                  full(n2), full(sa), full(sr), full(w_ba_b), full(w_br_b), full(w_o_b), full(w_pq_b)],
        out_specs=[row(D), row(D), row(nq)],
        out_shape=[jax.ShapeDtypeStruct((B, L, D), F32),
                   jax.ShapeDtypeStruct((B, L, D), BF16),
                   jax.ShapeDtypeStruct((B, L, nq), BF16)],
        compiler_params=_params(("arbitrary", "arbitrary")),
        name="out_mix",
    )(x, oa, orr, gr, ga, gb, g1.reshape(B, 1, D), sh2.reshape(B, 1, D), sc2.reshape(B, 1, D),
      n2, sa, sr, w_ba_b, w_br_b, w_o_b, w_pq_b)


def _topk_rows(s, k):
    n = s.shape[0]
    iota = lax.broadcasted_iota(jnp.int32, s.shape, 0)
    work = s
    rank = jnp.full(s.shape, float(PEER_TOPK * PEER_TOPK), F32)
    vals = []
    for r in range(k):
        m = jnp.max(work, axis=0, keepdims=True)
        idx = jnp.min(jnp.where(work == m, iota, n), axis=0, keepdims=True)
        sel = iota == idx
        rank = jnp.where(sel, float(r), rank)
        work = jnp.where(sel, -jnp.inf, work)
        vals.append(m)
    return vals, rank


def _route_kernel(q_ref, keys_ref, e1_ref, e2_ref, r2_ref, c_ref):
    q = q_ref[...]
    s1 = _dot_nt(keys_ref[0], q[:, 0:KEY_DIM])
    s2 = _dot_nt(keys_ref[1], q[:, KEY_DIM:2 * KEY_DIM])
    v1, rank1 = _topk_rows(s1, PEER_TOPK)
    v2, rank2 = _topk_rows(s2, PEER_TOPK)
    tm = s1.shape[1]
    v2m = jnp.concatenate(v2, axis=0)
    cand = jnp.concatenate([v1[a] + v2m for a in range(PEER_TOPK)], axis=0)
    _, crank = _topk_rows(cand, PEER_TOPK)
    sel = crank < float(PEER_TOPK)
    cmax = v1[0] + v2[0]
    z = jnp.sum(jnp.where(sel, jnp.exp(cand - cmax), 0.0), axis=0, keepdims=True)
    cidx = jnp.zeros((N_KEYS, tm), F32)
    for a in range(PEER_TOPK):
        cnt = jnp.sum(sel[a * PEER_TOPK:(a + 1) * PEER_TOPK].astype(F32), axis=0, keepdims=True)
        cidx = jnp.where(rank1 == float(a), cnt, cidx)
    e1_ref[...] = jnp.exp(s1 - v1[0]) / z
    e2_ref[...] = jnp.exp(s2 - v2[0])
    r2_ref[...] = rank2
    c_ref[...] = cidx


def peer_route(qp, keys_b, tm):
    T = qp.shape[0]
    out = pl.BlockSpec((None, N_KEYS, tm), lambda t, h: (h, 0, t))
    shp = jax.ShapeDtypeStruct((PEER_HEADS, N_KEYS, T), F32)
    return pl.pallas_call(
        _route_kernel,
        grid=(T // tm, PEER_HEADS),
        in_specs=[pl.BlockSpec((tm, 2 * KEY_DIM), lambda t, h: (t, h)),
                  pl.BlockSpec((2, N_KEYS, KEY_DIM), lambda t, h: (h, 0, 0))],
        out_specs=[out, out, out, out],
        out_shape=[shp, shp, shp, shp],
        compiler_params=_params(("arbitrary", "arbitrary")),
        name="peer_route",
    )(qp, keys_b)


def _peer_kernel(h2_ref, u_ref, vt_ref, e1_ref, e2_ref, r2_ref, c_ref, x1_ref, g2_ref, fn_ref,
                 y_ref, acc_sc, *, ni):
    e = pl.program_id(1)

    @pl.when(e == 0)
    def _():
        acc_sc[...] = jnp.zeros(acc_sc.shape, F32)

    h2 = h2_ref[...]
    ws = []
    for ii in range(ni):
        i = e * ni + ii
        a = _dot_nt(u_ref[ii * N_KEYS:(ii + 1) * N_KEYS, :], h2)
        g = jnp.zeros(a.shape, F32)
        for hh in range(PEER_HEADS):
            c_row = c_ref[hh, pl.ds(i, 1), :]
            e1_row = e1_ref[hh, pl.ds(i, 1), :]
            g = g + jnp.where(r2_ref[hh] < c_row, e2_ref[hh] * e1_row, 0.0)
        act = 0.5 * a * (1.0 + lax.erf(a * (2.0 ** -0.5)))
        ws.append((g * act).astype(BF16))
    w = jnp.concatenate(ws, axis=0)
    acc_sc[...] += _dot(vt_ref[...], w)

    @pl.when(e == pl.num_programs(1) - 1)
    def _():
        x2 = x1_ref[...] + g2_ref[...] * acc_sc[...].T
        y_ref[...] = _rms_rows(x2) * fn_ref[...]


def peer_experts(h2, u_b, vt_b, e1, e2, r2, cc, x1, g2tok, final_norm, tm, ni=4):
    T, D = h2.shape
    ne = u_b.shape[0]
    tok = pl.BlockSpec((tm, D), lambda t, e: (t, 0))
    gate = pl.BlockSpec((PEER_HEADS, N_KEYS, tm), lambda t, e: (0, 0, t))
    return pl.pallas_call(
        functools.partial(_peer_kernel, ni=ni),
        grid=(T // tm, ne // (ni * N_KEYS)),
        in_specs=[tok,
                  pl.BlockSpec((ni * N_KEYS, D), lambda t, e: (e, 0)),
                  pl.BlockSpec((D, ni * N_KEYS), lambda t, e: (0, e)),
                  gate, gate, gate, gate, tok, tok,
                  pl.BlockSpec((1, D), lambda t, e: (0, 0))],
        out_specs=tok,
        out_shape=jax.ShapeDtypeStruct((T, D), F32),
        scratch_shapes=[pltpu.VMEM((D, tm), F32)],
        compiler_params=_params(("arbitrary", "arbitrary")),
        name="peer_experts",
    )(h2, u_b, vt_b, e1, e2, r2, cc, x1, g2tok, final_norm.reshape(1, D))


def _rot_tables(pos):
    inv = 1.0 / (10000.0 ** jnp.linspace(0.0, 1.0, DKR // 2, dtype=F32))
    ang = pos[:, None].astype(F32) * inv[None, :]
    cos, sin = jnp.cos(ang), jnp.sin(ang)
    return jnp.concatenate([cos, cos], axis=-1), jnp.concatenate([-sin, sin], axis=-1)


def _pick_tile(n, pref):
    t = min(n, pref)
    assert n % t == 0, (n, t)
    return t


def _trunk(x, mods, pos, lam_init, lamv, rel_bias, w, attend, values_transposed, state0, ret_chunk,
           final_norm):
    (norm1, norm2, w_in_b, subln_a, subln_r, w_ba_b, w_br_b, w_o_b, w_pq_b, keys_b, u_b, vt_b) = w
    sh1, sc1, g1, sh2, sc2, g2 = mods
    B, L, D = x.shape
    T = B * L
    cos2, sin2 = _rot_tables(pos)
    tm = _pick_tile(L, 256)
    ka, va, qab, kab, vab, qrb, kr, vrb, gr, ga, gb, *vt = in_proj(
        x, sh1, sc1, norm1, w_in_b, cos2, sin2, tm, emit_vt=values_transposed)
    oa = attend(qab, kab, vt[0] if values_transposed else vab)
    orr, st = retention(qrb, kr, vrb, state0, ret_chunk)
    x1, h2, qp = out_mix(x, oa, orr, gr, ga, gb, g1, sh2, sc2, norm2, subln_a, subln_r,
                         w_ba_b, w_br_b, w_o_b, w_pq_b, lam_init, tm)
    e1, e2, r2, cc = peer_route(qp.reshape(T, -1), keys_b, _pick_tile(T, 256))
    g2tok = jnp.broadcast_to(g2[:, None, :], (B, L, D)).reshape(T, D)
    y = peer_experts(h2.reshape(T, D), u_b, vt_b, e1, e2, r2, cc, x1.reshape(T, D), g2tok,
                     final_norm, _pick_tile(T, 512))
    return y.reshape(B, L, D), ka.reshape(B, L, HA, 2 * DA), va.reshape(B, L, HA, DVA), st


def kernel(x_prompt, x_sample, cache_k, cache_v, state_ret, c_prompt, c_sample, w_ada, b_ada, norm1,
           norm2, w_in, lam_q1, lam_k1, lam_q2, lam_k2, subln_a, subln_r, w_ba, w_br, w_o, rel_bias,
           w_pq, peer_keys, peer_u, peer_v, final_norm):
    depth = w_ada.shape[0]
    assert depth == 1, "the fused final norm assumes a single layer"
    Bp, Lp, D = x_prompt.shape
    Bs, Ls, _ = x_sample.shape
    past = cache_k.shape[2]
    pos_p = jnp.arange(Lp, dtype=jnp.int32)
    pos_s = past + jnp.arange(Ls, dtype=jnp.int32)
    lk = past + Ls
    lk_pad = -(-lk // LANES) * LANES
    k_pos_s = jnp.arange(lk_pad, dtype=jnp.int32)
    k_valid_s = k_pos_s < lk

    l = 0
    lam_init = 0.8 - 0.6 * math.exp(-0.3 * l)
    lamv = jnp.stack([lam_q1[l], lam_k1[l], lam_q2[l], lam_k2[l]]).astype(F32)
    mod = ada_mod(jnp.concatenate([c_prompt, c_sample], axis=0), w_ada[l], b_ada[l])
    mods = jnp.split(mod, 6, axis=-1)
    mods_p = [m[:Bp] for m in mods]
    mods_s = [m[Bp:] for m in mods]
    w = (norm1[l], norm2[l], w_in[l].astype(BF16), subln_a[l], subln_r[l], w_ba[l].astype(BF16),
         w_br[l].astype(BF16), w_o[l].astype(BF16), w_pq[l].astype(BF16),
         peer_keys[l].reshape(PEER_HEADS * 2, N_KEYS, KEY_DIM).astype(BF16),
         peer_u[l].astype(BF16), peer_v[l].T.astype(BF16))

    attend_p = lambda q, k, v: attn_prompt(q, k, v, rel_bias, lamv, lam_init)
    zero_state = jnp.zeros((Bp, HR, DKR, DVR), F32)
    yp, kp, vp, sp = _trunk(x_prompt, mods_p, pos_p, lam_init, lamv, rel_bias, w, attend_p, True,
                            zero_state, _pick_tile(Lp, 256), final_norm)

    def attend_s(q, k, v):
        padk = jnp.zeros((Bs, lk_pad - lk, WA), BF16)
        kc = cache_k[l].reshape(Bs, past, HA * 2 * DA).astype(BF16)
        vc = cache_v[l].reshape(Bs, past, WA).astype(BF16)
        k_all = jnp.concatenate([kc, k, padk], axis=1)
        v_all = jnp.concatenate([vc, v, padk], axis=1)
        return attn_small(q, k_all, v_all, pos_s, k_pos_s, k_valid_s, rel_bias, lamv, lam_init)

    ys, ks, vs, ss = _trunk(x_sample, mods_s, pos_s, lam_init, lamv, rel_bias, w, attend_s, False,
                            state_ret[l].astype(F32), Ls, final_norm)
    return (yp, ys, kp[None], vp[None], sp[None], ks[None], vs[None], ss[None])
```

```python
import functools
import math

import jax
import jax.numpy as jnp
from jax import lax
from jax.experimental import pallas as pl
from jax.experimental.pallas import tpu as pltpu

F32 = jnp.float32
BF16 = jnp.bfloat16

CHUNK = 64
HA = 4
DA = 64
DVA = 2 * DA
HR = 4
DKR = 128
DVR = 128
N_BUCKETS = 32
MAX_DIST = 128
PEER_HEADS = 8
N_KEYS = 128
KEY_DIM = 128
PEER_TOPK = 16
EPS = 1e-6
WA = HA * DVA
WR = HR * DVR
NEG = -1e30
LANES = 128
VMEM_LIMIT = 56 * 1024 * 1024


def _params(sem, vmem=VMEM_LIMIT):
    return pltpu.CompilerParams(dimension_semantics=sem, vmem_limit_bytes=vmem)


def _dot(a, b):
    return jnp.dot(a, b, preferred_element_type=F32)


def _dot_nt(a, b):
    return lax.dot_general(a, b, (((1,), (1,)), ((), ())), preferred_element_type=F32)


def _dot_tn(a, b):
    return lax.dot_general(a, b, (((0,), (0,)), ((), ())), preferred_element_type=F32)


def _rms_rows(x):
    return x * lax.rsqrt(jnp.mean(x * x, axis=-1, keepdims=True) + EPS)


def _ada_kernel(c_ref, w_ref, b_ref, o_ref):
    c = c_ref[...]
    s = (c * jax.nn.sigmoid(c)).astype(BF16)
    o_ref[...] = _dot(s, w_ref[...].astype(BF16)) + b_ref[...]


def ada_mod(c, w_ada, b_ada, tn=1024):
    nb, d = c.shape
    n = w_ada.shape[1]
    return pl.pallas_call(
        _ada_kernel,
        grid=(n // tn,),
        in_specs=[pl.BlockSpec((nb, d), lambda j: (0, 0)),
                  pl.BlockSpec((d, tn), lambda j: (0, j)),
                  pl.BlockSpec((1, tn), lambda j: (0, j))],
        out_specs=pl.BlockSpec((nb, tn), lambda j: (0, j)),
        out_shape=jax.ShapeDtypeStruct((nb, n), F32),
        compiler_params=_params(("arbitrary",)),
        name="ada_mod",
    )(c, w_ada, b_ada.reshape(1, n))


def _inproj_kernel(x_ref, sh_ref, sc_ref, n1_ref, w_ref, cos_ref, sin_ref,
                   ka_ref, va_ref, qab_ref, kab_ref, vab_ref, qrb_ref, kr_ref, vrb_ref,
                   gr_ref, ga_ref, gb_ref, vt_ref=None):
    x = x_ref[...]
    h = _rms_rows(x) * n1_ref[...] * (1.0 + sc_ref[...]) + sh_ref[...]
    hb = h.astype(BF16)

    def proj(lo, n):
        return _dot(hb, w_ref[:, lo:lo + n])

    qa = proj(0, 512)
    qab_ref[...] = (qa * (DA ** -0.5)).astype(BF16)
    ka = proj(512, 512)
    ka_ref[...] = ka
    kab_ref[...] = ka.astype(BF16)
    va = proj(1024, 512)
    va_ref[...] = va
    vab_ref[...] = va.astype(BF16)
    if vt_ref is not None:
        vt_ref[...] = va.T.astype(BF16)
    cos2 = cos_ref[...]
    sin2 = sin_ref[...]

    def rot(z):
        parts = []
        for hh in range(HR):
            zh = z[:, hh * DKR:(hh + 1) * DKR]
            parts.append(zh * cos2 + pltpu.roll(zh, DKR // 2, 1) * sin2)
        return jnp.concatenate(parts, axis=-1)

    qrb_ref[...] = rot(proj(1536, 512)).astype(BF16)
    kr_ref[...] = rot(proj(2048, 512)) * (DKR ** -0.5)
    vrb_ref[...] = proj(2560, 512).astype(BF16)
    gr_ref[...] = proj(3072, 512)
    ga_ref[...] = proj(3584, 1024)
    gb_ref[...] = proj(4608, 1024)


def in_proj(x, sh1, sc1, norm1, w_in_b, cos2, sin2, tm, emit_vt):
    B, L, D = x.shape
    d_in = w_in_b.shape[1]
    row = lambda n: pl.BlockSpec((None, tm, n), lambda b, i: (b, i, 0))
    mod = pl.BlockSpec((None, 1, D), lambda b, i: (b, 0, 0))
    f = lambda n, dt: jax.ShapeDtypeStruct((B, L, n), dt)
    vt_spec = [pl.BlockSpec((None, WA, tm), lambda b, i: (b, 0, i))] if emit_vt else []
    vt_shape = [jax.ShapeDtypeStruct((B, WA, L), BF16)] if emit_vt else []
    return pl.pallas_call(
        _inproj_kernel,
        grid=(B, L // tm),
        in_specs=[row(D), mod, mod,
                  pl.BlockSpec((1, D), lambda b, i: (0, 0)),
                  pl.BlockSpec((D, d_in), lambda b, i: (0, 0)),
                  pl.BlockSpec((tm, DKR), lambda b, i: (i, 0)),
                  pl.BlockSpec((tm, DKR), lambda b, i: (i, 0))],
        out_specs=[row(512), row(512), row(512), row(512), row(512), row(512), row(512), row(512),
                   row(512), row(1024), row(1024)] + vt_spec,
        out_shape=[f(512, F32), f(512, F32), f(512, BF16), f(512, BF16), f(512, BF16),
                   f(512, BF16), f(512, F32), f(512, BF16), f(512, F32), f(1024, F32), f(1024, F32)]
        + vt_shape,
        compiler_params=_params(("arbitrary", "arbitrary")),
        name="in_proj",
    )(x, sh1.reshape(B, 1, D), sc1.reshape(B, 1, D), norm1.reshape(1, D), w_in_b, cos2, sin2)


def _t5_bucket(rel):
    nb = N_BUCKETS // 2
    ret = jnp.where(rel > 0, nb, 0)
    n = jnp.abs(rel)
    max_exact = nb // 2
    nf = jnp.maximum(n, max_exact).astype(F32)
    large = max_exact + (jnp.log(nf / max_exact) / math.log(MAX_DIST / max_exact)
                         * (nb - max_exact)).astype(jnp.int32)
    large = jnp.minimum(large, nb - 1)
    return ret + jnp.where(n < max_exact, n, large)


def _bias_from_buckets(bkt, rb_ref, h, shift):
    val = jnp.where(bkt < 0, NEG, 0.0).astype(F32)
    for n in range(N_BUCKETS):
        val = jnp.where(bkt == n, rb_ref[n, h] - shift, val)
    return val


def _lam_value(lv_ref, lam_init):
    lv = lv_ref[...]
    a = jnp.sum(lv[0:1] * lv[1:2], axis=-1, keepdims=True)
    b = jnp.sum(lv[2:3] * lv[3:4], axis=-1, keepdims=True)
    return jnp.exp(a) - jnp.exp(b) + lam_init


def _block_diag_q(q):
    lane = lax.broadcasted_iota(jnp.int32, q.shape, 1)
    zero = jnp.zeros_like(q)
    return jnp.concatenate([jnp.where(lane < DA, q, zero), jnp.where(lane >= DA, q, zero)], axis=0)


def _attn_kernel(rb_ref, lv_ref, q_ref, k_ref, vt_ref, bkt_ref, o_ref,
                 bias_sc, m_sc, l_sc, acc_sc, *, tq, tb, lam_init):
    h = pl.program_id(1)
    i = pl.program_id(2)

    @pl.when(i == 0)
    def _():
        far = rb_ref[N_BUCKETS // 2 - 1, h]
        for t in range(2):
            b = _bias_from_buckets(bkt_ref[t], rb_ref, h, far)
            bias_sc[t, :, 0:tq] = b
            bias_sc[t, :, tq:2 * tq] = b

    qbd = _block_diag_q(q_ref[...])
    m_sc[...] = jnp.full(m_sc.shape, NEG, F32)
    l_sc[...] = jnp.zeros(l_sc.shape, F32)
    acc_sc[...] = jnp.zeros(acc_sc.shape, F32)

    def tile(off, tk, bias=None, nvalid=None):
        s = _dot_nt(k_ref[pl.ds(off, tk), :], qbd)
        if bias is not None:
            s = s + bias
        if nvalid is not None:
            row = lax.broadcasted_iota(jnp.int32, (tk, 1), 0)
            s = jnp.where(row < nvalid, s, NEG)
        m_prev = m_sc[...]
        m_new = jnp.maximum(m_prev, jnp.max(s, axis=0, keepdims=True))
        alpha = jnp.exp(m_prev - m_new)
        p = jnp.exp(s - m_new)
        l_sc[...] = alpha * l_sc[...] + jnp.sum(p, axis=0, keepdims=True)
        acc_sc[...] = alpha * acc_sc[...] + _dot(vt_ref[:, pl.ds(off, tk)], p.astype(BF16))
        m_sc[...] = m_new

    far_end = jnp.maximum(i - 1, 0) * tq
    nfull = far_end // tb
    rem = far_end - nfull * tb

    def far_body(t, carry):
        tile(pl.multiple_of(t * tb, tb), tb)
        return carry

    lax.fori_loop(0, nfull, far_body, 0)

    @pl.when(rem > 0)
    def _():
        tile(pl.multiple_of(nfull * tb, tb), tb, nvalid=rem)

    @pl.when(i >= 1)
    def _():
        tile(pl.multiple_of((i - 1) * tq, tq), tq, bias=bias_sc[1])

    tile(pl.multiple_of(i * tq, tq), tq, bias=bias_sc[0])

    lam = _lam_value(lv_ref, lam_init)
    o = acc_sc[...] / l_sc[...]
    o_ref[...] = (o[:, 0:tq] - lam * o[:, tq:2 * tq]).T


def attn_prompt(qab, kab, vt, rel_bias, lamv, lam_init, tq=256, tb=1024):
    B, S, _ = qab.shape
    tb = min(tb, S)
    assert S % tb == 0 and tb % tq == 0
    c = jnp.arange(tq, dtype=jnp.int32)[:, None]
    r = jnp.arange(tq, dtype=jnp.int32)[None, :]
    diag = jnp.where((c // CHUNK) <= (r // CHUNK), _t5_bucket(c - r), -1)
    prev = _t5_bucket(c - r - tq)
    bkt = jnp.stack([diag, prev]).astype(jnp.int32)
    smem = pl.BlockSpec(memory_space=pltpu.SMEM)
    return pl.pallas_call(
        functools.partial(_attn_kernel, tq=tq, tb=tb, lam_init=lam_init),
        grid=(B, HA, S // tq),
        in_specs=[smem,
                  pl.BlockSpec((4, DA), lambda b, h, i: (0, 0)),
                  pl.BlockSpec((None, tq, DVA), lambda b, h, i: (b, i, h)),
                  pl.BlockSpec((None, S, DVA), lambda b, h, i: (b, 0, h)),
                  pl.BlockSpec((None, DVA, S), lambda b, h, i: (b, h, 0)),
                  pl.BlockSpec((2, tq, tq), lambda b, h, i: (0, 0, 0))],
        out_specs=pl.BlockSpec((None, tq, DVA), lambda b, h, i: (b, i, h)),
        out_shape=jax.ShapeDtypeStruct((B, S, WA), F32),
        scratch_shapes=[pltpu.VMEM((2, tq, 2 * tq), F32),
                        pltpu.VMEM((1, 2 * tq), F32),
                        pltpu.VMEM((1, 2 * tq), F32),
                        pltpu.VMEM((DVA, 2 * tq), F32)],
        compiler_params=_params(("arbitrary", "arbitrary", "arbitrary")),
        name="attn_prompt",
    )(rel_bias, lamv, qab, kab, vt, bkt)


def _attn_small_kernel(rb_ref, lv_ref, q_ref, k_ref, v_ref, bkt_ref, o_ref, *, lq, lam_init):
    h = pl.program_id(1)
    bias = _bias_from_buckets(bkt_ref[...], rb_ref, h, 0.0)
    qbd = _block_diag_q(q_ref[...])
    s = _dot_nt(qbd, k_ref[...]) + jnp.concatenate([bias, bias], axis=0)
    m = jnp.max(s, axis=-1, keepdims=True)
    p = jnp.exp(s - m)
    l = jnp.sum(p, axis=-1, keepdims=True)
    o = _dot(p.astype(BF16), v_ref[...]) / l
    lam = _lam_value(lv_ref, lam_init)
    o_ref[...] = o[0:lq] - lam * o[lq:2 * lq]


def attn_small(qab, k_all, v_all, q_pos, k_pos, k_valid, rel_bias, lamv, lam_init):
    B, Lq, _ = qab.shape
    Lk = k_all.shape[1]
    visible = ((k_pos[None, :] // CHUNK) <= (q_pos[:, None] // CHUNK)) & k_valid[None, :]
    bkt = jnp.where(visible, _t5_bucket(k_pos[None, :] - q_pos[:, None]), -1).astype(jnp.int32)
    smem = pl.BlockSpec(memory_space=pltpu.SMEM)
    return pl.pallas_call(
        functools.partial(_attn_small_kernel, lq=Lq, lam_init=lam_init),
        grid=(B, HA),
        in_specs=[smem,
                  pl.BlockSpec((4, DA), lambda b, h: (0, 0)),
                  pl.BlockSpec((None, Lq, DVA), lambda b, h: (b, 0, h)),
                  pl.BlockSpec((None, Lk, DVA), lambda b, h: (b, 0, h)),
                  pl.BlockSpec((None, Lk, DVA), lambda b, h: (b, 0, h)),
                  pl.BlockSpec((Lq, Lk), lambda b, h: (0, 0))],
        out_specs=pl.BlockSpec((None, Lq, DVA), lambda b, h: (b, 0, h)),
        out_shape=jax.ShapeDtypeStruct((B, Lq, WA), F32),
        compiler_params=_params(("arbitrary", "arbitrary")),
        name="attn_sample",
    )(rel_bias, lamv, qab, k_all, v_all, bkt)


def _ret_kernel(lg_ref, q_ref, k_ref, v_ref, s0_ref, o_ref, so_ref, state_sc, decay_sc, *, C):
    h = pl.program_id(1)
    c = pl.program_id(2)
    lg = lg_ref[h]

    @pl.when(c == 0)
    def _():
        state_sc[...] = s0_ref[...]
        r = lax.broadcasted_iota(jnp.int32, (C, C), 0)
        cc = lax.broadcasted_iota(jnp.int32, (C, C), 1)
        diff = (r - cc).astype(F32)
        decay_sc[...] = jnp.where(diff >= 0, jnp.exp(jnp.maximum(diff, 0.0) * lg), 0.0)

    n = lax.broadcasted_iota(jnp.int32, (C, 1), 0).astype(F32)
    xi = jnp.exp((n + 1.0) * lg)
    zeta = jnp.exp((C - 1.0 - n) * lg)
    q = q_ref[...]
    k = k_ref[...]
    v = v_ref[...]
    state = state_sc[...]
    scores = _dot_nt(q, k.astype(BF16)) * decay_sc[...]
    intra = _dot(scores.astype(BF16), v)
    cross = _dot(q, state.astype(BF16)) * xi
    o_ref[...] = intra + cross
    kz = (k * zeta).astype(BF16)
    state_sc[...] = jnp.exp(C * lg) * state + _dot_tn(kz, v)

    @pl.when(c == pl.num_programs(2) - 1)
    def _():
        so_ref[...] = state_sc[...]


def retention(qrb, kr, vrb, state0, C):
    B, L, _ = qrb.shape
    lg = jnp.log(1.0 - 2.0 ** (-5.0 - jnp.arange(HR, dtype=F32)))
    blk = pl.BlockSpec((None, C, DKR), lambda b, h, c: (b, c, h))
    st = pl.BlockSpec((None, None, DKR, DVR), lambda b, h, c: (b, h, 0, 0))
    return pl.pallas_call(
        functools.partial(_ret_kernel, C=C),
        grid=(B, HR, L // C),
        in_specs=[pl.BlockSpec(memory_space=pltpu.SMEM), blk, blk, blk, st],
        out_specs=[blk, st],
        out_shape=[jax.ShapeDtypeStruct((B, L, WR), F32),
                   jax.ShapeDtypeStruct((B, HR, DKR, DVR), F32)],
        scratch_shapes=[pltpu.VMEM((DKR, DVR), F32), pltpu.VMEM((C, C), F32)],
        compiler_params=_params(("arbitrary", "arbitrary", "arbitrary")),
        name="retention",
    )(lg, qrb, kr, vrb, state0)


def _outmix_kernel(x_ref, oa_ref, or_ref, gr_ref, ga_ref, gb_ref, g1_ref, sh2_ref, sc2_ref, n2_ref,
                   sa_ref, sr_ref, wba_ref, wbr_ref, wo_ref, wpq_ref,
                   x1_ref, h2_ref, qp_ref, *, lam_init):
    sa = sa_ref[...] * 1.0
    sr = sr_ref[...]
    gr = gr_ref[...]
    silu_gr = gr * jax.nn.sigmoid(gr)
    ya_parts, yr_parts = [], []
    for hh in range(HA):
        sl = slice(hh * DVA, (hh + 1) * DVA)
        ya_parts.append(_rms_rows(oa_ref[:, sl]) * sa * (1.0 - lam_init))
        yr_parts.append(silu_gr[:, sl] * (_rms_rows(or_ref[:, sl]) * sr))
    ya = jnp.concatenate(ya_parts, axis=-1).astype(BF16)
    yr = jnp.concatenate(yr_parts, axis=-1).astype(BF16)
    y = (jax.nn.sigmoid(ga_ref[...]) * _dot(ya, wba_ref[...])
         + jax.nn.sigmoid(gb_ref[...]) * _dot(yr, wbr_ref[...]))
    out = _dot(y.astype(BF16), wo_ref[...])
    x1 = x_ref[...] + g1_ref[...] * out
    x1_ref[...] = x1
    h2 = (_rms_rows(x1) * n2_ref[...] * (1.0 + sc2_ref[...]) + sh2_ref[...]).astype(BF16)
    h2_ref[...] = h2
    qp_ref[...] = _dot(h2, wpq_ref[...]).astype(BF16)


def out_mix(x, oa, orr, gr, ga, gb, g1, sh2, sc2, norm2, subln_a, subln_r,
            w_ba_b, w_br_b, w_o_b, w_pq_b, lam_init, tm):
    B, L, D = x.shape
    nq = w_pq_b.shape[1]
    row = lambda n: pl.BlockSpec((None, tm, n), lambda b, i: (b, i, 0))
    mod = pl.BlockSpec((None, 1, D), lambda b, i: (b, 0, 0))
    full = lambda a: pl.BlockSpec(a.shape, lambda b, i: (0,) * a.ndim)
    n2 = norm2.reshape(1, D)
    sa = subln_a.reshape(1, DVA)
    sr = subln_r.reshape(1, DVR)
    return pl.pallas_call(
        functools.partial(_outmix_kernel, lam_init=lam_init),
        grid=(B, L // tm),
        in_specs=[row(D), row(WA), row(WR), row(WR), row(D), row(D), mod, mod, mod,
                  full(n2), full(sa), full(sr), full(w_ba_b), full(w_br_b), full(w_o_b), full(w_pq_b)],
        out_specs=[row(D), row(D), row(nq)],
        out_shape=[jax.ShapeDtypeStruct((B, L, D), F32),
                   jax.ShapeDtypeStruct((B, L, D), BF16),
                   jax.ShapeDtypeStruct((B, L, nq), BF16)],
        compiler_params=_params(("arbitrary", "arbitrary")),
        name="out_mix",
    )(x, oa, orr, gr, ga, gb, g1.reshape(B, 1, D), sh2.reshape(B, 1, D), sc2.reshape(B, 1, D),
      n2, sa, sr, w_ba_b, w_br_b, w_o_b, w_pq_b)


def _topk_rows(s, k):
    n = s.shape[0]
    iota = lax.broadcasted_iota(jnp.int32, s.shape, 0).astype(F32)
    work = s
    rank = jnp.full(s.shape, float(PEER_TOPK * PEER_TOPK), F32)
    vals = []
    for r in range(k):
        m = jnp.max(work, axis=0, keepdims=True)
        idx = jnp.min(jnp.where(work == m, iota, float(n)), axis=0, keepdims=True)
        sel = iota == idx
        rank = jnp.where(sel, float(r), rank)
        work = jnp.where(sel, -jnp.inf, work)
        vals.append(m)
    return vals, rank


SUBLANES = 8
_HEAD_A = SUBLANES
_CAND_NB = [PEER_TOPK] + [SUBLANES] * (_HEAD_A - 1)


def _route_chunk(s1, s2):
    v1, rank1 = _topk_rows(s1, PEER_TOPK)
    v2, rank2 = _topk_rows(s2, PEER_TOPK)
    v2m = jnp.concatenate(v2, axis=0)
    v1t = jnp.concatenate(v1[_HEAD_A:], axis=0)
    blocks = [v1[a] + v2m[0:nb] for a, nb in enumerate(_CAND_NB)] + [v1t + v2[0]]
    cand = jnp.concatenate(blocks, axis=0)
    _, crank = _topk_rows(cand, PEER_TOPK)
    sel = crank < float(PEER_TOPK)
    cmax = v1[0] + v2[0]
    z = jnp.sum(jnp.where(sel, jnp.exp(cand - cmax), 0.0), axis=0, keepdims=True)
    self32 = sel.astype(F32)
    cidx = jnp.zeros(s1.shape, F32)
    lo = 0
    for a, nb in enumerate(_CAND_NB):
        cnt = jnp.sum(self32[lo:lo + nb], axis=0, keepdims=True)
        cidx = jnp.where(rank1 == float(a), cnt, cidx)
        lo += nb
    for a in range(_HEAD_A, PEER_TOPK):
        cidx = jnp.where(rank1 == float(a), self32[lo + a - _HEAD_A:lo + a - _HEAD_A + 1], cidx)
    e1 = jnp.exp(s1 - v1[0]) / z
    e2 = jnp.exp(s2 - v2[0])
    return e1, e2, rank2, cidx


def _route_kernel(q_ref, keys_ref, e1_ref, e2_ref, r2_ref, c_ref, s1_sc, s2_sc):
    q = q_ref[...]
    s1_sc[...] = _dot_nt(keys_ref[0], q[:, 0:KEY_DIM])
    s2_sc[...] = _dot_nt(keys_ref[1], q[:, KEY_DIM:2 * KEY_DIM])

    def chunk(ci, carry):
        sl = pl.ds(pl.multiple_of(ci * LANES, LANES), LANES)
        e1, e2, rank2, cidx = _route_chunk(s1_sc[:, sl], s2_sc[:, sl])
        e1_ref[:, sl] = e1
        e2_ref[:, sl] = e2.astype(BF16)
        r2_ref[:, sl] = rank2.astype(BF16)
        c_ref[:, sl] = cidx
        return carry

    lax.fori_loop(0, s1_sc.shape[1] // LANES, chunk, 0, unroll=2)


def peer_route(qp, keys_b, tm):
    T = qp.shape[0]
    out = pl.BlockSpec((None, N_KEYS, tm), lambda t, h: (h, 0, t))
    shp = lambda dt: jax.ShapeDtypeStruct((PEER_HEADS, N_KEYS, T), dt)
    return pl.pallas_call(
        _route_kernel,
        grid=(T // tm, PEER_HEADS),
        in_specs=[pl.BlockSpec((tm, 2 * KEY_DIM), lambda t, h: (t, h)),
                  pl.BlockSpec((2, N_KEYS, KEY_DIM), lambda t, h: (h, 0, 0))],
        out_specs=[out, out, out, out],
        out_shape=[shp(F32), shp(BF16), shp(BF16), shp(F32)],
        scratch_shapes=[pltpu.VMEM((N_KEYS, tm), F32), pltpu.VMEM((N_KEYS, tm), F32)],
        compiler_params=_params(("arbitrary", "arbitrary")),
        name="peer_route",
    )(qp, keys_b)


BF16_ROWS = 16


def _bcast_rows_bf16(row, n):
    t = row.shape[1]
    tile = jnp.broadcast_to(row, (BF16_ROWS, t)).astype(BF16)
    return jnp.broadcast_to(tile[None], (n // BF16_ROWS, BF16_ROWS, t)).reshape(n, t)


def _peer_kernel(h2_ref, u_ref, vt_ref, e1_ref, e2_ref, r2_ref, c_ref, x1_ref, g2_ref, fn_ref,
                 y_ref, acc_sc, *, ni):
    e = pl.program_id(1)

    @pl.when(e == 0)
    def _():
        acc_sc[...] = jnp.zeros(acc_sc.shape, F32)

    h2 = h2_ref[...]
    ws = []
    for ii in range(ni):
        i = e * ni + ii
        a = _dot_nt(u_ref[ii * N_KEYS:(ii + 1) * N_KEYS, :], h2)
        g = jnp.zeros(a.shape, BF16)
        for hh in range(PEER_HEADS):
            c_row = _bcast_rows_bf16(c_ref[hh, pl.ds(i, 1), :], N_KEYS)
            e1_row = _bcast_rows_bf16(e1_ref[hh, pl.ds(i, 1), :], N_KEYS)
            g = g + jnp.where(r2_ref[hh] < c_row, e2_ref[hh] * e1_row, jnp.zeros_like(g))
        act = 0.5 * a * (1.0 + lax.erf(a * (2.0 ** -0.5)))
        ws.append(g * act.astype(BF16))
    w = jnp.concatenate(ws, axis=0)
    acc_sc[...] += _dot(vt_ref[...], w)

    @pl.when(e == pl.num_programs(1) - 1)
    def _():
        x2 = x1_ref[...] + g2_ref[...] * acc_sc[...].T
        y_ref[...] = _rms_rows(x2) * fn_ref[...]


def peer_experts(h2, u_b, vt_b, e1, e2, r2, cc, x1, g2tok, final_norm, tm, ni=4):
    T, D = h2.shape
    ne = u_b.shape[0]
    tok = pl.BlockSpec((tm, D), lambda t, e: (t, 0))
    gate = pl.BlockSpec((PEER_HEADS, N_KEYS, tm), lambda t, e: (0, 0, t))
    return pl.pallas_call(
        functools.partial(_peer_kernel, ni=ni),
        grid=(T // tm, ne // (ni * N_KEYS)),
        in_specs=[tok,
                  pl.BlockSpec((ni * N_KEYS, D), lambda t, e: (e, 0)),
                  pl.BlockSpec((D, ni * N_KEYS), lambda t, e: (0, e)),
                  gate, gate, gate, gate, tok, tok,
                  pl.BlockSpec((1, D), lambda t, e: (0, 0))],
        out_specs=tok,
        out_shape=jax.ShapeDtypeStruct((T, D), F32),
        scratch_shapes=[pltpu.VMEM((D, tm), F32)],
        compiler_params=_params(("arbitrary", "arbitrary")),
        name="peer_experts",
    )(h2, u_b, vt_b, e1, e2, r2, cc, x1, g2tok, final_norm.reshape(1, D))


def _rot_tables(pos):
    inv = 1.0 / (10000.0 ** jnp.linspace(0.0, 1.0, DKR // 2, dtype=F32))
    ang = pos[:, None].astype(F32) * inv[None, :]
    cos, sin = jnp.cos(ang), jnp.sin(ang)
    return jnp.concatenate([cos, cos], axis=-1), jnp.concatenate([-sin, sin], axis=-1)


def _pick_tile(n, pref):
    t = min(n, pref)
    assert n % t == 0, (n, t)
    return t


def _trunk(x, mods, pos, lam_init, lamv, rel_bias, w, attend, values_transposed, state0, ret_chunk,
           final_norm):
    (norm1, norm2, w_in_b, subln_a, subln_r, w_ba_b, w_br_b, w_o_b, w_pq_b, keys_b, u_b, vt_b) = w
    sh1, sc1, g1, sh2, sc2, g2 = mods
    B, L, D = x.shape
    T = B * L
    cos2, sin2 = _rot_tables(pos)
    tm = _pick_tile(L, 256)
    ka, va, qab, kab, vab, qrb, kr, vrb, gr, ga, gb, *vt = in_proj(
        x, sh1, sc1, norm1, w_in_b, cos2, sin2, tm, emit_vt=values_transposed)
    oa = attend(qab, kab, vt[0] if values_transposed else vab)
    orr, st = retention(qrb, kr, vrb, state0, ret_chunk)
    x1, h2, qp = out_mix(x, oa, orr, gr, ga, gb, g1, sh2, sc2, norm2, subln_a, subln_r,
                         w_ba_b, w_br_b, w_o_b, w_pq_b, lam_init, tm)
    e1, e2, r2, cc = peer_route(qp.reshape(T, -1), keys_b, _pick_tile(T, 512))
    g2tok = jnp.broadcast_to(g2[:, None, :], (B, L, D)).reshape(T, D)
    y = peer_experts(h2.reshape(T, D), u_b, vt_b, e1, e2, r2, cc, x1.reshape(T, D), g2tok,
                     final_norm, _pick_tile(T, 512))
    return y.reshape(B, L, D), ka.reshape(B, L, HA, 2 * DA), va.reshape(B, L, HA, DVA), st


def kernel(x_prompt, x_sample, cache_k, cache_v, state_ret, c_prompt, c_sample, w_ada, b_ada, norm1,
           norm2, w_in, lam_q1, lam_k1, lam_q2, lam_k2, subln_a, subln_r, w_ba, w_br, w_o, rel_bias,
           w_pq, peer_keys, peer_u, peer_v, final_norm):
    depth = w_ada.shape[0]
    assert depth == 1, "the fused final norm assumes a single layer"
    Bp, Lp, D = x_prompt.shape
    Bs, Ls, _ = x_sample.shape
    past = cache_k.shape[2]
    pos_p = jnp.arange(Lp, dtype=jnp.int32)
    pos_s = past + jnp.arange(Ls, dtype=jnp.int32)
    lk = past + Ls
    lk_pad = -(-lk // LANES) * LANES
    k_pos_s = jnp.arange(lk_pad, dtype=jnp.int32)
    k_valid_s = k_pos_s < lk

    l = 0
    lam_init = 0.8 - 0.6 * math.exp(-0.3 * l)
    lamv = jnp.stack([lam_q1[l], lam_k1[l], lam_q2[l], lam_k2[l]]).astype(F32)
    mod = ada_mod(jnp.concatenate([c_prompt, c_sample], axis=0), w_ada[l], b_ada[l])
    mods = jnp.split(mod, 6, axis=-1)
    mods_p = [m[:Bp] for m in mods]
    mods_s = [m[Bp:] for m in mods]
    w = (norm1[l], norm2[l], w_in[l].astype(BF16), subln_a[l], subln_r[l], w_ba[l].astype(BF16),
         w_br[l].astype(BF16), w_o[l].astype(BF16), w_pq[l].astype(BF16),
         peer_keys[l].reshape(PEER_HEADS * 2, N_KEYS, KEY_DIM).astype(BF16),
         peer_u[l].astype(BF16), peer_v[l].T.astype(BF16))

    attend_p = lambda q, k, v: attn_prompt(q, k, v, rel_bias, lamv, lam_init)
    zero_state = jnp.zeros((Bp, HR, DKR, DVR), F32)
    yp, kp, vp, sp = _trunk(x_prompt, mods_p, pos_p, lam_init, lamv, rel_bias, w, attend_p, True,
                            zero_state, _pick_tile(Lp, 256), final_norm)

    def attend_s(q, k, v):
        padk = jnp.zeros((Bs, lk_pad - lk, WA), BF16)
        kc = cache_k[l].reshape(Bs, past, HA * 2 * DA).astype(BF16)
        vc = cache_v[l].reshape(Bs, past, WA).astype(BF16)
        k_all = jnp.concatenate([kc, k, padk], axis=1)
        v_all = jnp.concatenate([vc, v, padk], axis=1)
        return attn_small(q, k_all, v_all, pos_s, k_pos_s, k_valid_s, rel_bias, lamv, lam_init)

    ys, ks, vs, ss = _trunk(x_sample, mods_s, pos_s, lam_init, lamv, rel_bias, w, attend_s, False,
                            state_ret[l].astype(F32), Ls, final_norm)
    return (yp, ys, kp[None], vp[None], sp[None], ks[None], vs[None], ss[None])
```

```python
import functools
import math

import jax
import jax.numpy as jnp
from jax import lax
from jax.experimental import pallas as pl
from jax.experimental.pallas import tpu as pltpu

F32 = jnp.float32
BF16 = jnp.bfloat16

CHUNK = 64
HA = 4
DA = 64
DVA = 2 * DA
HR = 4
DKR = 128
DVR = 128
N_BUCKETS = 32
MAX_DIST = 128
PEER_HEADS = 8
N_KEYS = 128
KEY_DIM = 128
PEER_TOPK = 16
EPS = 1e-6
WA = HA * DVA
WR = HR * DVR
NEG = -1e30
LANES = 128
VMEM_LIMIT = 56 * 1024 * 1024


def _params(sem, vmem=VMEM_LIMIT):
    return pltpu.CompilerParams(dimension_semantics=sem, vmem_limit_bytes=vmem)


def _dot(a, b):
    return jnp.dot(a, b, preferred_element_type=F32)


def _dot_nt(a, b):
    return lax.dot_general(a, b, (((1,), (1,)), ((), ())), preferred_element_type=F32)


def _dot_tn(a, b):
    return lax.dot_general(a, b, (((0,), (0,)), ((), ())), preferred_element_type=F32)


def _rms_rows(x):
    return x * lax.rsqrt(jnp.mean(x * x, axis=-1, keepdims=True) + EPS)


def _ada_kernel(c_ref, w_ref, b_ref, o_ref):
    c = c_ref[...]
    s = (c * jax.nn.sigmoid(c)).astype(BF16)
    o_ref[...] = _dot(s, w_ref[...].astype(BF16)) + b_ref[...]


def ada_mod(c, w_ada, b_ada, tn=1024):
    nb, d = c.shape
    n = w_ada.shape[1]
    return pl.pallas_call(
        _ada_kernel,
        grid=(n // tn,),
        in_specs=[pl.BlockSpec((nb, d), lambda j: (0, 0)),
                  pl.BlockSpec((d, tn), lambda j: (0, j)),
                  pl.BlockSpec((1, tn), lambda j: (0, j))],
        out_specs=pl.BlockSpec((nb, tn), lambda j: (0, j)),
        out_shape=jax.ShapeDtypeStruct((nb, n), F32),
        compiler_params=_params(("arbitrary",)),
        name="ada_mod",
    )(c, w_ada, b_ada.reshape(1, n))


def _inproj_kernel(x_ref, sh_ref, sc_ref, n1_ref, w_ref, cos_ref, sin_ref,
                   ka_ref, va_ref, qab_ref, kab_ref, vab_ref, qrb_ref, kr_ref, vrb_ref,
                   gr_ref, ga_ref, gb_ref, vt_ref=None):
    x = x_ref[...]
    h = _rms_rows(x) * n1_ref[...] * (1.0 + sc_ref[...]) + sh_ref[...]
    hb = h.astype(BF16)

    def proj(lo, n):
        return _dot(hb, w_ref[:, lo:lo + n])

    qa = proj(0, 512)
    qab_ref[...] = (qa * (DA ** -0.5)).astype(BF16)
    ka = proj(512, 512)
    ka_ref[...] = ka
    kab_ref[...] = ka.astype(BF16)
    va = proj(1024, 512)
    va_ref[...] = va
    vab_ref[...] = va.astype(BF16)
    if vt_ref is not None:
        vt_ref[...] = va.T.astype(BF16)
    cos2 = cos_ref[...]
    sin2 = sin_ref[...]

    def rot(z):
        parts = []
        for hh in range(HR):
            zh = z[:, hh * DKR:(hh + 1) * DKR]
            parts.append(zh * cos2 + pltpu.roll(zh, DKR // 2, 1) * sin2)
        return jnp.concatenate(parts, axis=-1)

    qrb_ref[...] = rot(proj(1536, 512)).astype(BF16)
    kr_ref[...] = rot(proj(2048, 512)) * (DKR ** -0.5)
    vrb_ref[...] = proj(2560, 512).astype(BF16)
    gr_ref[...] = proj(3072, 512)
    ga_ref[...] = proj(3584, 1024)
    gb_ref[...] = proj(4608, 1024)


def in_proj(x, sh1, sc1, norm1, w_in_b, cos2, sin2, tm, emit_vt):
    B, L, D = x.shape
    d_in = w_in_b.shape[1]
    row = lambda n: pl.BlockSpec((None, tm, n), lambda b, i: (b, i, 0))
    mod = pl.BlockSpec((None, 1, D), lambda b, i: (b, 0, 0))
    f = lambda n, dt: jax.ShapeDtypeStruct((B, L, n), dt)
    vt_spec = [pl.BlockSpec((None, WA, tm), lambda b, i: (b, 0, i))] if emit_vt else []
    vt_shape = [jax.ShapeDtypeStruct((B, WA, L), BF16)] if emit_vt else []
    return pl.pallas_call(
        _inproj_kernel,
        grid=(B, L // tm),
        in_specs=[row(D), mod, mod,
                  pl.BlockSpec((1, D), lambda b, i: (0, 0)),
                  pl.BlockSpec((D, d_in), lambda b, i: (0, 0)),
                  pl.BlockSpec((tm, DKR), lambda b, i: (i, 0)),
                  pl.BlockSpec((tm, DKR), lambda b, i: (i, 0))],
        out_specs=[row(512), row(512), row(512), row(512), row(512), row(512), row(512), row(512),
                   row(512), row(1024), row(1024)] + vt_spec,
        out_shape=[f(512, F32), f(512, F32), f(512, BF16), f(512, BF16), f(512, BF16),
                   f(512, BF16), f(512, F32), f(512, BF16), f(512, F32), f(1024, F32), f(1024, F32)]
        + vt_shape,
        compiler_params=_params(("arbitrary", "arbitrary")),
        name="in_proj",
    )(x, sh1.reshape(B, 1, D), sc1.reshape(B, 1, D), norm1.reshape(1, D), w_in_b, cos2, sin2)


def _t5_bucket(rel):
    nb = N_BUCKETS // 2
    ret = jnp.where(rel > 0, nb, 0)
    n = jnp.abs(rel)
    max_exact = nb // 2
    nf = jnp.maximum(n, max_exact).astype(F32)
    large = max_exact + (jnp.log(nf / max_exact) / math.log(MAX_DIST / max_exact)
                         * (nb - max_exact)).astype(jnp.int32)
    large = jnp.minimum(large, nb - 1)
    return ret + jnp.where(n < max_exact, n, large)


def _bias_from_buckets(bkt, rb_ref, h, shift):
    val = jnp.where(bkt < 0, NEG, 0.0).astype(F32)
    for n in range(N_BUCKETS):
        val = jnp.where(bkt == n, rb_ref[n, h] - shift, val)
    return val


def _lam_value(lv_ref, lam_init):
    lv = lv_ref[...]
    a = jnp.sum(lv[0:1] * lv[1:2], axis=-1, keepdims=True)
    b = jnp.sum(lv[2:3] * lv[3:4], axis=-1, keepdims=True)
    return jnp.exp(a) - jnp.exp(b) + lam_init


def _block_diag_q(q):
    lane = lax.broadcasted_iota(jnp.int32, q.shape, 1)
    zero = jnp.zeros_like(q)
    return jnp.concatenate([jnp.where(lane < DA, q, zero), jnp.where(lane >= DA, q, zero)], axis=0)


def _attn_kernel(rb_ref, lv_ref, q_ref, k_ref, vt_ref, bkt_ref, o_ref,
                 bias_sc, m_sc, l_sc, acc_sc, *, tq, tb, lam_init):
    h = pl.program_id(1)
    i = pl.program_id(2)

    @pl.when(i == 0)
    def _():
        far = rb_ref[N_BUCKETS // 2 - 1, h]
        for t in range(2):
            bias_sc[t] = _bias_from_buckets(bkt_ref[t], rb_ref, h, far)

    qbd = _block_diag_q(q_ref[...])
    m_sc[...] = jnp.full(m_sc.shape, NEG, F32)
    l_sc[...] = jnp.zeros(l_sc.shape, F32)
    acc_sc[...] = jnp.zeros(acc_sc.shape, F32)

    def tile(off, tk, bias=None, nvalid=None):
        s = _dot_nt(k_ref[pl.ds(off, tk), :], qbd)
        if bias is not None:
            s = s + jnp.concatenate([bias, bias], axis=1)
        if nvalid is not None:
            row = lax.broadcasted_iota(jnp.int32, (tk, 1), 0)
            s = jnp.where(row < nvalid, s, NEG)
        m_prev = m_sc[...]
        m_new = jnp.maximum(m_prev, jnp.max(s, axis=0, keepdims=True))
        alpha = jnp.exp(m_prev - m_new)
        p = jnp.exp(s - m_new)
        l_sc[...] = alpha * l_sc[...] + jnp.sum(p, axis=0, keepdims=True)
        acc_sc[...] = alpha * acc_sc[...] + _dot(vt_ref[:, pl.ds(off, tk)], p.astype(BF16))
        m_sc[...] = m_new

    far_end = jnp.maximum(i - 1, 0) * tq
    nfull = far_end // tb
    rem = far_end - nfull * tb

    def far_body(t, carry):
        tile(pl.multiple_of(t * tb, tb), tb)
        return carry

    lax.fori_loop(0, nfull, far_body, 0)

    @pl.when(rem > 0)
    def _():
        tile(pl.multiple_of(nfull * tb, tb), tb, nvalid=rem)

    @pl.when(i >= 1)
    def _():
        tile(pl.multiple_of((i - 1) * tq, tq), tq, bias=bias_sc[1])

    tile(pl.multiple_of(i * tq, tq), tq, bias=bias_sc[0])

    lam = _lam_value(lv_ref, lam_init)
    o = acc_sc[...] / l_sc[...]
    o_ref[...] = (o[:, 0:tq] - lam * o[:, tq:2 * tq]).T


def attn_prompt(qab, kab, vt, rel_bias, lamv, lam_init, tq=256, tb=1024):
    B, S, _ = qab.shape
    tb = min(tb, S)
    assert S % tb == 0 and tb % tq == 0
    c = jnp.arange(tq, dtype=jnp.int32)[:, None]
    r = jnp.arange(tq, dtype=jnp.int32)[None, :]
    diag = jnp.where((c // CHUNK) <= (r // CHUNK), _t5_bucket(c - r), -1)
    prev = _t5_bucket(c - r - tq)
    bkt = jnp.stack([diag, prev]).astype(jnp.int32)
    smem = pl.BlockSpec(memory_space=pltpu.SMEM)
    return pl.pallas_call(
        functools.partial(_attn_kernel, tq=tq, tb=tb, lam_init=lam_init),
        grid=(B, HA, S // tq),
        in_specs=[smem,
                  pl.BlockSpec((4, DA), lambda b, h, i: (0, 0)),
                  pl.BlockSpec((None, tq, DVA), lambda b, h, i: (b, i, h)),
                  pl.BlockSpec((None, S, DVA), lambda b, h, i: (b, 0, h)),
                  pl.BlockSpec((None, DVA, S), lambda b, h, i: (b, h, 0)),
                  pl.BlockSpec((2, tq, tq), lambda b, h, i: (0, 0, 0))],
        out_specs=pl.BlockSpec((None, tq, DVA), lambda b, h, i: (b, i, h)),
        out_shape=jax.ShapeDtypeStruct((B, S, WA), F32),
        scratch_shapes=[pltpu.VMEM((2, tq, tq), F32),
                        pltpu.VMEM((1, 2 * tq), F32),
                        pltpu.VMEM((1, 2 * tq), F32),
                        pltpu.VMEM((DVA, 2 * tq), F32)],
        compiler_params=_params(("arbitrary", "arbitrary", "arbitrary")),
        name="attn_prompt",
    )(rel_bias, lamv, qab, kab, vt, bkt)


def _attn_small_kernel(rb_ref, lv_ref, q_ref, k_ref, v_ref, bkt_ref, o_ref, *, lq, lam_init):
    h = pl.program_id(1)
    bias = _bias_from_buckets(bkt_ref[...], rb_ref, h, 0.0)
    qbd = _block_diag_q(q_ref[...])
    s = _dot_nt(qbd, k_ref[...]) + jnp.concatenate([bias, bias], axis=0)
    m = jnp.max(s, axis=-1, keepdims=True)
    p = jnp.exp(s - m)
    l = jnp.sum(p, axis=-1, keepdims=True)
    o = _dot(p.astype(BF16), v_ref[...]) / l
    lam = _lam_value(lv_ref, lam_init)
    o_ref[...] = o[0:lq] - lam * o[lq:2 * lq]


def attn_small(qab, k_all, v_all, q_pos, k_pos, k_valid, rel_bias, lamv, lam_init):
    B, Lq, _ = qab.shape
    Lk = k_all.shape[1]
    visible = ((k_pos[None, :] // CHUNK) <= (q_pos[:, None] // CHUNK)) & k_valid[None, :]
    bkt = jnp.where(visible, _t5_bucket(k_pos[None, :] - q_pos[:, None]), -1).astype(jnp.int32)
    smem = pl.BlockSpec(memory_space=pltpu.SMEM)
    return pl.pallas_call(
        functools.partial(_attn_small_kernel, lq=Lq, lam_init=lam_init),
        grid=(B, HA),
        in_specs=[smem,
                  pl.BlockSpec((4, DA), lambda b, h: (0, 0)),
                  pl.BlockSpec((None, Lq, DVA), lambda b, h: (b, 0, h)),
                  pl.BlockSpec((None, Lk, DVA), lambda b, h: (b, 0, h)),
                  pl.BlockSpec((None, Lk, DVA), lambda b, h: (b, 0, h)),
                  pl.BlockSpec((Lq, Lk), lambda b, h: (0, 0))],
        out_specs=pl.BlockSpec((None, Lq, DVA), lambda b, h: (b, 0, h)),
        out_shape=jax.ShapeDtypeStruct((B, Lq, WA), F32),
        compiler_params=_params(("arbitrary", "arbitrary")),
        name="attn_sample",
    )(rel_bias, lamv, qab, k_all, v_all, bkt)


def _ret_kernel(lg_ref, q_ref, k_ref, v_ref, s0_ref, o_ref, so_ref, state_sc, decay_sc, *, C):
    h = pl.program_id(1)
    c = pl.program_id(2)
    lg = lg_ref[h]

    @pl.when(c == 0)
    def _():
        state_sc[...] = s0_ref[...]
        r = lax.broadcasted_iota(jnp.int32, (C, C), 0)
        cc = lax.broadcasted_iota(jnp.int32, (C, C), 1)
        diff = (r - cc).astype(F32)
        decay_sc[...] = jnp.where(diff >= 0, jnp.exp(jnp.maximum(diff, 0.0) * lg), 0.0)

    n = lax.broadcasted_iota(jnp.int32, (C, 1), 0).astype(F32)
    xi = jnp.exp((n + 1.0) * lg)
    zeta = jnp.exp((C - 1.0 - n) * lg)
    q = q_ref[...]
    k = k_ref[...]
    v = v_ref[...]
    state = state_sc[...]
    scores = _dot_nt(q, k.astype(BF16)) * decay_sc[...]
    intra = _dot(scores.astype(BF16), v)
    cross = _dot(q, state.astype(BF16)) * xi
    o_ref[...] = intra + cross
    kz = (k * zeta).astype(BF16)
    state_sc[...] = jnp.exp(C * lg) * state + _dot_tn(kz, v)

    @pl.when(c == pl.num_programs(2) - 1)
    def _():
        so_ref[...] = state_sc[...]


def retention(qrb, kr, vrb, state0, C):
    B, L, _ = qrb.shape
    lg = jnp.log(1.0 - 2.0 ** (-5.0 - jnp.arange(HR, dtype=F32)))
    blk = pl.BlockSpec((None, C, DKR), lambda b, h, c: (b, c, h))
    st = pl.BlockSpec((None, None, DKR, DVR), lambda b, h, c: (b, h, 0, 0))
    return pl.pallas_call(
        functools.partial(_ret_kernel, C=C),
        grid=(B, HR, L // C),
        in_specs=[pl.BlockSpec(memory_space=pltpu.SMEM), blk, blk, blk, st],
        out_specs=[blk, st],
        out_shape=[jax.ShapeDtypeStruct((B, L, WR), F32),
                   jax.ShapeDtypeStruct((B, HR, DKR, DVR), F32)],
        scratch_shapes=[pltpu.VMEM((DKR, DVR), F32), pltpu.VMEM((C, C), F32)],
        compiler_params=_params(("arbitrary", "arbitrary", "arbitrary")),
        name="retention",
    )(lg, qrb, kr, vrb, state0)


def _outmix_kernel(x_ref, oa_ref, or_ref, gr_ref, ga_ref, gb_ref, g1_ref, sh2_ref, sc2_ref, n2_ref,
                   sa_ref, sr_ref, wba_ref, wbr_ref, wo_ref, wpq_ref,
                   x1_ref, h2_ref, qp_ref, *, lam_init):
    sa = sa_ref[...] * 1.0
    sr = sr_ref[...]
    gr = gr_ref[...]
    silu_gr = gr * jax.nn.sigmoid(gr)
    ya_parts, yr_parts = [], []
    for hh in range(HA):
        sl = slice(hh * DVA, (hh + 1) * DVA)
        ya_parts.append(_rms_rows(oa_ref[:, sl]) * sa * (1.0 - lam_init))
        yr_parts.append(silu_gr[:, sl] * (_rms_rows(or_ref[:, sl]) * sr))
    ya = jnp.concatenate(ya_parts, axis=-1).astype(BF16)
    yr = jnp.concatenate(yr_parts, axis=-1).astype(BF16)
    y = (jax.nn.sigmoid(ga_ref[...]) * _dot(ya, wba_ref[...])
         + jax.nn.sigmoid(gb_ref[...]) * _dot(yr, wbr_ref[...]))
    out = _dot(y.astype(BF16), wo_ref[...])
    x1 = x_ref[...] + g1_ref[...] * out
    x1_ref[...] = x1
    h2 = (_rms_rows(x1) * n2_ref[...] * (1.0 + sc2_ref[...]) + sh2_ref[...]).astype(BF16)
    h2_ref[...] = h2
    qp_ref[...] = _dot(h2, wpq_ref[...]).astype(BF16)


def out_mix(x, oa, orr, gr, ga, gb, g1, sh2, sc2, norm2, subln_a, subln_r,
            w_ba_b, w_br_b, w_o_b, w_pq_b, lam_init, tm):
    B, L, D = x.shape
    nq = w_pq_b.shape[1]
    row = lambda n: pl.BlockSpec((None, tm, n), lambda b, i: (b, i, 0))
    mod = pl.BlockSpec((None, 1, D), lambda b, i: (b, 0, 0))
    full = lambda a: pl.BlockSpec(a.shape, lambda b, i: (0,) * a.ndim)
    n2 = norm2.reshape(1, D)
    sa = subln_a.reshape(1, DVA)
    sr = subln_r.reshape(1, DVR)
    return pl.pallas_call(
        functools.partial(_outmix_kernel, lam_init=lam_init),
        grid=(B, L // tm),
        in_specs=[row(D), row(WA), row(WR), row(WR), row(D), row(D), mod, mod, mod,
                  full(n2), full(sa), full(sr), full(w_ba_b), full(w_br_b), full(w_o_b), full(w_pq_b)],
        out_specs=[row(D), row(D), row(nq)],
        out_shape=[jax.ShapeDtypeStruct((B, L, D), F32),
                   jax.ShapeDtypeStruct((B, L, D), BF16),
                   jax.ShapeDtypeStruct((B, L, nq), BF16)],
        compiler_params=_params(("arbitrary", "arbitrary")),
        name="out_mix",
    )(x, oa, orr, gr, ga, gb, g1.reshape(B, 1, D), sh2.reshape(B, 1, D), sc2.reshape(B, 1, D),
      n2, sa, sr, w_ba_b, w_br_b, w_o_b, w_pq_b)


def _topk_rows(s, k):
    n = s.shape[0]
    iota = lax.broadcasted_iota(jnp.int32, s.shape, 0).astype(F32)
    work = s
    rank = jnp.full(s.shape, float(PEER_TOPK * PEER_TOPK), F32)
    vals = []
    for r in range(k):
        m = jnp.max(work, axis=0, keepdims=True)
        idx = jnp.min(jnp.where(work == m, iota, float(n)), axis=0, keepdims=True)
        sel = iota == idx
        rank = jnp.where(sel, float(r), rank)
        work = jnp.where(sel, -jnp.inf, work)
        vals.append(m)
    return vals, rank


SUBLANES = 8
_HEAD_A = SUBLANES
_CAND_NB = [PEER_TOPK] + [SUBLANES] * (_HEAD_A - 1)


def _route_chunk(s1, s2):
    v1, rank1 = _topk_rows(s1, PEER_TOPK)
    v2, rank2 = _topk_rows(s2, PEER_TOPK)
    v2m = jnp.concatenate(v2, axis=0)
    v1t = jnp.concatenate(v1[_HEAD_A:], axis=0)
    blocks = [v1[a] + v2m[0:nb] for a, nb in enumerate(_CAND_NB)] + [v1t + v2[0]]
    cand = jnp.concatenate(blocks, axis=0)
    _, crank = _topk_rows(cand, PEER_TOPK)
    sel = crank < float(PEER_TOPK)
    cmax = v1[0] + v2[0]
    z = jnp.sum(jnp.where(sel, jnp.exp(cand - cmax), 0.0), axis=0, keepdims=True)
    self32 = sel.astype(F32)
    cidx = jnp.zeros(s1.shape, F32)
    lo = 0
    for a, nb in enumerate(_CAND_NB):
        cnt = jnp.sum(self32[lo:lo + nb], axis=0, keepdims=True)
        cidx = jnp.where(rank1 == float(a), cnt, cidx)
        lo += nb
    for a in range(_HEAD_A, PEER_TOPK):
        cidx = jnp.where(rank1 == float(a), self32[lo + a - _HEAD_A:lo + a - _HEAD_A + 1], cidx)
    e1 = jnp.exp(s1 - v1[0]) / z
    e2 = jnp.exp(s2 - v2[0])
    return e1, e2, rank2, cidx


def _route_kernel(q_ref, keys_ref, e1_ref, e2_ref, r2_ref, c_ref, s1_sc, s2_sc):
    q = q_ref[...]
    s1_sc[...] = _dot_nt(keys_ref[0], q[:, 0:KEY_DIM])
    s2_sc[...] = _dot_nt(keys_ref[1], q[:, KEY_DIM:2 * KEY_DIM])

    def chunk(ci, carry):
        sl = pl.ds(pl.multiple_of(ci * LANES, LANES), LANES)
        e1, e2, rank2, cidx = _route_chunk(s1_sc[:, sl], s2_sc[:, sl])
        e1_ref[:, sl] = e1
        e2_ref[:, sl] = e2.astype(BF16)
        r2_ref[:, sl] = rank2.astype(BF16)
        c_ref[:, sl] = cidx
        return carry

    lax.fori_loop(0, s1_sc.shape[1] // LANES, chunk, 0, unroll=2)


def peer_route(qp, keys_b, tm):
    T = qp.shape[0]
    out = pl.BlockSpec((None, N_KEYS, tm), lambda t, h: (h, 0, t))
    shp = lambda dt: jax.ShapeDtypeStruct((PEER_HEADS, N_KEYS, T), dt)
    return pl.pallas_call(
        _route_kernel,
        grid=(T // tm, PEER_HEADS),
        in_specs=[pl.BlockSpec((tm, 2 * KEY_DIM), lambda t, h: (t, h)),
                  pl.BlockSpec((2, N_KEYS, KEY_DIM), lambda t, h: (h, 0, 0))],
        out_specs=[out, out, out, out],
        out_shape=[shp(F32), shp(BF16), shp(BF16), shp(F32)],
        scratch_shapes=[pltpu.VMEM((N_KEYS, tm), F32), pltpu.VMEM((N_KEYS, tm), F32)],
        compiler_params=_params(("arbitrary", "arbitrary")),
        name="peer_route",
    )(qp, keys_b)


BF16_ROWS = 16


def _bcast_rows_bf16(row, n):
    t = row.shape[1]
    tile = jnp.broadcast_to(row, (BF16_ROWS, t)).astype(BF16)
    return jnp.broadcast_to(tile[None], (n // BF16_ROWS, BF16_ROWS, t)).reshape(n, t)


def _peer_kernel(h2_ref, u0_ref, ub_ref, ua_ref, vtp_ref, vta_ref, e1_ref, e2_ref, r2_ref, c_ref,
                 x1_ref, g2_ref, fn_ref, y_ref, acc_sc, a0_sc, a1_sc, w0_sc, w1_sc, *, ni):
    e = pl.program_id(1)
    last = pl.num_programs(1) - 1

    @pl.when(e == 0)
    def _():
        acc_sc[...] = jnp.zeros(acc_sc.shape, F32)
        a0_sc[...] = _dot_nt(u0_ref[...], h2_ref[...])
        w1_sc[...] = jnp.zeros(w1_sc.shape, BF16)

    def gate(a_sc, w_sc, blk):
        for ii in range(ni):
            i = blk * ni + ii
            a = a_sc[ii * N_KEYS:(ii + 1) * N_KEYS, :]
            g = jnp.zeros(a.shape, BF16)
            for hh in range(PEER_HEADS):
                c_row = _bcast_rows_bf16(c_ref[hh, pl.ds(i, 1), :], N_KEYS)
                e1_row = _bcast_rows_bf16(e1_ref[hh, pl.ds(i, 1), :], N_KEYS)
                g = g + jnp.where(r2_ref[hh] < c_row, e2_ref[hh] * e1_row, jnp.zeros_like(g))
            act = 0.5 * a * (1.0 + lax.erf(a * (2.0 ** -0.5)))
            w_sc[ii * N_KEYS:(ii + 1) * N_KEYS, :] = g * act.astype(BF16)

    @pl.when(e < last)
    def _():
        h2 = h2_ref[...]
        acc_sc[...] += _dot(vtp_ref[...], w1_sc[...])
        a1_sc[...] = _dot_nt(ub_ref[...], h2)
        gate(a0_sc, w0_sc, 2 * e)
        acc_sc[...] += _dot(vta_ref[...], w0_sc[...])
        a0_sc[...] = _dot_nt(ua_ref[...], h2)
        gate(a1_sc, w1_sc, 2 * e + 1)

    @pl.when(e == last)
    def _():
        acc = acc_sc[...] + _dot(vtp_ref[...], w1_sc[...])
        x2 = x1_ref[...] + g2_ref[...] * acc.T
        y_ref[...] = _rms_rows(x2) * fn_ref[...]


def peer_experts(h2, u_b, vt_b, e1, e2, r2, cc, x1, g2tok, final_norm, tm, ni=4):
    T, D = h2.shape
    nb = ni * N_KEYS
    nblk = u_b.shape[0] // nb
    assert nblk % 2 == 0
    tok = pl.BlockSpec((tm, D), lambda t, e: (t, 0))
    gate = pl.BlockSpec((PEER_HEADS, N_KEYS, tm), lambda t, e: (0, 0, t))
    ublk = lambda f: pl.BlockSpec((nb, D), lambda t, e: (f(e), 0))
    vblk = lambda f: pl.BlockSpec((D, nb), lambda t, e: (0, f(e)))
    return pl.pallas_call(
        functools.partial(_peer_kernel, ni=ni),
        grid=(T // tm, nblk // 2 + 1),
        in_specs=[tok,
                  ublk(lambda e: 0),
                  ublk(lambda e: jnp.minimum(2 * e + 1, nblk - 1)),
                  ublk(lambda e: jnp.minimum(2 * e + 2, nblk - 1)),
                  vblk(lambda e: jnp.maximum(2 * e - 1, 0)),
                  vblk(lambda e: jnp.minimum(2 * e, nblk - 1)),
                  gate, gate, gate, gate, tok, tok,
                  pl.BlockSpec((1, D), lambda t, e: (0, 0))],
        out_specs=tok,
        out_shape=jax.ShapeDtypeStruct((T, D), F32),
        scratch_shapes=[pltpu.VMEM((D, tm), F32),
                        pltpu.VMEM((nb, tm), F32), pltpu.VMEM((nb, tm), F32),
                        pltpu.VMEM((nb, tm), BF16), pltpu.VMEM((nb, tm), BF16)],
        compiler_params=_params(("arbitrary", "arbitrary")),
        name="peer_experts",
    )(h2, u_b, u_b, u_b, vt_b, vt_b, e1, e2, r2, cc, x1, g2tok, final_norm.reshape(1, D))


def _rot_tables(pos):
    inv = 1.0 / (10000.0 ** jnp.linspace(0.0, 1.0, DKR // 2, dtype=F32))
    ang = pos[:, None].astype(F32) * inv[None, :]
    cos, sin = jnp.cos(ang), jnp.sin(ang)
    return jnp.concatenate([cos, cos], axis=-1), jnp.concatenate([-sin, sin], axis=-1)


def _pick_tile(n, pref):
    t = min(n, pref)
    assert n % t == 0, (n, t)
    return t


def _trunk(x, mods, pos, lam_init, lamv, rel_bias, w, attend, values_transposed, state0, ret_chunk,
           final_norm):
    (norm1, norm2, w_in_b, subln_a, subln_r, w_ba_b, w_br_b, w_o_b, w_pq_b, keys_b, u_b, vt_b) = w
    sh1, sc1, g1, sh2, sc2, g2 = mods
    B, L, D = x.shape
    T = B * L
    cos2, sin2 = _rot_tables(pos)
    tm = _pick_tile(L, 256)
    ka, va, qab, kab, vab, qrb, kr, vrb, gr, ga, gb, *vt = in_proj(
        x, sh1, sc1, norm1, w_in_b, cos2, sin2, tm, emit_vt=values_transposed)
    oa = attend(qab, kab, vt[0] if values_transposed else vab)
    orr, st = retention(qrb, kr, vrb, state0, ret_chunk)
    x1, h2, qp = out_mix(x, oa, orr, gr, ga, gb, g1, sh2, sc2, norm2, subln_a, subln_r,
                         w_ba_b, w_br_b, w_o_b, w_pq_b, lam_init, tm)
    e1, e2, r2, cc = peer_route(qp.reshape(T, -1), keys_b, _pick_tile(T, 512))
    g2tok = jnp.broadcast_to(g2[:, None, :], (B, L, D)).reshape(T, D)
    y = peer_experts(h2.reshape(T, D), u_b, vt_b, e1, e2, r2, cc, x1.reshape(T, D), g2tok,
                     final_norm, _pick_tile(T, 512))
    return y.reshape(B, L, D), ka.reshape(B, L, HA, 2 * DA), va.reshape(B, L, HA, DVA), st


def kernel(x_prompt, x_sample, cache_k, cache_v, state_ret, c_prompt, c_sample, w_ada, b_ada, norm1,
           norm2, w_in, lam_q1, lam_k1, lam_q2, lam_k2, subln_a, subln_r, w_ba, w_br, w_o, rel_bias,
           w_pq, peer_keys, peer_u, peer_v, final_norm):
    depth = w_ada.shape[0]
    assert depth == 1, "the fused final norm assumes a single layer"
    Bp, Lp, D = x_prompt.shape
    Bs, Ls, _ = x_sample.shape
    past = cache_k.shape[2]
    pos_p = jnp.arange(Lp, dtype=jnp.int32)
    pos_s = past + jnp.arange(Ls, dtype=jnp.int32)
    lk = past + Ls
    lk_pad = -(-lk // LANES) * LANES
    k_pos_s = jnp.arange(lk_pad, dtype=jnp.int32)
    k_valid_s = k_pos_s < lk

    l = 0
    lam_init = 0.8 - 0.6 * math.exp(-0.3 * l)
    lamv = jnp.stack([lam_q1[l], lam_k1[l], lam_q2[l], lam_k2[l]]).astype(F32)
    mod = ada_mod(jnp.concatenate([c_prompt, c_sample], axis=0), w_ada[l], b_ada[l])
    mods = jnp.split(mod, 6, axis=-1)
    mods_p = [m[:Bp] for m in mods]
    mods_s = [m[Bp:] for m in mods]
    w = (norm1[l], norm2[l], w_in[l].astype(BF16), subln_a[l], subln_r[l], w_ba[l].astype(BF16),
         w_br[l].astype(BF16), w_o[l].astype(BF16), w_pq[l].astype(BF16),
         peer_keys[l].reshape(PEER_HEADS * 2, N_KEYS, KEY_DIM).astype(BF16),
         peer_u[l].astype(BF16), peer_v[l].T.astype(BF16))

    attend_p = lambda q, k, v: attn_prompt(q, k, v, rel_bias, lamv, lam_init)
    zero_state = jnp.zeros((Bp, HR, DKR, DVR), F32)
    yp, kp, vp, sp = _trunk(x_prompt, mods_p, pos_p, lam_init, lamv, rel_bias, w, attend_p, True,
                            zero_state, _pick_tile(Lp, 256), final_norm)

    def attend_s(q, k, v):
        padk = jnp.zeros((Bs, lk_pad - lk, WA), BF16)
        kc = cache_k[l].reshape(Bs, past, HA * 2 * DA).astype(BF16)
        vc = cache_v[l].reshape(Bs, past, WA).astype(BF16)
        k_all = jnp.concatenate([kc, k, padk], axis=1)
        v_all = jnp.concatenate([vc, v, padk], axis=1)
        return attn_small(q, k_all, v_all, pos_s, k_pos_s, k_valid_s, rel_bias, lamv, lam_init)

    ys, ks, vs, ss = _trunk(x_sample, mods_s, pos_s, lam_init, lamv, rel_bias, w, attend_s, False,
                            state_ret[l].astype(F32), Ls, final_norm)
    return (yp, ys, kp[None], vp[None], sp[None], ks[None], vs[None], ss[None])
```

```python
import functools
import math

import jax
import jax.numpy as jnp
from jax import lax
from jax.experimental import pallas as pl
from jax.experimental.pallas import tpu as pltpu

F32 = jnp.float32
BF16 = jnp.bfloat16

CHUNK = 64
HA = 4
DA = 64
DVA = 2 * DA
HR = 4
DKR = 128
DVR = 128
N_BUCKETS = 32
MAX_DIST = 128
PEER_HEADS = 8
N_KEYS = 128
KEY_DIM = 128
PEER_TOPK = 16
EPS = 1e-6
WA = HA * DVA
WR = HR * DVR
NEG = -1e30
LOG2E = math.log2(math.e)
LANES = 128
VMEM_LIMIT = 56 * 1024 * 1024


def _params(sem, vmem=VMEM_LIMIT):
    return pltpu.CompilerParams(dimension_semantics=sem, vmem_limit_bytes=vmem)


def _dot(a, b):
    return jnp.dot(a, b, preferred_element_type=F32)


def _dot_nt(a, b):
    return lax.dot_general(a, b, (((1,), (1,)), ((), ())), preferred_element_type=F32)


def _dot_tn(a, b):
    return lax.dot_general(a, b, (((0,), (0,)), ((), ())), preferred_element_type=F32)


def _rms_rows(x):
    return x * lax.rsqrt(jnp.mean(x * x, axis=-1, keepdims=True) + EPS)


def _ada_kernel(c_ref, w_ref, b_ref, o_ref):
    c = c_ref[...]
    s = (c * jax.nn.sigmoid(c)).astype(BF16)
    o_ref[...] = _dot(s, w_ref[...].astype(BF16)) + b_ref[...]


def ada_mod(c, w_ada, b_ada, tn=1024):
    nb, d = c.shape
    n = w_ada.shape[1]
    return pl.pallas_call(
        _ada_kernel,
        grid=(n // tn,),
        in_specs=[pl.BlockSpec((nb, d), lambda j: (0, 0)),
                  pl.BlockSpec((d, tn), lambda j: (0, j)),
                  pl.BlockSpec((1, tn), lambda j: (0, j))],
        out_specs=pl.BlockSpec((nb, tn), lambda j: (0, j)),
        out_shape=jax.ShapeDtypeStruct((nb, n), F32),
        compiler_params=_params(("arbitrary",)),
        name="ada_mod",
    )(c, w_ada, b_ada.reshape(1, n))


def _inproj_kernel(x_ref, sh_ref, sc_ref, n1_ref, w_ref, cos_ref, sin_ref,
                   ka_ref, va_ref, qab_ref, kab_ref, vab_ref, qrb_ref, kr_ref, vrb_ref,
                   gr_ref, ga_ref, gb_ref, vt_ref=None):
    x = x_ref[...]
    h = _rms_rows(x) * n1_ref[...] * (1.0 + sc_ref[...]) + sh_ref[...]
    hb = h.astype(BF16)

    def proj(lo, n):
        return _dot(hb, w_ref[:, lo:lo + n])

    qa = proj(0, 512)
    qab_ref[...] = (qa * (DA ** -0.5 * LOG2E)).astype(BF16)
    ka = proj(512, 512)
    ka_ref[...] = ka
    kab_ref[...] = ka.astype(BF16)
    va = proj(1024, 512)
    va_ref[...] = va
    vab_ref[...] = va.astype(BF16)
    if vt_ref is not None:
        vt_ref[...] = va.T.astype(BF16)
    cos2 = cos_ref[...]
    sin2 = sin_ref[...]

    def rot(z):
        parts = []
        for hh in range(HR):
            zh = z[:, hh * DKR:(hh + 1) * DKR]
            parts.append(zh * cos2 + pltpu.roll(zh, DKR // 2, 1) * sin2)
        return jnp.concatenate(parts, axis=-1)

    qrb_ref[...] = rot(proj(1536, 512)).astype(BF16)
    kr_ref[...] = rot(proj(2048, 512)) * (DKR ** -0.5)
    vrb_ref[...] = proj(2560, 512).astype(BF16)
    gr_ref[...] = proj(3072, 512)
    ga_ref[...] = proj(3584, 1024)
    gb_ref[...] = proj(4608, 1024)


def in_proj(x, sh1, sc1, norm1, w_in_b, cos2, sin2, tm, emit_vt):
    B, L, D = x.shape
    d_in = w_in_b.shape[1]
    row = lambda n: pl.BlockSpec((None, tm, n), lambda b, i: (b, i, 0))
    mod = pl.BlockSpec((None, 1, D), lambda b, i: (b, 0, 0))
    f = lambda n, dt: jax.ShapeDtypeStruct((B, L, n), dt)
    vt_spec = [pl.BlockSpec((None, WA, tm), lambda b, i: (b, 0, i))] if emit_vt else []
    vt_shape = [jax.ShapeDtypeStruct((B, WA, L), BF16)] if emit_vt else []
    return pl.pallas_call(
        _inproj_kernel,
        grid=(B, L // tm),
        in_specs=[row(D), mod, mod,
                  pl.BlockSpec((1, D), lambda b, i: (0, 0)),
                  pl.BlockSpec((D, d_in), lambda b, i: (0, 0)),
                  pl.BlockSpec((tm, DKR), lambda b, i: (i, 0)),
                  pl.BlockSpec((tm, DKR), lambda b, i: (i, 0))],
        out_specs=[row(512), row(512), row(512), row(512), row(512), row(512), row(512), row(512),
                   row(512), row(1024), row(1024)] + vt_spec,
        out_shape=[f(512, F32), f(512, F32), f(512, BF16), f(512, BF16), f(512, BF16),
                   f(512, BF16), f(512, F32), f(512, BF16), f(512, F32), f(1024, F32), f(1024, F32)]
        + vt_shape,
        compiler_params=_params(("arbitrary", "arbitrary")),
        name="in_proj",
    )(x, sh1.reshape(B, 1, D), sc1.reshape(B, 1, D), norm1.reshape(1, D), w_in_b, cos2, sin2)


def _t5_bucket(rel):
    nb = N_BUCKETS // 2
    ret = jnp.where(rel > 0, nb, 0)
    n = jnp.abs(rel)
    max_exact = nb // 2
    nf = jnp.maximum(n, max_exact).astype(F32)
    large = max_exact + (jnp.log(nf / max_exact) / math.log(MAX_DIST / max_exact)
                         * (nb - max_exact)).astype(jnp.int32)
    large = jnp.minimum(large, nb - 1)
    return ret + jnp.where(n < max_exact, n, large)


def _bias_from_buckets(bkt, rb_ref, h, shift):
    val = jnp.where(bkt < 0, NEG, 0.0).astype(F32)
    for n in range(N_BUCKETS):
        val = jnp.where(bkt == n, (rb_ref[n, h] - shift) * LOG2E, val)
    return val


def _lam_value(lv_ref, lam_init):
    lv = lv_ref[...]
    a = jnp.sum(lv[0:1] * lv[1:2], axis=-1, keepdims=True)
    b = jnp.sum(lv[2:3] * lv[3:4], axis=-1, keepdims=True)
    return jnp.exp(a) - jnp.exp(b) + lam_init


def _block_diag_q(q):
    lane = lax.broadcasted_iota(jnp.int32, q.shape, 1)
    zero = jnp.zeros_like(q)
    return jnp.concatenate([jnp.where(lane < DA, q, zero), jnp.where(lane >= DA, q, zero)], axis=0)


def _attn_kernel(rb_ref, lv_ref, q_ref, k_ref, vt_ref, bkt_ref, o_ref,
                 bias_sc, m_sc, acc_sc, va_sc, *, tq, tb, lam_init):
    h = pl.program_id(1)
    i = pl.program_id(2)

    @pl.when(i == 0)
    def _():
        far = rb_ref[N_BUCKETS // 2 - 1, h]
        for t in range(2):
            bias_sc[t] = _bias_from_buckets(bkt_ref[t], rb_ref, h, far)
        va_sc[0:DVA, :] = vt_ref[...]
        va_sc[DVA:, :] = jnp.ones((BF16_ROWS, va_sc.shape[1]), BF16)

    qt = q_ref[...].astype(F32).T
    drow = lax.broadcasted_iota(jnp.int32, qt.shape, 0)
    qbd_t = jnp.concatenate([jnp.where(drow < DA, qt, 0.0), jnp.where(drow >= DA, qt, 0.0)],
                            axis=1).astype(BF16)
    m_sc[...] = jnp.full(m_sc.shape, NEG, F32)
    acc_sc[...] = jnp.zeros(acc_sc.shape, F32)

    def scores(off, tk, bias=None, nvalid=None):
        s = _dot(k_ref[pl.ds(off, tk), :], qbd_t)
        if bias is not None:
            s = s + jnp.concatenate([bias, bias], axis=1)
        if nvalid is not None:
            row = lax.broadcasted_iota(jnp.int32, (tk, 1), 0)
            s = jnp.where(row < nvalid, s, NEG)
        return s

    def absorb(s, off, tk):
        m_prev = m_sc[...]
        m_new = jnp.maximum(m_prev, jnp.max(s, axis=0, keepdims=True))
        alpha = jnp.exp2(m_prev - m_new)
        p = jnp.exp2(s - m_new)
        acc_sc[...] = alpha * acc_sc[...] + _dot(va_sc[:, pl.ds(off, tk)], p.astype(BF16))
        m_sc[...] = m_new

    def tiles(specs):
        ss = [scores(*sp) for sp in specs]
        for s, sp in zip(ss, specs):
            absorb(s, sp[0], sp[1])

    far_end = jnp.maximum(i - 1, 0) * tq
    group = 2 * tb
    nfull = far_end // group
    rem = far_end - nfull * group

    def far_body(t, carry):
        off = pl.multiple_of(t * group, group)
        tiles([(off, tb), (pl.multiple_of(off + tb, tb), tb)])
        return carry

    lax.fori_loop(0, nfull, far_body, 0)

    @pl.when(rem > 0)
    def _():
        off = pl.multiple_of(nfull * group, group)
        tiles([(off, tb, None, rem), (pl.multiple_of(off + tb, tb), tb, None, rem - tb)])

    @pl.when(i >= 1)
    def _():
        off = pl.multiple_of((i - 1) * tq, tq)
        tiles([(off, tq, bias_sc[1]), (pl.multiple_of(off + tq, tq), tq, bias_sc[0])])

    @pl.when(i == 0)
    def _():
        tiles([(0, tq, bias_sc[0])])

    lam = _lam_value(lv_ref, lam_init)
    o = acc_sc[0:DVA, :] / acc_sc[DVA:DVA + 1, :]
    o_ref[...] = (o[:, 0:tq] - lam * o[:, tq:2 * tq]).T


def attn_prompt(qab, kab, vt, rel_bias, lamv, lam_init, tq=256, tb=512):
    B, S, _ = qab.shape
    tb = min(tb, S // 2)
    assert S % (2 * tb) == 0 and tb % tq == 0
    c = jnp.arange(tq, dtype=jnp.int32)[:, None]
    r = jnp.arange(tq, dtype=jnp.int32)[None, :]
    diag = jnp.where((c // CHUNK) <= (r // CHUNK), _t5_bucket(c - r), -1)
    prev = _t5_bucket(c - r - tq)
    bkt = jnp.stack([diag, prev]).astype(jnp.int32)
    smem = pl.BlockSpec(memory_space=pltpu.SMEM)
    return pl.pallas_call(
        functools.partial(_attn_kernel, tq=tq, tb=tb, lam_init=lam_init),
        grid=(B, HA, S // tq),
        in_specs=[smem,
                  pl.BlockSpec((4, DA), lambda b, h, i: (0, 0)),
                  pl.BlockSpec((None, tq, DVA), lambda b, h, i: (b, i, h)),
                  pl.BlockSpec((None, S, DVA), lambda b, h, i: (b, 0, h)),
                  pl.BlockSpec((None, DVA, S), lambda b, h, i: (b, h, 0)),
                  pl.BlockSpec((2, tq, tq), lambda b, h, i: (0, 0, 0))],
        out_specs=pl.BlockSpec((None, tq, DVA), lambda b, h, i: (b, i, h)),
        out_shape=jax.ShapeDtypeStruct((B, S, WA), F32),
        scratch_shapes=[pltpu.VMEM((2, tq, tq), F32),
                        pltpu.VMEM((1, 2 * tq), F32),
                        pltpu.VMEM((DVA + BF16_ROWS, 2 * tq), F32),
                        pltpu.VMEM((DVA + BF16_ROWS, S), BF16)],
        compiler_params=_params(("arbitrary", "arbitrary", "arbitrary")),
        name="attn_prompt",
    )(rel_bias, lamv, qab, kab, vt, bkt)


def _attn_small_kernel(rb_ref, lv_ref, q_ref, k_ref, v_ref, bkt_ref, o_ref, *, lq, lam_init):
    h = pl.program_id(1)
    bias = _bias_from_buckets(bkt_ref[...], rb_ref, h, 0.0)
    qbd = _block_diag_q(q_ref[...])
    s = _dot_nt(qbd, k_ref[...]) + jnp.concatenate([bias, bias], axis=0)
    m = jnp.max(s, axis=-1, keepdims=True)
    p = jnp.exp2(s - m)
    l = jnp.sum(p, axis=-1, keepdims=True)
    o = _dot(p.astype(BF16), v_ref[...]) / l
    lam = _lam_value(lv_ref, lam_init)
    o_ref[...] = o[0:lq] - lam * o[lq:2 * lq]


def attn_small(qab, k_all, v_all, q_pos, k_pos, k_valid, rel_bias, lamv, lam_init):
    B, Lq, _ = qab.shape
    Lk = k_all.shape[1]
    visible = ((k_pos[None, :] // CHUNK) <= (q_pos[:, None] // CHUNK)) & k_valid[None, :]
    bkt = jnp.where(visible, _t5_bucket(k_pos[None, :] - q_pos[:, None]), -1).astype(jnp.int32)
    smem = pl.BlockSpec(memory_space=pltpu.SMEM)
    return pl.pallas_call(
        functools.partial(_attn_small_kernel, lq=Lq, lam_init=lam_init),
        grid=(B, HA),
        in_specs=[smem,
                  pl.BlockSpec((4, DA), lambda b, h: (0, 0)),
                  pl.BlockSpec((None, Lq, DVA), lambda b, h: (b, 0, h)),
                  pl.BlockSpec((None, Lk, DVA), lambda b, h: (b, 0, h)),
                  pl.BlockSpec((None, Lk, DVA), lambda b, h: (b, 0, h)),
                  pl.BlockSpec((Lq, Lk), lambda b, h: (0, 0))],
        out_specs=pl.BlockSpec((None, Lq, DVA), lambda b, h: (b, 0, h)),
        out_shape=jax.ShapeDtypeStruct((B, Lq, WA), F32),
        compiler_params=_params(("arbitrary", "arbitrary")),
        name="attn_sample",
    )(rel_bias, lamv, qab, k_all, v_all, bkt)


def _ret_kernel(lg_ref, q_ref, k_ref, v_ref, s0_ref, o_ref, so_ref, state_sc, decay_sc, *, C):
    h = pl.program_id(1)
    c = pl.program_id(2)
    lg = lg_ref[h]

    @pl.when(c == 0)
    def _():
        state_sc[...] = s0_ref[...]
        r = lax.broadcasted_iota(jnp.int32, (C, C), 0)
        cc = lax.broadcasted_iota(jnp.int32, (C, C), 1)
        diff = (r - cc).astype(F32)
        decay_sc[...] = jnp.where(diff >= 0, jnp.exp(jnp.maximum(diff, 0.0) * lg), 0.0)

    n = lax.broadcasted_iota(jnp.int32, (C, 1), 0).astype(F32)
    xi = jnp.exp((n + 1.0) * lg)
    zeta = jnp.exp((C - 1.0 - n) * lg)
    q = q_ref[...]
    k = k_ref[...]
    v = v_ref[...]
    state = state_sc[...]
    scores = _dot_nt(q, k.astype(BF16)) * decay_sc[...]
    intra = _dot(scores.astype(BF16), v)
    cross = _dot(q, state.astype(BF16)) * xi
    o_ref[...] = intra + cross
    kz = (k * zeta).astype(BF16)
    state_sc[...] = jnp.exp(C * lg) * state + _dot_tn(kz, v)

    @pl.when(c == pl.num_programs(2) - 1)
    def _():
        so_ref[...] = state_sc[...]


def retention(qrb, kr, vrb, state0, C):
    B, L, _ = qrb.shape
    lg = jnp.log(1.0 - 2.0 ** (-5.0 - jnp.arange(HR, dtype=F32)))
    blk = pl.BlockSpec((None, C, DKR), lambda b, h, c: (b, c, h))
    st = pl.BlockSpec((None, None, DKR, DVR), lambda b, h, c: (b, h, 0, 0))
    return pl.pallas_call(
        functools.partial(_ret_kernel, C=C),
        grid=(B, HR, L // C),
        in_specs=[pl.BlockSpec(memory_space=pltpu.SMEM), blk, blk, blk, st],
        out_specs=[blk, st],
        out_shape=[jax.ShapeDtypeStruct((B, L, WR), F32),
                   jax.ShapeDtypeStruct((B, HR, DKR, DVR), F32)],
        scratch_shapes=[pltpu.VMEM((DKR, DVR), F32), pltpu.VMEM((C, C), F32)],
        compiler_params=_params(("arbitrary", "arbitrary", "arbitrary")),
        name="retention",
    )(lg, qrb, kr, vrb, state0)


def _outmix_kernel(x_ref, oa_ref, or_ref, gr_ref, ga_ref, gb_ref, g1_ref, sh2_ref, sc2_ref, n2_ref,
                   sa_ref, sr_ref, wba_ref, wbr_ref, wo_ref, wpq_ref,
                   x1_ref, h2_ref, qp_ref, *, lam_init, h2_transposed):
    sa = sa_ref[...] * 1.0
    sr = sr_ref[...]
    gr = gr_ref[...]
    silu_gr = gr * jax.nn.sigmoid(gr)
    ya_parts, yr_parts = [], []
    for hh in range(HA):
        sl = slice(hh * DVA, (hh + 1) * DVA)
        ya_parts.append(_rms_rows(oa_ref[:, sl]) * sa * (1.0 - lam_init))
        yr_parts.append(silu_gr[:, sl] * (_rms_rows(or_ref[:, sl]) * sr))
    ya = jnp.concatenate(ya_parts, axis=-1).astype(BF16)
    yr = jnp.concatenate(yr_parts, axis=-1).astype(BF16)
    y = (jax.nn.sigmoid(ga_ref[...]) * _dot(ya, wba_ref[...])
         + jax.nn.sigmoid(gb_ref[...]) * _dot(yr, wbr_ref[...]))
    out = _dot(y.astype(BF16), wo_ref[...])
    x1 = x_ref[...] + g1_ref[...] * out
    x1_ref[...] = x1
    h2f = _rms_rows(x1) * n2_ref[...] * (1.0 + sc2_ref[...]) + sh2_ref[...]
    h2 = h2f.astype(BF16)
    h2_ref[...] = h2f.T.astype(BF16) if h2_transposed else h2
    qp_ref[...] = _dot(h2, wpq_ref[...]).astype(BF16)


def out_mix(x, oa, orr, gr, ga, gb, g1, sh2, sc2, norm2, subln_a, subln_r,
            w_ba_b, w_br_b, w_o_b, w_pq_b, lam_init, tm):
    B, L, D = x.shape
    nq = w_pq_b.shape[1]
    row = lambda n: pl.BlockSpec((None, tm, n), lambda b, i: (b, i, 0))
    mod = pl.BlockSpec((None, 1, D), lambda b, i: (b, 0, 0))
    full = lambda a: pl.BlockSpec(a.shape, lambda b, i: (0,) * a.ndim)
    n2 = norm2.reshape(1, D)
    sa = subln_a.reshape(1, DVA)
    sr = subln_r.reshape(1, DVR)
    h2_transposed = tm % LANES == 0
    nt = L // tm
    if h2_transposed:
        h2_spec = pl.BlockSpec((D, tm), lambda b, i: (0, b * nt + i))
        h2_shape = jax.ShapeDtypeStruct((D, B * L), BF16)
    else:
        h2_spec = row(D)
        h2_shape = jax.ShapeDtypeStruct((B, L, D), BF16)
    x1, h2, qp = pl.pallas_call(
        functools.partial(_outmix_kernel, lam_init=lam_init, h2_transposed=h2_transposed),
        grid=(B, nt),
        in_specs=[row(D), row(WA), row(WR), row(WR), row(D), row(D), mod, mod, mod,
                  full(n2), full(sa), full(sr), full(w_ba_b), full(w_br_b), full(w_o_b), full(w_pq_b)],
        out_specs=[row(D), h2_spec, row(nq)],
        out_shape=[jax.ShapeDtypeStruct((B, L, D), F32), h2_shape,
                   jax.ShapeDtypeStruct((B, L, nq), BF16)],
        compiler_params=_params(("arbitrary", "arbitrary")),
        name="out_mix",
    )(x, oa, orr, gr, ga, gb, g1.reshape(B, 1, D), sh2.reshape(B, 1, D), sc2.reshape(B, 1, D),
      n2, sa, sr, w_ba_b, w_br_b, w_o_b, w_pq_b)
    h2t = h2 if h2_transposed else h2.reshape(B * L, D).T
    return x1, h2t, qp


def _topk_rows(s, k):
    n = s.shape[0]
    iota = lax.broadcasted_iota(jnp.int32, s.shape, 0).astype(F32)
    work = s
    rank = jnp.full(s.shape, float(PEER_TOPK * PEER_TOPK), F32)
    vals = []
    for r in range(k):
        m = jnp.max(work, axis=0, keepdims=True)
        idx = jnp.min(jnp.where(work == m, iota, float(n)), axis=0, keepdims=True)
        sel = iota == idx
        rank = jnp.where(sel, float(r), rank)
        work = jnp.where(sel, -jnp.inf, work)
        vals.append(m)
    return vals, rank


SUBLANES = 8
_HEAD_A = SUBLANES
_CAND_NB = [PEER_TOPK] + [SUBLANES] * (_HEAD_A - 1)


def _route_chunk(s1, s2):
    v1, rank1 = _topk_rows(s1, PEER_TOPK)
    v2, rank2 = _topk_rows(s2, PEER_TOPK)
    v2m = jnp.concatenate(v2, axis=0)
    v1t = jnp.concatenate(v1[_HEAD_A:], axis=0)
    blocks = [v1[a] + v2m[0:nb] for a, nb in enumerate(_CAND_NB)] + [v1t + v2[0]]
    cand = jnp.concatenate(blocks, axis=0)
    _, crank = _topk_rows(cand, PEER_TOPK)
    sel = crank < float(PEER_TOPK)
    cmax = v1[0] + v2[0]
    z = jnp.sum(jnp.where(sel, jnp.exp(cand - cmax), 0.0), axis=0, keepdims=True)
    self32 = sel.astype(F32)
    cidx = jnp.zeros(s1.shape, F32)
    lo = 0
    for a, nb in enumerate(_CAND_NB):
        cnt = jnp.sum(self32[lo:lo + nb], axis=0, keepdims=True)
        cidx = jnp.where(rank1 == float(a), cnt, cidx)
        lo += nb
    for a in range(_HEAD_A, PEER_TOPK):
        cidx = jnp.where(rank1 == float(a), self32[lo + a - _HEAD_A:lo + a - _HEAD_A + 1], cidx)
    e1 = jnp.exp(s1 - v1[0]) / z
    e2 = jnp.exp(s2 - v2[0])
    return e1, e2, rank2, cidx


def _route_kernel(q_ref, keys_ref, e1_ref, e2_ref, r2_ref, c_ref, s1_sc, s2_sc):
    q = q_ref[...]
    s1_sc[...] = _dot_nt(keys_ref[0], q[:, 0:KEY_DIM])
    s2_sc[...] = _dot_nt(keys_ref[1], q[:, KEY_DIM:2 * KEY_DIM])

    def chunk(ci, carry):
        sl = pl.ds(pl.multiple_of(ci * LANES, LANES), LANES)
        e1, e2, rank2, cidx = _route_chunk(s1_sc[:, sl], s2_sc[:, sl])
        e1_ref[:, sl] = e1
        e2_ref[:, sl] = e2.astype(BF16)
        r2_ref[:, sl] = rank2.astype(BF16)
        c_ref[:, sl] = cidx
        return carry

    lax.fori_loop(0, s1_sc.shape[1] // LANES, chunk, 0, unroll=2)


def peer_route(qp, keys_b, tm):
    T = qp.shape[0]
    out = pl.BlockSpec((None, N_KEYS, tm), lambda t, h: (h, 0, t))
    shp = lambda dt: jax.ShapeDtypeStruct((PEER_HEADS, N_KEYS, T), dt)
    return pl.pallas_call(
        _route_kernel,
        grid=(T // tm, PEER_HEADS),
        in_specs=[pl.BlockSpec((tm, 2 * KEY_DIM), lambda t, h: (t, h)),
                  pl.BlockSpec((2, N_KEYS, KEY_DIM), lambda t, h: (h, 0, 0))],
        out_specs=[out, out, out, out],
        out_shape=[shp(F32), shp(BF16), shp(BF16), shp(F32)],
        scratch_shapes=[pltpu.VMEM((N_KEYS, tm), F32), pltpu.VMEM((N_KEYS, tm), F32)],
        compiler_params=_params(("arbitrary", "arbitrary")),
        name="peer_route",
    )(qp, keys_b)


BF16_ROWS = 16


def _bcast_rows_bf16(row, n):
    t = row.shape[1]
    tile = jnp.broadcast_to(row, (BF16_ROWS, t)).astype(BF16)
    return jnp.broadcast_to(tile[None], (n // BF16_ROWS, BF16_ROWS, t)).reshape(n, t)


def _peer_kernel(h2_ref, u0_ref, ub_ref, ua_ref, vtp_ref, vta_ref, e1_ref, e2_ref, r2_ref, c_ref,
                 x1_ref, g2_ref, fn_ref, y_ref, acc_sc, a0_sc, a1_sc, w0_sc, w1_sc, *, ni):
    e = pl.program_id(1)
    last = pl.num_programs(1) - 1

    @pl.when(e == 0)
    def _():
        acc_sc[...] = jnp.zeros(acc_sc.shape, F32)
        a0_sc[...] = _dot(u0_ref[...], h2_ref[...])
        w1_sc[...] = jnp.zeros(w1_sc.shape, BF16)

    def gate(a_sc, w_sc, blk):
        for ii in range(ni):
            i = blk * ni + ii
            a = a_sc[ii * N_KEYS:(ii + 1) * N_KEYS, :]
            g = jnp.zeros(a.shape, BF16)
            for hh in range(PEER_HEADS):
                c_row = _bcast_rows_bf16(c_ref[hh, pl.ds(i, 1), :], N_KEYS)
                e1_row = _bcast_rows_bf16(e1_ref[hh, pl.ds(i, 1), :], N_KEYS)
                g = g + jnp.where(r2_ref[hh] < c_row, e2_ref[hh] * e1_row, jnp.zeros_like(g))
            act = 0.5 * a * (1.0 + lax.erf(a * (2.0 ** -0.5)))
            w_sc[ii * N_KEYS:(ii + 1) * N_KEYS, :] = g * act.astype(BF16)

    @pl.when(e < last)
    def _():
        h2 = h2_ref[...]
        gate(a0_sc, w0_sc, 2 * e)
        acc_sc[...] += _dot(vtp_ref[...], w1_sc[...])
        a1_sc[...] = _dot(ub_ref[...], h2)
        acc_sc[...] += _dot(vta_ref[...], w0_sc[...])
        a0_sc[...] = _dot(ua_ref[...], h2)
        gate(a1_sc, w1_sc, 2 * e + 1)

    @pl.when(e == last)
    def _():
        acc = acc_sc[...] + _dot(vtp_ref[...], w1_sc[...])
        x2 = x1_ref[...] + g2_ref[...] * acc.T
        y_ref[...] = _rms_rows(x2) * fn_ref[...]


def peer_experts(h2t, u_b, vt_b, e1, e2, r2, cc, x1, g2tok, final_norm, tm, ni=4):
    D, T = h2t.shape
    nb = ni * N_KEYS
    nblk = u_b.shape[0] // nb
    assert nblk % 2 == 0
    tok = pl.BlockSpec((tm, D), lambda t, e: (t, 0))
    gate = pl.BlockSpec((PEER_HEADS, N_KEYS, tm), lambda t, e: (0, 0, t))
    ublk = lambda f: pl.BlockSpec((nb, D), lambda t, e: (f(e), 0))
    vblk = lambda f: pl.BlockSpec((D, nb), lambda t, e: (0, f(e)))
    return pl.pallas_call(
        functools.partial(_peer_kernel, ni=ni),
        grid=(T // tm, nblk // 2 + 1),
        in_specs=[pl.BlockSpec((D, tm), lambda t, e: (0, t)),
                  ublk(lambda e: 0),
                  ublk(lambda e: jnp.minimum(2 * e + 1, nblk - 1)),
                  ublk(lambda e: jnp.minimum(2 * e + 2, nblk - 1)),
                  vblk(lambda e: jnp.maximum(2 * e - 1, 0)),
                  vblk(lambda e: jnp.minimum(2 * e, nblk - 1)),
                  gate, gate, gate, gate, tok, tok,
                  pl.BlockSpec((1, D), lambda t, e: (0, 0))],
        out_specs=tok,
        out_shape=jax.ShapeDtypeStruct((T, D), F32),
        scratch_shapes=[pltpu.VMEM((D, tm), F32),
                        pltpu.VMEM((nb, tm), F32), pltpu.VMEM((nb, tm), F32),
                        pltpu.VMEM((nb, tm), BF16), pltpu.VMEM((nb, tm), BF16)],
        compiler_params=_params(("arbitrary", "arbitrary")),
        name="peer_experts",
    )(h2t, u_b, u_b, u_b, vt_b, vt_b, e1, e2, r2, cc, x1, g2tok, final_norm.reshape(1, D))


def _rot_tables(pos):
    inv = 1.0 / (10000.0 ** jnp.linspace(0.0, 1.0, DKR // 2, dtype=F32))
    ang = pos[:, None].astype(F32) * inv[None, :]
    cos, sin = jnp.cos(ang), jnp.sin(ang)
    return jnp.concatenate([cos, cos], axis=-1), jnp.concatenate([-sin, sin], axis=-1)


def _pick_tile(n, pref):
    t = min(n, pref)
    assert n % t == 0, (n, t)
    return t


def _trunk(x, mods, pos, lam_init, lamv, rel_bias, w, attend, values_transposed, state0, ret_chunk,
           final_norm):
    (norm1, norm2, w_in_b, subln_a, subln_r, w_ba_b, w_br_b, w_o_b, w_pq_b, keys_b, u_b, vt_b) = w
    sh1, sc1, g1, sh2, sc2, g2 = mods
    B, L, D = x.shape
    T = B * L
    cos2, sin2 = _rot_tables(pos)
    tm = _pick_tile(L, 256)
    ka, va, qab, kab, vab, qrb, kr, vrb, gr, ga, gb, *vt = in_proj(
        x, sh1, sc1, norm1, w_in_b, cos2, sin2, tm, emit_vt=values_transposed)
    oa = attend(qab, kab, vt[0] if values_transposed else vab)
    orr, st = retention(qrb, kr, vrb, state0, ret_chunk)
    x1, h2t, qp = out_mix(x, oa, orr, gr, ga, gb, g1, sh2, sc2, norm2, subln_a, subln_r,
                          w_ba_b, w_br_b, w_o_b, w_pq_b, lam_init, tm)
    e1, e2, r2, cc = peer_route(qp.reshape(T, -1), keys_b, _pick_tile(T, 512))
    g2tok = jnp.broadcast_to(g2[:, None, :], (B, L, D)).reshape(T, D)
    y = peer_experts(h2t, u_b, vt_b, e1, e2, r2, cc, x1.reshape(T, D), g2tok,
                     final_norm, _pick_tile(T, 512))
    return y.reshape(B, L, D), ka.reshape(B, L, HA, 2 * DA), va.reshape(B, L, HA, DVA), st


def kernel(x_prompt, x_sample, cache_k, cache_v, state_ret, c_prompt, c_sample, w_ada, b_ada, norm1,
           norm2, w_in, lam_q1, lam_k1, lam_q2, lam_k2, subln_a, subln_r, w_ba, w_br, w_o, rel_bias,
           w_pq, peer_keys, peer_u, peer_v, final_norm):
    depth = w_ada.shape[0]
    assert depth == 1, "the fused final norm assumes a single layer"
    Bp, Lp, D = x_prompt.shape
    Bs, Ls, _ = x_sample.shape
    past = cache_k.shape[2]
    pos_p = jnp.arange(Lp, dtype=jnp.int32)
    pos_s = past + jnp.arange(Ls, dtype=jnp.int32)
    lk = past + Ls
    lk_pad = -(-lk // LANES) * LANES
    k_pos_s = jnp.arange(lk_pad, dtype=jnp.int32)
    k_valid_s = k_pos_s < lk

    l = 0
    lam_init = 0.8 - 0.6 * math.exp(-0.3 * l)
    lamv = jnp.stack([lam_q1[l], lam_k1[l], lam_q2[l], lam_k2[l]]).astype(F32)
    mod = ada_mod(jnp.concatenate([c_prompt, c_sample], axis=0), w_ada[l], b_ada[l])
    mods = jnp.split(mod, 6, axis=-1)
    mods_p = [m[:Bp] for m in mods]
    mods_s = [m[Bp:] for m in mods]
    w = (norm1[l], norm2[l], w_in[l].astype(BF16), subln_a[l], subln_r[l], w_ba[l].astype(BF16),
         w_br[l].astype(BF16), w_o[l].astype(BF16), w_pq[l].astype(BF16),
         peer_keys[l].reshape(PEER_HEADS * 2, N_KEYS, KEY_DIM).astype(BF16),
         peer_u[l].astype(BF16), peer_v[l].T.astype(BF16))

    attend_p = lambda q, k, v: attn_prompt(q, k, v, rel_bias, lamv, lam_init)
    zero_state = jnp.zeros((Bp, HR, DKR, DVR), F32)
    yp, kp, vp, sp = _trunk(x_prompt, mods_p, pos_p, lam_init, lamv, rel_bias, w, attend_p, True,
                            zero_state, _pick_tile(Lp, 256), final_norm)

    def attend_s(q, k, v):
        padk = jnp.zeros((Bs, lk_pad - lk, WA), BF16)
        kc = cache_k[l].reshape(Bs, past, HA * 2 * DA).astype(BF16)
        vc = cache_v[l].reshape(Bs, past, WA).astype(BF16)
        k_all = jnp.concatenate([kc, k, padk], axis=1)
        v_all = jnp.concatenate([vc, v, padk], axis=1)
        return attn_small(q, k_all, v_all, pos_s, k_pos_s, k_valid_s, rel_bias, lamv, lam_init)

    ys, ks, vs, ss = _trunk(x_sample, mods_s, pos_s, lam_init, lamv, rel_bias, w, attend_s, False,
                            state_ret[l].astype(F32), Ls, final_norm)
    return (yp, ys, kp[None], vp[None], sp[None], ks[None], vs[None], ss[None])
```

```python
import functools
import math

import jax
import jax.numpy as jnp
from jax import lax
from jax.experimental import pallas as pl
from jax.experimental.pallas import tpu as pltpu

F32 = jnp.float32
BF16 = jnp.bfloat16

CHUNK = 64
HA = 4
DA = 64
DVA = 2 * DA
HR = 4
DKR = 128
DVR = 128
N_BUCKETS = 32
MAX_DIST = 128
PEER_HEADS = 8
N_KEYS = 128
KEY_DIM = 128
PEER_TOPK = 16
EPS = 1e-6
WA = HA * DVA
WR = HR * DVR
NEG = -1e30
LOG2E = math.log2(math.e)
LANES = 128
VMEM_LIMIT = 56 * 1024 * 1024


def _params(sem, vmem=VMEM_LIMIT):
    return pltpu.CompilerParams(dimension_semantics=sem, vmem_limit_bytes=vmem)


def _dot(a, b):
    return jnp.dot(a, b, preferred_element_type=F32)


def _dot_nt(a, b):
    return lax.dot_general(a, b, (((1,), (1,)), ((), ())), preferred_element_type=F32)


def _dot_tn(a, b):
    return lax.dot_general(a, b, (((0,), (0,)), ((), ())), preferred_element_type=F32)


def _rms_rows(x):
    return x * lax.rsqrt(jnp.mean(x * x, axis=-1, keepdims=True) + EPS)


def _ada_kernel(c_ref, w_ref, b_ref, o_ref):
    c = c_ref[...]
    s = (c * jax.nn.sigmoid(c)).astype(BF16)
    o_ref[...] = _dot(s, w_ref[...].astype(BF16)) + b_ref[...]


def ada_mod(c, w_ada, b_ada, tn=1024):
    nb, d = c.shape
    n = w_ada.shape[1]
    return pl.pallas_call(
        _ada_kernel,
        grid=(n // tn,),
        in_specs=[pl.BlockSpec((nb, d), lambda j: (0, 0)),
                  pl.BlockSpec((d, tn), lambda j: (0, j)),
                  pl.BlockSpec((1, tn), lambda j: (0, j))],
        out_specs=pl.BlockSpec((nb, tn), lambda j: (0, j)),
        out_shape=jax.ShapeDtypeStruct((nb, n), F32),
        compiler_params=_params(("arbitrary",)),
        name="ada_mod",
    )(c, w_ada, b_ada.reshape(1, n))


def _inproj_kernel(x_ref, sh_ref, sc_ref, n1_ref, w_ref, cos_ref, sin_ref,
                   ka_ref, va_ref, qab_ref, kab_ref, vab_ref, qrb_ref, kr_ref, vrb_ref,
                   gr_ref, ga_ref, gb_ref, vt_ref=None):
    x = x_ref[...]
    h = _rms_rows(x) * n1_ref[...] * (1.0 + sc_ref[...]) + sh_ref[...]
    hb = h.astype(BF16)

    def proj(lo, n):
        return _dot(hb, w_ref[:, lo:lo + n])

    qa = proj(0, 512)
    qab_ref[...] = (qa * (DA ** -0.5 * LOG2E)).astype(BF16)
    ka = proj(512, 512)
    ka_ref[...] = ka
    kab_ref[...] = ka.astype(BF16)
    va = proj(1024, 512)
    va_ref[...] = va
    vab_ref[...] = va.astype(BF16)
    if vt_ref is not None:
        vt_ref[...] = va.T.astype(BF16)
    cos2 = cos_ref[...]
    sin2 = sin_ref[...]

    def rot(z):
        parts = []
        for hh in range(HR):
            zh = z[:, hh * DKR:(hh + 1) * DKR]
            parts.append(zh * cos2 + pltpu.roll(zh, DKR // 2, 1) * sin2)
        return jnp.concatenate(parts, axis=-1)

    qrb_ref[...] = rot(proj(1536, 512)).astype(BF16)
    kr_ref[...] = rot(proj(2048, 512)) * (DKR ** -0.5)
    vrb_ref[...] = proj(2560, 512).astype(BF16)
    gr_ref[...] = proj(3072, 512)
    ga_ref[...] = proj(3584, 1024)
    gb_ref[...] = proj(4608, 1024)


def in_proj(x, sh1, sc1, norm1, w_in_b, cos2, sin2, tm, emit_vt):
    B, L, D = x.shape
    d_in = w_in_b.shape[1]
    row = lambda n: pl.BlockSpec((None, tm, n), lambda b, i: (b, i, 0))
    mod = pl.BlockSpec((None, 1, D), lambda b, i: (b, 0, 0))
    f = lambda n, dt: jax.ShapeDtypeStruct((B, L, n), dt)
    vt_spec = [pl.BlockSpec((None, WA, tm), lambda b, i: (b, 0, i))] if emit_vt else []
    vt_shape = [jax.ShapeDtypeStruct((B, WA, L), BF16)] if emit_vt else []
    return pl.pallas_call(
        _inproj_kernel,
        grid=(B, L // tm),
        in_specs=[row(D), mod, mod,
                  pl.BlockSpec((1, D), lambda b, i: (0, 0)),
                  pl.BlockSpec((D, d_in), lambda b, i: (0, 0)),
                  pl.BlockSpec((tm, DKR), lambda b, i: (i, 0)),
                  pl.BlockSpec((tm, DKR), lambda b, i: (i, 0))],
        out_specs=[row(512), row(512), row(512), row(512), row(512), row(512), row(512), row(512),
                   row(512), row(1024), row(1024)] + vt_spec,
        out_shape=[f(512, F32), f(512, F32), f(512, BF16), f(512, BF16), f(512, BF16),
                   f(512, BF16), f(512, F32), f(512, BF16), f(512, F32), f(1024, F32), f(1024, F32)]
        + vt_shape,
        compiler_params=_params(("arbitrary", "arbitrary")),
        name="in_proj",
    )(x, sh1.reshape(B, 1, D), sc1.reshape(B, 1, D), norm1.reshape(1, D), w_in_b, cos2, sin2)


def _t5_bucket(rel):
    nb = N_BUCKETS // 2
    ret = jnp.where(rel > 0, nb, 0)
    n = jnp.abs(rel)
    max_exact = nb // 2
    nf = jnp.maximum(n, max_exact).astype(F32)
    large = max_exact + (jnp.log(nf / max_exact) / math.log(MAX_DIST / max_exact)
                         * (nb - max_exact)).astype(jnp.int32)
    large = jnp.minimum(large, nb - 1)
    return ret + jnp.where(n < max_exact, n, large)


def _bias_from_buckets(bkt, rb_ref, h, shift):
    val = jnp.where(bkt < 0, NEG, 0.0).astype(F32)
    for n in range(N_BUCKETS):
        val = jnp.where(bkt == n, (rb_ref[n, h] - shift) * LOG2E, val)
    return val


def _lam_value(lv_ref, lam_init):
    lv = lv_ref[...]
    a = jnp.sum(lv[0:1] * lv[1:2], axis=-1, keepdims=True)
    b = jnp.sum(lv[2:3] * lv[3:4], axis=-1, keepdims=True)
    return jnp.exp(a) - jnp.exp(b) + lam_init


def _block_diag_q(q):
    lane = lax.broadcasted_iota(jnp.int32, q.shape, 1)
    zero = jnp.zeros_like(q)
    return jnp.concatenate([jnp.where(lane < DA, q, zero), jnp.where(lane >= DA, q, zero)], axis=0)


def _attn_kernel(rb_ref, lv_ref, q_ref, k_ref, vt_ref, bkt_ref, o_ref,
                 bias_sc, m_sc, acc_sc, va_sc, *, tq, tb, nsub, lam_init):
    h = pl.program_id(1)
    i = pl.program_id(2)

    @pl.when(i == 0)
    def _():
        far = rb_ref[N_BUCKETS // 2 - 1, h]
        for t in range(2):
            bias_sc[t] = _bias_from_buckets(bkt_ref[t], rb_ref, h, far)
        va_sc[0:DVA, :] = vt_ref[...]
        va_sc[DVA:, :] = jnp.ones((BF16_ROWS, va_sc.shape[1]), BF16)

    qt = q_ref[...].astype(F32).T
    drow = lax.broadcasted_iota(jnp.int32, qt.shape, 0)
    qbd_t = jnp.concatenate([jnp.where(drow < DA, qt, 0.0), jnp.where(drow >= DA, qt, 0.0)],
                            axis=1).astype(BF16)
    m_sc[...] = jnp.full(m_sc.shape, NEG, F32)
    acc_sc[...] = jnp.zeros(acc_sc.shape, F32)

    def scores(off, tk, bias=None, nvalid=None):
        s = _dot(k_ref[pl.ds(off, tk), :], qbd_t)
        if bias is not None:
            s = s + jnp.concatenate([bias, bias], axis=1)
        if nvalid is not None:
            row = lax.broadcasted_iota(jnp.int32, (tk, 1), 0)
            s = jnp.where(row < nvalid, s, NEG)
        return s

    def absorb(s, off, tk):
        m_prev = m_sc[...]
        m_new = jnp.maximum(m_prev, jnp.max(s, axis=0, keepdims=True))
        alpha = jnp.exp2(m_prev - m_new)
        p = jnp.exp2(s - m_new).astype(BF16)
        acc_sc[...] = alpha * acc_sc[...] + _dot(va_sc[:, pl.ds(off, tk)], p)
        m_sc[...] = m_new

    def tiles(specs):
        ss = [scores(*sp) for sp in specs]
        for s, sp in zip(ss, specs):
            absorb(s, sp[0], sp[1])

    far_end = jnp.maximum(i - 1, 0) * tq
    group = nsub * tb
    half = group // 2
    nfull = far_end // group
    rem = far_end - nfull * group

    def run(off, n, nvalid=None):
        tiles([(pl.multiple_of(off + j * tb, tb), tb, None,
                None if nvalid is None else nvalid - j * tb) for j in range(n)])

    def far_body(t, carry):
        run(pl.multiple_of(t * group, group), nsub)
        return carry

    lax.fori_loop(0, nfull, far_body, 0)
    base = pl.multiple_of(nfull * group, group)

    @pl.when(rem >= half)
    def _():
        run(base, nsub // 2)

    rem2 = jnp.where(rem >= half, rem - half, rem)

    @pl.when(rem2 > 0)
    def _():
        run(pl.multiple_of(base + rem - rem2, half), nsub // 2, nvalid=rem2)

    @pl.when(i >= 1)
    def _():
        off = pl.multiple_of((i - 1) * tq, tq)
        tiles([(off, tq, bias_sc[1]), (pl.multiple_of(off + tq, tq), tq, bias_sc[0])])

    @pl.when(i == 0)
    def _():
        tiles([(0, tq, bias_sc[0])])

    lam = _lam_value(lv_ref, lam_init)
    o = acc_sc[0:DVA, :] / acc_sc[DVA:DVA + 1, :]
    o_ref[...] = (o[:, 0:tq] - lam * o[:, tq:2 * tq]).T


def attn_prompt(qab, kab, vt, rel_bias, lamv, lam_init, tq=256, tb=256, nsub=8):
    B, S, _ = qab.shape
    tb = min(tb, S // (2 * nsub))
    assert S % (nsub * tb) == 0 and (nsub * tb) % tq == 0
    c = jnp.arange(tq, dtype=jnp.int32)[:, None]
    r = jnp.arange(tq, dtype=jnp.int32)[None, :]
    diag = jnp.where((c // CHUNK) <= (r // CHUNK), _t5_bucket(c - r), -1)
    prev = _t5_bucket(c - r - tq)
    bkt = jnp.stack([diag, prev]).astype(jnp.int32)
    smem = pl.BlockSpec(memory_space=pltpu.SMEM)
    return pl.pallas_call(
        functools.partial(_attn_kernel, tq=tq, tb=tb, nsub=nsub, lam_init=lam_init),
        grid=(B, HA, S // tq),
        in_specs=[smem,
                  pl.BlockSpec((4, DA), lambda b, h, i: (0, 0)),
                  pl.BlockSpec((None, tq, DVA), lambda b, h, i: (b, i, h)),
                  pl.BlockSpec((None, S, DVA), lambda b, h, i: (b, 0, h)),
                  pl.BlockSpec((None, DVA, S), lambda b, h, i: (b, h, 0)),
                  pl.BlockSpec((2, tq, tq), lambda b, h, i: (0, 0, 0))],
        out_specs=pl.BlockSpec((None, tq, DVA), lambda b, h, i: (b, i, h)),
        out_shape=jax.ShapeDtypeStruct((B, S, WA), F32),
        scratch_shapes=[pltpu.VMEM((2, tq, tq), F32),
                        pltpu.VMEM((1, 2 * tq), F32),
                        pltpu.VMEM((DVA + BF16_ROWS, 2 * tq), F32),
                        pltpu.VMEM((DVA + BF16_ROWS, S), BF16)],
        compiler_params=_params(("arbitrary", "arbitrary", "arbitrary")),
        name="attn_prompt",
    )(rel_bias, lamv, qab, kab, vt, bkt)


def _attn_small_kernel(rb_ref, lv_ref, q_ref, k_ref, v_ref, bkt_ref, o_ref, *, lq, lam_init):
    h = pl.program_id(1)
    bias = _bias_from_buckets(bkt_ref[...], rb_ref, h, 0.0)
    qbd = _block_diag_q(q_ref[...])
    s = _dot_nt(qbd, k_ref[...]) + jnp.concatenate([bias, bias], axis=0)
    m = jnp.max(s, axis=-1, keepdims=True)
    p = jnp.exp2(s - m)
    l = jnp.sum(p, axis=-1, keepdims=True)
    o = _dot(p.astype(BF16), v_ref[...]) / l
    lam = _lam_value(lv_ref, lam_init)
    o_ref[...] = o[0:lq] - lam * o[lq:2 * lq]


def attn_small(qab, k_all, v_all, q_pos, k_pos, k_valid, rel_bias, lamv, lam_init):
    B, Lq, _ = qab.shape
    Lk = k_all.shape[1]
    visible = ((k_pos[None, :] // CHUNK) <= (q_pos[:, None] // CHUNK)) & k_valid[None, :]
    bkt = jnp.where(visible, _t5_bucket(k_pos[None, :] - q_pos[:, None]), -1).astype(jnp.int32)
    smem = pl.BlockSpec(memory_space=pltpu.SMEM)
    return pl.pallas_call(
        functools.partial(_attn_small_kernel, lq=Lq, lam_init=lam_init),
        grid=(B, HA),
        in_specs=[smem,
                  pl.BlockSpec((4, DA), lambda b, h: (0, 0)),
                  pl.BlockSpec((None, Lq, DVA), lambda b, h: (b, 0, h)),
                  pl.BlockSpec((None, Lk, DVA), lambda b, h: (b, 0, h)),
                  pl.BlockSpec((None, Lk, DVA), lambda b, h: (b, 0, h)),
                  pl.BlockSpec((Lq, Lk), lambda b, h: (0, 0))],
        out_specs=pl.BlockSpec((None, Lq, DVA), lambda b, h: (b, 0, h)),
        out_shape=jax.ShapeDtypeStruct((B, Lq, WA), F32),
        compiler_params=_params(("arbitrary", "arbitrary")),
        name="attn_sample",
    )(rel_bias, lamv, qab, k_all, v_all, bkt)


def _ret_kernel(lg_ref, q_ref, k_ref, v_ref, s0_ref, o_ref, so_ref, state_sc, decay_sc, *, C):
    h = pl.program_id(1)
    c = pl.program_id(2)
    lg = lg_ref[h]

    @pl.when(c == 0)
    def _():
        state_sc[...] = s0_ref[...]
        r = lax.broadcasted_iota(jnp.int32, (C, C), 0)
        cc = lax.broadcasted_iota(jnp.int32, (C, C), 1)
        diff = (r - cc).astype(F32)
        decay_sc[...] = jnp.where(diff >= 0, jnp.exp(jnp.maximum(diff, 0.0) * lg), 0.0)

    n = lax.broadcasted_iota(jnp.int32, (C, 1), 0).astype(F32)
    xi = jnp.exp((n + 1.0) * lg)
    zeta = jnp.exp((C - 1.0 - n) * lg)
    q = q_ref[...]
    k = k_ref[...]
    v = v_ref[...]
    state = state_sc[...]
    scores = _dot_nt(q, k.astype(BF16)) * decay_sc[...]
    intra = _dot(scores.astype(BF16), v)
    cross = _dot(q, state.astype(BF16)) * xi
    o_ref[...] = intra + cross
    kz = (k * zeta).astype(BF16)
    state_sc[...] = jnp.exp(C * lg) * state + _dot_tn(kz, v)

    @pl.when(c == pl.num_programs(2) - 1)
    def _():
        so_ref[...] = state_sc[...]


def retention(qrb, kr, vrb, state0, C):
    B, L, _ = qrb.shape
    lg = jnp.log(1.0 - 2.0 ** (-5.0 - jnp.arange(HR, dtype=F32)))
    blk = pl.BlockSpec((None, C, DKR), lambda b, h, c: (b, c, h))
    st = pl.BlockSpec((None, None, DKR, DVR), lambda b, h, c: (b, h, 0, 0))
    return pl.pallas_call(
        functools.partial(_ret_kernel, C=C),
        grid=(B, HR, L // C),
        in_specs=[pl.BlockSpec(memory_space=pltpu.SMEM), blk, blk, blk, st],
        out_specs=[blk, st],
        out_shape=[jax.ShapeDtypeStruct((B, L, WR), F32),
                   jax.ShapeDtypeStruct((B, HR, DKR, DVR), F32)],
        scratch_shapes=[pltpu.VMEM((DKR, DVR), F32), pltpu.VMEM((C, C), F32)],
        compiler_params=_params(("arbitrary", "arbitrary", "arbitrary")),
        name="retention",
    )(lg, qrb, kr, vrb, state0)


def _outmix_kernel(x_ref, oa_ref, or_ref, gr_ref, ga_ref, gb_ref, g1_ref, sh2_ref, sc2_ref, n2_ref,
                   sa_ref, sr_ref, wba_ref, wbr_ref, wo_ref, wpq_ref,
                   x1_ref, h2_ref, qp_ref, *, lam_init, h2_transposed):
    sa = sa_ref[...] * 1.0
    sr = sr_ref[...]
    gr = gr_ref[...]
    silu_gr = gr * jax.nn.sigmoid(gr)
    ya_parts, yr_parts = [], []
    for hh in range(HA):
        sl = slice(hh * DVA, (hh + 1) * DVA)
        ya_parts.append(_rms_rows(oa_ref[:, sl]) * sa * (1.0 - lam_init))
        yr_parts.append(silu_gr[:, sl] * (_rms_rows(or_ref[:, sl]) * sr))
    ya = jnp.concatenate(ya_parts, axis=-1).astype(BF16)
    yr = jnp.concatenate(yr_parts, axis=-1).astype(BF16)
    y = (jax.nn.sigmoid(ga_ref[...]) * _dot(ya, wba_ref[...])
         + jax.nn.sigmoid(gb_ref[...]) * _dot(yr, wbr_ref[...]))
    out = _dot(y.astype(BF16), wo_ref[...])
    x1 = x_ref[...] + g1_ref[...] * out
    x1_ref[...] = x1
    h2f = _rms_rows(x1) * n2_ref[...] * (1.0 + sc2_ref[...]) + sh2_ref[...]
    h2 = h2f.astype(BF16)
    h2_ref[...] = h2f.T.astype(BF16) if h2_transposed else h2
    qp_ref[...] = _dot(h2, wpq_ref[...]).astype(BF16)


def out_mix(x, oa, orr, gr, ga, gb, g1, sh2, sc2, norm2, subln_a, subln_r,
            w_ba_b, w_br_b, w_o_b, w_pq_b, lam_init, tm):
    B, L, D = x.shape
    nq = w_pq_b.shape[1]
    row = lambda n: pl.BlockSpec((None, tm, n), lambda b, i: (b, i, 0))
    mod = pl.BlockSpec((None, 1, D), lambda b, i: (b, 0, 0))
    full = lambda a: pl.BlockSpec(a.shape, lambda b, i: (0,) * a.ndim)
    n2 = norm2.reshape(1, D)
    sa = subln_a.reshape(1, DVA)
    sr = subln_r.reshape(1, DVR)
    h2_transposed = tm % LANES == 0
    nt = L // tm
    if h2_transposed:
        h2_spec = pl.BlockSpec((D, tm), lambda b, i: (0, b * nt + i))
        h2_shape = jax.ShapeDtypeStruct((D, B * L), BF16)
    else:
        h2_spec = row(D)
        h2_shape = jax.ShapeDtypeStruct((B, L, D), BF16)
    x1, h2, qp = pl.pallas_call(
        functools.partial(_outmix_kernel, lam_init=lam_init, h2_transposed=h2_transposed),
        grid=(B, nt),
        in_specs=[row(D), row(WA), row(WR), row(WR), row(D), row(D), mod, mod, mod,
                  full(n2), full(sa), full(sr), full(w_ba_b), full(w_br_b), full(w_o_b), full(w_pq_b)],
        out_specs=[row(D), h2_spec, row(nq)],
        out_shape=[jax.ShapeDtypeStruct((B, L, D), F32), h2_shape,
                   jax.ShapeDtypeStruct((B, L, nq), BF16)],
        compiler_params=_params(("arbitrary", "arbitrary")),
        name="out_mix",
    )(x, oa, orr, gr, ga, gb, g1.reshape(B, 1, D), sh2.reshape(B, 1, D), sc2.reshape(B, 1, D),
      n2, sa, sr, w_ba_b, w_br_b, w_o_b, w_pq_b)
    h2t = h2 if h2_transposed else h2.reshape(B * L, D).T
    return x1, h2t, qp


def _topk_rows(s, k):
    n = s.shape[0]
    iota = lax.broadcasted_iota(jnp.int32, s.shape, 0).astype(F32)
    work = s
    rank = jnp.full(s.shape, float(PEER_TOPK * PEER_TOPK), F32)
    vals = []
    for r in range(k):
        m = jnp.max(work, axis=0, keepdims=True)
        idx = jnp.min(jnp.where(work == m, iota, float(n)), axis=0, keepdims=True)
        sel = iota == idx
        rank = jnp.where(sel, float(r), rank)
        work = jnp.where(sel, -jnp.inf, work)
        vals.append(m)
    return vals, rank


SUBLANES = 8
_HEAD_A = SUBLANES
_CAND_NB = [PEER_TOPK] + [SUBLANES] * (_HEAD_A - 1)


def _route_chunk(s1, s2):
    v1, rank1 = _topk_rows(s1, PEER_TOPK)
    v2, rank2 = _topk_rows(s2, PEER_TOPK)
    v2m = jnp.concatenate(v2, axis=0)
    v1t = jnp.concatenate(v1[_HEAD_A:], axis=0)
    blocks = [v1[a] + v2m[0:nb] for a, nb in enumerate(_CAND_NB)] + [v1t + v2[0]]
    cand = jnp.concatenate(blocks, axis=0)
    _, crank = _topk_rows(cand, PEER_TOPK)
    sel = crank < float(PEER_TOPK)
    cmax = v1[0] + v2[0]
    z = jnp.sum(jnp.where(sel, jnp.exp(cand - cmax), 0.0), axis=0, keepdims=True)
    self32 = sel.astype(F32)
    cidx = jnp.zeros(s1.shape, F32)
    lo = 0
    for a, nb in enumerate(_CAND_NB):
        cnt = jnp.sum(self32[lo:lo + nb], axis=0, keepdims=True)
        cidx = jnp.where(rank1 == float(a), cnt, cidx)
        lo += nb
    for a in range(_HEAD_A, PEER_TOPK):
        cidx = jnp.where(rank1 == float(a), self32[lo + a - _HEAD_A:lo + a - _HEAD_A + 1], cidx)
    e1 = jnp.exp(s1 - v1[0]) / z
    e2 = jnp.exp(s2 - v2[0])
    return e1, e2, rank2, cidx


def _route_kernel(q_ref, keys_ref, e1_ref, e2_ref, r2_ref, c_ref, s1_sc, s2_sc):
    q = q_ref[...]
    s1_sc[...] = _dot_nt(keys_ref[0], q[:, 0:KEY_DIM])
    s2_sc[...] = _dot_nt(keys_ref[1], q[:, KEY_DIM:2 * KEY_DIM])

    def chunk(ci, carry):
        sl = pl.ds(pl.multiple_of(ci * LANES, LANES), LANES)
        e1, e2, rank2, cidx = _route_chunk(s1_sc[:, sl], s2_sc[:, sl])
        e1_ref[:, sl] = e1
        e2_ref[:, sl] = e2.astype(BF16)
        r2_ref[:, sl] = rank2.astype(BF16)
        c_ref[:, sl] = cidx
        return carry

    lax.fori_loop(0, s1_sc.shape[1] // LANES, chunk, 0, unroll=2)


def peer_route(qp, keys_b, tm):
    T = qp.shape[0]
    out = pl.BlockSpec((None, N_KEYS, tm), lambda t, h: (h, 0, t))
    shp = lambda dt: jax.ShapeDtypeStruct((PEER_HEADS, N_KEYS, T), dt)
    return pl.pallas_call(
        _route_kernel,
        grid=(T // tm, PEER_HEADS),
        in_specs=[pl.BlockSpec((tm, 2 * KEY_DIM), lambda t, h: (t, h)),
                  pl.BlockSpec((2, N_KEYS, KEY_DIM), lambda t, h: (h, 0, 0))],
        out_specs=[out, out, out, out],
        out_shape=[shp(F32), shp(BF16), shp(BF16), shp(F32)],
        scratch_shapes=[pltpu.VMEM((N_KEYS, tm), F32), pltpu.VMEM((N_KEYS, tm), F32)],
        compiler_params=_params(("arbitrary", "arbitrary")),
        name="peer_route",
    )(qp, keys_b)


BF16_ROWS = 16


def _bcast_rows_bf16(row, n):
    t = row.shape[1]
    tile = jnp.broadcast_to(row, (BF16_ROWS, t)).astype(BF16)
    return jnp.broadcast_to(tile[None], (n // BF16_ROWS, BF16_ROWS, t)).reshape(n, t)


def _peer_kernel(h2_ref, u0_ref, ub_ref, ua_ref, vtp_ref, vta_ref, e1_ref, e2_ref, r2_ref, c_ref,
                 x1_ref, g2_ref, fn_ref, y_ref, acc_sc, a0_sc, a1_sc, w0_sc, w1_sc, *, ni):
    e = pl.program_id(1)
    last = pl.num_programs(1) - 1

    @pl.when(e == 0)
    def _():
        acc_sc[...] = jnp.zeros(acc_sc.shape, F32)
        a0_sc[...] = _dot(u0_ref[...], h2_ref[...])
        w1_sc[...] = jnp.zeros(w1_sc.shape, BF16)

    def gate(a_sc, w_sc, blk):
        for ii in range(ni):
            i = blk * ni + ii
            a = a_sc[ii * N_KEYS:(ii + 1) * N_KEYS, :]
            g = jnp.zeros(a.shape, BF16)
            for hh in range(PEER_HEADS):
                c_row = _bcast_rows_bf16(c_ref[hh, pl.ds(i, 1), :], N_KEYS)
                e1_row = _bcast_rows_bf16(e1_ref[hh, pl.ds(i, 1), :], N_KEYS)
                g = g + jnp.where(r2_ref[hh] < c_row, e2_ref[hh] * e1_row, jnp.zeros_like(g))
            act = 0.5 * a * (1.0 + lax.erf(a * (2.0 ** -0.5)))
            w_sc[ii * N_KEYS:(ii + 1) * N_KEYS, :] = g * act.astype(BF16)

    @pl.when(e < last)
    def _():
        h2 = h2_ref[...]
        gate(a0_sc, w0_sc, 2 * e)
        acc_sc[...] += _dot(vtp_ref[...], w1_sc[...])
        a1_sc[...] = _dot(ub_ref[...], h2)
        acc_sc[...] += _dot(vta_ref[...], w0_sc[...])
        a0_sc[...] = _dot(ua_ref[...], h2)
        gate(a1_sc, w1_sc, 2 * e + 1)

    @pl.when(e == last)
    def _():
        acc = acc_sc[...] + _dot(vtp_ref[...], w1_sc[...])
        x2 = x1_ref[...] + g2_ref[...] * acc.T
        y_ref[...] = _rms_rows(x2) * fn_ref[...]


def peer_experts(h2t, u_b, vt_b, e1, e2, r2, cc, x1, g2tok, final_norm, tm, ni=4):
    D, T = h2t.shape
    nb = ni * N_KEYS
    nblk = u_b.shape[0] // nb
    assert nblk % 2 == 0
    tok = pl.BlockSpec((tm, D), lambda t, e: (t, 0))
    gate = pl.BlockSpec((PEER_HEADS, N_KEYS, tm), lambda t, e: (0, 0, t))
    ublk = lambda f: pl.BlockSpec((nb, D), lambda t, e: (f(e), 0))
    vblk = lambda f: pl.BlockSpec((D, nb), lambda t, e: (0, f(e)))
    return pl.pallas_call(
        functools.partial(_peer_kernel, ni=ni),
        grid=(T // tm, nblk // 2 + 1),
        in_specs=[pl.BlockSpec((D, tm), lambda t, e: (0, t)),
                  ublk(lambda e: 0),
                  ublk(lambda e: jnp.minimum(2 * e + 1, nblk - 1)),
                  ublk(lambda e: jnp.minimum(2 * e + 2, nblk - 1)),
                  vblk(lambda e: jnp.maximum(2 * e - 1, 0)),
                  vblk(lambda e: jnp.minimum(2 * e, nblk - 1)),
                  gate, gate, gate, gate, tok, tok,
                  pl.BlockSpec((1, D), lambda t, e: (0, 0))],
        out_specs=tok,
        out_shape=jax.ShapeDtypeStruct((T, D), F32),
        scratch_shapes=[pltpu.VMEM((D, tm), F32),
                        pltpu.VMEM((nb, tm), F32), pltpu.VMEM((nb, tm), F32),
                        pltpu.VMEM((nb, tm), BF16), pltpu.VMEM((nb, tm), BF16)],
        compiler_params=_params(("arbitrary", "arbitrary")),
        name="peer_experts",
    )(h2t, u_b, u_b, u_b, vt_b, vt_b, e1, e2, r2, cc, x1, g2tok, final_norm.reshape(1, D))


def _rot_tables(pos):
    inv = 1.0 / (10000.0 ** jnp.linspace(0.0, 1.0, DKR // 2, dtype=F32))
    ang = pos[:, None].astype(F32) * inv[None, :]
    cos, sin = jnp.cos(ang), jnp.sin(ang)
    return jnp.concatenate([cos, cos], axis=-1), jnp.concatenate([-sin, sin], axis=-1)


def _pick_tile(n, pref):
    t = min(n, pref)
    assert n % t == 0, (n, t)
    return t


def _trunk(x, mods, pos, lam_init, lamv, rel_bias, w, attend, values_transposed, state0, ret_chunk,
           final_norm):
    (norm1, norm2, w_in_b, subln_a, subln_r, w_ba_b, w_br_b, w_o_b, w_pq_b, keys_b, u_b, vt_b) = w
    sh1, sc1, g1, sh2, sc2, g2 = mods
    B, L, D = x.shape
    T = B * L
    cos2, sin2 = _rot_tables(pos)
    tm = _pick_tile(L, 256)
    ka, va, qab, kab, vab, qrb, kr, vrb, gr, ga, gb, *vt = in_proj(
        x, sh1, sc1, norm1, w_in_b, cos2, sin2, tm, emit_vt=values_transposed)
    oa = attend(qab, kab, vt[0] if values_transposed else vab)
    orr, st = retention(qrb, kr, vrb, state0, ret_chunk)
    x1, h2t, qp = out_mix(x, oa, orr, gr, ga, gb, g1, sh2, sc2, norm2, subln_a, subln_r,
                          w_ba_b, w_br_b, w_o_b, w_pq_b, lam_init, tm)
    e1, e2, r2, cc = peer_route(qp.reshape(T, -1), keys_b, _pick_tile(T, 512))
    g2tok = jnp.broadcast_to(g2[:, None, :], (B, L, D)).reshape(T, D)
    y = peer_experts(h2t, u_b, vt_b, e1, e2, r2, cc, x1.reshape(T, D), g2tok,
                     final_norm, _pick_tile(T, 512))
    return y.reshape(B, L, D), ka.reshape(B, L, HA, 2 * DA), va.reshape(B, L, HA, DVA), st


def kernel(x_prompt, x_sample, cache_k, cache_v, state_ret, c_prompt, c_sample, w_ada, b_ada, norm1,
           norm2, w_in, lam_q1, lam_k1, lam_q2, lam_k2, subln_a, subln_r, w_ba, w_br, w_o, rel_bias,
           w_pq, peer_keys, peer_u, peer_v, final_norm):
    depth = w_ada.shape[0]
    assert depth == 1, "the fused final norm assumes a single layer"
    Bp, Lp, D = x_prompt.shape
    Bs, Ls, _ = x_sample.shape
    past = cache_k.shape[2]
    pos_p = jnp.arange(Lp, dtype=jnp.int32)
    pos_s = past + jnp.arange(Ls, dtype=jnp.int32)
    lk = past + Ls
    lk_pad = -(-lk // LANES) * LANES
    k_pos_s = jnp.arange(lk_pad, dtype=jnp.int32)
    k_valid_s = k_pos_s < lk

    l = 0
    lam_init = 0.8 - 0.6 * math.exp(-0.3 * l)
    lamv = jnp.stack([lam_q1[l], lam_k1[l], lam_q2[l], lam_k2[l]]).astype(F32)
    mod = ada_mod(jnp.concatenate([c_prompt, c_sample], axis=0), w_ada[l], b_ada[l])
    mods = jnp.split(mod, 6, axis=-1)
    mods_p = [m[:Bp] for m in mods]
    mods_s = [m[Bp:] for m in mods]
    w = (norm1[l], norm2[l], w_in[l].astype(BF16), subln_a[l], subln_r[l], w_ba[l].astype(BF16),
         w_br[l].astype(BF16), w_o[l].astype(BF16), w_pq[l].astype(BF16),
         peer_keys[l].reshape(PEER_HEADS * 2, N_KEYS, KEY_DIM).astype(BF16),
         peer_u[l].astype(BF16), peer_v[l].T.astype(BF16))

    attend_p = lambda q, k, v: attn_prompt(q, k, v, rel_bias, lamv, lam_init)
    zero_state = jnp.zeros((Bp, HR, DKR, DVR), F32)
    yp, kp, vp, sp = _trunk(x_prompt, mods_p, pos_p, lam_init, lamv, rel_bias, w, attend_p, True,
                            zero_state, _pick_tile(Lp, 256), final_norm)

    def attend_s(q, k, v):
        padk = jnp.zeros((Bs, lk_pad - lk, WA), BF16)
        kc = cache_k[l].reshape(Bs, past, HA * 2 * DA).astype(BF16)
        vc = cache_v[l].reshape(Bs, past, WA).astype(BF16)
        k_all = jnp.concatenate([kc, k, padk], axis=1)
        v_all = jnp.concatenate([vc, v, padk], axis=1)
        return attn_small(q, k_all, v_all, pos_s, k_pos_s, k_valid_s, rel_bias, lamv, lam_init)

    ys, ks, vs, ss = _trunk(x_sample, mods_s, pos_s, lam_init, lamv, rel_bias, w, attend_s, False,
                            state_ret[l].astype(F32), Ls, final_norm)
    return (yp, ys, kp[None], vp[None], sp[None], ks[None], vs[None], ss[None])
```

```python
import functools
import math

import jax
import jax.numpy as jnp
from jax import lax
from jax.experimental import pallas as pl
from jax.experimental.pallas import tpu as pltpu

F32 = jnp.float32
BF16 = jnp.bfloat16

CHUNK = 64
HA = 4
DA = 64
DVA = 2 * DA
HR = 4
DKR = 128
DVR = 128
N_BUCKETS = 32
MAX_DIST = 128
PEER_HEADS = 8
N_KEYS = 128
KEY_DIM = 128
PEER_TOPK = 16
EPS = 1e-6
WA = HA * DVA
WR = HR * DVR
NEG = -1e30
LOG2E = math.log2(math.e)
LANES = 128
VMEM_LIMIT = 56 * 1024 * 1024


def _params(sem, vmem=VMEM_LIMIT):
    return pltpu.CompilerParams(dimension_semantics=sem, vmem_limit_bytes=vmem)


def _dot(a, b):
    return jnp.dot(a, b, preferred_element_type=F32)


def _dot_nt(a, b):
    return lax.dot_general(a, b, (((1,), (1,)), ((), ())), preferred_element_type=F32)


def _dot_tn(a, b):
    return lax.dot_general(a, b, (((0,), (0,)), ((), ())), preferred_element_type=F32)


def _rms_rows(x):
    return x * lax.rsqrt(jnp.mean(x * x, axis=-1, keepdims=True) + EPS)


def _ada_kernel(c_ref, w_ref, b_ref, o_ref):
    c = c_ref[...]
    s = (c * jax.nn.sigmoid(c)).astype(BF16)
    o_ref[...] = _dot(s, w_ref[...].astype(BF16)) + b_ref[...]


def ada_mod(c, w_ada, b_ada, tn=1024):
    nb, d = c.shape
    n = w_ada.shape[1]
    return pl.pallas_call(
        _ada_kernel,
        grid=(n // tn,),
        in_specs=[pl.BlockSpec((nb, d), lambda j: (0, 0)),
                  pl.BlockSpec((d, tn), lambda j: (0, j)),
                  pl.BlockSpec((1, tn), lambda j: (0, j))],
        out_specs=pl.BlockSpec((nb, tn), lambda j: (0, j)),
        out_shape=jax.ShapeDtypeStruct((nb, n), F32),
        compiler_params=_params(("arbitrary",)),
        name="ada_mod",
    )(c, w_ada, b_ada.reshape(1, n))


def _inproj_kernel(x_ref, sh_ref, sc_ref, n1_ref, w_ref, cos_ref, sin_ref,
                   ka_ref, va_ref, qab_ref, kab_ref, vab_ref, qrb_ref, kr_ref, vrb_ref,
                   gr_ref, ga_ref, gb_ref, vt_ref=None):
    x = x_ref[...]
    h = _rms_rows(x) * n1_ref[...] * (1.0 + sc_ref[...]) + sh_ref[...]
    hb = h.astype(BF16)

    def proj(lo, n):
        return _dot(hb, w_ref[:, lo:lo + n])

    qa = proj(0, 512)
    qab_ref[...] = (qa * (DA ** -0.5 * LOG2E)).astype(BF16)
    ka = proj(512, 512)
    ka_ref[...] = ka
    kab_ref[...] = ka.astype(BF16)
    va = proj(1024, 512)
    va_ref[...] = va
    vab_ref[...] = va.astype(BF16)
    if vt_ref is not None:
        vt_ref[...] = va.T.astype(BF16)
    cos2 = cos_ref[...]
    sin2 = sin_ref[...]

    def rot(z):
        parts = []
        for hh in range(HR):
            zh = z[:, hh * DKR:(hh + 1) * DKR]
            parts.append(zh * cos2 + pltpu.roll(zh, DKR // 2, 1) * sin2)
        return jnp.concatenate(parts, axis=-1)

    qrb_ref[...] = rot(proj(1536, 512)).astype(BF16)
    kr_ref[...] = rot(proj(2048, 512)) * (DKR ** -0.5)
    vrb_ref[...] = proj(2560, 512).astype(BF16)
    gr_ref[...] = proj(3072, 512)
    ga_ref[...] = proj(3584, 1024)
    gb_ref[...] = proj(4608, 1024)


def in_proj(x, sh1, sc1, norm1, w_in_b, cos2, sin2, tm, emit_vt):
    B, L, D = x.shape
    d_in = w_in_b.shape[1]
    row = lambda n: pl.BlockSpec((None, tm, n), lambda b, i: (b, i, 0))
    mod = pl.BlockSpec((None, 1, D), lambda b, i: (b, 0, 0))
    f = lambda n, dt: jax.ShapeDtypeStruct((B, L, n), dt)
    vt_spec = [pl.BlockSpec((None, WA, tm), lambda b, i: (b, 0, i))] if emit_vt else []
    vt_shape = [jax.ShapeDtypeStruct((B, WA, L), BF16)] if emit_vt else []
    return pl.pallas_call(
        _inproj_kernel,
        grid=(B, L // tm),
        in_specs=[row(D), mod, mod,
                  pl.BlockSpec((1, D), lambda b, i: (0, 0)),
                  pl.BlockSpec((D, d_in), lambda b, i: (0, 0)),
                  pl.BlockSpec((tm, DKR), lambda b, i: (i, 0)),
                  pl.BlockSpec((tm, DKR), lambda b, i: (i, 0))],
        out_specs=[row(512), row(512), row(512), row(512), row(512), row(512), row(512), row(512),
                   row(512), row(1024), row(1024)] + vt_spec,
        out_shape=[f(512, F32), f(512, F32), f(512, BF16), f(512, BF16), f(512, BF16),
                   f(512, BF16), f(512, F32), f(512, BF16), f(512, F32), f(1024, F32), f(1024, F32)]
        + vt_shape,
        compiler_params=_params(("arbitrary", "arbitrary")),
        name="in_proj",
    )(x, sh1.reshape(B, 1, D), sc1.reshape(B, 1, D), norm1.reshape(1, D), w_in_b, cos2, sin2)


def _t5_bucket(rel):
    nb = N_BUCKETS // 2
    ret = jnp.where(rel > 0, nb, 0)
    n = jnp.abs(rel)
    max_exact = nb // 2
    nf = jnp.maximum(n, max_exact).astype(F32)
    large = max_exact + (jnp.log(nf / max_exact) / math.log(MAX_DIST / max_exact)
                         * (nb - max_exact)).astype(jnp.int32)
    large = jnp.minimum(large, nb - 1)
    return ret + jnp.where(n < max_exact, n, large)


def _bias_from_buckets(bkt, rb_ref, h, shift):
    val = jnp.where(bkt < 0, NEG, 0.0).astype(F32)
    for n in range(N_BUCKETS):
        val = jnp.where(bkt == n, (rb_ref[n, h] - shift) * LOG2E, val)
    return val


def _lam_value(lv_ref, lam_init):
    lv = lv_ref[...]
    a = jnp.sum(lv[0:1] * lv[1:2], axis=-1, keepdims=True)
    b = jnp.sum(lv[2:3] * lv[3:4], axis=-1, keepdims=True)
    return jnp.exp(a) - jnp.exp(b) + lam_init


def _block_diag_q(q):
    lane = lax.broadcasted_iota(jnp.int32, q.shape, 1)
    zero = jnp.zeros_like(q)
    return jnp.concatenate([jnp.where(lane < DA, q, zero), jnp.where(lane >= DA, q, zero)], axis=0)


def _attn_kernel(rb_ref, lv_ref, q_ref, k_ref, vt_ref, bkt_ref, o_ref,
                 bias_sc, m_sc, acc_sc, va_sc, *, tq, tb, nsub, lam_init):
    h = pl.program_id(1)
    i = pl.program_id(2)

    @pl.when(i == 0)
    def _():
        far = rb_ref[N_BUCKETS // 2 - 1, h]
        for t in range(2):
            bias_sc[t] = _bias_from_buckets(bkt_ref[t], rb_ref, h, far)
        va_sc[0:DVA, :] = vt_ref[...]
        va_sc[DVA:, :] = jnp.ones((BF16_ROWS, va_sc.shape[1]), BF16)

    qt = q_ref[...].astype(F32).T
    drow = lax.broadcasted_iota(jnp.int32, qt.shape, 0)
    qbd_t = jnp.concatenate([jnp.where(drow < DA, qt, 0.0), jnp.where(drow >= DA, qt, 0.0)],
                            axis=1).astype(BF16)
    m_sc[...] = jnp.full(m_sc.shape, NEG, F32)
    acc_sc[...] = jnp.zeros(acc_sc.shape, F32)

    def scores(off, tk, bias=None, nvalid=None):
        s = _dot(k_ref[pl.ds(off, tk), :], qbd_t)
        if bias is not None:
            s = s + jnp.concatenate([bias, bias], axis=1)
        if nvalid is not None:
            row = lax.broadcasted_iota(jnp.int32, (tk, 1), 0)
            s = jnp.where(row < nvalid, s, NEG)
        return s

    def absorb(s, off, tk):
        m_prev = m_sc[...]
        m_new = jnp.maximum(m_prev, jnp.max(s, axis=0, keepdims=True))
        alpha = jnp.exp2(m_prev - m_new)
        p = jnp.exp2(s - m_new).astype(BF16)
        acc_sc[...] = alpha * acc_sc[...] + _dot(va_sc[:, pl.ds(off, tk)], p)
        m_sc[...] = m_new

    def tiles(specs):
        ss = [scores(*sp) for sp in specs]
        for s, sp in zip(ss, specs):
            absorb(s, sp[0], sp[1])

    far_end = jnp.maximum(i - 1, 0) * tq
    group = nsub * tb
    half = group // 2
    nfull = far_end // group
    rem = far_end - nfull * group

    def run(off, n, nvalid=None):
        tiles([(pl.multiple_of(off + j * tb, tb), tb, None,
                None if nvalid is None else nvalid - j * tb) for j in range(n)])

    def far_body(t, carry):
        run(pl.multiple_of(t * group, group), nsub)
        return carry

    lax.fori_loop(0, nfull, far_body, 0)
    base = pl.multiple_of(nfull * group, group)

    @pl.when(rem >= half)
    def _():
        run(base, nsub // 2)

    rem2 = jnp.where(rem >= half, rem - half, rem)

    @pl.when(rem2 > 0)
    def _():
        run(pl.multiple_of(base + rem - rem2, half), nsub // 2, nvalid=rem2)

    @pl.when(i >= 1)
    def _():
        off = pl.multiple_of((i - 1) * tq, tq)
        tiles([(off, tq, bias_sc[1]), (pl.multiple_of(off + tq, tq), tq, bias_sc[0])])

    @pl.when(i == 0)
    def _():
        tiles([(0, tq, bias_sc[0])])

    lam = _lam_value(lv_ref, lam_init)
    o = acc_sc[0:DVA, :] / acc_sc[DVA:DVA + 1, :]
    o_ref[...] = (o[:, 0:tq] - lam * o[:, tq:2 * tq]).T


def attn_prompt(qab, kab, vt, rel_bias, lamv, lam_init, tq=256, tb=256, nsub=8):
    B, S, _ = qab.shape
    tb = min(tb, S // (2 * nsub))
    assert S % (nsub * tb) == 0 and (nsub * tb) % tq == 0
    c = jnp.arange(tq, dtype=jnp.int32)[:, None]
    r = jnp.arange(tq, dtype=jnp.int32)[None, :]
    diag = jnp.where((c // CHUNK) <= (r // CHUNK), _t5_bucket(c - r), -1)
    prev = _t5_bucket(c - r - tq)
    bkt = jnp.stack([diag, prev]).astype(jnp.int32)
    smem = pl.BlockSpec(memory_space=pltpu.SMEM)
    return pl.pallas_call(
        functools.partial(_attn_kernel, tq=tq, tb=tb, nsub=nsub, lam_init=lam_init),
        grid=(B, HA, S // tq),
        in_specs=[smem,
                  pl.BlockSpec((4, DA), lambda b, h, i: (0, 0)),
                  pl.BlockSpec((None, tq, DVA), lambda b, h, i: (b, i, h)),
                  pl.BlockSpec((None, S, DVA), lambda b, h, i: (b, 0, h)),
                  pl.BlockSpec((None, DVA, S), lambda b, h, i: (b, h, 0)),
                  pl.BlockSpec((2, tq, tq), lambda b, h, i: (0, 0, 0))],
        out_specs=pl.BlockSpec((None, tq, DVA), lambda b, h, i: (b, i, h)),
        out_shape=jax.ShapeDtypeStruct((B, S, WA), F32),
        scratch_shapes=[pltpu.VMEM((2, tq, tq), F32),
                        pltpu.VMEM((1, 2 * tq), F32),
                        pltpu.VMEM((DVA + BF16_ROWS, 2 * tq), F32),
                        pltpu.VMEM((DVA + BF16_ROWS, S), BF16)],
        compiler_params=_params(("arbitrary", "arbitrary", "arbitrary")),
        name="attn_prompt",
    )(rel_bias, lamv, qab, kab, vt, bkt)


def _attn_small_kernel(rb_ref, lv_ref, q_ref, k_ref, v_ref, bkt_ref, o_ref, *, lq, lam_init):
    h = pl.program_id(1)
    bias = _bias_from_buckets(bkt_ref[...], rb_ref, h, 0.0)
    qbd = _block_diag_q(q_ref[...])
    s = _dot_nt(qbd, k_ref[...]) + jnp.concatenate([bias, bias], axis=0)
    m = jnp.max(s, axis=-1, keepdims=True)
    p = jnp.exp2(s - m)
    l = jnp.sum(p, axis=-1, keepdims=True)
    o = _dot(p.astype(BF16), v_ref[...]) / l
    lam = _lam_value(lv_ref, lam_init)
    o_ref[...] = o[0:lq] - lam * o[lq:2 * lq]


def attn_small(qab, k_all, v_all, q_pos, k_pos, k_valid, rel_bias, lamv, lam_init):
    B, Lq, _ = qab.shape
    Lk = k_all.shape[1]
    visible = ((k_pos[None, :] // CHUNK) <= (q_pos[:, None] // CHUNK)) & k_valid[None, :]
    bkt = jnp.where(visible, _t5_bucket(k_pos[None, :] - q_pos[:, None]), -1).astype(jnp.int32)
    smem = pl.BlockSpec(memory_space=pltpu.SMEM)
    return pl.pallas_call(
        functools.partial(_attn_small_kernel, lq=Lq, lam_init=lam_init),
        grid=(B, HA),
        in_specs=[smem,
                  pl.BlockSpec((4, DA), lambda b, h: (0, 0)),
                  pl.BlockSpec((None, Lq, DVA), lambda b, h: (b, 0, h)),
                  pl.BlockSpec((None, Lk, DVA), lambda b, h: (b, 0, h)),
                  pl.BlockSpec((None, Lk, DVA), lambda b, h: (b, 0, h)),
                  pl.BlockSpec((Lq, Lk), lambda b, h: (0, 0))],
        out_specs=pl.BlockSpec((None, Lq, DVA), lambda b, h: (b, 0, h)),
        out_shape=jax.ShapeDtypeStruct((B, Lq, WA), F32),
        compiler_params=_params(("arbitrary", "arbitrary")),
        name="attn_sample",
    )(rel_bias, lamv, qab, k_all, v_all, bkt)


def _ret_kernel(lg_ref, q_ref, k_ref, v_ref, s0_ref, o_ref, so_ref, state_sc, decay_sc, *, C):
    h = pl.program_id(1)
    c = pl.program_id(2)
    lg = lg_ref[h]

    @pl.when(c == 0)
    def _():
        state_sc[...] = s0_ref[...]
        r = lax.broadcasted_iota(jnp.int32, (C, C), 0)
        cc = lax.broadcasted_iota(jnp.int32, (C, C), 1)
        diff = (r - cc).astype(F32)
        decay_sc[...] = jnp.where(diff >= 0, jnp.exp(jnp.maximum(diff, 0.0) * lg), 0.0)

    n = lax.broadcasted_iota(jnp.int32, (C, 1), 0).astype(F32)
    xi = jnp.exp((n + 1.0) * lg)
    zeta = jnp.exp((C - 1.0 - n) * lg)
    q = q_ref[...]
    k = k_ref[...]
    v = v_ref[...]
    state = state_sc[...]
    scores = _dot_nt(q, k.astype(BF16)) * decay_sc[...]
    intra = _dot(scores.astype(BF16), v)
    cross = _dot(q, state.astype(BF16)) * xi
    o_ref[...] = intra + cross
    kz = (k * zeta).astype(BF16)
    state_sc[...] = jnp.exp(C * lg) * state + _dot_tn(kz, v)

    @pl.when(c == pl.num_programs(2) - 1)
    def _():
        so_ref[...] = state_sc[...]


def retention(qrb, kr, vrb, state0, C):
    B, L, _ = qrb.shape
    lg = jnp.log(1.0 - 2.0 ** (-5.0 - jnp.arange(HR, dtype=F32)))
    blk = pl.BlockSpec((None, C, DKR), lambda b, h, c: (b, c, h))
    st = pl.BlockSpec((None, None, DKR, DVR), lambda b, h, c: (b, h, 0, 0))
    return pl.pallas_call(
        functools.partial(_ret_kernel, C=C),
        grid=(B, HR, L // C),
        in_specs=[pl.BlockSpec(memory_space=pltpu.SMEM), blk, blk, blk, st],
        out_specs=[blk, st],
        out_shape=[jax.ShapeDtypeStruct((B, L, WR), F32),
                   jax.ShapeDtypeStruct((B, HR, DKR, DVR), F32)],
        scratch_shapes=[pltpu.VMEM((DKR, DVR), F32), pltpu.VMEM((C, C), F32)],
        compiler_params=_params(("arbitrary", "arbitrary", "arbitrary")),
        name="retention",
    )(lg, qrb, kr, vrb, state0)


def _outmix_kernel(x_ref, oa_ref, or_ref, gr_ref, ga_ref, gb_ref, g1_ref, sh2_ref, sc2_ref, n2_ref,
                   sa_ref, sr_ref, wba_ref, wbr_ref, wo_ref, wpq_ref,
                   x1_ref, h2_ref, qp_ref, *, lam_init, h2_transposed):
    sa = sa_ref[...] * 1.0
    sr = sr_ref[...]
    gr = gr_ref[...]
    silu_gr = gr * jax.nn.sigmoid(gr)
    ya_parts, yr_parts = [], []
    for hh in range(HA):
        sl = slice(hh * DVA, (hh + 1) * DVA)
        ya_parts.append(_rms_rows(oa_ref[:, sl]) * sa * (1.0 - lam_init))
        yr_parts.append(silu_gr[:, sl] * (_rms_rows(or_ref[:, sl]) * sr))
    ya = jnp.concatenate(ya_parts, axis=-1).astype(BF16)
    yr = jnp.concatenate(yr_parts, axis=-1).astype(BF16)
    y = (jax.nn.sigmoid(ga_ref[...]) * _dot(ya, wba_ref[...])
         + jax.nn.sigmoid(gb_ref[...]) * _dot(yr, wbr_ref[...]))
    out = _dot(y.astype(BF16), wo_ref[...])
    x1 = x_ref[...] + g1_ref[...] * out
    x1_ref[...] = x1
    h2f = _rms_rows(x1) * n2_ref[...] * (1.0 + sc2_ref[...]) + sh2_ref[...]
    h2 = h2f.astype(BF16)
    h2_ref[...] = h2f.T.astype(BF16) if h2_transposed else h2
    qp_ref[...] = _dot(h2, wpq_ref[...]).astype(BF16)


def out_mix(x, oa, orr, gr, ga, gb, g1, sh2, sc2, norm2, subln_a, subln_r,
            w_ba_b, w_br_b, w_o_b, w_pq_b, lam_init, tm):
    B, L, D = x.shape
    nq = w_pq_b.shape[1]
    row = lambda n: pl.BlockSpec((None, tm, n), lambda b, i: (b, i, 0))
    mod = pl.BlockSpec((None, 1, D), lambda b, i: (b, 0, 0))
    full = lambda a: pl.BlockSpec(a.shape, lambda b, i: (0,) * a.ndim)
    n2 = norm2.reshape(1, D)
    sa = subln_a.reshape(1, DVA)
    sr = subln_r.reshape(1, DVR)
    h2_transposed = tm % LANES == 0
    nt = L // tm
    if h2_transposed:
        h2_spec = pl.BlockSpec((D, tm), lambda b, i: (0, b * nt + i))
        h2_shape = jax.ShapeDtypeStruct((D, B * L), BF16)
    else:
        h2_spec = row(D)
        h2_shape = jax.ShapeDtypeStruct((B, L, D), BF16)
    x1, h2, qp = pl.pallas_call(
        functools.partial(_outmix_kernel, lam_init=lam_init, h2_transposed=h2_transposed),
        grid=(B, nt),
        in_specs=[row(D), row(WA), row(WR), row(WR), row(D), row(D), mod, mod, mod,
                  full(n2), full(sa), full(sr), full(w_ba_b), full(w_br_b), full(w_o_b), full(w_pq_b)],
        out_specs=[row(D), h2_spec, row(nq)],
        out_shape=[jax.ShapeDtypeStruct((B, L, D), F32), h2_shape,
                   jax.ShapeDtypeStruct((B, L, nq), BF16)],
        compiler_params=_params(("arbitrary", "arbitrary")),
        name="out_mix",
    )(x, oa, orr, gr, ga, gb, g1.reshape(B, 1, D), sh2.reshape(B, 1, D), sc2.reshape(B, 1, D),
      n2, sa, sr, w_ba_b, w_br_b, w_o_b, w_pq_b)
    h2t = h2 if h2_transposed else h2.reshape(B * L, D).T
    return x1, h2t, qp


def _topk_rows(s, k):
    n = s.shape[0]
    iota = lax.broadcasted_iota(jnp.int32, s.shape, 0).astype(F32)
    work = s
    rank = jnp.full(s.shape, float(PEER_TOPK * PEER_TOPK), F32)
    vals = []
    for r in range(k):
        m = jnp.max(work, axis=0, keepdims=True)
        idx = jnp.min(jnp.where(work == m, iota, float(n)), axis=0, keepdims=True)
        sel = iota == idx
        rank = jnp.where(sel, float(r), rank)
        work = jnp.where(sel, -jnp.inf, work)
        vals.append(m)
    return vals, rank


SUBLANES = 8
_HEAD_A = SUBLANES
_CAND_NB = [PEER_TOPK] + [SUBLANES] * (_HEAD_A - 1)


def _route_chunk(s1, s2):
    v1, rank1 = _topk_rows(s1, PEER_TOPK)
    v2, rank2 = _topk_rows(s2, PEER_TOPK)
    v2m = jnp.concatenate(v2, axis=0)
    v1t = jnp.concatenate(v1[_HEAD_A:], axis=0)
    blocks = [v1[a] + v2m[0:nb] for a, nb in enumerate(_CAND_NB)] + [v1t + v2[0]]
    cand = jnp.concatenate(blocks, axis=0)
    _, crank = _topk_rows(cand, PEER_TOPK)
    sel = crank < float(PEER_TOPK)
    cmax = v1[0] + v2[0]
    z = jnp.sum(jnp.where(sel, jnp.exp(cand - cmax), 0.0), axis=0, keepdims=True)
    self32 = sel.astype(F32)
    cidx = jnp.zeros(s1.shape, F32)
    lo = 0
    for a, nb in enumerate(_CAND_NB):
        cnt = jnp.sum(self32[lo:lo + nb], axis=0, keepdims=True)
        cidx = jnp.where(rank1 == float(a), cnt, cidx)
        lo += nb
    for a in range(_HEAD_A, PEER_TOPK):
        cidx = jnp.where(rank1 == float(a), self32[lo + a - _HEAD_A:lo + a - _HEAD_A + 1], cidx)
    e1 = jnp.exp(s1 - v1[0]) / z
    e2 = jnp.exp(s2 - v2[0])
    return e1, e2, rank2, cidx


def _route_kernel(q_ref, keys_ref, e1_ref, e2_ref, r2_ref, c_ref, s1_sc, s2_sc):
    q = q_ref[...]
    s1_sc[...] = _dot_nt(keys_ref[0], q[:, 0:KEY_DIM])
    s2_sc[...] = _dot_nt(keys_ref[1], q[:, KEY_DIM:2 * KEY_DIM])

    def chunk(ci, carry):
        sl = pl.ds(pl.multiple_of(ci * LANES, LANES), LANES)
        e1, e2, rank2, cidx = _route_chunk(s1_sc[:, sl], s2_sc[:, sl])
        e1_ref[:, sl] = e1
        e2_ref[:, sl] = e2.astype(BF16)
        r2_ref[:, sl] = rank2.astype(BF16)
        c_ref[:, sl] = cidx
        return carry

    lax.fori_loop(0, s1_sc.shape[1] // LANES, chunk, 0, unroll=2)


def peer_route(qp, keys_b, tm):
    T = qp.shape[0]
    out = pl.BlockSpec((None, N_KEYS, tm), lambda t, h: (h, 0, t))
    shp = lambda dt: jax.ShapeDtypeStruct((PEER_HEADS, N_KEYS, T), dt)
    return pl.pallas_call(
        _route_kernel,
        grid=(T // tm, PEER_HEADS),
        in_specs=[pl.BlockSpec((tm, 2 * KEY_DIM), lambda t, h: (t, h)),
                  pl.BlockSpec((2, N_KEYS, KEY_DIM), lambda t, h: (h, 0, 0))],
        out_specs=[out, out, out, out],
        out_shape=[shp(F32), shp(BF16), shp(BF16), shp(F32)],
        scratch_shapes=[pltpu.VMEM((N_KEYS, tm), F32), pltpu.VMEM((N_KEYS, tm), F32)],
        compiler_params=_params(("arbitrary", "arbitrary")),
        name="peer_route",
    )(qp, keys_b)


BF16_ROWS = 16


def _bcast_rows_bf16(row, n):
    t = row.shape[1]
    tile = jnp.broadcast_to(row, (BF16_ROWS, t)).astype(BF16)
    return jnp.broadcast_to(tile[None], (n // BF16_ROWS, BF16_ROWS, t)).reshape(n, t)


def _peer_kernel(h2_ref, u_ref, vt_ref, e1_ref, e2_ref, r2_ref, c_ref, x1_ref, g2_ref, fn_ref,
                 y_ref, acc_sc, *, ni):
    e = pl.program_id(1)

    @pl.when(e == 0)
    def _():
        acc_sc[...] = jnp.zeros(acc_sc.shape, F32)

    h2 = h2_ref[...]
    pair = 2 * N_KEYS
    a_pairs = [_dot(u_ref[p * pair:(p + 1) * pair, :], h2) for p in range(ni // 2)]
    for p in range(ni // 2):
        ws = []
        for jj in range(2):
            i = e * ni + 2 * p + jj
            a = a_pairs[p][jj * N_KEYS:(jj + 1) * N_KEYS]
            g = jnp.zeros(a.shape, BF16)
            for hh in range(PEER_HEADS):
                c_row = _bcast_rows_bf16(c_ref[hh, pl.ds(i, 1), :], N_KEYS)
                e1_row = _bcast_rows_bf16(e1_ref[hh, pl.ds(i, 1), :], N_KEYS)
                g = g + jnp.where(r2_ref[hh] < c_row, e2_ref[hh] * e1_row, jnp.zeros_like(g))
            act = 0.5 * a * (1.0 + lax.erf(a * (2.0 ** -0.5)))
            ws.append(g * act.astype(BF16))
        w = jnp.concatenate(ws, axis=0)
        acc_sc[...] += _dot(vt_ref[:, p * pair:(p + 1) * pair], w)

    @pl.when(e == pl.num_programs(1) - 1)
    def _():
        x2 = x1_ref[...] + g2_ref[...] * acc_sc[...].T
        y_ref[...] = _rms_rows(x2) * fn_ref[...]


def peer_experts(h2t, u_b, vt_b, e1, e2, r2, cc, x1, g2tok, final_norm, tm, ni=16):
    D, T = h2t.shape
    nb = ni * N_KEYS
    tok = pl.BlockSpec((tm, D), lambda t, e: (t, 0))
    gate = pl.BlockSpec((PEER_HEADS, N_KEYS, tm), lambda t, e: (0, 0, t))
    return pl.pallas_call(
        functools.partial(_peer_kernel, ni=ni),
        grid=(T // tm, u_b.shape[0] // nb),
        in_specs=[pl.BlockSpec((D, tm), lambda t, e: (0, t)),
                  pl.BlockSpec((nb, D), lambda t, e: (e, 0)),
                  pl.BlockSpec((D, nb), lambda t, e: (0, e)),
                  gate, gate, gate, gate, tok, tok,
                  pl.BlockSpec((1, D), lambda t, e: (0, 0))],
        out_specs=tok,
        out_shape=jax.ShapeDtypeStruct((T, D), F32),
        scratch_shapes=[pltpu.VMEM((D, tm), F32)],
        compiler_params=_params(("arbitrary", "arbitrary")),
        name="peer_experts",
    )(h2t, u_b, vt_b, e1, e2, r2, cc, x1, g2tok, final_norm.reshape(1, D))


def _rot_tables(pos):
    inv = 1.0 / (10000.0 ** jnp.linspace(0.0, 1.0, DKR // 2, dtype=F32))
    ang = pos[:, None].astype(F32) * inv[None, :]
    cos, sin = jnp.cos(ang), jnp.sin(ang)
    return jnp.concatenate([cos, cos], axis=-1), jnp.concatenate([-sin, sin], axis=-1)


def _pick_tile(n, pref):
    t = min(n, pref)
    assert n % t == 0, (n, t)
    return t


def _trunk(x, mods, pos, lam_init, lamv, rel_bias, w, attend, values_transposed, state0, ret_chunk,
           final_norm):
    (norm1, norm2, w_in_b, subln_a, subln_r, w_ba_b, w_br_b, w_o_b, w_pq_b, keys_b, u_b, vt_b) = w
    sh1, sc1, g1, sh2, sc2, g2 = mods
    B, L, D = x.shape
    T = B * L
    cos2, sin2 = _rot_tables(pos)
    tm = _pick_tile(L, 256)
    ka, va, qab, kab, vab, qrb, kr, vrb, gr, ga, gb, *vt = in_proj(
        x, sh1, sc1, norm1, w_in_b, cos2, sin2, tm, emit_vt=values_transposed)
    oa = attend(qab, kab, vt[0] if values_transposed else vab)
    orr, st = retention(qrb, kr, vrb, state0, ret_chunk)
    x1, h2t, qp = out_mix(x, oa, orr, gr, ga, gb, g1, sh2, sc2, norm2, subln_a, subln_r,
                          w_ba_b, w_br_b, w_o_b, w_pq_b, lam_init, tm)
    e1, e2, r2, cc = peer_route(qp.reshape(T, -1), keys_b, _pick_tile(T, 512))
    g2tok = jnp.broadcast_to(g2[:, None, :], (B, L, D)).reshape(T, D)
    y = peer_experts(h2t, u_b, vt_b, e1, e2, r2, cc, x1.reshape(T, D), g2tok,
                     final_norm, _pick_tile(T, 512))
    return y.reshape(B, L, D), ka.reshape(B, L, HA, 2 * DA), va.reshape(B, L, HA, DVA), st


def kernel(x_prompt, x_sample, cache_k, cache_v, state_ret, c_prompt, c_sample, w_ada, b_ada, norm1,
           norm2, w_in, lam_q1, lam_k1, lam_q2, lam_k2, subln_a, subln_r, w_ba, w_br, w_o, rel_bias,
           w_pq, peer_keys, peer_u, peer_v, final_norm):
    depth = w_ada.shape[0]
    assert depth == 1, "the fused final norm assumes a single layer"
    Bp, Lp, D = x_prompt.shape
    Bs, Ls, _ = x_sample.shape
    past = cache_k.shape[2]
    pos_p = jnp.arange(Lp, dtype=jnp.int32)
    pos_s = past + jnp.arange(Ls, dtype=jnp.int32)
    lk = past + Ls
    lk_pad = -(-lk // LANES) * LANES
    k_pos_s = jnp.arange(lk_pad, dtype=jnp.int32)
    k_valid_s = k_pos_s < lk

    l = 0
    lam_init = 0.8 - 0.6 * math.exp(-0.3 * l)
    lamv = jnp.stack([lam_q1[l], lam_k1[l], lam_q2[l], lam_k2[l]]).astype(F32)
    mod = ada_mod(jnp.concatenate([c_prompt, c_sample], axis=0), w_ada[l], b_ada[l])
    mods = jnp.split(mod, 6, axis=-1)
    mods_p = [m[:Bp] for m in mods]
    mods_s = [m[Bp:] for m in mods]
    w = (norm1[l], norm2[l], w_in[l].astype(BF16), subln_a[l], subln_r[l], w_ba[l].astype(BF16),
         w_br[l].astype(BF16), w_o[l].astype(BF16), w_pq[l].astype(BF16),
         peer_keys[l].reshape(PEER_HEADS * 2, N_KEYS, KEY_DIM).astype(BF16),
         peer_u[l].astype(BF16), peer_v[l].T.astype(BF16))

    attend_p = lambda q, k, v: attn_prompt(q, k, v, rel_bias, lamv, lam_init)
    zero_state = jnp.zeros((Bp, HR, DKR, DVR), F32)
    yp, kp, vp, sp = _trunk(x_prompt, mods_p, pos_p, lam_init, lamv, rel_bias, w, attend_p, True,
                            zero_state, _pick_tile(Lp, 256), final_norm)

    def attend_s(q, k, v):
        padk = jnp.zeros((Bs, lk_pad - lk, WA), BF16)
        kc = cache_k[l].reshape(Bs, past, HA * 2 * DA).astype(BF16)
        vc = cache_v[l].reshape(Bs, past, WA).astype(BF16)
        k_all = jnp.concatenate([kc, k, padk], axis=1)
        v_all = jnp.concatenate([vc, v, padk], axis=1)
        return attn_small(q, k_all, v_all, pos_s, k_pos_s, k_valid_s, rel_bias, lamv, lam_init)

    ys, ks, vs, ss = _trunk(x_sample, mods_s, pos_s, lam_init, lamv, rel_bias, w, attend_s, False,
                            state_ret[l].astype(F32), Ls, final_norm)
    return (yp, ys, kp[None], vp[None], sp[None], ks[None], vs[None], ss[None])
```

```python
import functools
import math

import jax
import jax.numpy as jnp
from jax import lax
from jax.experimental import pallas as pl
from jax.experimental.pallas import tpu as pltpu

F32 = jnp.float32
BF16 = jnp.bfloat16

CHUNK = 64
HA = 4
DA = 64
DVA = 2 * DA
HR = 4
DKR = 128
DVR = 128
N_BUCKETS = 32
MAX_DIST = 128
PEER_HEADS = 8
N_KEYS = 128
KEY_DIM = 128
PEER_TOPK = 16
EPS = 1e-6
WA = HA * DVA
WR = HR * DVR
NEG = -1e30
LOG2E = math.log2(math.e)
LANES = 128
VMEM_LIMIT = 56 * 1024 * 1024


def _params(sem, vmem=VMEM_LIMIT):
    return pltpu.CompilerParams(dimension_semantics=sem, vmem_limit_bytes=vmem)


def _dot(a, b):
    return jnp.dot(a, b, preferred_element_type=F32)


def _dot_nt(a, b):
    return lax.dot_general(a, b, (((1,), (1,)), ((), ())), preferred_element_type=F32)


def _dot_tn(a, b):
    return lax.dot_general(a, b, (((0,), (0,)), ((), ())), preferred_element_type=F32)


def _rms_rows(x):
    return x * lax.rsqrt(jnp.mean(x * x, axis=-1, keepdims=True) + EPS)


def _ada_kernel(c_ref, w_ref, b_ref, o_ref):
    c = c_ref[...]
    s = (c * jax.nn.sigmoid(c)).astype(BF16)
    o_ref[...] = _dot(s, w_ref[...].astype(BF16)) + b_ref[...]


def ada_mod(c, w_ada, b_ada, tn=1024):
    nb, d = c.shape
    n = w_ada.shape[1]
    return pl.pallas_call(
        _ada_kernel,
        grid=(n // tn,),
        in_specs=[pl.BlockSpec((nb, d), lambda j: (0, 0)),
                  pl.BlockSpec((d, tn), lambda j: (0, j)),
                  pl.BlockSpec((1, tn), lambda j: (0, j))],
        out_specs=pl.BlockSpec((nb, tn), lambda j: (0, j)),
        out_shape=jax.ShapeDtypeStruct((nb, n), F32),
        compiler_params=_params(("arbitrary",)),
        name="ada_mod",
    )(c, w_ada, b_ada.reshape(1, n))


def _inproj_kernel(x_ref, sh_ref, sc_ref, n1_ref, w_ref, cos_ref, sin_ref,
                   ka_ref, va_ref, qab_ref, kab_ref, vab_ref, qrb_ref, kr_ref, vrb_ref,
                   gr_ref, ga_ref, gb_ref, vt_ref=None):
    x = x_ref[...]
    h = _rms_rows(x) * n1_ref[...] * (1.0 + sc_ref[...]) + sh_ref[...]
    hb = h.astype(BF16)

    def proj(lo, n):
        return _dot(hb, w_ref[:, lo:lo + n])

    qa = proj(0, 512)
    qab_ref[...] = (qa * (DA ** -0.5 * LOG2E)).astype(BF16)
    ka = proj(512, 512)
    ka_ref[...] = ka
    kab_ref[...] = ka.astype(BF16)
    va = proj(1024, 512)
    va_ref[...] = va
    vab_ref[...] = va.astype(BF16)
    if vt_ref is not None:
        vt_ref[...] = va.T.astype(BF16)
    cos2 = cos_ref[...]
    sin2 = sin_ref[...]

    def rot(z):
        parts = []
        for hh in range(HR):
            zh = z[:, hh * DKR:(hh + 1) * DKR]
            parts.append(zh * cos2 + pltpu.roll(zh, DKR // 2, 1) * sin2)
        return jnp.concatenate(parts, axis=-1)

    qrb_ref[...] = rot(proj(1536, 512)).astype(BF16)
    kr_ref[...] = rot(proj(2048, 512)) * (DKR ** -0.5)
    vrb_ref[...] = proj(2560, 512).astype(BF16)
    gr_ref[...] = proj(3072, 512)
    ga_ref[...] = proj(3584, 1024)
    gb_ref[...] = proj(4608, 1024)


def in_proj(x, sh1, sc1, norm1, w_in_b, cos2, sin2, tm, emit_vt):
    B, L, D = x.shape
    d_in = w_in_b.shape[1]
    row = lambda n: pl.BlockSpec((None, tm, n), lambda b, i: (b, i, 0))
    mod = pl.BlockSpec((None, 1, D), lambda b, i: (b, 0, 0))
    f = lambda n, dt: jax.ShapeDtypeStruct((B, L, n), dt)
    vt_spec = [pl.BlockSpec((None, WA, tm), lambda b, i: (b, 0, i))] if emit_vt else []
    vt_shape = [jax.ShapeDtypeStruct((B, WA, L), BF16)] if emit_vt else []
    return pl.pallas_call(
        _inproj_kernel,
        grid=(B, L // tm),
        in_specs=[row(D), mod, mod,
                  pl.BlockSpec((1, D), lambda b, i: (0, 0)),
                  pl.BlockSpec((D, d_in), lambda b, i: (0, 0)),
                  pl.BlockSpec((tm, DKR), lambda b, i: (i, 0)),
                  pl.BlockSpec((tm, DKR), lambda b, i: (i, 0))],
        out_specs=[row(512), row(512), row(512), row(512), row(512), row(512), row(512), row(512),
                   row(512), row(1024), row(1024)] + vt_spec,
        out_shape=[f(512, F32), f(512, F32), f(512, BF16), f(512, BF16), f(512, BF16),
                   f(512, BF16), f(512, F32), f(512, BF16), f(512, F32), f(1024, F32), f(1024, F32)]
        + vt_shape,
        compiler_params=_params(("arbitrary", "arbitrary")),
        name="in_proj",
    )(x, sh1.reshape(B, 1, D), sc1.reshape(B, 1, D), norm1.reshape(1, D), w_in_b, cos2, sin2)


def _t5_bucket(rel):
    nb = N_BUCKETS // 2
    ret = jnp.where(rel > 0, nb, 0)
    n = jnp.abs(rel)
    max_exact = nb // 2
    nf = jnp.maximum(n, max_exact).astype(F32)
    large = max_exact + (jnp.log(nf / max_exact) / math.log(MAX_DIST / max_exact)
                         * (nb - max_exact)).astype(jnp.int32)
    large = jnp.minimum(large, nb - 1)
    return ret + jnp.where(n < max_exact, n, large)


def _bias_from_buckets(bkt, rb_ref, h, shift):
    val = jnp.where(bkt < 0, NEG, 0.0).astype(F32)
    for n in range(N_BUCKETS):
        val = jnp.where(bkt == n, (rb_ref[n, h] - shift) * LOG2E, val)
    return val


def _lam_value(lv_ref, lam_init):
    lv = lv_ref[...]
    a = jnp.sum(lv[0:1] * lv[1:2], axis=-1, keepdims=True)
    b = jnp.sum(lv[2:3] * lv[3:4], axis=-1, keepdims=True)
    return jnp.exp(a) - jnp.exp(b) + lam_init


def _block_diag_q(q):
    lane = lax.broadcasted_iota(jnp.int32, q.shape, 1)
    zero = jnp.zeros_like(q)
    return jnp.concatenate([jnp.where(lane < DA, q, zero), jnp.where(lane >= DA, q, zero)], axis=0)


def _attn_kernel(rb_ref, lv_ref, q_ref, k_ref, vt_ref, bkt_ref, o_ref,
                 bias_sc, m_sc, acc_sc, va_sc, *, tq, tb, nsub, lam_init):
    h = pl.program_id(1)
    i = pl.program_id(2)

    @pl.when(i == 0)
    def _():
        far = rb_ref[N_BUCKETS // 2 - 1, h]
        for t in range(2):
            bias_sc[t] = _bias_from_buckets(bkt_ref[t], rb_ref, h, far)
        va_sc[0:DVA, :] = vt_ref[...]
        va_sc[DVA:, :] = jnp.ones((BF16_ROWS, va_sc.shape[1]), BF16)

    qt = q_ref[...].astype(F32).T
    drow = lax.broadcasted_iota(jnp.int32, qt.shape, 0)
    qbd_t = jnp.concatenate([jnp.where(drow < DA, qt, 0.0), jnp.where(drow >= DA, qt, 0.0)],
                            axis=1).astype(BF16)
    m_sc[...] = jnp.full(m_sc.shape, NEG, F32)
    acc_sc[...] = jnp.zeros(acc_sc.shape, F32)

    def scores(off, tk, bias=None, nvalid=None):
        s = _dot(k_ref[pl.ds(off, tk), :], qbd_t)
        if bias is not None:
            s = s + jnp.concatenate([bias, bias], axis=1)
        if nvalid is not None:
            row = lax.broadcasted_iota(jnp.int32, (tk, 1), 0)
            s = jnp.where(row < nvalid, s, NEG)
        return s

    def absorb(s, off, tk):
        m_prev = m_sc[...]
        m_new = jnp.maximum(m_prev, jnp.max(s, axis=0, keepdims=True))
        alpha = jnp.exp2(m_prev - m_new)
        p = jnp.exp2(s - m_new).astype(BF16)
        acc_sc[...] = alpha * acc_sc[...] + _dot(va_sc[:, pl.ds(off, tk)], p)
        m_sc[...] = m_new

    def tiles(specs):
        ss = [scores(*sp) for sp in specs]
        for s, sp in zip(ss, specs):
            absorb(s, sp[0], sp[1])

    far_end = jnp.maximum(i - 1, 0) * tq
    group = nsub * tb
    half = group // 2
    nfull = far_end // group
    rem = far_end - nfull * group

    def run(off, n, nvalid=None):
        tiles([(pl.multiple_of(off + j * tb, tb), tb, None,
                None if nvalid is None else nvalid - j * tb) for j in range(n)])

    def far_body(t, carry):
        run(pl.multiple_of(t * group, group), nsub)
        return carry

    lax.fori_loop(0, nfull, far_body, 0)
    base = pl.multiple_of(nfull * group, group)

    @pl.when(rem >= half)
    def _():
        run(base, nsub // 2)

    rem2 = jnp.where(rem >= half, rem - half, rem)

    @pl.when(rem2 > 0)
    def _():
        run(pl.multiple_of(base + rem - rem2, half), nsub // 2, nvalid=rem2)

    @pl.when(i >= 1)
    def _():
        off = pl.multiple_of((i - 1) * tq, tq)
        tiles([(off, tq, bias_sc[1]), (pl.multiple_of(off + tq, tq), tq, bias_sc[0])])

    @pl.when(i == 0)
    def _():
        tiles([(0, tq, bias_sc[0])])

    lam = _lam_value(lv_ref, lam_init)
    o = acc_sc[0:DVA, :] / acc_sc[DVA:DVA + 1, :]
    o_ref[...] = (o[:, 0:tq] - lam * o[:, tq:2 * tq]).T


def attn_prompt(qab, kab, vt, rel_bias, lamv, lam_init, tq=256, tb=256, nsub=8):
    B, S, _ = qab.shape
    tb = min(tb, S // (2 * nsub))
    assert S % (nsub * tb) == 0 and (nsub * tb) % tq == 0
    c = jnp.arange(tq, dtype=jnp.int32)[:, None]
    r = jnp.arange(tq, dtype=jnp.int32)[None, :]
    diag = jnp.where((c // CHUNK) <= (r // CHUNK), _t5_bucket(c - r), -1)
    prev = _t5_bucket(c - r - tq)
    bkt = jnp.stack([diag, prev]).astype(jnp.int32)
    smem = pl.BlockSpec(memory_space=pltpu.SMEM)
    return pl.pallas_call(
        functools.partial(_attn_kernel, tq=tq, tb=tb, nsub=nsub, lam_init=lam_init),
        grid=(B, HA, S // tq),
        in_specs=[smem,
                  pl.BlockSpec((4, DA), lambda b, h, i: (0, 0)),
                  pl.BlockSpec((None, tq, DVA), lambda b, h, i: (b, i, h)),
                  pl.BlockSpec((None, S, DVA), lambda b, h, i: (b, 0, h)),
                  pl.BlockSpec((None, DVA, S), lambda b, h, i: (b, h, 0)),
                  pl.BlockSpec((2, tq, tq), lambda b, h, i: (0, 0, 0))],
        out_specs=pl.BlockSpec((None, tq, DVA), lambda b, h, i: (b, i, h)),
        out_shape=jax.ShapeDtypeStruct((B, S, WA), F32),
        scratch_shapes=[pltpu.VMEM((2, tq, tq), F32),
                        pltpu.VMEM((1, 2 * tq), F32),
                        pltpu.VMEM((DVA + BF16_ROWS, 2 * tq), F32),
                        pltpu.VMEM((DVA + BF16_ROWS, S), BF16)],
        compiler_params=_params(("arbitrary", "arbitrary", "arbitrary")),
        name="attn_prompt",
    )(rel_bias, lamv, qab, kab, vt, bkt)


def _attn_small_kernel(rb_ref, lv_ref, q_ref, k_ref, v_ref, bkt_ref, o_ref, *, lq, lam_init):
    h = pl.program_id(1)
    bias = _bias_from_buckets(bkt_ref[...], rb_ref, h, 0.0)
    qbd = _block_diag_q(q_ref[...])
    s = _dot_nt(qbd, k_ref[...]) + jnp.concatenate([bias, bias], axis=0)
    m = jnp.max(s, axis=-1, keepdims=True)
    p = jnp.exp2(s - m)
    l = jnp.sum(p, axis=-1, keepdims=True)
    o = _dot(p.astype(BF16), v_ref[...]) / l
    lam = _lam_value(lv_ref, lam_init)
    o_ref[...] = o[0:lq] - lam * o[lq:2 * lq]


def attn_small(qab, k_all, v_all, q_pos, k_pos, k_valid, rel_bias, lamv, lam_init):
    B, Lq, _ = qab.shape
    Lk = k_all.shape[1]
    visible = ((k_pos[None, :] // CHUNK) <= (q_pos[:, None] // CHUNK)) & k_valid[None, :]
    bkt = jnp.where(visible, _t5_bucket(k_pos[None, :] - q_pos[:, None]), -1).astype(jnp.int32)
    smem = pl.BlockSpec(memory_space=pltpu.SMEM)
    return pl.pallas_call(
        functools.partial(_attn_small_kernel, lq=Lq, lam_init=lam_init),
        grid=(B, HA),
        in_specs=[smem,
                  pl.BlockSpec((4, DA), lambda b, h: (0, 0)),
                  pl.BlockSpec((None, Lq, DVA), lambda b, h: (b, 0, h)),
                  pl.BlockSpec((None, Lk, DVA), lambda b, h: (b, 0, h)),
                  pl.BlockSpec((None, Lk, DVA), lambda b, h: (b, 0, h)),
                  pl.BlockSpec((Lq, Lk), lambda b, h: (0, 0))],
        out_specs=pl.BlockSpec((None, Lq, DVA), lambda b, h: (b, 0, h)),
        out_shape=jax.ShapeDtypeStruct((B, Lq, WA), F32),
        compiler_params=_params(("arbitrary", "arbitrary")),
        name="attn_sample",
    )(rel_bias, lamv, qab, k_all, v_all, bkt)


def _ret_kernel(lg_ref, q_ref, k_ref, v_ref, s0_ref, o_ref, so_ref, state_sc, decay_sc, *, C):
    h = pl.program_id(1)
    c = pl.program_id(2)
    lg = lg_ref[h]

    @pl.when(c == 0)
    def _():
        state_sc[...] = s0_ref[...]
        r = lax.broadcasted_iota(jnp.int32, (C, C), 0)
        cc = lax.broadcasted_iota(jnp.int32, (C, C), 1)
        diff = (r - cc).astype(F32)
        decay_sc[...] = jnp.where(diff >= 0, jnp.exp(jnp.maximum(diff, 0.0) * lg), 0.0)

    n = lax.broadcasted_iota(jnp.int32, (C, 1), 0).astype(F32)
    xi = jnp.exp((n + 1.0) * lg)
    zeta = jnp.exp((C - 1.0 - n) * lg)
    q = q_ref[...]
    k = k_ref[...]
    v = v_ref[...]
    state = state_sc[...]
    scores = _dot_nt(q, k.astype(BF16)) * decay_sc[...]
    intra = _dot(scores.astype(BF16), v)
    cross = _dot(q, state.astype(BF16)) * xi
    o_ref[...] = intra + cross
    kz = (k * zeta).astype(BF16)
    state_sc[...] = jnp.exp(C * lg) * state + _dot_tn(kz, v)

    @pl.when(c == pl.num_programs(2) - 1)
    def _():
        so_ref[...] = state_sc[...]


def retention(qrb, kr, vrb, state0, C):
    B, L, _ = qrb.shape
    lg = jnp.log(1.0 - 2.0 ** (-5.0 - jnp.arange(HR, dtype=F32)))
    blk = pl.BlockSpec((None, C, DKR), lambda b, h, c: (b, c, h))
    st = pl.BlockSpec((None, None, DKR, DVR), lambda b, h, c: (b, h, 0, 0))
    return pl.pallas_call(
        functools.partial(_ret_kernel, C=C),
        grid=(B, HR, L // C),
        in_specs=[pl.BlockSpec(memory_space=pltpu.SMEM), blk, blk, blk, st],
        out_specs=[blk, st],
        out_shape=[jax.ShapeDtypeStruct((B, L, WR), F32),
                   jax.ShapeDtypeStruct((B, HR, DKR, DVR), F32)],
        scratch_shapes=[pltpu.VMEM((DKR, DVR), F32), pltpu.VMEM((C, C), F32)],
        compiler_params=_params(("arbitrary", "arbitrary", "arbitrary")),
        name="retention",
    )(lg, qrb, kr, vrb, state0)


def _outmix_kernel(x_ref, oa_ref, or_ref, gr_ref, ga_ref, gb_ref, g1_ref, sh2_ref, sc2_ref, n2_ref,
                   sa_ref, sr_ref, wba_ref, wbr_ref, wo_ref, wpq_ref,
                   x1_ref, h2_ref, qp_ref, *, lam_init, h2_transposed):
    sa = sa_ref[...] * 1.0
    sr = sr_ref[...]
    gr = gr_ref[...]
    silu_gr = gr * jax.nn.sigmoid(gr)
    ya_parts, yr_parts = [], []
    for hh in range(HA):
        sl = slice(hh * DVA, (hh + 1) * DVA)
        ya_parts.append(_rms_rows(oa_ref[:, sl]) * sa * (1.0 - lam_init))
        yr_parts.append(silu_gr[:, sl] * (_rms_rows(or_ref[:, sl]) * sr))
    ya = jnp.concatenate(ya_parts, axis=-1).astype(BF16)
    yr = jnp.concatenate(yr_parts, axis=-1).astype(BF16)
    y = (jax.nn.sigmoid(ga_ref[...]) * _dot(ya, wba_ref[...])
         + jax.nn.sigmoid(gb_ref[...]) * _dot(yr, wbr_ref[...]))
    out = _dot(y.astype(BF16), wo_ref[...])
    x1 = x_ref[...] + g1_ref[...] * out
    x1_ref[...] = x1
    h2f = _rms_rows(x1) * n2_ref[...] * (1.0 + sc2_ref[...]) + sh2_ref[...]
    h2 = h2f.astype(BF16)
    h2_ref[...] = h2f.T.astype(BF16) if h2_transposed else h2
    qp_ref[...] = _dot(h2, wpq_ref[...]).astype(BF16)


def out_mix(x, oa, orr, gr, ga, gb, g1, sh2, sc2, norm2, subln_a, subln_r,
            w_ba_b, w_br_b, w_o_b, w_pq_b, lam_init, tm):
    B, L, D = x.shape
    nq = w_pq_b.shape[1]
    row = lambda n: pl.BlockSpec((None, tm, n), lambda b, i: (b, i, 0))
    mod = pl.BlockSpec((None, 1, D), lambda b, i: (b, 0, 0))
    full = lambda a: pl.BlockSpec(a.shape, lambda b, i: (0,) * a.ndim)
    n2 = norm2.reshape(1, D)
    sa = subln_a.reshape(1, DVA)
    sr = subln_r.reshape(1, DVR)
    h2_transposed = tm % LANES == 0
    nt = L // tm
    if h2_transposed:
        h2_spec = pl.BlockSpec((D, tm), lambda b, i: (0, b * nt + i))
        h2_shape = jax.ShapeDtypeStruct((D, B * L), BF16)
    else:
        h2_spec = row(D)
        h2_shape = jax.ShapeDtypeStruct((B, L, D), BF16)
    x1, h2, qp = pl.pallas_call(
        functools.partial(_outmix_kernel, lam_init=lam_init, h2_transposed=h2_transposed),
        grid=(B, nt),
        in_specs=[row(D), row(WA), row(WR), row(WR), row(D), row(D), mod, mod, mod,
                  full(n2), full(sa), full(sr), full(w_ba_b), full(w_br_b), full(w_o_b), full(w_pq_b)],
        out_specs=[row(D), h2_spec, row(nq)],
        out_shape=[jax.ShapeDtypeStruct((B, L, D), F32), h2_shape,
                   jax.ShapeDtypeStruct((B, L, nq), BF16)],
        compiler_params=_params(("arbitrary", "arbitrary")),
        name="out_mix",
    )(x, oa, orr, gr, ga, gb, g1.reshape(B, 1, D), sh2.reshape(B, 1, D), sc2.reshape(B, 1, D),
      n2, sa, sr, w_ba_b, w_br_b, w_o_b, w_pq_b)
    h2t = h2 if h2_transposed else h2.reshape(B * L, D).T
    return x1, h2t, qp


UNRANKED = float(PEER_TOPK * PEER_TOPK)


def _topk_rows(s, k, break_ties):
    n = s.shape[0]
    iota = lax.broadcasted_iota(jnp.int32, s.shape, 0).astype(F32)
    work = s
    rank = jnp.full(s.shape, UNRANKED, F32)
    vals = []
    for r in range(k):
        m = jnp.max(work, axis=0, keepdims=True)
        sel = work == m
        if break_ties:
            idx = jnp.min(jnp.where(sel, iota, float(n)), axis=0, keepdims=True)
            sel = iota == idx
        rank = jnp.where(sel, float(r), rank)
        work = jnp.where(sel, -jnp.inf, work)
        vals.append(m)
    return vals, rank


def _ranked_count(rank):
    return jnp.sum((rank < UNRANKED).astype(F32), axis=0, keepdims=True)


SUBLANES = 8
_HEAD_A = SUBLANES
_CAND_NB = [PEER_TOPK] + [SUBLANES] * (_HEAD_A - 1)


def _route_chunk(s1, s2, break_ties):
    v1, rank1 = _topk_rows(s1, PEER_TOPK, break_ties)
    v2, rank2 = _topk_rows(s2, PEER_TOPK, break_ties)
    v2m = jnp.concatenate(v2, axis=0)
    v1t = jnp.concatenate(v1[_HEAD_A:], axis=0)
    blocks = [v1[a] + v2m[0:nb] for a, nb in enumerate(_CAND_NB)] + [v1t + v2[0]]
    cand = jnp.concatenate(blocks, axis=0)
    _, crank = _topk_rows(cand, PEER_TOPK, break_ties)
    k = float(PEER_TOPK)
    bad = ((_ranked_count(rank1) != k) | (_ranked_count(rank2) != k) | (_ranked_count(crank) != k))
    nbad = jnp.sum(bad.astype(jnp.int32))
    sel = crank < UNRANKED
    cmax = v1[0] + v2[0]
    z = jnp.sum(jnp.where(sel, jnp.exp(cand - cmax), 0.0), axis=0, keepdims=True)
    self32 = sel.astype(F32)
    cidx = jnp.zeros(s1.shape, F32)
    lo = 0
    for a, nb in enumerate(_CAND_NB):
        cnt = jnp.sum(self32[lo:lo + nb], axis=0, keepdims=True)
        cidx = jnp.where(rank1 == float(a), cnt, cidx)
        lo += nb
    for a in range(_HEAD_A, PEER_TOPK):
        cidx = jnp.where(rank1 == float(a), self32[lo + a - _HEAD_A:lo + a - _HEAD_A + 1], cidx)
    e1 = jnp.exp(s1 - v1[0]) / z
    e2 = jnp.exp(s2 - v2[0])
    return (e1, e2, rank2, cidx), nbad


def _route_kernel(q_ref, keys_ref, e1_ref, e2_ref, r2_ref, c_ref, s1_sc, s2_sc):
    q = q_ref[...]
    s1_sc[...] = _dot_nt(keys_ref[0], q[:, 0:KEY_DIM])
    s2_sc[...] = _dot_nt(keys_ref[1], q[:, KEY_DIM:2 * KEY_DIM])

    def store(sl, res):
        e1, e2, rank2, cidx = res
        e1_ref[:, sl] = e1
        e2_ref[:, sl] = e2.astype(BF16)
        r2_ref[:, sl] = rank2.astype(BF16)
        c_ref[:, sl] = cidx

    group = 2
    assert s1_sc.shape[1] % (group * LANES) == 0

    def chunks(ci, carry):
        sls = [pl.ds(pl.multiple_of((ci * group + g) * LANES, LANES), LANES) for g in range(group)]
        fast = [_route_chunk(s1_sc[:, sl], s2_sc[:, sl], break_ties=False) for sl in sls]
        for sl, (res, _) in zip(sls, fast):
            store(sl, res)
        for sl, (_, nbad) in zip(sls, fast):
            @pl.when(nbad > 0)
            def _():
                store(sl, _route_chunk(s1_sc[:, sl], s2_sc[:, sl], break_ties=True)[0])
        return carry

    lax.fori_loop(0, s1_sc.shape[1] // (group * LANES), chunks, 0)


def peer_route(qp, keys_b, tm):
    T = qp.shape[0]
    out = pl.BlockSpec((None, N_KEYS, tm), lambda t, h: (h, 0, t))
    shp = lambda dt: jax.ShapeDtypeStruct((PEER_HEADS, N_KEYS, T), dt)
    return pl.pallas_call(
        _route_kernel,
        grid=(T // tm, PEER_HEADS),
        in_specs=[pl.BlockSpec((tm, 2 * KEY_DIM), lambda t, h: (t, h)),
                  pl.BlockSpec((2, N_KEYS, KEY_DIM), lambda t, h: (h, 0, 0))],
        out_specs=[out, out, out, out],
        out_shape=[shp(F32), shp(BF16), shp(BF16), shp(F32)],
        scratch_shapes=[pltpu.VMEM((N_KEYS, tm), F32), pltpu.VMEM((N_KEYS, tm), F32)],
        compiler_params=_params(("arbitrary", "arbitrary")),
        name="peer_route",
    )(qp, keys_b)


BF16_ROWS = 16


def _bcast_rows_bf16(row, n):
    t = row.shape[1]
    tile = jnp.broadcast_to(row, (BF16_ROWS, t)).astype(BF16)
    return jnp.broadcast_to(tile[None], (n // BF16_ROWS, BF16_ROWS, t)).reshape(n, t)


def _peer_kernel(h2_ref, u_ref, vt_ref, e1_ref, e2_ref, r2_ref, c_ref, x1_ref, g2_ref, fn_ref,
                 y_ref, acc_sc, *, ni):
    e = pl.program_id(1)

    @pl.when(e == 0)
    def _():
        acc_sc[...] = jnp.zeros(acc_sc.shape, F32)

    h2 = h2_ref[...]
    pair = 2 * N_KEYS
    a_pairs = [_dot(u_ref[p * pair:(p + 1) * pair, :], h2) for p in range(ni // 2)]
    for p in range(ni // 2):
        ws = []
        for jj in range(2):
            i = e * ni + 2 * p + jj
            a = a_pairs[p][jj * N_KEYS:(jj + 1) * N_KEYS]
            g = jnp.zeros(a.shape, BF16)
            for hh in range(PEER_HEADS):
                c_row = _bcast_rows_bf16(c_ref[hh, pl.ds(i, 1), :], N_KEYS)
                e1_row = _bcast_rows_bf16(e1_ref[hh, pl.ds(i, 1), :], N_KEYS)
                g = g + jnp.where(r2_ref[hh] < c_row, e2_ref[hh] * e1_row, jnp.zeros_like(g))
            act = 0.5 * a * (1.0 + lax.erf(a * (2.0 ** -0.5)))
            ws.append(g * act.astype(BF16))
        w = jnp.concatenate(ws, axis=0)
        acc_sc[...] += _dot(vt_ref[:, p * pair:(p + 1) * pair], w)

    @pl.when(e == pl.num_programs(1) - 1)
    def _():
        x2 = x1_ref[...] + g2_ref[...] * acc_sc[...].T
        y_ref[...] = _rms_rows(x2) * fn_ref[...]


def peer_experts(h2t, u_b, vt_b, e1, e2, r2, cc, x1, g2tok, final_norm, tm, ni=16):
    D, T = h2t.shape
    nb = ni * N_KEYS
    tok = pl.BlockSpec((tm, D), lambda t, e: (t, 0))
    gate = pl.BlockSpec((PEER_HEADS, N_KEYS, tm), lambda t, e: (0, 0, t))
    return pl.pallas_call(
        functools.partial(_peer_kernel, ni=ni),
        grid=(T // tm, u_b.shape[0] // nb),
        in_specs=[pl.BlockSpec((D, tm), lambda t, e: (0, t)),
                  pl.BlockSpec((nb, D), lambda t, e: (e, 0)),
                  pl.BlockSpec((D, nb), lambda t, e: (0, e)),
                  gate, gate, gate, gate, tok, tok,
                  pl.BlockSpec((1, D), lambda t, e: (0, 0))],
        out_specs=tok,
        out_shape=jax.ShapeDtypeStruct((T, D), F32),
        scratch_shapes=[pltpu.VMEM((D, tm), F32)],
        compiler_params=_params(("arbitrary", "arbitrary")),
        name="peer_experts",
    )(h2t, u_b, vt_b, e1, e2, r2, cc, x1, g2tok, final_norm.reshape(1, D))


def _rot_tables(pos):
    inv = 1.0 / (10000.0 ** jnp.linspace(0.0, 1.0, DKR // 2, dtype=F32))
    ang = pos[:, None].astype(F32) * inv[None, :]
    cos, sin = jnp.cos(ang), jnp.sin(ang)
    return jnp.concatenate([cos, cos], axis=-1), jnp.concatenate([-sin, sin], axis=-1)


def _pick_tile(n, pref):
    t = min(n, pref)
    assert n % t == 0, (n, t)
    return t


def _trunk(x, mods, pos, lam_init, lamv, rel_bias, w, attend, values_transposed, state0, ret_chunk,
           final_norm):
    (norm1, norm2, w_in_b, subln_a, subln_r, w_ba_b, w_br_b, w_o_b, w_pq_b, keys_b, u_b, vt_b) = w
    sh1, sc1, g1, sh2, sc2, g2 = mods
    B, L, D = x.shape
    T = B * L
    cos2, sin2 = _rot_tables(pos)
    tm = _pick_tile(L, 256)
    ka, va, qab, kab, vab, qrb, kr, vrb, gr, ga, gb, *vt = in_proj(
        x, sh1, sc1, norm1, w_in_b, cos2, sin2, tm, emit_vt=values_transposed)
    oa = attend(qab, kab, vt[0] if values_transposed else vab)
    orr, st = retention(qrb, kr, vrb, state0, ret_chunk)
    x1, h2t, qp = out_mix(x, oa, orr, gr, ga, gb, g1, sh2, sc2, norm2, subln_a, subln_r,
                          w_ba_b, w_br_b, w_o_b, w_pq_b, lam_init, tm)
    e1, e2, r2, cc = peer_route(qp.reshape(T, -1), keys_b, _pick_tile(T, 512))
    g2tok = jnp.broadcast_to(g2[:, None, :], (B, L, D)).reshape(T, D)
    y = peer_experts(h2t, u_b, vt_b, e1, e2, r2, cc, x1.reshape(T, D), g2tok,
                     final_norm, _pick_tile(T, 512))
    return y.reshape(B, L, D), ka.reshape(B, L, HA, 2 * DA), va.reshape(B, L, HA, DVA), st


def kernel(x_prompt, x_sample, cache_k, cache_v, state_ret, c_prompt, c_sample, w_ada, b_ada, norm1,
           norm2, w_in, lam_q1, lam_k1, lam_q2, lam_k2, subln_a, subln_r, w_ba, w_br, w_o, rel_bias,
           w_pq, peer_keys, peer_u, peer_v, final_norm):
    depth = w_ada.shape[0]
    assert depth == 1, "the fused final norm assumes a single layer"
    Bp, Lp, D = x_prompt.shape
    Bs, Ls, _ = x_sample.shape
    past = cache_k.shape[2]
    pos_p = jnp.arange(Lp, dtype=jnp.int32)
    pos_s = past + jnp.arange(Ls, dtype=jnp.int32)
    lk = past + Ls
    lk_pad = -(-lk // LANES) * LANES
    k_pos_s = jnp.arange(lk_pad, dtype=jnp.int32)
    k_valid_s = k_pos_s < lk

    l = 0
    lam_init = 0.8 - 0.6 * math.exp(-0.3 * l)
    lamv = jnp.stack([lam_q1[l], lam_k1[l], lam_q2[l], lam_k2[l]]).astype(F32)
    mod = ada_mod(jnp.concatenate([c_prompt, c_sample], axis=0), w_ada[l], b_ada[l])
    mods = jnp.split(mod, 6, axis=-1)
    mods_p = [m[:Bp] for m in mods]
    mods_s = [m[Bp:] for m in mods]
    w = (norm1[l], norm2[l], w_in[l].astype(BF16), subln_a[l], subln_r[l], w_ba[l].astype(BF16),
         w_br[l].astype(BF16), w_o[l].astype(BF16), w_pq[l].astype(BF16),
         peer_keys[l].reshape(PEER_HEADS * 2, N_KEYS, KEY_DIM).astype(BF16),
         peer_u[l].astype(BF16), peer_v[l].T.astype(BF16))

    attend_p = lambda q, k, v: attn_prompt(q, k, v, rel_bias, lamv, lam_init)
    zero_state = jnp.zeros((Bp, HR, DKR, DVR), F32)
    yp, kp, vp, sp = _trunk(x_prompt, mods_p, pos_p, lam_init, lamv, rel_bias, w, attend_p, True,
                            zero_state, _pick_tile(Lp, 256), final_norm)

    def attend_s(q, k, v):
        padk = jnp.zeros((Bs, lk_pad - lk, WA), BF16)
        kc = cache_k[l].reshape(Bs, past, HA * 2 * DA).astype(BF16)
        vc = cache_v[l].reshape(Bs, past, WA).astype(BF16)
        k_all = jnp.concatenate([kc, k, padk], axis=1)
        v_all = jnp.concatenate([vc, v, padk], axis=1)
        return attn_small(q, k_all, v_all, pos_s, k_pos_s, k_valid_s, rel_bias, lamv, lam_init)

    ys, ks, vs, ss = _trunk(x_sample, mods_s, pos_s, lam_init, lamv, rel_bias, w, attend_s, False,
                            state_ret[l].astype(F32), Ls, final_norm)
    return (yp, ys, kp[None], vp[None], sp[None], ks[None], vs[None], ss[None])
```

```python
import functools
import math

import jax
import jax.numpy as jnp
from jax import lax
from jax.experimental import pallas as pl
from jax.experimental.pallas import tpu as pltpu

F32 = jnp.float32
BF16 = jnp.bfloat16

CHUNK = 64
HA = 4
DA = 64
DVA = 2 * DA
HR = 4
DKR = 128
DVR = 128
N_BUCKETS = 32
MAX_DIST = 128
PEER_HEADS = 8
N_KEYS = 128
KEY_DIM = 128
PEER_TOPK = 16
EPS = 1e-6
WA = HA * DVA
WR = HR * DVR
NEG = -1e30
LOG2E = math.log2(math.e)
LANES = 128
VMEM_LIMIT = 56 * 1024 * 1024


def _params(sem, vmem=VMEM_LIMIT):
    return pltpu.CompilerParams(dimension_semantics=sem, vmem_limit_bytes=vmem)


def _dot(a, b):
    return jnp.dot(a, b, preferred_element_type=F32)


def _dot_nt(a, b):
    return lax.dot_general(a, b, (((1,), (1,)), ((), ())), preferred_element_type=F32)


def _dot_tn(a, b):
    return lax.dot_general(a, b, (((0,), (0,)), ((), ())), preferred_element_type=F32)


def _rms_rows(x):
    return x * lax.rsqrt(jnp.mean(x * x, axis=-1, keepdims=True) + EPS)


def _ada_kernel(c_ref, w_ref, b_ref, o_ref):
    c = c_ref[...]
    s = (c * jax.nn.sigmoid(c)).astype(BF16)
    o_ref[...] = _dot(s, w_ref[...].astype(BF16)) + b_ref[...]


def ada_mod(c, w_ada, b_ada, tn=1024):
    nb, d = c.shape
    n = w_ada.shape[1]
    return pl.pallas_call(
        _ada_kernel,
        grid=(n // tn,),
        in_specs=[pl.BlockSpec((nb, d), lambda j: (0, 0)),
                  pl.BlockSpec((d, tn), lambda j: (0, j)),
                  pl.BlockSpec((1, tn), lambda j: (0, j))],
        out_specs=pl.BlockSpec((nb, tn), lambda j: (0, j)),
        out_shape=jax.ShapeDtypeStruct((nb, n), F32),
        compiler_params=_params(("arbitrary",)),
        name="ada_mod",
    )(c, w_ada, b_ada.reshape(1, n))


def _inproj_kernel(x_ref, sh_ref, sc_ref, n1_ref, w_ref, cos_ref, sin_ref,
                   ka_ref, va_ref, qab_ref, kab_ref, vab_ref, qrb_ref, kr_ref, vrb_ref,
                   gr_ref, ga_ref, gb_ref, vt_ref=None):
    x = x_ref[...]
    h = _rms_rows(x) * n1_ref[...] * (1.0 + sc_ref[...]) + sh_ref[...]
    hb = h.astype(BF16)

    def proj(lo, n):
        return _dot(hb, w_ref[:, lo:lo + n])

    qa = proj(0, 512)
    qab_ref[...] = (qa * (DA ** -0.5 * LOG2E)).astype(BF16)
    ka = proj(512, 512)
    ka_ref[...] = ka
    kab_ref[...] = ka.astype(BF16)
    va = proj(1024, 512)
    va_ref[...] = va
    vab_ref[...] = va.astype(BF16)
    if vt_ref is not None:
        vt_ref[...] = va.T.astype(BF16)
    cos2 = cos_ref[...]
    sin2 = sin_ref[...]

    def rot(z):
        parts = []
        for hh in range(HR):
            zh = z[:, hh * DKR:(hh + 1) * DKR]
            parts.append(zh * cos2 + pltpu.roll(zh, DKR // 2, 1) * sin2)
        return jnp.concatenate(parts, axis=-1)

    qrb_ref[...] = rot(proj(1536, 512)).astype(BF16)
    kr_ref[...] = rot(proj(2048, 512)) * (DKR ** -0.5)
    vrb_ref[...] = proj(2560, 512).astype(BF16)
    gr_ref[...] = proj(3072, 512)
    ga_ref[...] = proj(3584, 1024)
    gb_ref[...] = proj(4608, 1024)


def in_proj(x, sh1, sc1, norm1, w_in_b, cos2, sin2, tm, emit_vt):
    B, L, D = x.shape
    d_in = w_in_b.shape[1]
    row = lambda n: pl.BlockSpec((None, tm, n), lambda b, i: (b, i, 0))
    mod = pl.BlockSpec((None, 1, D), lambda b, i: (b, 0, 0))
    f = lambda n, dt: jax.ShapeDtypeStruct((B, L, n), dt)
    vt_spec = [pl.BlockSpec((None, WA, tm), lambda b, i: (b, 0, i))] if emit_vt else []
    vt_shape = [jax.ShapeDtypeStruct((B, WA, L), BF16)] if emit_vt else []
    return pl.pallas_call(
        _inproj_kernel,
        grid=(B, L // tm),
        in_specs=[row(D), mod, mod,
                  pl.BlockSpec((1, D), lambda b, i: (0, 0)),
                  pl.BlockSpec((D, d_in), lambda b, i: (0, 0)),
                  pl.BlockSpec((tm, DKR), lambda b, i: (i, 0)),
                  pl.BlockSpec((tm, DKR), lambda b, i: (i, 0))],
        out_specs=[row(512), row(512), row(512), row(512), row(512), row(512), row(512), row(512),
                   row(512), row(1024), row(1024)] + vt_spec,
        out_shape=[f(512, F32), f(512, F32), f(512, BF16), f(512, BF16), f(512, BF16),
                   f(512, BF16), f(512, F32), f(512, BF16), f(512, F32), f(1024, F32), f(1024, F32)]
        + vt_shape,
        compiler_params=_params(("arbitrary", "arbitrary")),
        name="in_proj",
    )(x, sh1.reshape(B, 1, D), sc1.reshape(B, 1, D), norm1.reshape(1, D), w_in_b, cos2, sin2)


def _t5_bucket(rel):
    nb = N_BUCKETS // 2
    ret = jnp.where(rel > 0, nb, 0)
    n = jnp.abs(rel)
    max_exact = nb // 2
    nf = jnp.maximum(n, max_exact).astype(F32)
    large = max_exact + (jnp.log(nf / max_exact) / math.log(MAX_DIST / max_exact)
                         * (nb - max_exact)).astype(jnp.int32)
    large = jnp.minimum(large, nb - 1)
    return ret + jnp.where(n < max_exact, n, large)


def _bias_from_buckets(bkt, rb_ref, h, shift):
    val = jnp.where(bkt < 0, NEG, 0.0).astype(F32)
    for n in range(N_BUCKETS):
        val = jnp.where(bkt == n, (rb_ref[n, h] - shift) * LOG2E, val)
    return val


def _lam_value(lv_ref, lam_init):
    lv = lv_ref[...]
    a = jnp.sum(lv[0:1] * lv[1:2], axis=-1, keepdims=True)
    b = jnp.sum(lv[2:3] * lv[3:4], axis=-1, keepdims=True)
    return jnp.exp(a) - jnp.exp(b) + lam_init


def _block_diag_q(q):
    lane = lax.broadcasted_iota(jnp.int32, q.shape, 1)
    zero = jnp.zeros_like(q)
    return jnp.concatenate([jnp.where(lane < DA, q, zero), jnp.where(lane >= DA, q, zero)], axis=0)


def _attn_kernel(rb_ref, lv_ref, q_ref, k_ref, vt_ref, bkt_ref, o_ref,
                 bias_sc, m_sc, acc_sc, va_sc, *, tq, tb, nsub, lam_init):
    h = pl.program_id(1)
    i = pl.program_id(2)

    @pl.when(i == 0)
    def _():
        far = rb_ref[N_BUCKETS // 2 - 1, h]
        for t in range(2):
            bias_sc[t] = _bias_from_buckets(bkt_ref[t], rb_ref, h, far)
        va_sc[0:DVA, :] = vt_ref[...]
        va_sc[DVA:, :] = jnp.ones((BF16_ROWS, va_sc.shape[1]), BF16)

    qt = q_ref[...].astype(F32).T
    drow = lax.broadcasted_iota(jnp.int32, qt.shape, 0)
    qbd_t = jnp.concatenate([jnp.where(drow < DA, qt, 0.0), jnp.where(drow >= DA, qt, 0.0)],
                            axis=1).astype(BF16)
    m_sc[...] = jnp.full(m_sc.shape, NEG, F32)
    acc_sc[...] = jnp.zeros(acc_sc.shape, F32)

    def scores(off, tk, bias=None, nvalid=None):
        s = _dot(k_ref[pl.ds(off, tk), :], qbd_t)
        if bias is not None:
            s = s + jnp.concatenate([bias, bias], axis=1)
        if nvalid is not None:
            row = lax.broadcasted_iota(jnp.int32, (tk, 1), 0)
            s = jnp.where(row < nvalid, s, NEG)
        return s

    def absorb(s, off, tk):
        m_prev = m_sc[...]
        m_new = jnp.maximum(m_prev, jnp.max(s, axis=0, keepdims=True))
        alpha = jnp.exp2(m_prev - m_new)
        p = jnp.exp2(s - m_new).astype(BF16)
        acc_sc[...] = alpha * acc_sc[...] + _dot(va_sc[:, pl.ds(off, tk)], p)
        m_sc[...] = m_new

    def tiles(specs):
        ss = [scores(*sp) for sp in specs]
        for s, sp in zip(ss, specs):
            absorb(s, sp[0], sp[1])

    ntile = jnp.maximum(i - 1, 0) * (tq // tb)
    nfull = ntile // nsub

    def run(first_tile, n):
        off = first_tile * tb
        tiles([(pl.multiple_of(off + j * tb, tb), tb) for j in range(n)])

    def far_body(t, carry):
        run(t * nsub, nsub)
        return carry

    lax.fori_loop(0, nfull, far_body, 0)
    done = nfull * nsub
    n = nsub // 2
    while n >= 1:
        take = ((ntile - done) // n) > 0

        @pl.when(take)
        def _(done=done, n=n):
            run(done, n)

        done = done + jnp.where(take, n, 0)
        n //= 2

    @pl.when(i >= 1)
    def _():
        off = pl.multiple_of((i - 1) * tq, tq)
        tiles([(off, tq, bias_sc[1]), (pl.multiple_of(off + tq, tq), tq, bias_sc[0])])

    @pl.when(i == 0)
    def _():
        tiles([(0, tq, bias_sc[0])])

    lam = _lam_value(lv_ref, lam_init)
    o = acc_sc[0:DVA, :] / acc_sc[DVA:DVA + 1, :]
    o_ref[...] = (o[:, 0:tq] - lam * o[:, tq:2 * tq]).T


def attn_prompt(qab, kab, vt, rel_bias, lamv, lam_init, tq=256, tb=256, nsub=8):
    B, S, _ = qab.shape
    tb = min(tb, S // (2 * nsub))
    assert S % (nsub * tb) == 0 and (nsub * tb) % tq == 0
    c = jnp.arange(tq, dtype=jnp.int32)[:, None]
    r = jnp.arange(tq, dtype=jnp.int32)[None, :]
    diag = jnp.where((c // CHUNK) <= (r // CHUNK), _t5_bucket(c - r), -1)
    prev = _t5_bucket(c - r - tq)
    bkt = jnp.stack([diag, prev]).astype(jnp.int32)
    smem = pl.BlockSpec(memory_space=pltpu.SMEM)
    return pl.pallas_call(
        functools.partial(_attn_kernel, tq=tq, tb=tb, nsub=nsub, lam_init=lam_init),
        grid=(B, HA, S // tq),
        in_specs=[smem,
                  pl.BlockSpec((4, DA), lambda b, h, i: (0, 0)),
                  pl.BlockSpec((None, tq, DVA), lambda b, h, i: (b, i, h)),
                  pl.BlockSpec((None, S, DVA), lambda b, h, i: (b, 0, h)),
                  pl.BlockSpec((None, DVA, S), lambda b, h, i: (b, h, 0)),
                  pl.BlockSpec((2, tq, tq), lambda b, h, i: (0, 0, 0))],
        out_specs=pl.BlockSpec((None, tq, DVA), lambda b, h, i: (b, i, h)),
        out_shape=jax.ShapeDtypeStruct((B, S, WA), F32),
        scratch_shapes=[pltpu.VMEM((2, tq, tq), F32),
                        pltpu.VMEM((1, 2 * tq), F32),
                        pltpu.VMEM((DVA + BF16_ROWS, 2 * tq), F32),
                        pltpu.VMEM((DVA + BF16_ROWS, S), BF16)],
        compiler_params=_params(("arbitrary", "arbitrary", "arbitrary")),
        name="attn_prompt",
    )(rel_bias, lamv, qab, kab, vt, bkt)


def _attn_small_kernel(rb_ref, lv_ref, q_ref, k_ref, v_ref, bkt_ref, o_ref, *, lq, lam_init):
    h = pl.program_id(1)
    bias = _bias_from_buckets(bkt_ref[...], rb_ref, h, 0.0)
    qbd = _block_diag_q(q_ref[...])
    s = _dot_nt(qbd, k_ref[...]) + jnp.concatenate([bias, bias], axis=0)
    m = jnp.max(s, axis=-1, keepdims=True)
    p = jnp.exp2(s - m)
    l = jnp.sum(p, axis=-1, keepdims=True)
    o = _dot(p.astype(BF16), v_ref[...]) / l
    lam = _lam_value(lv_ref, lam_init)
    o_ref[...] = o[0:lq] - lam * o[lq:2 * lq]


def attn_small(qab, k_all, v_all, q_pos, k_pos, k_valid, rel_bias, lamv, lam_init):
    B, Lq, _ = qab.shape
    Lk = k_all.shape[1]
    visible = ((k_pos[None, :] // CHUNK) <= (q_pos[:, None] // CHUNK)) & k_valid[None, :]
    bkt = jnp.where(visible, _t5_bucket(k_pos[None, :] - q_pos[:, None]), -1).astype(jnp.int32)
    smem = pl.BlockSpec(memory_space=pltpu.SMEM)
    return pl.pallas_call(
        functools.partial(_attn_small_kernel, lq=Lq, lam_init=lam_init),
        grid=(B, HA),
        in_specs=[smem,
                  pl.BlockSpec((4, DA), lambda b, h: (0, 0)),
                  pl.BlockSpec((None, Lq, DVA), lambda b, h: (b, 0, h)),
                  pl.BlockSpec((None, Lk, DVA), lambda b, h: (b, 0, h)),
                  pl.BlockSpec((None, Lk, DVA), lambda b, h: (b, 0, h)),
                  pl.BlockSpec((Lq, Lk), lambda b, h: (0, 0))],
        out_specs=pl.BlockSpec((None, Lq, DVA), lambda b, h: (b, 0, h)),
        out_shape=jax.ShapeDtypeStruct((B, Lq, WA), F32),
        compiler_params=_params(("arbitrary", "arbitrary")),
        name="attn_sample",
    )(rel_bias, lamv, qab, k_all, v_all, bkt)


def _ret_kernel(lg_ref, q_ref, k_ref, v_ref, s0_ref, o_ref, so_ref, state_sc, decay_sc, *, C):
    c = pl.program_id(1)

    @pl.when(c == 0)
    def _():
        state_sc[...] = s0_ref[...]
        r = lax.broadcasted_iota(jnp.int32, (C, C), 0)
        cc = lax.broadcasted_iota(jnp.int32, (C, C), 1)
        diff = (r - cc).astype(F32)
        for h in range(HR):
            decay_sc[h] = jnp.where(diff >= 0, jnp.exp(jnp.maximum(diff, 0.0) * lg_ref[h]), 0.0)

    n = lax.broadcasted_iota(jnp.int32, (C, 1), 0).astype(F32)
    for h in range(HR):
        lg = lg_ref[h]
        cols = slice(h * DKR, (h + 1) * DKR)
        xi = jnp.exp((n + 1.0) * lg)
        zeta = jnp.exp((C - 1.0 - n) * lg)
        q = q_ref[:, cols]
        k = k_ref[:, cols]
        v = v_ref[:, cols]
        state = state_sc[h]
        scores = _dot_nt(q, k.astype(BF16)) * decay_sc[h]
        intra = _dot(scores.astype(BF16), v)
        cross = _dot(q, state.astype(BF16)) * xi
        o_ref[:, cols] = intra + cross
        kz = (k * zeta).astype(BF16)
        state_sc[h] = jnp.exp(C * lg) * state + _dot_tn(kz, v)

    @pl.when(c == pl.num_programs(1) - 1)
    def _():
        so_ref[...] = state_sc[...]


def retention(qrb, kr, vrb, state0, C):
    B, L, _ = qrb.shape
    lg = jnp.log(1.0 - 2.0 ** (-5.0 - jnp.arange(HR, dtype=F32)))
    blk = pl.BlockSpec((None, C, WR), lambda b, c: (b, c, 0))
    st = pl.BlockSpec((None, HR, DKR, DVR), lambda b, c: (b, 0, 0, 0))
    return pl.pallas_call(
        functools.partial(_ret_kernel, C=C),
        grid=(B, L // C),
        in_specs=[pl.BlockSpec(memory_space=pltpu.SMEM), blk, blk, blk, st],
        out_specs=[blk, st],
        out_shape=[jax.ShapeDtypeStruct((B, L, WR), F32),
                   jax.ShapeDtypeStruct((B, HR, DKR, DVR), F32)],
        scratch_shapes=[pltpu.VMEM((HR, DKR, DVR), F32), pltpu.VMEM((HR, C, C), F32)],
        compiler_params=_params(("arbitrary", "arbitrary")),
        name="retention",
    )(lg, qrb, kr, vrb, state0)


def _outmix_kernel(x_ref, oa_ref, or_ref, gr_ref, ga_ref, gb_ref, g1_ref, sh2_ref, sc2_ref, n2_ref,
                   sa_ref, sr_ref, wba_ref, wbr_ref, wo_ref, wpq_ref,
                   x1_ref, h2_ref, qp_ref, *, lam_init, h2_transposed):
    sa = sa_ref[...] * 1.0
    sr = sr_ref[...]
    gr = gr_ref[...]
    silu_gr = gr * jax.nn.sigmoid(gr)
    ya_parts, yr_parts = [], []
    for hh in range(HA):
        sl = slice(hh * DVA, (hh + 1) * DVA)
        ya_parts.append(_rms_rows(oa_ref[:, sl]) * sa * (1.0 - lam_init))
        yr_parts.append(silu_gr[:, sl] * (_rms_rows(or_ref[:, sl]) * sr))
    ya = jnp.concatenate(ya_parts, axis=-1).astype(BF16)
    yr = jnp.concatenate(yr_parts, axis=-1).astype(BF16)
    y = (jax.nn.sigmoid(ga_ref[...]) * _dot(ya, wba_ref[...])
         + jax.nn.sigmoid(gb_ref[...]) * _dot(yr, wbr_ref[...]))
    out = _dot(y.astype(BF16), wo_ref[...])
    x1 = x_ref[...] + g1_ref[...] * out
    x1_ref[...] = x1
    h2f = _rms_rows(x1) * n2_ref[...] * (1.0 + sc2_ref[...]) + sh2_ref[...]
    h2 = h2f.astype(BF16)
    h2_ref[...] = h2f.T.astype(BF16) if h2_transposed else h2
    qp_ref[...] = _dot(h2, wpq_ref[...]).astype(BF16)


def out_mix(x, oa, orr, gr, ga, gb, g1, sh2, sc2, norm2, subln_a, subln_r,
            w_ba_b, w_br_b, w_o_b, w_pq_b, lam_init, tm):
    B, L, D = x.shape
    nq = w_pq_b.shape[1]
    row = lambda n: pl.BlockSpec((None, tm, n), lambda b, i: (b, i, 0))
    mod = pl.BlockSpec((None, 1, D), lambda b, i: (b, 0, 0))
    full = lambda a: pl.BlockSpec(a.shape, lambda b, i: (0,) * a.ndim)
    n2 = norm2.reshape(1, D)
    sa = subln_a.reshape(1, DVA)
    sr = subln_r.reshape(1, DVR)
    h2_transposed = tm % LANES == 0
    nt = L // tm
    if h2_transposed:
        h2_spec = pl.BlockSpec((D, tm), lambda b, i: (0, b * nt + i))
        h2_shape = jax.ShapeDtypeStruct((D, B * L), BF16)
    else:
        h2_spec = row(D)
        h2_shape = jax.ShapeDtypeStruct((B, L, D), BF16)
    x1, h2, qp = pl.pallas_call(
        functools.partial(_outmix_kernel, lam_init=lam_init, h2_transposed=h2_transposed),
        grid=(B, nt),
        in_specs=[row(D), row(WA), row(WR), row(WR), row(D), row(D), mod, mod, mod,
                  full(n2), full(sa), full(sr), full(w_ba_b), full(w_br_b), full(w_o_b), full(w_pq_b)],
        out_specs=[row(D), h2_spec, row(nq)],
        out_shape=[jax.ShapeDtypeStruct((B, L, D), F32), h2_shape,
                   jax.ShapeDtypeStruct((B, L, nq), BF16)],
        compiler_params=_params(("arbitrary", "arbitrary")),
        name="out_mix",
    )(x, oa, orr, gr, ga, gb, g1.reshape(B, 1, D), sh2.reshape(B, 1, D), sc2.reshape(B, 1, D),
      n2, sa, sr, w_ba_b, w_br_b, w_o_b, w_pq_b)
    h2t = h2 if h2_transposed else h2.reshape(B * L, D).T
    return x1, h2t, qp


UNRANKED = float(PEER_TOPK * PEER_TOPK)


def _topk_rows(s, k, break_ties):
    n = s.shape[0]
    iota = lax.broadcasted_iota(jnp.int32, s.shape, 0).astype(F32)
    work = s
    rank = jnp.full(s.shape, UNRANKED, F32)
    vals = []
    for r in range(k):
        m = jnp.max(work, axis=0, keepdims=True)
        sel = work == m
        if break_ties:
            idx = jnp.min(jnp.where(sel, iota, float(n)), axis=0, keepdims=True)
            sel = iota == idx
        rank = jnp.where(sel, float(r), rank)
        work = jnp.where(sel, -jnp.inf, work)
        vals.append(m)
    return vals, rank


def _ranked_count(rank):
    return jnp.sum((rank < UNRANKED).astype(F32), axis=0, keepdims=True)


SUBLANES = 8
_HEAD_A = SUBLANES
_CAND_NB = [PEER_TOPK] + [SUBLANES] * (_HEAD_A - 1)


def _route_chunk(s1, s2, break_ties):
    v1, rank1 = _topk_rows(s1, PEER_TOPK, break_ties)
    v2, rank2 = _topk_rows(s2, PEER_TOPK, break_ties)
    v2m = jnp.concatenate(v2, axis=0)
    v1t = jnp.concatenate(v1[_HEAD_A:], axis=0)
    blocks = [v1[a] + v2m[0:nb] for a, nb in enumerate(_CAND_NB)] + [v1t + v2[0]]
    cand = jnp.concatenate(blocks, axis=0)
    _, crank = _topk_rows(cand, PEER_TOPK, break_ties)
    k = float(PEER_TOPK)
    bad = ((_ranked_count(rank1) != k) | (_ranked_count(rank2) != k) | (_ranked_count(crank) != k))
    nbad = jnp.sum(bad.astype(jnp.int32))
    sel = crank < UNRANKED
    cmax = v1[0] + v2[0]
    z = jnp.sum(jnp.where(sel, jnp.exp(cand - cmax), 0.0), axis=0, keepdims=True)
    self32 = sel.astype(F32)
    cidx = jnp.zeros(s1.shape, F32)
    lo = 0
    for a, nb in enumerate(_CAND_NB):
        cnt = jnp.sum(self32[lo:lo + nb], axis=0, keepdims=True)
        cidx = jnp.where(rank1 == float(a), cnt, cidx)
        lo += nb
    for a in range(_HEAD_A, PEER_TOPK):
        cidx = jnp.where(rank1 == float(a), self32[lo + a - _HEAD_A:lo + a - _HEAD_A + 1], cidx)
    e1 = jnp.exp(s1 - v1[0]) / z
    e2 = jnp.exp(s2 - v2[0])
    return (e1, e2, rank2, cidx), nbad


def _route_kernel(q_ref, keys_ref, e1_ref, e2_ref, r2_ref, c_ref, s1_sc, s2_sc):
    q = q_ref[...]
    s1_sc[...] = _dot_nt(keys_ref[0], q[:, 0:KEY_DIM])
    s2_sc[...] = _dot_nt(keys_ref[1], q[:, KEY_DIM:2 * KEY_DIM])

    def store(sl, res):
        e1, e2, rank2, cidx = res
        e1_ref[:, sl] = e1
        e2_ref[:, sl] = e2.astype(BF16)
        r2_ref[:, sl] = rank2.astype(BF16)
        c_ref[:, sl] = cidx

    group = 2
    assert s1_sc.shape[1] % (group * LANES) == 0

    def chunks(ci, carry):
        sls = [pl.ds(pl.multiple_of((ci * group + g) * LANES, LANES), LANES) for g in range(group)]
        fast = [_route_chunk(s1_sc[:, sl], s2_sc[:, sl], break_ties=False) for sl in sls]
        for sl, (res, _) in zip(sls, fast):
            store(sl, res)
        for sl, (_, nbad) in zip(sls, fast):
            @pl.when(nbad > 0)
            def _():
                store(sl, _route_chunk(s1_sc[:, sl], s2_sc[:, sl], break_ties=True)[0])
        return carry

    lax.fori_loop(0, s1_sc.shape[1] // (group * LANES), chunks, 0)


def peer_route(qp, keys_b, tm):
    T = qp.shape[0]
    out = pl.BlockSpec((None, N_KEYS, tm), lambda t, h: (h, 0, t))
    shp = lambda dt: jax.ShapeDtypeStruct((PEER_HEADS, N_KEYS, T), dt)
    return pl.pallas_call(
        _route_kernel,
        grid=(T // tm, PEER_HEADS),
        in_specs=[pl.BlockSpec((tm, 2 * KEY_DIM), lambda t, h: (t, h)),
                  pl.BlockSpec((2, N_KEYS, KEY_DIM), lambda t, h: (h, 0, 0))],
        out_specs=[out, out, out, out],
        out_shape=[shp(F32), shp(BF16), shp(BF16), shp(F32)],
        scratch_shapes=[pltpu.VMEM((N_KEYS, tm), F32), pltpu.VMEM((N_KEYS, tm), F32)],
        compiler_params=_params(("arbitrary", "arbitrary")),
        name="peer_route",
    )(qp, keys_b)


BF16_ROWS = 16


def _bcast_rows_bf16(row, n):
    t = row.shape[1]
    tile = jnp.broadcast_to(row, (BF16_ROWS, t)).astype(BF16)
    return jnp.broadcast_to(tile[None], (n // BF16_ROWS, BF16_ROWS, t)).reshape(n, t)


def _peer_kernel(h2_ref, u_ref, vt_ref, e1_ref, e2_ref, r2_ref, c_ref, x1_ref, g2_ref, fn_ref,
                 y_ref, acc_sc, *, ni):
    e = pl.program_id(1)

    @pl.when(e == 0)
    def _():
        acc_sc[...] = jnp.zeros(acc_sc.shape, F32)

    h2 = h2_ref[...]
    pair = 2 * N_KEYS
    a_pairs = [_dot(u_ref[p * pair:(p + 1) * pair, :], h2) for p in range(ni // 2)]
    for p in range(ni // 2):
        ws = []
        for jj in range(2):
            i = e * ni + 2 * p + jj
            a = a_pairs[p][jj * N_KEYS:(jj + 1) * N_KEYS]
            g = jnp.zeros(a.shape, BF16)
            for hh in range(PEER_HEADS):
                c_row = _bcast_rows_bf16(c_ref[hh, pl.ds(i, 1), :], N_KEYS)
                e1_row = _bcast_rows_bf16(e1_ref[hh, pl.ds(i, 1), :], N_KEYS)
                g = g + jnp.where(r2_ref[hh] < c_row, e2_ref[hh] * e1_row, jnp.zeros_like(g))
            act = 0.5 * a * (1.0 + lax.erf(a * (2.0 ** -0.5)))
            ws.append(g * act.astype(BF16))
        w = jnp.concatenate(ws, axis=0)
        acc_sc[...] += _dot(vt_ref[:, p * pair:(p + 1) * pair], w)

    @pl.when(e == pl.num_programs(1) - 1)
    def _():
        x2 = x1_ref[...] + g2_ref[...] * acc_sc[...].T
        y_ref[...] = _rms_rows(x2) * fn_ref[...]


def peer_experts(h2t, u_b, vt_b, e1, e2, r2, cc, x1, g2, final_norm, tm, ni=16):
    D, T = h2t.shape
    nb = ni * N_KEYS
    tok = pl.BlockSpec((tm, D), lambda t, e: (t, 0))
    gate = pl.BlockSpec((PEER_HEADS, N_KEYS, tm), lambda t, e: (0, 0, t))
    L = T // g2.shape[0]
    if L % tm == 0:
        g2_arg = g2.reshape(g2.shape[0], 1, D)
        g2_spec = pl.BlockSpec((None, 1, D), lambda t, e: (t // (L // tm), 0, 0))
    else:
        g2_arg = jnp.repeat(g2, L, axis=0)
        g2_spec = tok
    return pl.pallas_call(
        functools.partial(_peer_kernel, ni=ni),
        grid=(T // tm, u_b.shape[0] // nb),
        in_specs=[pl.BlockSpec((D, tm), lambda t, e: (0, t)),
                  pl.BlockSpec((nb, D), lambda t, e: (e, 0)),
                  pl.BlockSpec((D, nb), lambda t, e: (0, e)),
                  gate, gate, gate, gate, tok, g2_spec,
                  pl.BlockSpec((1, D), lambda t, e: (0, 0))],
        out_specs=tok,
        out_shape=jax.ShapeDtypeStruct((T, D), F32),
        scratch_shapes=[pltpu.VMEM((D, tm), F32)],
        compiler_params=_params(("arbitrary", "arbitrary")),
        name="peer_experts",
    )(h2t, u_b, vt_b, e1, e2, r2, cc, x1, g2_arg, final_norm.reshape(1, D))


def _rot_tables(pos):
    inv = 1.0 / (10000.0 ** jnp.linspace(0.0, 1.0, DKR // 2, dtype=F32))
    ang = pos[:, None].astype(F32) * inv[None, :]
    cos, sin = jnp.cos(ang), jnp.sin(ang)
    return jnp.concatenate([cos, cos], axis=-1), jnp.concatenate([-sin, sin], axis=-1)


def _pick_tile(n, pref):
    t = min(n, pref)
    assert n % t == 0, (n, t)
    return t


def _trunk(x, mods, pos, lam_init, lamv, rel_bias, w, attend, values_transposed, state0, ret_chunk,
           final_norm):
    (norm1, norm2, w_in_b, subln_a, subln_r, w_ba_b, w_br_b, w_o_b, w_pq_b, keys_b, u_b, vt_b) = w
    sh1, sc1, g1, sh2, sc2, g2 = mods
    B, L, D = x.shape
    T = B * L
    cos2, sin2 = _rot_tables(pos)
    tm = _pick_tile(L, 256)
    ka, va, qab, kab, vab, qrb, kr, vrb, gr, ga, gb, *vt = in_proj(
        x, sh1, sc1, norm1, w_in_b, cos2, sin2, tm, emit_vt=values_transposed)
    oa = attend(qab, kab, vt[0] if values_transposed else vab)
    orr, st = retention(qrb, kr, vrb, state0, ret_chunk)
    x1, h2t, qp = out_mix(x, oa, orr, gr, ga, gb, g1, sh2, sc2, norm2, subln_a, subln_r,
                          w_ba_b, w_br_b, w_o_b, w_pq_b, lam_init, tm)
    e1, e2, r2, cc = peer_route(qp.reshape(T, -1), keys_b, _pick_tile(T, 512))
    y = peer_experts(h2t, u_b, vt_b, e1, e2, r2, cc, x1.reshape(T, D), g2,
                     final_norm, _pick_tile(T, 512))
    return y.reshape(B, L, D), ka.reshape(B, L, HA, 2 * DA), va.reshape(B, L, HA, DVA), st


def kernel(x_prompt, x_sample, cache_k, cache_v, state_ret, c_prompt, c_sample, w_ada, b_ada, norm1,
           norm2, w_in, lam_q1, lam_k1, lam_q2, lam_k2, subln_a, subln_r, w_ba, w_br, w_o, rel_bias,
           w_pq, peer_keys, peer_u, peer_v, final_norm):
    depth = w_ada.shape[0]
    assert depth == 1, "the fused final norm assumes a single layer"
    Bp, Lp, D = x_prompt.shape
    Bs, Ls, _ = x_sample.shape
    past = cache_k.shape[2]
    pos_p = jnp.arange(Lp, dtype=jnp.int32)
    pos_s = past + jnp.arange(Ls, dtype=jnp.int32)
    lk = past + Ls
    lk_pad = -(-lk // LANES) * LANES
    k_pos_s = jnp.arange(lk_pad, dtype=jnp.int32)
    k_valid_s = k_pos_s < lk

    l = 0
    lam_init = 0.8 - 0.6 * math.exp(-0.3 * l)
    lamv = jnp.stack([lam_q1[l], lam_k1[l], lam_q2[l], lam_k2[l]]).astype(F32)
    mod = ada_mod(jnp.concatenate([c_prompt, c_sample], axis=0), w_ada[l], b_ada[l])
    mods = jnp.split(mod, 6, axis=-1)
    mods_p = [m[:Bp] for m in mods]
    mods_s = [m[Bp:] for m in mods]
    w = (norm1[l], norm2[l], w_in[l].astype(BF16), subln_a[l], subln_r[l], w_ba[l].astype(BF16),
         w_br[l].astype(BF16), w_o[l].astype(BF16), w_pq[l].astype(BF16),
         peer_keys[l].reshape(PEER_HEADS * 2, N_KEYS, KEY_DIM).astype(BF16),
         peer_u[l].astype(BF16), peer_v[l].T.astype(BF16))

    attend_p = lambda q, k, v: attn_prompt(q, k, v, rel_bias, lamv, lam_init)
    zero_state = jnp.zeros((Bp, HR, DKR, DVR), F32)
    yp, kp, vp, sp = _trunk(x_prompt, mods_p, pos_p, lam_init, lamv, rel_bias, w, attend_p, True,
                            zero_state, _pick_tile(Lp, 256), final_norm)

    def attend_s(q, k, v):
        padk = jnp.zeros((Bs, lk_pad - lk, WA), BF16)
        kc = cache_k[l].reshape(Bs, past, HA * 2 * DA).astype(BF16)
        vc = cache_v[l].reshape(Bs, past, WA).astype(BF16)
        k_all = jnp.concatenate([kc, k, padk], axis=1)
        v_all = jnp.concatenate([vc, v, padk], axis=1)
        return attn_small(q, k_all, v_all, pos_s, k_pos_s, k_valid_s, rel_bias, lamv, lam_init)

    ys, ks, vs, ss = _trunk(x_sample, mods_s, pos_s, lam_init, lamv, rel_bias, w, attend_s, False,
                            state_ret[l].astype(F32), Ls, final_norm)
    return (yp, ys, kp[None], vp[None], sp[None], ks[None], vs[None], ss[None])
```

```python
import functools
import math

import jax
import jax.numpy as jnp
from jax import lax
from jax.experimental import pallas as pl
from jax.experimental.pallas import tpu as pltpu

F32 = jnp.float32
BF16 = jnp.bfloat16

CHUNK = 64
HA = 4
DA = 64
DVA = 2 * DA
HR = 4
DKR = 128
DVR = 128
N_BUCKETS = 32
MAX_DIST = 128
PEER_HEADS = 8
N_KEYS = 128
KEY_DIM = 128
PEER_TOPK = 16
EPS = 1e-6
WA = HA * DVA
WR = HR * DVR
NEG = -1e30
LOG2E = math.log2(math.e)
LANES = 128
VMEM_LIMIT = 56 * 1024 * 1024


def _params(sem, vmem=VMEM_LIMIT):
    return pltpu.CompilerParams(dimension_semantics=sem, vmem_limit_bytes=vmem)


def _dot(a, b):
    return jnp.dot(a, b, preferred_element_type=F32)


def _dot_nt(a, b):
    return lax.dot_general(a, b, (((1,), (1,)), ((), ())), preferred_element_type=F32)


def _dot_tn(a, b):
    return lax.dot_general(a, b, (((0,), (0,)), ((), ())), preferred_element_type=F32)


def _rms_rows(x):
    return x * lax.rsqrt(jnp.mean(x * x, axis=-1, keepdims=True) + EPS)


def _ada_kernel(c_ref, w_ref, b_ref, o_ref):
    c = c_ref[...]
    s = (c * jax.nn.sigmoid(c)).astype(BF16)
    o_ref[...] = _dot(s, w_ref[...].astype(BF16)) + b_ref[...]


def ada_mod(c, w_ada, b_ada, tn=1024):
    nb, d = c.shape
    n = w_ada.shape[1]
    return pl.pallas_call(
        _ada_kernel,
        grid=(n // tn,),
        in_specs=[pl.BlockSpec((nb, d), lambda j: (0, 0)),
                  pl.BlockSpec((d, tn), lambda j: (0, j)),
                  pl.BlockSpec((1, tn), lambda j: (0, j))],
        out_specs=pl.BlockSpec((nb, tn), lambda j: (0, j)),
        out_shape=jax.ShapeDtypeStruct((nb, n), F32),
        compiler_params=_params(("arbitrary",)),
        name="ada_mod",
    )(c, w_ada, b_ada.reshape(1, n))


def _inproj_kernel(x_ref, sh_ref, sc_ref, n1_ref, w_ref, cos_ref, sin_ref,
                   ka_ref, va_ref, qab_ref, kab_ref, vab_ref, qrb_ref, kr_ref, vrb_ref,
                   gr_ref, ga_ref, gb_ref, vt_ref=None):
    x = x_ref[...]
    h = _rms_rows(x) * n1_ref[...] * (1.0 + sc_ref[...]) + sh_ref[...]
    hb = h.astype(BF16)

    def proj(lo, n):
        return _dot(hb, w_ref[:, lo:lo + n])

    qa = proj(0, 512)
    qab_ref[...] = (qa * (DA ** -0.5 * LOG2E)).astype(BF16)
    ka = proj(512, 512)
    ka_ref[...] = ka
    kab_ref[...] = ka.astype(BF16)
    va = proj(1024, 512)
    va_ref[...] = va
    vab_ref[...] = va.astype(BF16)
    if vt_ref is not None:
        vt_ref[...] = va.T.astype(BF16)
    cos2 = cos_ref[...]
    sin2 = sin_ref[...]

    def rot(z):
        parts = []
        for hh in range(HR):
            zh = z[:, hh * DKR:(hh + 1) * DKR]
            parts.append(zh * cos2 + pltpu.roll(zh, DKR // 2, 1) * sin2)
        return jnp.concatenate(parts, axis=-1)

    qrb_ref[...] = rot(proj(1536, 512)).astype(BF16)
    kr_ref[...] = rot(proj(2048, 512)) * (DKR ** -0.5)
    vrb_ref[...] = proj(2560, 512).astype(BF16)
    gr_ref[...] = proj(3072, 512)
    ga_ref[...] = proj(3584, 1024)
    gb_ref[...] = proj(4608, 1024)


def in_proj(x, sh1, sc1, norm1, w_in_b, cos2, sin2, tm, emit_vt):
    B, L, D = x.shape
    d_in = w_in_b.shape[1]
    row = lambda n: pl.BlockSpec((None, tm, n), lambda b, i: (b, i, 0))
    mod = pl.BlockSpec((None, 1, D), lambda b, i: (b, 0, 0))
    f = lambda n, dt: jax.ShapeDtypeStruct((B, L, n), dt)
    vt_spec = [pl.BlockSpec((None, WA, tm), lambda b, i: (b, 0, i))] if emit_vt else []
    vt_shape = [jax.ShapeDtypeStruct((B, WA, L), BF16)] if emit_vt else []
    return pl.pallas_call(
        _inproj_kernel,
        grid=(B, L // tm),
        in_specs=[row(D), mod, mod,
                  pl.BlockSpec((1, D), lambda b, i: (0, 0)),
                  pl.BlockSpec((D, d_in), lambda b, i: (0, 0)),
                  pl.BlockSpec((tm, DKR), lambda b, i: (i, 0)),
                  pl.BlockSpec((tm, DKR), lambda b, i: (i, 0))],
        out_specs=[row(512), row(512), row(512), row(512), row(512), row(512), row(512), row(512),
                   row(512), row(1024), row(1024)] + vt_spec,
        out_shape=[f(512, F32), f(512, F32), f(512, BF16), f(512, BF16), f(512, BF16),
                   f(512, BF16), f(512, F32), f(512, BF16), f(512, F32), f(1024, F32), f(1024, F32)]
        + vt_shape,
        compiler_params=_params(("arbitrary", "arbitrary")),
        name="in_proj",
    )(x, sh1.reshape(B, 1, D), sc1.reshape(B, 1, D), norm1.reshape(1, D), w_in_b, cos2, sin2)


def _t5_bucket(rel):
    nb = N_BUCKETS // 2
    ret = jnp.where(rel > 0, nb, 0)
    n = jnp.abs(rel)
    max_exact = nb // 2
    nf = jnp.maximum(n, max_exact).astype(F32)
    large = max_exact + (jnp.log(nf / max_exact) / math.log(MAX_DIST / max_exact)
                         * (nb - max_exact)).astype(jnp.int32)
    large = jnp.minimum(large, nb - 1)
    return ret + jnp.where(n < max_exact, n, large)


def _bias_from_buckets(bkt, rb_ref, h, shift):
    val = jnp.where(bkt < 0, NEG, 0.0).astype(F32)
    for n in range(N_BUCKETS):
        val = jnp.where(bkt == n, (rb_ref[n, h] - shift) * LOG2E, val)
    return val


def _lam_value(lv_ref, lam_init):
    lv = lv_ref[...]
    a = jnp.sum(lv[0:1] * lv[1:2], axis=-1, keepdims=True)
    b = jnp.sum(lv[2:3] * lv[3:4], axis=-1, keepdims=True)
    return jnp.exp(a) - jnp.exp(b) + lam_init


def _block_diag_q(q):
    lane = lax.broadcasted_iota(jnp.int32, q.shape, 1)
    zero = jnp.zeros_like(q)
    return jnp.concatenate([jnp.where(lane < DA, q, zero), jnp.where(lane >= DA, q, zero)], axis=0)


ATTN_TAIL = 4


def _attn_kernel(rb_ref, lv_ref, q_ref, k_ref, vt_ref, bkt_ref, o_ref,
                 bias_sc, m_sc, acc_sc, va_sc, *, tq, tb, nsub, lam_init):
    h = pl.program_id(1)
    i = pl.program_id(2)

    @pl.when(i == 0)
    def _():
        far = rb_ref[N_BUCKETS // 2 - 1, h]
        for t in range(2):
            bias_sc[t] = _bias_from_buckets(bkt_ref[t], rb_ref, h, far)
        va_sc[0:DVA, :] = vt_ref[...]
        va_sc[DVA:, :] = jnp.ones((BF16_ROWS, va_sc.shape[1]), BF16)

    qt = q_ref[...].astype(F32).T
    drow = lax.broadcasted_iota(jnp.int32, qt.shape, 0)
    qbd_t = jnp.concatenate([jnp.where(drow < DA, qt, 0.0), jnp.where(drow >= DA, qt, 0.0)],
                            axis=1).astype(BF16)
    m_sc[...] = jnp.full(m_sc.shape, NEG, F32)
    acc_sc[...] = jnp.zeros(acc_sc.shape, F32)

    def scores(off, tk, bias=None):
        s = _dot(k_ref[pl.ds(off, tk), :], qbd_t)
        if bias is not None:
            s = s + jnp.concatenate([bias, bias], axis=1)
        return s

    def absorb(s, off, tk):
        m_prev = m_sc[...]
        m_new = jnp.maximum(m_prev, jnp.max(s, axis=0, keepdims=True))
        alpha = jnp.exp2(m_prev - m_new)
        p = jnp.exp2(s - m_new).astype(BF16)
        acc_sc[...] = alpha * acc_sc[...] + _dot(va_sc[:, pl.ds(off, tk)], p)
        m_sc[...] = m_new

    def tiles(specs):
        ss = [scores(*sp) for sp in specs]
        for s, sp in zip(ss, specs):
            absorb(s, sp[0], sp[1])

    assert tb == tq
    ntile = jnp.maximum(i - 1, 0)
    nfull = ntile // nsub

    def far(first_tile, n):
        return [(pl.multiple_of((first_tile + j) * tb, tb), tb) for j in range(n)]

    def far_body(t, carry):
        tiles(far(t * nsub, nsub))
        return carry

    lax.fori_loop(0, nfull, far_body, 0)
    done = nfull * nsub
    n = nsub // 2
    while n >= ATTN_TAIL:
        take = ((ntile - done) // n) > 0

        @pl.when(take)
        def _(done=done, n=n):
            tiles(far(done, n))

        done = done + jnp.where(take, n, 0)
        n //= 2

    rest = ntile - done
    for r in range(ATTN_TAIL):
        @pl.when((i >= 1) & (rest == r))
        def _(r=r):
            off = pl.multiple_of((i - 1) * tq, tq)
            tiles(far(done, r) + [(off, tq, bias_sc[1]),
                                  (pl.multiple_of(off + tq, tq), tq, bias_sc[0])])

    @pl.when(i == 0)
    def _():
        tiles([(0, tq, bias_sc[0])])

    lam = _lam_value(lv_ref, lam_init)
    o = acc_sc[0:DVA, :] / acc_sc[DVA:DVA + 1, :]
    o_ref[...] = (o[:, 0:tq] - lam * o[:, tq:2 * tq]).T


def attn_prompt(qab, kab, vt, rel_bias, lamv, lam_init, tq=256, nsub=8):
    B, S, _ = qab.shape
    tb = tq
    assert S % tq == 0
    c = jnp.arange(tq, dtype=jnp.int32)[:, None]
    r = jnp.arange(tq, dtype=jnp.int32)[None, :]
    diag = jnp.where((c // CHUNK) <= (r // CHUNK), _t5_bucket(c - r), -1)
    prev = _t5_bucket(c - r - tq)
    bkt = jnp.stack([diag, prev]).astype(jnp.int32)
    smem = pl.BlockSpec(memory_space=pltpu.SMEM)
    return pl.pallas_call(
        functools.partial(_attn_kernel, tq=tq, tb=tb, nsub=nsub, lam_init=lam_init),
        grid=(B, HA, S // tq),
        in_specs=[smem,
                  pl.BlockSpec((4, DA), lambda b, h, i: (0, 0)),
                  pl.BlockSpec((None, tq, DVA), lambda b, h, i: (b, i, h)),
                  pl.BlockSpec((None, S, DVA), lambda b, h, i: (b, 0, h)),
                  pl.BlockSpec((None, DVA, S), lambda b, h, i: (b, h, 0)),
                  pl.BlockSpec((2, tq, tq), lambda b, h, i: (0, 0, 0))],
        out_specs=pl.BlockSpec((None, tq, DVA), lambda b, h, i: (b, i, h)),
        out_shape=jax.ShapeDtypeStruct((B, S, WA), F32),
        scratch_shapes=[pltpu.VMEM((2, tq, tq), F32),
                        pltpu.VMEM((1, 2 * tq), F32),
                        pltpu.VMEM((DVA + BF16_ROWS, 2 * tq), F32),
                        pltpu.VMEM((DVA + BF16_ROWS, S), BF16)],
        compiler_params=_params(("arbitrary", "arbitrary", "arbitrary")),
        name="attn_prompt",
    )(rel_bias, lamv, qab, kab, vt, bkt)


def _attn_small_kernel(rb_ref, lv_ref, q_ref, k_ref, v_ref, bkt_ref, o_ref, *, lq, lam_init):
    h = pl.program_id(1)
    bias = _bias_from_buckets(bkt_ref[...], rb_ref, h, 0.0)
    qbd = _block_diag_q(q_ref[...])
    s = _dot_nt(qbd, k_ref[...]) + jnp.concatenate([bias, bias], axis=0)
    m = jnp.max(s, axis=-1, keepdims=True)
    p = jnp.exp2(s - m)
    l = jnp.sum(p, axis=-1, keepdims=True)
    o = _dot(p.astype(BF16), v_ref[...]) / l
    lam = _lam_value(lv_ref, lam_init)
    o_ref[...] = o[0:lq] - lam * o[lq:2 * lq]


def attn_small(qab, k_all, v_all, q_pos, k_pos, k_valid, rel_bias, lamv, lam_init):
    B, Lq, _ = qab.shape
    Lk = k_all.shape[1]
    visible = ((k_pos[None, :] // CHUNK) <= (q_pos[:, None] // CHUNK)) & k_valid[None, :]
    bkt = jnp.where(visible, _t5_bucket(k_pos[None, :] - q_pos[:, None]), -1).astype(jnp.int32)
    smem = pl.BlockSpec(memory_space=pltpu.SMEM)
    return pl.pallas_call(
        functools.partial(_attn_small_kernel, lq=Lq, lam_init=lam_init),
        grid=(B, HA),
        in_specs=[smem,
                  pl.BlockSpec((4, DA), lambda b, h: (0, 0)),
                  pl.BlockSpec((None, Lq, DVA), lambda b, h: (b, 0, h)),
                  pl.BlockSpec((None, Lk, DVA), lambda b, h: (b, 0, h)),
                  pl.BlockSpec((None, Lk, DVA), lambda b, h: (b, 0, h)),
                  pl.BlockSpec((Lq, Lk), lambda b, h: (0, 0))],
        out_specs=pl.BlockSpec((None, Lq, DVA), lambda b, h: (b, 0, h)),
        out_shape=jax.ShapeDtypeStruct((B, Lq, WA), F32),
        compiler_params=_params(("arbitrary", "arbitrary")),
        name="attn_sample",
    )(rel_bias, lamv, qab, k_all, v_all, bkt)


def _ret_kernel(lg_ref, q_ref, k_ref, v_ref, s0_ref, o_ref, so_ref, state_sc, decay_sc, *, C):
    c = pl.program_id(1)

    @pl.when(c == 0)
    def _():
        state_sc[...] = s0_ref[...]
        r = lax.broadcasted_iota(jnp.int32, (C, C), 0)
        cc = lax.broadcasted_iota(jnp.int32, (C, C), 1)
        diff = (r - cc).astype(F32)
        for h in range(HR):
            decay_sc[h] = jnp.where(diff >= 0, jnp.exp(jnp.maximum(diff, 0.0) * lg_ref[h]), 0.0)

    n = lax.broadcasted_iota(jnp.int32, (C, 1), 0).astype(F32)
    for h in range(HR):
        lg = lg_ref[h]
        cols = slice(h * DKR, (h + 1) * DKR)
        xi = jnp.exp((n + 1.0) * lg)
        zeta = jnp.exp((C - 1.0 - n) * lg)
        q = q_ref[:, cols]
        k = k_ref[:, cols]
        v = v_ref[:, cols]
        state = state_sc[h]
        scores = _dot_nt(q, k.astype(BF16)) * decay_sc[h]
        intra = _dot(scores.astype(BF16), v)
        cross = _dot(q, state.astype(BF16)) * xi
        o_ref[:, cols] = intra + cross
        kz = (k * zeta).astype(BF16)
        state_sc[h] = jnp.exp(C * lg) * state + _dot_tn(kz, v)

    @pl.when(c == pl.num_programs(1) - 1)
    def _():
        so_ref[...] = state_sc[...]


def retention(qrb, kr, vrb, state0, C):
    B, L, _ = qrb.shape
    lg = jnp.log(1.0 - 2.0 ** (-5.0 - jnp.arange(HR, dtype=F32)))
    blk = pl.BlockSpec((None, C, WR), lambda b, c: (b, c, 0))
    st = pl.BlockSpec((None, HR, DKR, DVR), lambda b, c: (b, 0, 0, 0))
    return pl.pallas_call(
        functools.partial(_ret_kernel, C=C),
        grid=(B, L // C),
        in_specs=[pl.BlockSpec(memory_space=pltpu.SMEM), blk, blk, blk, st],
        out_specs=[blk, st],
        out_shape=[jax.ShapeDtypeStruct((B, L, WR), F32),
                   jax.ShapeDtypeStruct((B, HR, DKR, DVR), F32)],
        scratch_shapes=[pltpu.VMEM((HR, DKR, DVR), F32), pltpu.VMEM((HR, C, C), F32)],
        compiler_params=_params(("arbitrary", "arbitrary")),
        name="retention",
    )(lg, qrb, kr, vrb, state0)


def _outmix_kernel(x_ref, oa_ref, or_ref, gr_ref, ga_ref, gb_ref, g1_ref, sh2_ref, sc2_ref, n2_ref,
                   sa_ref, sr_ref, wba_ref, wbr_ref, wo_ref, wpq_ref,
                   x1_ref, h2_ref, qp_ref, *, lam_init, h2_transposed):
    sa = sa_ref[...] * 1.0
    sr = sr_ref[...]
    gr = gr_ref[...]
    silu_gr = gr * jax.nn.sigmoid(gr)
    ya_parts, yr_parts = [], []
    for hh in range(HA):
        sl = slice(hh * DVA, (hh + 1) * DVA)
        ya_parts.append(_rms_rows(oa_ref[:, sl]) * sa * (1.0 - lam_init))
        yr_parts.append(silu_gr[:, sl] * (_rms_rows(or_ref[:, sl]) * sr))
    ya = jnp.concatenate(ya_parts, axis=-1).astype(BF16)
    yr = jnp.concatenate(yr_parts, axis=-1).astype(BF16)
    y = (jax.nn.sigmoid(ga_ref[...]) * _dot(ya, wba_ref[...])
         + jax.nn.sigmoid(gb_ref[...]) * _dot(yr, wbr_ref[...]))
    out = _dot(y.astype(BF16), wo_ref[...])
    x1 = x_ref[...] + g1_ref[...] * out
    x1_ref[...] = x1
    h2f = _rms_rows(x1) * n2_ref[...] * (1.0 + sc2_ref[...]) + sh2_ref[...]
    h2 = h2f.astype(BF16)
    h2_ref[...] = h2f.T.astype(BF16) if h2_transposed else h2
    qp_ref[...] = _dot(h2, wpq_ref[...]).astype(BF16)


def out_mix(x, oa, orr, gr, ga, gb, g1, sh2, sc2, norm2, subln_a, subln_r,
            w_ba_b, w_br_b, w_o_b, w_pq_b, lam_init, tm):
    B, L, D = x.shape
    nq = w_pq_b.shape[1]
    row = lambda n: pl.BlockSpec((None, tm, n), lambda b, i: (b, i, 0))
    mod = pl.BlockSpec((None, 1, D), lambda b, i: (b, 0, 0))
    full = lambda a: pl.BlockSpec(a.shape, lambda b, i: (0,) * a.ndim)
    n2 = norm2.reshape(1, D)
    sa = subln_a.reshape(1, DVA)
    sr = subln_r.reshape(1, DVR)
    h2_transposed = tm % LANES == 0
    nt = L // tm
    if h2_transposed:
        h2_spec = pl.BlockSpec((D, tm), lambda b, i: (0, b * nt + i))
        h2_shape = jax.ShapeDtypeStruct((D, B * L), BF16)
    else:
        h2_spec = row(D)
        h2_shape = jax.ShapeDtypeStruct((B, L, D), BF16)
    x1, h2, qp = pl.pallas_call(
        functools.partial(_outmix_kernel, lam_init=lam_init, h2_transposed=h2_transposed),
        grid=(B, nt),
        in_specs=[row(D), row(WA), row(WR), row(WR), row(D), row(D), mod, mod, mod,
                  full(n2), full(sa), full(sr), full(w_ba_b), full(w_br_b), full(w_o_b), full(w_pq_b)],
        out_specs=[row(D), h2_spec, row(nq)],
        out_shape=[jax.ShapeDtypeStruct((B, L, D), F32), h2_shape,
                   jax.ShapeDtypeStruct((B, L, nq), BF16)],
        compiler_params=_params(("arbitrary", "arbitrary")),
        name="out_mix",
    )(x, oa, orr, gr, ga, gb, g1.reshape(B, 1, D), sh2.reshape(B, 1, D), sc2.reshape(B, 1, D),
      n2, sa, sr, w_ba_b, w_br_b, w_o_b, w_pq_b)
    h2t = h2 if h2_transposed else h2.reshape(B * L, D).T
    return x1, h2t, qp


UNRANKED = float(PEER_TOPK * PEER_TOPK)


def _topk_rows(s, k, break_ties):
    n = s.shape[0]
    iota = lax.broadcasted_iota(jnp.int32, s.shape, 0).astype(F32)
    work = s
    rank = jnp.full(s.shape, UNRANKED, F32)
    vals = []
    for r in range(k):
        m = jnp.max(work, axis=0, keepdims=True)
        sel = work == m
        if break_ties:
            idx = jnp.min(jnp.where(sel, iota, float(n)), axis=0, keepdims=True)
            sel = iota == idx
        rank = jnp.where(sel, float(r), rank)
        work = jnp.where(sel, -jnp.inf, work)
        vals.append(m)
    return vals, rank


def _ranked_count(rank):
    return jnp.sum((rank < UNRANKED).astype(F32), axis=0, keepdims=True)


SUBLANES = 8
_HEAD_A = SUBLANES
_CAND_NB = [PEER_TOPK] + [SUBLANES] * (_HEAD_A - 1)


def _route_chunk(s1, s2, break_ties):
    v1, rank1 = _topk_rows(s1, PEER_TOPK, break_ties)
    v2, rank2 = _topk_rows(s2, PEER_TOPK, break_ties)
    v2m = jnp.concatenate(v2, axis=0)
    v1t = jnp.concatenate(v1[_HEAD_A:], axis=0)
    blocks = [v1[a] + v2m[0:nb] for a, nb in enumerate(_CAND_NB)] + [v1t + v2[0]]
    cand = jnp.concatenate(blocks, axis=0)
    _, crank = _topk_rows(cand, PEER_TOPK, break_ties)
    k = float(PEER_TOPK)
    bad = ((_ranked_count(rank1) != k) | (_ranked_count(rank2) != k) | (_ranked_count(crank) != k))
    nbad = jnp.sum(bad.astype(jnp.int32))
    sel = crank < UNRANKED
    cmax = v1[0] + v2[0]
    z = jnp.sum(jnp.where(sel, jnp.exp(cand - cmax), 0.0), axis=0, keepdims=True)
    self32 = sel.astype(F32)
    cidx = jnp.zeros(s1.shape, F32)
    lo = 0
    for a, nb in enumerate(_CAND_NB):
        cnt = jnp.sum(self32[lo:lo + nb], axis=0, keepdims=True)
        cidx = jnp.where(rank1 == float(a), cnt, cidx)
        lo += nb
    for a in range(_HEAD_A, PEER_TOPK):
        cidx = jnp.where(rank1 == float(a), self32[lo + a - _HEAD_A:lo + a - _HEAD_A + 1], cidx)
    e1 = jnp.exp(s1 - v1[0]) / z
    e2 = jnp.exp(s2 - v2[0])
    return (e1, e2, rank2, cidx), nbad


def _route_kernel(q_ref, keys_ref, e1_ref, e2_ref, r2_ref, c_ref, s1_sc, s2_sc):
    q = q_ref[...]
    s1_sc[...] = _dot_nt(keys_ref[0], q[:, 0:KEY_DIM])
    s2_sc[...] = _dot_nt(keys_ref[1], q[:, KEY_DIM:2 * KEY_DIM])

    def store(sl, res):
        e1, e2, rank2, cidx = res
        e1_ref[:, sl] = e1
        e2_ref[:, sl] = e2.astype(BF16)
        r2_ref[:, sl] = rank2.astype(BF16)
        c_ref[:, sl] = cidx

    group = 2
    assert s1_sc.shape[1] % (group * LANES) == 0

    def chunks(ci, carry):
        sls = [pl.ds(pl.multiple_of((ci * group + g) * LANES, LANES), LANES) for g in range(group)]
        fast = [_route_chunk(s1_sc[:, sl], s2_sc[:, sl], break_ties=False) for sl in sls]
        for sl, (res, _) in zip(sls, fast):
            store(sl, res)
        for sl, (_, nbad) in zip(sls, fast):
            @pl.when(nbad > 0)
            def _():
                store(sl, _route_chunk(s1_sc[:, sl], s2_sc[:, sl], break_ties=True)[0])
        return carry

    lax.fori_loop(0, s1_sc.shape[1] // (group * LANES), chunks, 0)


def peer_route(qp, keys_b, tm):
    T = qp.shape[0]
    out = pl.BlockSpec((None, N_KEYS, tm), lambda t, h: (h, 0, t))
    shp = lambda dt: jax.ShapeDtypeStruct((PEER_HEADS, N_KEYS, T), dt)
    return pl.pallas_call(
        _route_kernel,
        grid=(T // tm, PEER_HEADS),
        in_specs=[pl.BlockSpec((tm, 2 * KEY_DIM), lambda t, h: (t, h)),
                  pl.BlockSpec((2, N_KEYS, KEY_DIM), lambda t, h: (h, 0, 0))],
        out_specs=[out, out, out, out],
        out_shape=[shp(F32), shp(BF16), shp(BF16), shp(F32)],
        scratch_shapes=[pltpu.VMEM((N_KEYS, tm), F32), pltpu.VMEM((N_KEYS, tm), F32)],
        compiler_params=_params(("arbitrary", "arbitrary")),
        name="peer_route",
    )(qp, keys_b)


BF16_ROWS = 16


def _bcast_rows_bf16(row, n):
    t = row.shape[1]
    tile = jnp.broadcast_to(row, (BF16_ROWS, t)).astype(BF16)
    return jnp.broadcast_to(tile[None], (n // BF16_ROWS, BF16_ROWS, t)).reshape(n, t)


def _peer_kernel(h2_ref, u_ref, vt_ref, e1_ref, e2_ref, r2_ref, c_ref, x1_ref, g2_ref, fn_ref,
                 y_ref, acc_sc, *, ni):
    e = pl.program_id(1)

    @pl.when(e == 0)
    def _():
        acc_sc[...] = jnp.zeros(acc_sc.shape, F32)

    h2 = h2_ref[...]
    pair = 2 * N_KEYS
    a_pairs = [_dot(u_ref[p * pair:(p + 1) * pair, :], h2) for p in range(ni // 2)]
    for p in range(ni // 2):
        ws = []
        for jj in range(2):
            i = e * ni + 2 * p + jj
            a = a_pairs[p][jj * N_KEYS:(jj + 1) * N_KEYS]
            g = jnp.zeros(a.shape, BF16)
            for hh in range(PEER_HEADS):
                c_row = _bcast_rows_bf16(c_ref[hh, pl.ds(i, 1), :], N_KEYS)
                e1_row = _bcast_rows_bf16(e1_ref[hh, pl.ds(i, 1), :], N_KEYS)
                g = g + jnp.where(r2_ref[hh] < c_row, e2_ref[hh] * e1_row, jnp.zeros_like(g))
            act = 0.5 * a * (1.0 + lax.erf(a * (2.0 ** -0.5)))
            ws.append(g * act.astype(BF16))
        w = jnp.concatenate(ws, axis=0)
        acc_sc[...] += _dot(vt_ref[:, p * pair:(p + 1) * pair], w)

    @pl.when(e == pl.num_programs(1) - 1)
    def _():
        x2 = x1_ref[...] + g2_ref[...] * acc_sc[...].T
        y_ref[...] = _rms_rows(x2) * fn_ref[...]


def peer_experts(h2t, u_b, vt_b, e1, e2, r2, cc, x1, g2, final_norm, tm, ni=16):
    D, T = h2t.shape
    nb = ni * N_KEYS
    tok = pl.BlockSpec((tm, D), lambda t, e: (t, 0))
    gate = pl.BlockSpec((PEER_HEADS, N_KEYS, tm), lambda t, e: (0, 0, t))
    L = T // g2.shape[0]
    if L % tm == 0:
        g2_arg = g2.reshape(g2.shape[0], 1, D)
        g2_spec = pl.BlockSpec((None, 1, D), lambda t, e: (t // (L // tm), 0, 0))
    else:
        g2_arg = jnp.repeat(g2, L, axis=0)
        g2_spec = tok
    return pl.pallas_call(
        functools.partial(_peer_kernel, ni=ni),
        grid=(T // tm, u_b.shape[0] // nb),
        in_specs=[pl.BlockSpec((D, tm), lambda t, e: (0, t)),
                  pl.BlockSpec((nb, D), lambda t, e: (e, 0)),
                  pl.BlockSpec((D, nb), lambda t, e: (0, e)),
                  gate, gate, gate, gate, tok, g2_spec,
                  pl.BlockSpec((1, D), lambda t, e: (0, 0))],
        out_specs=tok,
        out_shape=jax.ShapeDtypeStruct((T, D), F32),
        scratch_shapes=[pltpu.VMEM((D, tm), F32)],
        compiler_params=_params(("arbitrary", "arbitrary")),
        name="peer_experts",
    )(h2t, u_b, vt_b, e1, e2, r2, cc, x1, g2_arg, final_norm.reshape(1, D))


def _rot_tables(pos):
    inv = 1.0 / (10000.0 ** jnp.linspace(0.0, 1.0, DKR // 2, dtype=F32))
    ang = pos[:, None].astype(F32) * inv[None, :]
    cos, sin = jnp.cos(ang), jnp.sin(ang)
    return jnp.concatenate([cos, cos], axis=-1), jnp.concatenate([-sin, sin], axis=-1)


def _pick_tile(n, pref):
    t = min(n, pref)
    assert n % t == 0, (n, t)
    return t


def _trunk(x, mods, pos, lam_init, lamv, rel_bias, w, attend, values_transposed, state0, ret_chunk,
           final_norm):
    (norm1, norm2, w_in_b, subln_a, subln_r, w_ba_b, w_br_b, w_o_b, w_pq_b, keys_b, u_b, vt_b) = w
    sh1, sc1, g1, sh2, sc2, g2 = mods
    B, L, D = x.shape
    T = B * L
    cos2, sin2 = _rot_tables(pos)
    tm = _pick_tile(L, 256)
    ka, va, qab, kab, vab, qrb, kr, vrb, gr, ga, gb, *vt = in_proj(
        x, sh1, sc1, norm1, w_in_b, cos2, sin2, tm, emit_vt=values_transposed)
    oa = attend(qab, kab, vt[0] if values_transposed else vab)
    orr, st = retention(qrb, kr, vrb, state0, ret_chunk)
    x1, h2t, qp = out_mix(x, oa, orr, gr, ga, gb, g1, sh2, sc2, norm2, subln_a, subln_r,
                          w_ba_b, w_br_b, w_o_b, w_pq_b, lam_init, tm)
    e1, e2, r2, cc = peer_route(qp.reshape(T, -1), keys_b, _pick_tile(T, 512))
    y = peer_experts(h2t, u_b, vt_b, e1, e2, r2, cc, x1.reshape(T, D), g2,
                     final_norm, _pick_tile(T, 512))
    return y.reshape(B, L, D), ka.reshape(B, L, HA, 2 * DA), va.reshape(B, L, HA, DVA), st


def kernel(x_prompt, x_sample, cache_k, cache_v, state_ret, c_prompt, c_sample, w_ada, b_ada, norm1,
           norm2, w_in, lam_q1, lam_k1, lam_q2, lam_k2, subln_a, subln_r, w_ba, w_br, w_o, rel_bias,
           w_pq, peer_keys, peer_u, peer_v, final_norm):
    depth = w_ada.shape[0]
    assert depth == 1, "the fused final norm assumes a single layer"
    Bp, Lp, D = x_prompt.shape
    Bs, Ls, _ = x_sample.shape
    past = cache_k.shape[2]
    pos_p = jnp.arange(Lp, dtype=jnp.int32)
    pos_s = past + jnp.arange(Ls, dtype=jnp.int32)
    lk = past + Ls
    lk_pad = -(-lk // LANES) * LANES
    k_pos_s = jnp.arange(lk_pad, dtype=jnp.int32)
    k_valid_s = k_pos_s < lk

    l = 0
    lam_init = 0.8 - 0.6 * math.exp(-0.3 * l)
    lamv = jnp.stack([lam_q1[l], lam_k1[l], lam_q2[l], lam_k2[l]]).astype(F32)
    mod = ada_mod(jnp.concatenate([c_prompt, c_sample], axis=0), w_ada[l], b_ada[l])
    mods = jnp.split(mod, 6, axis=-1)
    mods_p = [m[:Bp] for m in mods]
    mods_s = [m[Bp:] for m in mods]
    w = (norm1[l], norm2[l], w_in[l].astype(BF16), subln_a[l], subln_r[l], w_ba[l].astype(BF16),
         w_br[l].astype(BF16), w_o[l].astype(BF16), w_pq[l].astype(BF16),
         peer_keys[l].reshape(PEER_HEADS * 2, N_KEYS, KEY_DIM).astype(BF16),
         peer_u[l].astype(BF16), peer_v[l].T.astype(BF16))

    attend_p = lambda q, k, v: attn_prompt(q, k, v, rel_bias, lamv, lam_init)
    zero_state = jnp.zeros((Bp, HR, DKR, DVR), F32)
    yp, kp, vp, sp = _trunk(x_prompt, mods_p, pos_p, lam_init, lamv, rel_bias, w, attend_p, True,
                            zero_state, _pick_tile(Lp, 256), final_norm)

    def attend_s(q, k, v):
        padk = jnp.zeros((Bs, lk_pad - lk, WA), BF16)
        kc = cache_k[l].reshape(Bs, past, HA * 2 * DA).astype(BF16)
        vc = cache_v[l].reshape(Bs, past, WA).astype(BF16)
        k_all = jnp.concatenate([kc, k, padk], axis=1)
        v_all = jnp.concatenate([vc, v, padk], axis=1)
        return attn_small(q, k_all, v_all, pos_s, k_pos_s, k_valid_s, rel_bias, lamv, lam_init)

    ys, ks, vs, ss = _trunk(x_sample, mods_s, pos_s, lam_init, lamv, rel_bias, w, attend_s, False,
                            state_ret[l].astype(F32), Ls, final_norm)
    return (yp, ys, kp[None], vp[None], sp[None], ks[None], vs[None], ss[None])
```

```python
import functools
import math

import jax
import jax.numpy as jnp
from jax import lax
from jax.experimental import pallas as pl
from jax.experimental.pallas import tpu as pltpu

F32 = jnp.float32
BF16 = jnp.bfloat16

CHUNK = 64
HA = 4
DA = 64
DVA = 2 * DA
HR = 4
DKR = 128
DVR = 128
N_BUCKETS = 32
MAX_DIST = 128
PEER_HEADS = 8
N_KEYS = 128
KEY_DIM = 128
PEER_TOPK = 16
EPS = 1e-6
WA = HA * DVA
WR = HR * DVR
NEG = -1e30
LOG2E = math.log2(math.e)
LANES = 128
VMEM_LIMIT = 56 * 1024 * 1024


def _params(sem, vmem=VMEM_LIMIT):
    return pltpu.CompilerParams(dimension_semantics=sem, vmem_limit_bytes=vmem)


def _dot(a, b):
    return jnp.dot(a, b, preferred_element_type=F32)


def _dot_nt(a, b):
    return lax.dot_general(a, b, (((1,), (1,)), ((), ())), preferred_element_type=F32)


def _dot_tn(a, b):
    return lax.dot_general(a, b, (((0,), (0,)), ((), ())), preferred_element_type=F32)


def _rms_rows(x):
    return x * lax.rsqrt(jnp.mean(x * x, axis=-1, keepdims=True) + EPS)


def _ada_kernel(c_ref, w_ref, b_ref, o_ref):
    c = c_ref[...]
    s = (c * jax.nn.sigmoid(c)).astype(BF16)
    o_ref[...] = _dot(s, w_ref[...].astype(BF16)) + b_ref[...]


def ada_mod(c, w_ada, b_ada, tn=1024):
    nb, d = c.shape
    n = w_ada.shape[1]
    return pl.pallas_call(
        _ada_kernel,
        grid=(n // tn,),
        in_specs=[pl.BlockSpec((nb, d), lambda j: (0, 0)),
                  pl.BlockSpec((d, tn), lambda j: (0, j)),
                  pl.BlockSpec((1, tn), lambda j: (0, j))],
        out_specs=pl.BlockSpec((nb, tn), lambda j: (0, j)),
        out_shape=jax.ShapeDtypeStruct((nb, n), F32),
        compiler_params=_params(("arbitrary",)),
        name="ada_mod",
    )(c, w_ada, b_ada.reshape(1, n))


def _inproj_kernel(x_ref, sh_ref, sc_ref, n1_ref, w_ref, cos_ref, sin_ref,
                   ka_ref, va_ref, qab_ref, kab_ref, vab_ref, qrb_ref, kr_ref, vrb_ref,
                   gr_ref, ga_ref, gb_ref, vt_ref=None):
    x = x_ref[...]
    h = _rms_rows(x) * n1_ref[...] * (1.0 + sc_ref[...]) + sh_ref[...]
    hb = h.astype(BF16)

    def proj(lo, n):
        return _dot(hb, w_ref[:, lo:lo + n])

    qa = proj(0, 512)
    qab_ref[...] = (qa * (DA ** -0.5 * LOG2E)).astype(BF16)
    ka = proj(512, 512)
    va = proj(1024, 512)
    for hh in range(HA):
        ka_ref[:, hh, :] = ka[:, hh * DVA:(hh + 1) * DVA]
        va_ref[:, hh, :] = va[:, hh * DVA:(hh + 1) * DVA]
    kab_ref[...] = ka.astype(BF16)
    vab_ref[...] = va.astype(BF16)
    if vt_ref is not None:
        vt_ref[...] = va.T.astype(BF16)
    cos2 = cos_ref[...]
    sin2 = sin_ref[...]

    def rot(z):
        parts = []
        for hh in range(HR):
            zh = z[:, hh * DKR:(hh + 1) * DKR]
            parts.append(zh * cos2 + pltpu.roll(zh, DKR // 2, 1) * sin2)
        return jnp.concatenate(parts, axis=-1)

    qrb_ref[...] = rot(proj(1536, 512)).astype(BF16)
    kr_ref[...] = rot(proj(2048, 512)) * (DKR ** -0.5)
    vrb_ref[...] = proj(2560, 512).astype(BF16)
    gr_ref[...] = proj(3072, 512)
    ga_ref[...] = proj(3584, 1024)
    gb_ref[...] = proj(4608, 1024)


def in_proj(x, sh1, sc1, norm1, w_in_b, cos2, sin2, tm, emit_vt):
    B, L, D = x.shape
    d_in = w_in_b.shape[1]
    row = lambda n: pl.BlockSpec((None, tm, n), lambda b, i: (b, i, 0))
    mod = pl.BlockSpec((None, 1, D), lambda b, i: (b, 0, 0))
    f = lambda n, dt: jax.ShapeDtypeStruct((B, L, n), dt)
    cache = pl.BlockSpec((None, tm, HA, DVA), lambda b, i: (b, i, 0, 0))
    cache_shape = jax.ShapeDtypeStruct((B, L, HA, DVA), F32)
    vt_spec = [pl.BlockSpec((None, WA, tm), lambda b, i: (b, 0, i))] if emit_vt else []
    vt_shape = [jax.ShapeDtypeStruct((B, WA, L), BF16)] if emit_vt else []
    return pl.pallas_call(
        _inproj_kernel,
        grid=(B, L // tm),
        in_specs=[row(D), mod, mod,
                  pl.BlockSpec((1, D), lambda b, i: (0, 0)),
                  pl.BlockSpec((D, d_in), lambda b, i: (0, 0)),
                  pl.BlockSpec((tm, DKR), lambda b, i: (i, 0)),
                  pl.BlockSpec((tm, DKR), lambda b, i: (i, 0))],
        out_specs=[cache, cache, row(512), row(512), row(512), row(512), row(512), row(512),
                   row(512), row(1024), row(1024)] + vt_spec,
        out_shape=[cache_shape, cache_shape, f(512, BF16), f(512, BF16), f(512, BF16),
                   f(512, BF16), f(512, F32), f(512, BF16), f(512, F32), f(1024, F32), f(1024, F32)]
        + vt_shape,
        compiler_params=_params(("arbitrary", "arbitrary")),
        name="in_proj",
    )(x, sh1.reshape(B, 1, D), sc1.reshape(B, 1, D), norm1.reshape(1, D), w_in_b, cos2, sin2)


def _t5_bucket(rel):
    nb = N_BUCKETS // 2
    ret = jnp.where(rel > 0, nb, 0)
    n = jnp.abs(rel)
    max_exact = nb // 2
    nf = jnp.maximum(n, max_exact).astype(F32)
    large = max_exact + (jnp.log(nf / max_exact) / math.log(MAX_DIST / max_exact)
                         * (nb - max_exact)).astype(jnp.int32)
    large = jnp.minimum(large, nb - 1)
    return ret + jnp.where(n < max_exact, n, large)


def _bias_from_buckets(bkt, rb_ref, h, shift):
    val = jnp.where(bkt < 0, NEG, 0.0).astype(F32)
    for n in range(N_BUCKETS):
        val = jnp.where(bkt == n, (rb_ref[n, h] - shift) * LOG2E, val)
    return val


def _lam_value(lv_ref, lam_init):
    lv = lv_ref[...]
    a = jnp.sum(lv[0:1] * lv[1:2], axis=-1, keepdims=True)
    b = jnp.sum(lv[2:3] * lv[3:4], axis=-1, keepdims=True)
    return jnp.exp(a) - jnp.exp(b) + lam_init


def _block_diag_q(q):
    lane = lax.broadcasted_iota(jnp.int32, q.shape, 1)
    zero = jnp.zeros_like(q)
    return jnp.concatenate([jnp.where(lane < DA, q, zero), jnp.where(lane >= DA, q, zero)], axis=0)


ATTN_TAIL = 4


def _attn_kernel(rb_ref, lv_ref, q_ref, k_ref, vt_ref, bkt_ref, o_ref,
                 bias_sc, m_sc, acc_sc, va_sc, *, tq, tb, nsub, lam_init):
    h = pl.program_id(1)
    i = pl.program_id(2)

    @pl.when(i == 0)
    def _():
        far = rb_ref[N_BUCKETS // 2 - 1, h]
        for t in range(2):
            bias_sc[t] = _bias_from_buckets(bkt_ref[t], rb_ref, h, far)
        va_sc[0:DVA, :] = vt_ref[...]
        va_sc[DVA:, :] = jnp.ones((BF16_ROWS, va_sc.shape[1]), BF16)

    qt = q_ref[...].astype(F32).T
    drow = lax.broadcasted_iota(jnp.int32, qt.shape, 0)
    qbd_t = jnp.concatenate([jnp.where(drow < DA, qt, 0.0), jnp.where(drow >= DA, qt, 0.0)],
                            axis=1).astype(BF16)
    m_sc[...] = jnp.full(m_sc.shape, NEG, F32)
    acc_sc[...] = jnp.zeros(acc_sc.shape, F32)

    def scores(off, tk, bias=None):
        s = _dot(k_ref[pl.ds(off, tk), :], qbd_t)
        if bias is not None:
            s = s + jnp.concatenate([bias, bias], axis=1)
        return s

    def absorb(s, off, tk):
        m_prev = m_sc[...]
        m_new = jnp.maximum(m_prev, jnp.max(s, axis=0, keepdims=True))
        alpha = jnp.exp2(m_prev - m_new)
        p = jnp.exp2(s - m_new).astype(BF16)
        acc_sc[...] = alpha * acc_sc[...] + _dot(va_sc[:, pl.ds(off, tk)], p)
        m_sc[...] = m_new

    def tiles(specs):
        ss = [scores(*sp) for sp in specs]
        for s, sp in zip(ss, specs):
            absorb(s, sp[0], sp[1])

    assert tb == tq
    ntile = jnp.maximum(i - 1, 0)
    nfull = ntile // nsub

    def far(first_tile, n):
        return [(pl.multiple_of((first_tile + j) * tb, tb), tb) for j in range(n)]

    def far_body(t, carry):
        tiles(far(t * nsub, nsub))
        return carry

    lax.fori_loop(0, nfull, far_body, 0)
    done = nfull * nsub
    n = nsub // 2
    while n >= ATTN_TAIL:
        take = ((ntile - done) // n) > 0

        @pl.when(take)
        def _(done=done, n=n):
            tiles(far(done, n))

        done = done + jnp.where(take, n, 0)
        n //= 2

    rest = ntile - done
    for r in range(ATTN_TAIL):
        @pl.when((i >= 1) & (rest == r))
        def _(r=r):
            off = pl.multiple_of((i - 1) * tq, tq)
            tiles(far(done, r) + [(off, tq, bias_sc[1]),
                                  (pl.multiple_of(off + tq, tq), tq, bias_sc[0])])

    @pl.when(i == 0)
    def _():
        tiles([(0, tq, bias_sc[0])])

    lam = _lam_value(lv_ref, lam_init)
    o = acc_sc[0:DVA, :] / acc_sc[DVA:DVA + 1, :]
    o_ref[...] = (o[:, 0:tq] - lam * o[:, tq:2 * tq]).T


def attn_prompt(qab, kab, vt, rel_bias, lamv, lam_init, tq=256, nsub=8):
    B, S, _ = qab.shape
    tb = tq
    assert S % tq == 0
    c = jnp.arange(tq, dtype=jnp.int32)[:, None]
    r = jnp.arange(tq, dtype=jnp.int32)[None, :]
    diag = jnp.where((c // CHUNK) <= (r // CHUNK), _t5_bucket(c - r), -1)
    prev = _t5_bucket(c - r - tq)
    bkt = jnp.stack([diag, prev]).astype(jnp.int32)
    smem = pl.BlockSpec(memory_space=pltpu.SMEM)
    return pl.pallas_call(
        functools.partial(_attn_kernel, tq=tq, tb=tb, nsub=nsub, lam_init=lam_init),
        grid=(B, HA, S // tq),
        in_specs=[smem,
                  pl.BlockSpec((4, DA), lambda b, h, i: (0, 0)),
                  pl.BlockSpec((None, tq, DVA), lambda b, h, i: (b, i, h)),
                  pl.BlockSpec((None, S, DVA), lambda b, h, i: (b, 0, h)),
                  pl.BlockSpec((None, DVA, S), lambda b, h, i: (b, h, 0)),
                  pl.BlockSpec((2, tq, tq), lambda b, h, i: (0, 0, 0))],
        out_specs=pl.BlockSpec((None, tq, DVA), lambda b, h, i: (b, i, h)),
        out_shape=jax.ShapeDtypeStruct((B, S, WA), F32),
        scratch_shapes=[pltpu.VMEM((2, tq, tq), F32),
                        pltpu.VMEM((1, 2 * tq), F32),
                        pltpu.VMEM((DVA + BF16_ROWS, 2 * tq), F32),
                        pltpu.VMEM((DVA + BF16_ROWS, S), BF16)],
        compiler_params=_params(("arbitrary", "arbitrary", "arbitrary")),
        name="attn_prompt",
    )(rel_bias, lamv, qab, kab, vt, bkt)


def _attn_small_kernel(rb_ref, lv_ref, q_ref, k_ref, v_ref, bkt_ref, o_ref, *, lq, lam_init):
    h = pl.program_id(1)
    bias = _bias_from_buckets(bkt_ref[...], rb_ref, h, 0.0)
    qbd = _block_diag_q(q_ref[...])
    s = _dot_nt(qbd, k_ref[...]) + jnp.concatenate([bias, bias], axis=0)
    m = jnp.max(s, axis=-1, keepdims=True)
    p = jnp.exp2(s - m)
    l = jnp.sum(p, axis=-1, keepdims=True)
    o = _dot(p.astype(BF16), v_ref[...]) / l
    lam = _lam_value(lv_ref, lam_init)
    o_ref[...] = o[0:lq] - lam * o[lq:2 * lq]


def attn_small(qab, k_all, v_all, q_pos, k_pos, k_valid, rel_bias, lamv, lam_init):
    B, Lq, _ = qab.shape
    Lk = k_all.shape[1]
    visible = ((k_pos[None, :] // CHUNK) <= (q_pos[:, None] // CHUNK)) & k_valid[None, :]
    bkt = jnp.where(visible, _t5_bucket(k_pos[None, :] - q_pos[:, None]), -1).astype(jnp.int32)
    smem = pl.BlockSpec(memory_space=pltpu.SMEM)
    return pl.pallas_call(
        functools.partial(_attn_small_kernel, lq=Lq, lam_init=lam_init),
        grid=(B, HA),
        in_specs=[smem,
                  pl.BlockSpec((4, DA), lambda b, h: (0, 0)),
                  pl.BlockSpec((None, Lq, DVA), lambda b, h: (b, 0, h)),
                  pl.BlockSpec((None, Lk, DVA), lambda b, h: (b, 0, h)),
                  pl.BlockSpec((None, Lk, DVA), lambda b, h: (b, 0, h)),
                  pl.BlockSpec((Lq, Lk), lambda b, h: (0, 0))],
        out_specs=pl.BlockSpec((None, Lq, DVA), lambda b, h: (b, 0, h)),
        out_shape=jax.ShapeDtypeStruct((B, Lq, WA), F32),
        compiler_params=_params(("arbitrary", "arbitrary")),
        name="attn_sample",
    )(rel_bias, lamv, qab, k_all, v_all, bkt)


def _ret_kernel(lg_ref, q_ref, k_ref, v_ref, s0_ref, o_ref, so_ref, state_sc, decay_sc, *, C):
    c = pl.program_id(1)

    @pl.when(c == 0)
    def _():
        state_sc[...] = s0_ref[...]
        r = lax.broadcasted_iota(jnp.int32, (C, C), 0)
        cc = lax.broadcasted_iota(jnp.int32, (C, C), 1)
        diff = (r - cc).astype(F32)
        for h in range(HR):
            decay_sc[h] = jnp.where(diff >= 0, jnp.exp(jnp.maximum(diff, 0.0) * lg_ref[h]), 0.0)

    n = lax.broadcasted_iota(jnp.int32, (C, 1), 0).astype(F32)
    for h in range(HR):
        lg = lg_ref[h]
        cols = slice(h * DKR, (h + 1) * DKR)
        xi = jnp.exp((n + 1.0) * lg)
        zeta = jnp.exp((C - 1.0 - n) * lg)
        q = q_ref[:, cols]
        k = k_ref[:, cols]
        v = v_ref[:, cols]
        state = state_sc[h]
        scores = _dot_nt(q, k.astype(BF16)) * decay_sc[h]
        intra = _dot(scores.astype(BF16), v)
        cross = _dot(q, state.astype(BF16)) * xi
        o_ref[:, cols] = intra + cross
        kz = (k * zeta).astype(BF16)
        state_sc[h] = jnp.exp(C * lg) * state + _dot_tn(kz, v)

    @pl.when(c == pl.num_programs(1) - 1)
    def _():
        so_ref[...] = state_sc[...]


def retention(qrb, kr, vrb, state0, C):
    B, L, _ = qrb.shape
    lg = jnp.log(1.0 - 2.0 ** (-5.0 - jnp.arange(HR, dtype=F32)))
    blk = pl.BlockSpec((None, C, WR), lambda b, c: (b, c, 0))
    st = pl.BlockSpec((None, HR, DKR, DVR), lambda b, c: (b, 0, 0, 0))
    return pl.pallas_call(
        functools.partial(_ret_kernel, C=C),
        grid=(B, L // C),
        in_specs=[pl.BlockSpec(memory_space=pltpu.SMEM), blk, blk, blk, st],
        out_specs=[blk, st],
        out_shape=[jax.ShapeDtypeStruct((B, L, WR), F32),
                   jax.ShapeDtypeStruct((B, HR, DKR, DVR), F32)],
        scratch_shapes=[pltpu.VMEM((HR, DKR, DVR), F32), pltpu.VMEM((HR, C, C), F32)],
        compiler_params=_params(("arbitrary", "arbitrary")),
        name="retention",
    )(lg, qrb, kr, vrb, state0)


def _outmix_kernel(x_ref, oa_ref, or_ref, gr_ref, ga_ref, gb_ref, g1_ref, sh2_ref, sc2_ref, n2_ref,
                   sa_ref, sr_ref, wba_ref, wbr_ref, wo_ref, wpq_ref,
                   x1_ref, h2_ref, qp_ref, *, lam_init, h2_transposed):
    sa = sa_ref[...] * 1.0
    sr = sr_ref[...]
    gr = gr_ref[...]
    silu_gr = gr * jax.nn.sigmoid(gr)
    ya_parts, yr_parts = [], []
    for hh in range(HA):
        sl = slice(hh * DVA, (hh + 1) * DVA)
        ya_parts.append(_rms_rows(oa_ref[:, sl]) * sa * (1.0 - lam_init))
        yr_parts.append(silu_gr[:, sl] * (_rms_rows(or_ref[:, sl]) * sr))
    ya = jnp.concatenate(ya_parts, axis=-1).astype(BF16)
    yr = jnp.concatenate(yr_parts, axis=-1).astype(BF16)
    y = (jax.nn.sigmoid(ga_ref[...]) * _dot(ya, wba_ref[...])
         + jax.nn.sigmoid(gb_ref[...]) * _dot(yr, wbr_ref[...]))
    out = _dot(y.astype(BF16), wo_ref[...])
    x1 = x_ref[...] + g1_ref[...] * out
    x1_ref[...] = x1
    h2f = _rms_rows(x1) * n2_ref[...] * (1.0 + sc2_ref[...]) + sh2_ref[...]
    h2 = h2f.astype(BF16)
    h2_ref[...] = h2f.T.astype(BF16) if h2_transposed else h2
    qp_ref[...] = _dot(h2, wpq_ref[...]).astype(BF16)


def out_mix(x, oa, orr, gr, ga, gb, g1, sh2, sc2, norm2, subln_a, subln_r,
            w_ba_b, w_br_b, w_o_b, w_pq_b, lam_init, tm):
    B, L, D = x.shape
    nq = w_pq_b.shape[1]
    row = lambda n: pl.BlockSpec((None, tm, n), lambda b, i: (b, i, 0))
    mod = pl.BlockSpec((None, 1, D), lambda b, i: (b, 0, 0))
    full = lambda a: pl.BlockSpec(a.shape, lambda b, i: (0,) * a.ndim)
    n2 = norm2.reshape(1, D)
    sa = subln_a.reshape(1, DVA)
    sr = subln_r.reshape(1, DVR)
    h2_transposed = tm % LANES == 0
    nt = L // tm
    if h2_transposed:
        h2_spec = pl.BlockSpec((D, tm), lambda b, i: (0, b * nt + i))
        h2_shape = jax.ShapeDtypeStruct((D, B * L), BF16)
    else:
        h2_spec = row(D)
        h2_shape = jax.ShapeDtypeStruct((B, L, D), BF16)
    x1, h2, qp = pl.pallas_call(
        functools.partial(_outmix_kernel, lam_init=lam_init, h2_transposed=h2_transposed),
        grid=(B, nt),
        in_specs=[row(D), row(WA), row(WR), row(WR), row(D), row(D), mod, mod, mod,
                  full(n2), full(sa), full(sr), full(w_ba_b), full(w_br_b), full(w_o_b), full(w_pq_b)],
        out_specs=[row(D), h2_spec, row(nq)],
        out_shape=[jax.ShapeDtypeStruct((B, L, D), F32), h2_shape,
                   jax.ShapeDtypeStruct((B, L, nq), BF16)],
        compiler_params=_params(("arbitrary", "arbitrary")),
        name="out_mix",
    )(x, oa, orr, gr, ga, gb, g1.reshape(B, 1, D), sh2.reshape(B, 1, D), sc2.reshape(B, 1, D),
      n2, sa, sr, w_ba_b, w_br_b, w_o_b, w_pq_b)
    h2t = h2 if h2_transposed else h2.reshape(B * L, D).T
    return x1, h2t, qp


UNRANKED = float(PEER_TOPK * PEER_TOPK)


def _topk_rows(s, k, break_ties):
    n = s.shape[0]
    iota = lax.broadcasted_iota(jnp.int32, s.shape, 0).astype(F32)
    work = s
    rank = jnp.full(s.shape, UNRANKED, F32)
    vals = []
    for r in range(k):
        m = jnp.max(work, axis=0, keepdims=True)
        sel = work == m
        if break_ties:
            idx = jnp.min(jnp.where(sel, iota, float(n)), axis=0, keepdims=True)
            sel = iota == idx
        rank = jnp.where(sel, float(r), rank)
        work = jnp.where(sel, -jnp.inf, work)
        vals.append(m)
    return vals, rank


def _ranked_count(rank):
    return jnp.sum((rank < UNRANKED).astype(F32), axis=0, keepdims=True)


SUBLANES = 8
_HEAD_A = SUBLANES
_CAND_NB = [PEER_TOPK] + [SUBLANES] * (_HEAD_A - 1)


def _route_chunk(s1, s2, break_ties):
    v1, rank1 = _topk_rows(s1, PEER_TOPK, break_ties)
    v2, rank2 = _topk_rows(s2, PEER_TOPK, break_ties)
    v2m = jnp.concatenate(v2, axis=0)
    v1t = jnp.concatenate(v1[_HEAD_A:], axis=0)
    blocks = [v1[a] + v2m[0:nb] for a, nb in enumerate(_CAND_NB)] + [v1t + v2[0]]
    cand = jnp.concatenate(blocks, axis=0)
    _, crank = _topk_rows(cand, PEER_TOPK, break_ties)
    k = float(PEER_TOPK)
    bad = ((_ranked_count(rank1) != k) | (_ranked_count(rank2) != k) | (_ranked_count(crank) != k))
    nbad = jnp.sum(bad.astype(jnp.int32))
    sel = crank < UNRANKED
    cmax = v1[0] + v2[0]
    z = jnp.sum(jnp.where(sel, jnp.exp(cand - cmax), 0.0), axis=0, keepdims=True)
    self32 = sel.astype(F32)
    cidx = jnp.zeros(s1.shape, F32)
    lo = 0
    for a, nb in enumerate(_CAND_NB):
        cnt = jnp.sum(self32[lo:lo + nb], axis=0, keepdims=True)
        cidx = jnp.where(rank1 == float(a), cnt, cidx)
        lo += nb
    for a in range(_HEAD_A, PEER_TOPK):
        cidx = jnp.where(rank1 == float(a), self32[lo + a - _HEAD_A:lo + a - _HEAD_A + 1], cidx)
    e1 = jnp.exp(s1 - v1[0]) / z
    e2 = jnp.exp(s2 - v2[0])
    return (e1, e2, rank2, cidx), nbad


def _route_kernel(q_ref, keys_ref, e1_ref, e2_ref, r2_ref, c_ref, s1_sc, s2_sc):
    q = q_ref[...]
    s1_sc[...] = _dot_nt(keys_ref[0], q[:, 0:KEY_DIM])
    s2_sc[...] = _dot_nt(keys_ref[1], q[:, KEY_DIM:2 * KEY_DIM])

    def store(sl, res):
        e1, e2, rank2, cidx = res
        e1_ref[:, sl] = e1
        e2_ref[:, sl] = e2.astype(BF16)
        r2_ref[:, sl] = rank2.astype(BF16)
        c_ref[:, sl] = cidx

    group = 2
    assert s1_sc.shape[1] % (group * LANES) == 0

    def chunks(ci, carry):
        sls = [pl.ds(pl.multiple_of((ci * group + g) * LANES, LANES), LANES) for g in range(group)]
        fast = [_route_chunk(s1_sc[:, sl], s2_sc[:, sl], break_ties=False) for sl in sls]
        for sl, (res, _) in zip(sls, fast):
            store(sl, res)
        for sl, (_, nbad) in zip(sls, fast):
            @pl.when(nbad > 0)
            def _():
                store(sl, _route_chunk(s1_sc[:, sl], s2_sc[:, sl], break_ties=True)[0])
        return carry

    lax.fori_loop(0, s1_sc.shape[1] // (group * LANES), chunks, 0)


def peer_route(qp, keys_b, tm):
    T = qp.shape[0]
    out = pl.BlockSpec((None, N_KEYS, tm), lambda t, h: (h, 0, t))
    shp = lambda dt: jax.ShapeDtypeStruct((PEER_HEADS, N_KEYS, T), dt)
    return pl.pallas_call(
        _route_kernel,
        grid=(T // tm, PEER_HEADS),
        in_specs=[pl.BlockSpec((tm, 2 * KEY_DIM), lambda t, h: (t, h)),
                  pl.BlockSpec((2, N_KEYS, KEY_DIM), lambda t, h: (h, 0, 0))],
        out_specs=[out, out, out, out],
        out_shape=[shp(F32), shp(BF16), shp(BF16), shp(F32)],
        scratch_shapes=[pltpu.VMEM((N_KEYS, tm), F32), pltpu.VMEM((N_KEYS, tm), F32)],
        compiler_params=_params(("arbitrary", "arbitrary")),
        name="peer_route",
    )(qp, keys_b)


BF16_ROWS = 16


def _bcast_rows_bf16(row, n):
    t = row.shape[1]
    tile = jnp.broadcast_to(row, (BF16_ROWS, t)).astype(BF16)
    return jnp.broadcast_to(tile[None], (n // BF16_ROWS, BF16_ROWS, t)).reshape(n, t)


def _peer_kernel(h2_ref, u_ref, vt_ref, e1_ref, e2_ref, r2_ref, c_ref, x1_ref, g2_ref, fn_ref,
                 y_ref, acc_sc, *, ni):
    e = pl.program_id(1)

    @pl.when(e == 0)
    def _():
        acc_sc[...] = jnp.zeros(acc_sc.shape, F32)

    h2 = h2_ref[...]
    pair = 2 * N_KEYS
    a_pairs = [_dot(u_ref[p * pair:(p + 1) * pair, :], h2) for p in range(ni // 2)]
    for p in range(ni // 2):
        ws = []
        for jj in range(2):
            i = e * ni + 2 * p + jj
            a = a_pairs[p][jj * N_KEYS:(jj + 1) * N_KEYS]
            g = jnp.zeros(a.shape, BF16)
            for hh in range(PEER_HEADS):
                c_row = _bcast_rows_bf16(c_ref[hh, pl.ds(i, 1), :], N_KEYS)
                e1_row = _bcast_rows_bf16(e1_ref[hh, pl.ds(i, 1), :], N_KEYS)
                g = g + jnp.where(r2_ref[hh] < c_row, e2_ref[hh] * e1_row, jnp.zeros_like(g))
            act = 0.5 * a * (1.0 + lax.erf(a * (2.0 ** -0.5)))
            ws.append(g * act.astype(BF16))
        w = jnp.concatenate(ws, axis=0)
        acc_sc[...] += _dot(vt_ref[:, p * pair:(p + 1) * pair], w)

    @pl.when(e == pl.num_programs(1) - 1)
    def _():
        x2 = x1_ref[...] + g2_ref[...] * acc_sc[...].T
        y_ref[...] = _rms_rows(x2) * fn_ref[...]


def peer_experts(h2t, u_b, vt_b, e1, e2, r2, cc, x1, g2, final_norm, tm, ni=16):
    D, T = h2t.shape
    nb = ni * N_KEYS
    tok = pl.BlockSpec((tm, D), lambda t, e: (t, 0))
    gate = pl.BlockSpec((PEER_HEADS, N_KEYS, tm), lambda t, e: (0, 0, t))
    L = T // g2.shape[0]
    if L % tm == 0:
        g2_arg = g2.reshape(g2.shape[0], 1, D)
        g2_spec = pl.BlockSpec((None, 1, D), lambda t, e: (t // (L // tm), 0, 0))
    else:
        g2_arg = jnp.repeat(g2, L, axis=0)
        g2_spec = tok
    return pl.pallas_call(
        functools.partial(_peer_kernel, ni=ni),
        grid=(T // tm, u_b.shape[0] // nb),
        in_specs=[pl.BlockSpec((D, tm), lambda t, e: (0, t)),
                  pl.BlockSpec((nb, D), lambda t, e: (e, 0)),
                  pl.BlockSpec((D, nb), lambda t, e: (0, e)),
                  gate, gate, gate, gate, tok, g2_spec,
                  pl.BlockSpec((1, D), lambda t, e: (0, 0))],
        out_specs=tok,
        out_shape=jax.ShapeDtypeStruct((T, D), F32),
        scratch_shapes=[pltpu.VMEM((D, tm), F32)],
        compiler_params=_params(("arbitrary", "arbitrary")),
        name="peer_experts",
    )(h2t, u_b, vt_b, e1, e2, r2, cc, x1, g2_arg, final_norm.reshape(1, D))


def _rot_tables(pos):
    inv = 1.0 / (10000.0 ** jnp.linspace(0.0, 1.0, DKR // 2, dtype=F32))
    ang = pos[:, None].astype(F32) * inv[None, :]
    cos, sin = jnp.cos(ang), jnp.sin(ang)
    return jnp.concatenate([cos, cos], axis=-1), jnp.concatenate([-sin, sin], axis=-1)


def _pick_tile(n, pref):
    t = min(n, pref)
    assert n % t == 0, (n, t)
    return t


def _trunk(x, mods, pos, lam_init, lamv, rel_bias, w, attend, values_transposed, state0, ret_chunk,
           final_norm):
    (norm1, norm2, w_in_b, subln_a, subln_r, w_ba_b, w_br_b, w_o_b, w_pq_b, keys_b, u_b, vt_b) = w
    sh1, sc1, g1, sh2, sc2, g2 = mods
    B, L, D = x.shape
    T = B * L
    cos2, sin2 = _rot_tables(pos)
    tm = _pick_tile(L, 256)
    ka, va, qab, kab, vab, qrb, kr, vrb, gr, ga, gb, *vt = in_proj(
        x, sh1, sc1, norm1, w_in_b, cos2, sin2, tm, emit_vt=values_transposed)
    oa = attend(qab, kab, vt[0] if values_transposed else vab)
    orr, st = retention(qrb, kr, vrb, state0, ret_chunk)
    x1, h2t, qp = out_mix(x, oa, orr, gr, ga, gb, g1, sh2, sc2, norm2, subln_a, subln_r,
                          w_ba_b, w_br_b, w_o_b, w_pq_b, lam_init, tm)
    e1, e2, r2, cc = peer_route(qp.reshape(T, -1), keys_b, _pick_tile(T, 512))
    y = peer_experts(h2t, u_b, vt_b, e1, e2, r2, cc, x1.reshape(T, D), g2,
                     final_norm, _pick_tile(T, 512))
    return y.reshape(B, L, D), ka, va, st


def kernel(x_prompt, x_sample, cache_k, cache_v, state_ret, c_prompt, c_sample, w_ada, b_ada, norm1,
           norm2, w_in, lam_q1, lam_k1, lam_q2, lam_k2, subln_a, subln_r, w_ba, w_br, w_o, rel_bias,
           w_pq, peer_keys, peer_u, peer_v, final_norm):
    depth = w_ada.shape[0]
    assert depth == 1, "the fused final norm assumes a single layer"
    Bp, Lp, D = x_prompt.shape
    Bs, Ls, _ = x_sample.shape
    past = cache_k.shape[2]
    pos_p = jnp.arange(Lp, dtype=jnp.int32)
    pos_s = past + jnp.arange(Ls, dtype=jnp.int32)
    lk = past + Ls
    lk_pad = -(-lk // LANES) * LANES
    k_pos_s = jnp.arange(lk_pad, dtype=jnp.int32)
    k_valid_s = k_pos_s < lk

    l = 0
    lam_init = 0.8 - 0.6 * math.exp(-0.3 * l)
    lamv = jnp.stack([lam_q1[l], lam_k1[l], lam_q2[l], lam_k2[l]]).astype(F32)
    mod = ada_mod(jnp.concatenate([c_prompt, c_sample], axis=0), w_ada[l], b_ada[l])
    mods = jnp.split(mod, 6, axis=-1)
    mods_p = [m[:Bp] for m in mods]
    mods_s = [m[Bp:] for m in mods]
    w = (norm1[l], norm2[l], w_in[l].astype(BF16), subln_a[l], subln_r[l], w_ba[l].astype(BF16),
         w_br[l].astype(BF16), w_o[l].astype(BF16), w_pq[l].astype(BF16),
         peer_keys[l].reshape(PEER_HEADS * 2, N_KEYS, KEY_DIM).astype(BF16),
         peer_u[l].astype(BF16), peer_v[l].T.astype(BF16))

    attend_p = lambda q, k, v: attn_prompt(q, k, v, rel_bias, lamv, lam_init)
    zero_state = jnp.zeros((Bp, HR, DKR, DVR), F32)
    yp, kp, vp, sp = _trunk(x_prompt, mods_p, pos_p, lam_init, lamv, rel_bias, w, attend_p, True,
                            zero_state, _pick_tile(Lp, 256), final_norm)

    def attend_s(q, k, v):
        padk = jnp.zeros((Bs, lk_pad - lk, WA), BF16)
        kc = cache_k[l].reshape(Bs, past, HA * 2 * DA).astype(BF16)
        vc = cache_v[l].reshape(Bs, past, WA).astype(BF16)
        k_all = jnp.concatenate([kc, k, padk], axis=1)
        v_all = jnp.concatenate([vc, v, padk], axis=1)
        return attn_small(q, k_all, v_all, pos_s, k_pos_s, k_valid_s, rel_bias, lamv, lam_init)

    ys, ks, vs, ss = _trunk(x_sample, mods_s, pos_s, lam_init, lamv, rel_bias, w, attend_s, False,
                            state_ret[l].astype(F32), Ls, final_norm)
    return (yp, ys, kp[None], vp[None], sp[None], ks[None], vs[None], ss[None])
```

```python
import functools
import math

import jax
import jax.numpy as jnp
from jax import lax
from jax.experimental import pallas as pl
from jax.experimental.pallas import tpu as pltpu

F32 = jnp.float32
BF16 = jnp.bfloat16

CHUNK = 64
HA = 4
DA = 64
DVA = 2 * DA
HR = 4
DKR = 128
DVR = 128
N_BUCKETS = 32
MAX_DIST = 128
PEER_HEADS = 8
N_KEYS = 128
KEY_DIM = 128
PEER_TOPK = 16
EPS = 1e-6
WA = HA * DVA
WR = HR * DVR
NEG = -1e30
LOG2E = math.log2(math.e)
LANES = 128
VMEM_LIMIT = 56 * 1024 * 1024


def _params(sem, vmem=VMEM_LIMIT):
    return pltpu.CompilerParams(dimension_semantics=sem, vmem_limit_bytes=vmem)


def _dot(a, b):
    return jnp.dot(a, b, preferred_element_type=F32)


def _dot_nt(a, b):
    return lax.dot_general(a, b, (((1,), (1,)), ((), ())), preferred_element_type=F32)


def _dot_tn(a, b):
    return lax.dot_general(a, b, (((0,), (0,)), ((), ())), preferred_element_type=F32)


def _rms_rows(x):
    return x * lax.rsqrt(jnp.mean(x * x, axis=-1, keepdims=True) + EPS)


def _ada_kernel(c_ref, w_ref, b_ref, o_ref):
    c = c_ref[...]
    s = (c * jax.nn.sigmoid(c)).astype(BF16)
    o_ref[...] = _dot(s, w_ref[...].astype(BF16)) + b_ref[...]


def ada_mod(c, w_ada, b_ada, tn=1024):
    nb, d = c.shape
    n = w_ada.shape[1]
    return pl.pallas_call(
        _ada_kernel,
        grid=(n // tn,),
        in_specs=[pl.BlockSpec((nb, d), lambda j: (0, 0)),
                  pl.BlockSpec((d, tn), lambda j: (0, j)),
                  pl.BlockSpec((1, tn), lambda j: (0, j))],
        out_specs=pl.BlockSpec((nb, tn), lambda j: (0, j)),
        out_shape=jax.ShapeDtypeStruct((nb, n), F32),
        compiler_params=_params(("arbitrary",)),
        name="ada_mod",
    )(c, w_ada, b_ada.reshape(1, n))


def _inproj_kernel(x_ref, sh_ref, sc_ref, n1_ref, w_ref, cos_ref, sin_ref,
                   ka_ref, va_ref, qab_ref, kab_ref, vab_ref, qrb_ref, kr_ref, vrb_ref,
                   gr_ref, ga_ref, gb_ref, vt_ref=None):
    x = x_ref[...]
    h = _rms_rows(x) * n1_ref[...] * (1.0 + sc_ref[...]) + sh_ref[...]
    hb = h.astype(BF16)

    def proj(lo, n):
        return _dot(hb, w_ref[:, lo:lo + n])

    qa = proj(0, 512)
    qab_ref[...] = (qa * (DA ** -0.5 * LOG2E)).astype(BF16)
    ka = proj(512, 512)
    va = proj(1024, 512)
    for hh in range(HA):
        ka_ref[:, hh, :] = ka[:, hh * DVA:(hh + 1) * DVA]
        va_ref[:, hh, :] = va[:, hh * DVA:(hh + 1) * DVA]
    kab_ref[...] = ka.astype(BF16)
    vab_ref[...] = va.astype(BF16)
    if vt_ref is not None:
        vt_ref[...] = va.T.astype(BF16)
    cos2 = cos_ref[...]
    sin2 = sin_ref[...]

    def rot(z):
        parts = []
        for hh in range(HR):
            zh = z[:, hh * DKR:(hh + 1) * DKR]
            parts.append(zh * cos2 + pltpu.roll(zh, DKR // 2, 1) * sin2)
        return jnp.concatenate(parts, axis=-1)

    qrb_ref[...] = rot(proj(1536, 512)).astype(BF16)
    kr_ref[...] = rot(proj(2048, 512)) * (DKR ** -0.5)
    vrb_ref[...] = proj(2560, 512).astype(BF16)
    gr_ref[...] = proj(3072, 512)
    ga_ref[...] = proj(3584, 1024)
    gb_ref[...] = proj(4608, 1024)


def in_proj(x, sh1, sc1, norm1, w_in_b, cos2, sin2, tm, emit_vt):
    B, L, D = x.shape
    d_in = w_in_b.shape[1]
    row = lambda n: pl.BlockSpec((None, tm, n), lambda b, i: (b, i, 0))
    mod = pl.BlockSpec((None, 1, D), lambda b, i: (b, 0, 0))
    f = lambda n, dt: jax.ShapeDtypeStruct((B, L, n), dt)
    cache = pl.BlockSpec((None, tm, HA, DVA), lambda b, i: (b, i, 0, 0))
    cache_shape = jax.ShapeDtypeStruct((B, L, HA, DVA), F32)
    vt_spec = [pl.BlockSpec((None, WA, tm), lambda b, i: (b, 0, i))] if emit_vt else []
    vt_shape = [jax.ShapeDtypeStruct((B, WA, L), BF16)] if emit_vt else []
    return pl.pallas_call(
        _inproj_kernel,
        grid=(B, L // tm),
        in_specs=[row(D), mod, mod,
                  pl.BlockSpec((1, D), lambda b, i: (0, 0)),
                  pl.BlockSpec((D, d_in), lambda b, i: (0, 0)),
                  pl.BlockSpec((tm, DKR), lambda b, i: (i, 0)),
                  pl.BlockSpec((tm, DKR), lambda b, i: (i, 0))],
        out_specs=[cache, cache, row(512), row(512), row(512), row(512), row(512), row(512),
                   row(512), row(1024), row(1024)] + vt_spec,
        out_shape=[cache_shape, cache_shape, f(512, BF16), f(512, BF16), f(512, BF16),
                   f(512, BF16), f(512, F32), f(512, BF16), f(512, F32), f(1024, F32), f(1024, F32)]
        + vt_shape,
        compiler_params=_params(("arbitrary", "arbitrary")),
        name="in_proj",
    )(x, sh1.reshape(B, 1, D), sc1.reshape(B, 1, D), norm1.reshape(1, D), w_in_b, cos2, sin2)


def _t5_bucket(rel):
    nb = N_BUCKETS // 2
    ret = jnp.where(rel > 0, nb, 0)
    n = jnp.abs(rel)
    max_exact = nb // 2
    nf = jnp.maximum(n, max_exact).astype(F32)
    large = max_exact + (jnp.log(nf / max_exact) / math.log(MAX_DIST / max_exact)
                         * (nb - max_exact)).astype(jnp.int32)
    large = jnp.minimum(large, nb - 1)
    return ret + jnp.where(n < max_exact, n, large)


def _bias_from_buckets(bkt, rb_ref, h, shift):
    val = jnp.where(bkt < 0, NEG, 0.0).astype(F32)
    for n in range(N_BUCKETS):
        val = jnp.where(bkt == n, (rb_ref[n, h] - shift) * LOG2E, val)
    return val


def _lam_value(lv_ref, lam_init):
    lv = lv_ref[...]
    a = jnp.sum(lv[0:1] * lv[1:2], axis=-1, keepdims=True)
    b = jnp.sum(lv[2:3] * lv[3:4], axis=-1, keepdims=True)
    return jnp.exp(a) - jnp.exp(b) + lam_init


def _block_diag_q(q):
    lane = lax.broadcasted_iota(jnp.int32, q.shape, 1)
    zero = jnp.zeros_like(q)
    return jnp.concatenate([jnp.where(lane < DA, q, zero), jnp.where(lane >= DA, q, zero)], axis=0)


ATTN_TAIL = 4


def _attn_kernel(rb_ref, lv_ref, q_ref, k_ref, vt_ref, bkt_ref, o_ref,
                 bias_sc, m_sc, acc_sc, va_sc, *, tq, tb, nsub, lam_init):
    h = pl.program_id(1)
    i = pl.program_id(2)

    @pl.when(i == 0)
    def _():
        far = rb_ref[N_BUCKETS // 2 - 1, h]
        for t in range(2):
            bias_sc[t] = _bias_from_buckets(bkt_ref[t], rb_ref, h, far)
        va_sc[0:DVA, :] = vt_ref[...]
        va_sc[DVA:, :] = jnp.ones((BF16_ROWS, va_sc.shape[1]), BF16)

    qt = q_ref[...].astype(F32).T
    drow = lax.broadcasted_iota(jnp.int32, qt.shape, 0)
    qbd_t = jnp.concatenate([jnp.where(drow < DA, qt, 0.0), jnp.where(drow >= DA, qt, 0.0)],
                            axis=1).astype(BF16)
    m_sc[...] = jnp.full(m_sc.shape, NEG, F32)
    acc_sc[...] = jnp.zeros(acc_sc.shape, F32)

    def scores(off, tk, bias=None):
        s = _dot(k_ref[pl.ds(off, tk), :], qbd_t)
        if bias is not None:
            s = s + jnp.concatenate([bias, bias], axis=1)
        return s

    def absorb(s, off, tk):
        m_prev = m_sc[...]
        m_new = jnp.maximum(m_prev, jnp.max(s, axis=0, keepdims=True))
        alpha = jnp.exp2(m_prev - m_new)
        p = jnp.exp2(s - m_new).astype(BF16)
        acc_sc[...] = alpha * acc_sc[...] + _dot(va_sc[:, pl.ds(off, tk)], p)
        m_sc[...] = m_new

    def tiles(specs):
        ss = [scores(*sp) for sp in specs]
        for s, sp in zip(ss, specs):
            absorb(s, sp[0], sp[1])

    assert tb == tq
    ntile = jnp.maximum(i - 1, 0)
    nfull = ntile // nsub

    def far(first_tile, n):
        return [(pl.multiple_of((first_tile + j) * tb, tb), tb) for j in range(n)]

    def far_body(t, carry):
        tiles(far(t * nsub, nsub))
        return carry

    lax.fori_loop(0, nfull, far_body, 0)
    done = nfull * nsub
    n = nsub // 2
    while n >= ATTN_TAIL:
        take = ((ntile - done) // n) > 0

        @pl.when(take)
        def _(done=done, n=n):
            tiles(far(done, n))

        done = done + jnp.where(take, n, 0)
        n //= 2

    rest = ntile - done
    for r in range(ATTN_TAIL):
        @pl.when((i >= 1) & (rest == r))
        def _(r=r):
            off = pl.multiple_of((i - 1) * tq, tq)
            tiles(far(done, r) + [(off, tq, bias_sc[1]),
                                  (pl.multiple_of(off + tq, tq), tq, bias_sc[0])])

    @pl.when(i == 0)
    def _():
        tiles([(0, tq, bias_sc[0])])

    lam = _lam_value(lv_ref, lam_init)
    o = acc_sc[0:DVA, :] / acc_sc[DVA:DVA + 1, :]
    o_ref[...] = (o[:, 0:tq] - lam * o[:, tq:2 * tq]).T


def attn_prompt(qab, kab, vt, rel_bias, lamv, lam_init, tq=256, nsub=16):
    B, S, _ = qab.shape
    tb = tq
    assert S % tq == 0
    c = jnp.arange(tq, dtype=jnp.int32)[:, None]
    r = jnp.arange(tq, dtype=jnp.int32)[None, :]
    diag = jnp.where((c // CHUNK) <= (r // CHUNK), _t5_bucket(c - r), -1)
    prev = _t5_bucket(c - r - tq)
    bkt = jnp.stack([diag, prev]).astype(jnp.int32)
    smem = pl.BlockSpec(memory_space=pltpu.SMEM)
    return pl.pallas_call(
        functools.partial(_attn_kernel, tq=tq, tb=tb, nsub=nsub, lam_init=lam_init),
        grid=(B, HA, S // tq),
        in_specs=[smem,
                  pl.BlockSpec((4, DA), lambda b, h, i: (0, 0)),
                  pl.BlockSpec((None, tq, DVA), lambda b, h, i: (b, i, h)),
                  pl.BlockSpec((None, S, DVA), lambda b, h, i: (b, 0, h)),
                  pl.BlockSpec((None, DVA, S), lambda b, h, i: (b, h, 0)),
                  pl.BlockSpec((2, tq, tq), lambda b, h, i: (0, 0, 0))],
        out_specs=pl.BlockSpec((None, tq, DVA), lambda b, h, i: (b, i, h)),
        out_shape=jax.ShapeDtypeStruct((B, S, WA), F32),
        scratch_shapes=[pltpu.VMEM((2, tq, tq), F32),
                        pltpu.VMEM((1, 2 * tq), F32),
                        pltpu.VMEM((DVA + BF16_ROWS, 2 * tq), F32),
                        pltpu.VMEM((DVA + BF16_ROWS, S), BF16)],
        compiler_params=_params(("arbitrary", "arbitrary", "arbitrary")),
        name="attn_prompt",
    )(rel_bias, lamv, qab, kab, vt, bkt)


def _attn_small_kernel(rb_ref, lv_ref, q_ref, k_ref, v_ref, bkt_ref, o_ref, *, lq, lam_init):
    h = pl.program_id(1)
    bias = _bias_from_buckets(bkt_ref[...], rb_ref, h, 0.0)
    qbd = _block_diag_q(q_ref[...])
    s = _dot_nt(qbd, k_ref[...]) + jnp.concatenate([bias, bias], axis=0)
    m = jnp.max(s, axis=-1, keepdims=True)
    p = jnp.exp2(s - m)
    l = jnp.sum(p, axis=-1, keepdims=True)
    o = _dot(p.astype(BF16), v_ref[...]) / l
    lam = _lam_value(lv_ref, lam_init)
    o_ref[...] = o[0:lq] - lam * o[lq:2 * lq]


def attn_small(qab, k_all, v_all, q_pos, k_pos, k_valid, rel_bias, lamv, lam_init):
    B, Lq, _ = qab.shape
    Lk = k_all.shape[1]
    visible = ((k_pos[None, :] // CHUNK) <= (q_pos[:, None] // CHUNK)) & k_valid[None, :]
    bkt = jnp.where(visible, _t5_bucket(k_pos[None, :] - q_pos[:, None]), -1).astype(jnp.int32)
    smem = pl.BlockSpec(memory_space=pltpu.SMEM)
    return pl.pallas_call(
        functools.partial(_attn_small_kernel, lq=Lq, lam_init=lam_init),
        grid=(B, HA),
        in_specs=[smem,
                  pl.BlockSpec((4, DA), lambda b, h: (0, 0)),
                  pl.BlockSpec((None, Lq, DVA), lambda b, h: (b, 0, h)),
                  pl.BlockSpec((None, Lk, DVA), lambda b, h: (b, 0, h)),
                  pl.BlockSpec((None, Lk, DVA), lambda b, h: (b, 0, h)),
                  pl.BlockSpec((Lq, Lk), lambda b, h: (0, 0))],
        out_specs=pl.BlockSpec((None, Lq, DVA), lambda b, h: (b, 0, h)),
        out_shape=jax.ShapeDtypeStruct((B, Lq, WA), F32),
        compiler_params=_params(("arbitrary", "arbitrary")),
        name="attn_sample",
    )(rel_bias, lamv, qab, k_all, v_all, bkt)


def _ret_kernel(lg_ref, q_ref, k_ref, v_ref, s0_ref, o_ref, so_ref, state_sc, decay_sc, *, C):
    c = pl.program_id(1)

    @pl.when(c == 0)
    def _():
        state_sc[...] = s0_ref[...]
        r = lax.broadcasted_iota(jnp.int32, (C, C), 0)
        cc = lax.broadcasted_iota(jnp.int32, (C, C), 1)
        diff = (r - cc).astype(F32)
        for h in range(HR):
            decay_sc[h] = jnp.where(diff >= 0, jnp.exp(jnp.maximum(diff, 0.0) * lg_ref[h]), 0.0)

    n = lax.broadcasted_iota(jnp.int32, (C, 1), 0).astype(F32)
    for h in range(HR):
        lg = lg_ref[h]
        cols = slice(h * DKR, (h + 1) * DKR)
        xi = jnp.exp((n + 1.0) * lg)
        zeta = jnp.exp((C - 1.0 - n) * lg)
        q = q_ref[:, cols]
        k = k_ref[:, cols]
        v = v_ref[:, cols]
        state = state_sc[h]
        scores = _dot_nt(q, k.astype(BF16)) * decay_sc[h]
        intra = _dot(scores.astype(BF16), v)
        cross = _dot(q, state.astype(BF16)) * xi
        o_ref[:, cols] = intra + cross
        kz = (k * zeta).astype(BF16)
        state_sc[h] = jnp.exp(C * lg) * state + _dot_tn(kz, v)

    @pl.when(c == pl.num_programs(1) - 1)
    def _():
        so_ref[...] = state_sc[...]


def retention(qrb, kr, vrb, state0, C):
    B, L, _ = qrb.shape
    lg = jnp.log(1.0 - 2.0 ** (-5.0 - jnp.arange(HR, dtype=F32)))
    blk = pl.BlockSpec((None, C, WR), lambda b, c: (b, c, 0))
    st = pl.BlockSpec((None, HR, DKR, DVR), lambda b, c: (b, 0, 0, 0))
    return pl.pallas_call(
        functools.partial(_ret_kernel, C=C),
        grid=(B, L // C),
        in_specs=[pl.BlockSpec(memory_space=pltpu.SMEM), blk, blk, blk, st],
        out_specs=[blk, st],
        out_shape=[jax.ShapeDtypeStruct((B, L, WR), F32),
                   jax.ShapeDtypeStruct((B, HR, DKR, DVR), F32)],
        scratch_shapes=[pltpu.VMEM((HR, DKR, DVR), F32), pltpu.VMEM((HR, C, C), F32)],
        compiler_params=_params(("arbitrary", "arbitrary")),
        name="retention",
    )(lg, qrb, kr, vrb, state0)


def _outmix_kernel(x_ref, oa_ref, or_ref, gr_ref, ga_ref, gb_ref, g1_ref, sh2_ref, sc2_ref, n2_ref,
                   sa_ref, sr_ref, wba_ref, wbr_ref, wo_ref, wpq_ref,
                   x1_ref, h2_ref, qp_ref, *, lam_init, h2_transposed):
    sa = sa_ref[...] * 1.0
    sr = sr_ref[...]
    gr = gr_ref[...]
    silu_gr = gr * jax.nn.sigmoid(gr)
    ya_parts, yr_parts = [], []
    for hh in range(HA):
        sl = slice(hh * DVA, (hh + 1) * DVA)
        ya_parts.append(_rms_rows(oa_ref[:, sl]) * sa * (1.0 - lam_init))
        yr_parts.append(silu_gr[:, sl] * (_rms_rows(or_ref[:, sl]) * sr))
    ya = jnp.concatenate(ya_parts, axis=-1).astype(BF16)
    yr = jnp.concatenate(yr_parts, axis=-1).astype(BF16)
    y = (jax.nn.sigmoid(ga_ref[...]) * _dot(ya, wba_ref[...])
         + jax.nn.sigmoid(gb_ref[...]) * _dot(yr, wbr_ref[...]))
    out = _dot(y.astype(BF16), wo_ref[...])
    x1 = x_ref[...] + g1_ref[...] * out
    x1_ref[...] = x1
    h2f = _rms_rows(x1) * n2_ref[...] * (1.0 + sc2_ref[...]) + sh2_ref[...]
    h2 = h2f.astype(BF16)
    h2_ref[...] = h2f.T.astype(BF16) if h2_transposed else h2
    qp_ref[...] = _dot(h2, wpq_ref[...]).astype(BF16)


def out_mix(x, oa, orr, gr, ga, gb, g1, sh2, sc2, norm2, subln_a, subln_r,
            w_ba_b, w_br_b, w_o_b, w_pq_b, lam_init, tm):
    B, L, D = x.shape
    nq = w_pq_b.shape[1]
    row = lambda n: pl.BlockSpec((None, tm, n), lambda b, i: (b, i, 0))
    mod = pl.BlockSpec((None, 1, D), lambda b, i: (b, 0, 0))
    full = lambda a: pl.BlockSpec(a.shape, lambda b, i: (0,) * a.ndim)
    n2 = norm2.reshape(1, D)
    sa = subln_a.reshape(1, DVA)
    sr = subln_r.reshape(1, DVR)
    h2_transposed = tm % LANES == 0
    nt = L // tm
    if h2_transposed:
        h2_spec = pl.BlockSpec((D, tm), lambda b, i: (0, b * nt + i))
        h2_shape = jax.ShapeDtypeStruct((D, B * L), BF16)
    else:
        h2_spec = row(D)
        h2_shape = jax.ShapeDtypeStruct((B, L, D), BF16)
    x1, h2, qp = pl.pallas_call(
        functools.partial(_outmix_kernel, lam_init=lam_init, h2_transposed=h2_transposed),
        grid=(B, nt),
        in_specs=[row(D), row(WA), row(WR), row(WR), row(D), row(D), mod, mod, mod,
                  full(n2), full(sa), full(sr), full(w_ba_b), full(w_br_b), full(w_o_b), full(w_pq_b)],
        out_specs=[row(D), h2_spec, row(nq)],
        out_shape=[jax.ShapeDtypeStruct((B, L, D), F32), h2_shape,
                   jax.ShapeDtypeStruct((B, L, nq), BF16)],
        compiler_params=_params(("arbitrary", "arbitrary")),
        name="out_mix",
    )(x, oa, orr, gr, ga, gb, g1.reshape(B, 1, D), sh2.reshape(B, 1, D), sc2.reshape(B, 1, D),
      n2, sa, sr, w_ba_b, w_br_b, w_o_b, w_pq_b)
    h2t = h2 if h2_transposed else h2.reshape(B * L, D).T
    return x1, h2t, qp


UNRANKED = float(PEER_TOPK * PEER_TOPK)


def _topk_rows(s, k, break_ties):
    n = s.shape[0]
    iota = lax.broadcasted_iota(jnp.int32, s.shape, 0).astype(F32)
    work = s
    rank = jnp.full(s.shape, UNRANKED, F32)
    vals = []
    for r in range(k):
        m = jnp.max(work, axis=0, keepdims=True)
        sel = work == m
        if break_ties:
            idx = jnp.min(jnp.where(sel, iota, float(n)), axis=0, keepdims=True)
            sel = iota == idx
        rank = jnp.where(sel, float(r), rank)
        work = jnp.where(sel, -jnp.inf, work)
        vals.append(m)
    return vals, rank


def _ranked_count(rank):
    return jnp.sum((rank < UNRANKED).astype(F32), axis=0, keepdims=True)


SUBLANES = 8
_HEAD_A = SUBLANES
_CAND_NB = [PEER_TOPK] + [SUBLANES] * (_HEAD_A - 1)


def _route_chunk(s1, s2, break_ties):
    v1, rank1 = _topk_rows(s1, PEER_TOPK, break_ties)
    v2, rank2 = _topk_rows(s2, PEER_TOPK, break_ties)
    v2m = jnp.concatenate(v2, axis=0)
    v1t = jnp.concatenate(v1[_HEAD_A:], axis=0)
    blocks = [v1[a] + v2m[0:nb] for a, nb in enumerate(_CAND_NB)] + [v1t + v2[0]]
    cand = jnp.concatenate(blocks, axis=0)
    _, crank = _topk_rows(cand, PEER_TOPK, break_ties)
    k = float(PEER_TOPK)
    bad = ((_ranked_count(rank1) != k) | (_ranked_count(rank2) != k) | (_ranked_count(crank) != k))
    nbad = jnp.sum(bad.astype(jnp.int32))
    sel = crank < UNRANKED
    cmax = v1[0] + v2[0]
    z = jnp.sum(jnp.where(sel, jnp.exp(cand - cmax), 0.0), axis=0, keepdims=True)
    self32 = sel.astype(F32)
    cidx = jnp.zeros(s1.shape, F32)
    lo = 0
    for a, nb in enumerate(_CAND_NB):
        cnt = jnp.sum(self32[lo:lo + nb], axis=0, keepdims=True)
        cidx = jnp.where(rank1 == float(a), cnt, cidx)
        lo += nb
    for a in range(_HEAD_A, PEER_TOPK):
        cidx = jnp.where(rank1 == float(a), self32[lo + a - _HEAD_A:lo + a - _HEAD_A + 1], cidx)
    e1 = jnp.exp(s1 - v1[0]) / z
    e2 = jnp.exp(s2 - v2[0])
    return (e1, e2, rank2, cidx), nbad


def _route_kernel(q_ref, keys_ref, e1_ref, e2_ref, r2_ref, c_ref, s1_sc, s2_sc):
    q = q_ref[...]
    s1_sc[...] = _dot_nt(keys_ref[0], q[:, 0:KEY_DIM])
    s2_sc[...] = _dot_nt(keys_ref[1], q[:, KEY_DIM:2 * KEY_DIM])

    def store(sl, res):
        e1, e2, rank2, cidx = res
        e1_ref[:, sl] = e1
        e2_ref[:, sl] = e2.astype(BF16)
        r2_ref[:, sl] = rank2.astype(BF16)
        c_ref[:, sl] = cidx

    group = 2
    assert s1_sc.shape[1] % (group * LANES) == 0

    def chunks(ci, carry):
        sls = [pl.ds(pl.multiple_of((ci * group + g) * LANES, LANES), LANES) for g in range(group)]
        fast = [_route_chunk(s1_sc[:, sl], s2_sc[:, sl], break_ties=False) for sl in sls]
        for sl, (res, _) in zip(sls, fast):
            store(sl, res)
        for sl, (_, nbad) in zip(sls, fast):
            @pl.when(nbad > 0)
            def _():
                store(sl, _route_chunk(s1_sc[:, sl], s2_sc[:, sl], break_ties=True)[0])
        return carry

    lax.fori_loop(0, s1_sc.shape[1] // (group * LANES), chunks, 0)


def peer_route(qp, keys_b, tm):
    T = qp.shape[0]
    out = pl.BlockSpec((None, N_KEYS, tm), lambda t, h: (h, 0, t))
    shp = lambda dt: jax.ShapeDtypeStruct((PEER_HEADS, N_KEYS, T), dt)
    return pl.pallas_call(
        _route_kernel,
        grid=(T // tm, PEER_HEADS),
        in_specs=[pl.BlockSpec((tm, 2 * KEY_DIM), lambda t, h: (t, h)),
                  pl.BlockSpec((2, N_KEYS, KEY_DIM), lambda t, h: (h, 0, 0))],
        out_specs=[out, out, out, out],
        out_shape=[shp(F32), shp(BF16), shp(BF16), shp(F32)],
        scratch_shapes=[pltpu.VMEM((N_KEYS, tm), F32), pltpu.VMEM((N_KEYS, tm), F32)],
        compiler_params=_params(("arbitrary", "arbitrary")),
        name="peer_route",
    )(qp, keys_b)


BF16_ROWS = 16


def _bcast_rows_bf16(row, n):
    t = row.shape[1]
    tile = jnp.broadcast_to(row, (BF16_ROWS, t)).astype(BF16)
    return jnp.broadcast_to(tile[None], (n // BF16_ROWS, BF16_ROWS, t)).reshape(n, t)


def _peer_kernel(h2_ref, u_ref, vt_ref, e1_ref, e2_ref, r2_ref, c_ref, x1_ref, g2_ref, fn_ref,
                 y_ref, acc_sc, *, ni):
    e = pl.program_id(1)

    @pl.when(e == 0)
    def _():
        acc_sc[...] = jnp.zeros(acc_sc.shape, F32)

    h2 = h2_ref[...]
    pair = 2 * N_KEYS
    a_pairs = [_dot(u_ref[p * pair:(p + 1) * pair, :], h2) for p in range(ni // 2)]
    for p in range(ni // 2):
        ws = []
        for jj in range(2):
            i = e * ni + 2 * p + jj
            a = a_pairs[p][jj * N_KEYS:(jj + 1) * N_KEYS]
            g = jnp.zeros(a.shape, BF16)
            for hh in range(PEER_HEADS):
                c_row = _bcast_rows_bf16(c_ref[hh, pl.ds(i, 1), :], N_KEYS)
                e1_row = _bcast_rows_bf16(e1_ref[hh, pl.ds(i, 1), :], N_KEYS)
                g = g + jnp.where(r2_ref[hh] < c_row, e2_ref[hh] * e1_row, jnp.zeros_like(g))
            act = 0.5 * a * (1.0 + lax.erf(a * (2.0 ** -0.5)))
            ws.append(g * act.astype(BF16))
        w = jnp.concatenate(ws, axis=0)
        acc_sc[...] += _dot(vt_ref[:, p * pair:(p + 1) * pair], w)

    @pl.when(e == pl.num_programs(1) - 1)
    def _():
        x2 = x1_ref[...] + g2_ref[...] * acc_sc[...].T
        y_ref[...] = _rms_rows(x2) * fn_ref[...]


def peer_experts(h2t, u_b, vt_b, e1, e2, r2, cc, x1, g2, final_norm, tm, ni=16):
    D, T = h2t.shape
    nb = ni * N_KEYS
    tok = pl.BlockSpec((tm, D), lambda t, e: (t, 0))
    gate = pl.BlockSpec((PEER_HEADS, N_KEYS, tm), lambda t, e: (0, 0, t))
    L = T // g2.shape[0]
    if L % tm == 0:
        g2_arg = g2.reshape(g2.shape[0], 1, D)
        g2_spec = pl.BlockSpec((None, 1, D), lambda t, e: (t // (L // tm), 0, 0))
    else:
        g2_arg = jnp.repeat(g2, L, axis=0)
        g2_spec = tok
    return pl.pallas_call(
        functools.partial(_peer_kernel, ni=ni),
        grid=(T // tm, u_b.shape[0] // nb),
        in_specs=[pl.BlockSpec((D, tm), lambda t, e: (0, t)),
                  pl.BlockSpec((nb, D), lambda t, e: (e, 0)),
                  pl.BlockSpec((D, nb), lambda t, e: (0, e)),
                  gate, gate, gate, gate, tok, g2_spec,
                  pl.BlockSpec((1, D), lambda t, e: (0, 0))],
        out_specs=tok,
        out_shape=jax.ShapeDtypeStruct((T, D), F32),
        scratch_shapes=[pltpu.VMEM((D, tm), F32)],
        compiler_params=_params(("arbitrary", "arbitrary")),
        name="peer_experts",
    )(h2t, u_b, vt_b, e1, e2, r2, cc, x1, g2_arg, final_norm.reshape(1, D))


def _rot_tables(pos):
    inv = 1.0 / (10000.0 ** jnp.linspace(0.0, 1.0, DKR // 2, dtype=F32))
    ang = pos[:, None].astype(F32) * inv[None, :]
    cos, sin = jnp.cos(ang), jnp.sin(ang)
    return jnp.concatenate([cos, cos], axis=-1), jnp.concatenate([-sin, sin], axis=-1)


def _pick_tile(n, pref):
    t = min(n, pref)
    assert n % t == 0, (n, t)
    return t


def _trunk(x, mods, pos, lam_init, lamv, rel_bias, w, attend, values_transposed, state0, ret_chunk,
           final_norm):
    (norm1, norm2, w_in_b, subln_a, subln_r, w_ba_b, w_br_b, w_o_b, w_pq_b, keys_b, u_b, vt_b) = w
    sh1, sc1, g1, sh2, sc2, g2 = mods
    B, L, D = x.shape
    T = B * L
    cos2, sin2 = _rot_tables(pos)
    tm = _pick_tile(L, 256)
    ka, va, qab, kab, vab, qrb, kr, vrb, gr, ga, gb, *vt = in_proj(
        x, sh1, sc1, norm1, w_in_b, cos2, sin2, tm, emit_vt=values_transposed)
    oa = attend(qab, kab, vt[0] if values_transposed else vab)
    orr, st = retention(qrb, kr, vrb, state0, ret_chunk)
    x1, h2t, qp = out_mix(x, oa, orr, gr, ga, gb, g1, sh2, sc2, norm2, subln_a, subln_r,
                          w_ba_b, w_br_b, w_o_b, w_pq_b, lam_init, tm)
    e1, e2, r2, cc = peer_route(qp.reshape(T, -1), keys_b, _pick_tile(T, 512))
    y = peer_experts(h2t, u_b, vt_b, e1, e2, r2, cc, x1.reshape(T, D), g2,
                     final_norm, _pick_tile(T, 512))
    return y.reshape(B, L, D), ka, va, st


def kernel(x_prompt, x_sample, cache_k, cache_v, state_ret, c_prompt, c_sample, w_ada, b_ada, norm1,
           norm2, w_in, lam_q1, lam_k1, lam_q2, lam_k2, subln_a, subln_r, w_ba, w_br, w_o, rel_bias,
           w_pq, peer_keys, peer_u, peer_v, final_norm):
    depth = w_ada.shape[0]
    assert depth == 1, "the fused final norm assumes a single layer"
    Bp, Lp, D = x_prompt.shape
    Bs, Ls, _ = x_sample.shape
    past = cache_k.shape[2]
    pos_p = jnp.arange(Lp, dtype=jnp.int32)
    pos_s = past + jnp.arange(Ls, dtype=jnp.int32)
    lk = past + Ls
    lk_pad = -(-lk // LANES) * LANES
    k_pos_s = jnp.arange(lk_pad, dtype=jnp.int32)
    k_valid_s = k_pos_s < lk

    l = 0
    lam_init = 0.8 - 0.6 * math.exp(-0.3 * l)
    lamv = jnp.stack([lam_q1[l], lam_k1[l], lam_q2[l], lam_k2[l]]).astype(F32)
    mod = ada_mod(jnp.concatenate([c_prompt, c_sample], axis=0), w_ada[l], b_ada[l])
    mods = jnp.split(mod, 6, axis=-1)
    mods_p = [m[:Bp] for m in mods]
    mods_s = [m[Bp:] for m in mods]
    w = (norm1[l], norm2[l], w_in[l].astype(BF16), subln_a[l], subln_r[l], w_ba[l].astype(BF16),
         w_br[l].astype(BF16), w_o[l].astype(BF16), w_pq[l].astype(BF16),
         peer_keys[l].reshape(PEER_HEADS * 2, N_KEYS, KEY_DIM).astype(BF16),
         peer_u[l].astype(BF16), peer_v[l].T.astype(BF16))

    attend_p = lambda q, k, v: attn_prompt(q, k, v, rel_bias, lamv, lam_init)
    zero_state = jnp.zeros((Bp, HR, DKR, DVR), F32)
    yp, kp, vp, sp = _trunk(x_prompt, mods_p, pos_p, lam_init, lamv, rel_bias, w, attend_p, True,
                            zero_state, _pick_tile(Lp, 256), final_norm)

    def attend_s(q, k, v):
        padk = jnp.zeros((Bs, lk_pad - lk, WA), BF16)
        kc = cache_k[l].reshape(Bs, past, HA * 2 * DA).astype(BF16)
        vc = cache_v[l].reshape(Bs, past, WA).astype(BF16)
        k_all = jnp.concatenate([kc, k, padk], axis=1)
        v_all = jnp.concatenate([vc, v, padk], axis=1)
        return attn_small(q, k_all, v_all, pos_s, k_pos_s, k_valid_s, rel_bias, lamv, lam_init)

    ys, ks, vs, ss = _trunk(x_sample, mods_s, pos_s, lam_init, lamv, rel_bias, w, attend_s, False,
                            state_ret[l].astype(F32), Ls, final_norm)
    return (yp, ys, kp[None], vp[None], sp[None], ks[None], vs[None], ss[None])
```

```python
import functools
import math

import jax
import jax.numpy as jnp
from jax import lax
from jax.experimental import pallas as pl
from jax.experimental.pallas import tpu as pltpu

F32 = jnp.float32
BF16 = jnp.bfloat16

CHUNK = 64
HA = 4
DA = 64
DVA = 2 * DA
HR = 4
DKR = 128
DVR = 128
N_BUCKETS = 32
MAX_DIST = 128
PEER_HEADS = 8
N_KEYS = 128
KEY_DIM = 128
PEER_TOPK = 16
EPS = 1e-6
WA = HA * DVA
WR = HR * DVR
NEG = -1e30
LOG2E = math.log2(math.e)
LANES = 128
VMEM_LIMIT = 56 * 1024 * 1024


def _params(sem, vmem=VMEM_LIMIT):
    return pltpu.CompilerParams(dimension_semantics=sem, vmem_limit_bytes=vmem)


def _dot(a, b):
    return jnp.dot(a, b, preferred_element_type=F32)


def _dot_nt(a, b):
    return lax.dot_general(a, b, (((1,), (1,)), ((), ())), preferred_element_type=F32)


def _dot_tn(a, b):
    return lax.dot_general(a, b, (((0,), (0,)), ((), ())), preferred_element_type=F32)


def _rms_rows(x):
    return x * lax.rsqrt(jnp.mean(x * x, axis=-1, keepdims=True) + EPS)


def _ada_kernel(c_ref, w_ref, b_ref, o_ref):
    c = c_ref[...]
    s = (c * jax.nn.sigmoid(c)).astype(BF16)
    o_ref[...] = _dot(s, w_ref[...].astype(BF16)) + b_ref[...]


def ada_mod(c, w_ada, b_ada, tn=1024):
    nb, d = c.shape
    n = w_ada.shape[1]
    return pl.pallas_call(
        _ada_kernel,
        grid=(n // tn,),
        in_specs=[pl.BlockSpec((nb, d), lambda j: (0, 0)),
                  pl.BlockSpec((d, tn), lambda j: (0, j)),
                  pl.BlockSpec((1, tn), lambda j: (0, j))],
        out_specs=pl.BlockSpec((nb, tn), lambda j: (0, j)),
        out_shape=jax.ShapeDtypeStruct((nb, n), F32),
        compiler_params=_params(("arbitrary",)),
        name="ada_mod",
    )(c, w_ada, b_ada.reshape(1, n))


def _inproj_kernel(x_ref, sh_ref, sc_ref, n1_ref, w_ref, cos_ref, sin_ref,
                   ka_ref, va_ref, qab_ref, kab_ref, vab_ref, qrb_ref, kr_ref, vrb_ref,
                   gr_ref, ga_ref, gb_ref, vt_ref=None):
    x = x_ref[...]
    h = _rms_rows(x) * n1_ref[...] * (1.0 + sc_ref[...]) + sh_ref[...]
    hb = h.astype(BF16)

    def proj(lo, n):
        return _dot(hb, w_ref[:, lo:lo + n])

    qa = proj(0, 512)
    qab_ref[...] = (qa * (DA ** -0.5 * LOG2E)).astype(BF16)
    ka = proj(512, 512)
    va = proj(1024, 512)
    for hh in range(HA):
        ka_ref[:, hh, :] = ka[:, hh * DVA:(hh + 1) * DVA]
        va_ref[:, hh, :] = va[:, hh * DVA:(hh + 1) * DVA]
    kab_ref[...] = ka.astype(BF16)
    vab_ref[...] = va.astype(BF16)
    if vt_ref is not None:
        vt_ref[...] = va.T.astype(BF16)
    cos2 = cos_ref[...]
    sin2 = sin_ref[...]

    def rot(z):
        parts = []
        for hh in range(HR):
            zh = z[:, hh * DKR:(hh + 1) * DKR]
            parts.append(zh * cos2 + pltpu.roll(zh, DKR // 2, 1) * sin2)
        return jnp.concatenate(parts, axis=-1)

    qrb_ref[...] = rot(proj(1536, 512)).astype(BF16)
    kr_ref[...] = rot(proj(2048, 512)) * (DKR ** -0.5)
    vrb_ref[...] = proj(2560, 512).astype(BF16)
    gr_ref[...] = proj(3072, 512)
    ga_ref[...] = proj(3584, 1024)
    gb_ref[...] = proj(4608, 1024)


def in_proj(x, sh1, sc1, norm1, w_in_b, cos2, sin2, tm, emit_vt):
    B, L, D = x.shape
    d_in = w_in_b.shape[1]
    row = lambda n: pl.BlockSpec((None, tm, n), lambda b, i: (b, i, 0))
    mod = pl.BlockSpec((None, 1, D), lambda b, i: (b, 0, 0))
    f = lambda n, dt: jax.ShapeDtypeStruct((B, L, n), dt)
    cache = pl.BlockSpec((None, tm, HA, DVA), lambda b, i: (b, i, 0, 0))
    cache_shape = jax.ShapeDtypeStruct((B, L, HA, DVA), F32)
    vt_spec = [pl.BlockSpec((None, WA, tm), lambda b, i: (b, 0, i))] if emit_vt else []
    vt_shape = [jax.ShapeDtypeStruct((B, WA, L), BF16)] if emit_vt else []
    return pl.pallas_call(
        _inproj_kernel,
        grid=(B, L // tm),
        in_specs=[row(D), mod, mod,
                  pl.BlockSpec((1, D), lambda b, i: (0, 0)),
                  pl.BlockSpec((D, d_in), lambda b, i: (0, 0)),
                  pl.BlockSpec((tm, DKR), lambda b, i: (i, 0)),
                  pl.BlockSpec((tm, DKR), lambda b, i: (i, 0))],
        out_specs=[cache, cache, row(512), row(512), row(512), row(512), row(512), row(512),
                   row(512), row(1024), row(1024)] + vt_spec,
        out_shape=[cache_shape, cache_shape, f(512, BF16), f(512, BF16), f(512, BF16),
                   f(512, BF16), f(512, F32), f(512, BF16), f(512, F32), f(1024, F32), f(1024, F32)]
        + vt_shape,
        compiler_params=_params(("arbitrary", "arbitrary")),
        name="in_proj",
    )(x, sh1.reshape(B, 1, D), sc1.reshape(B, 1, D), norm1.reshape(1, D), w_in_b, cos2, sin2)


def _t5_bucket(rel):
    nb = N_BUCKETS // 2
    ret = jnp.where(rel > 0, nb, 0)
    n = jnp.abs(rel)
    max_exact = nb // 2
    nf = jnp.maximum(n, max_exact).astype(F32)
    large = max_exact + (jnp.log(nf / max_exact) / math.log(MAX_DIST / max_exact)
                         * (nb - max_exact)).astype(jnp.int32)
    large = jnp.minimum(large, nb - 1)
    return ret + jnp.where(n < max_exact, n, large)


def _bias_from_buckets(bkt, rb_ref, h, shift):
    val = jnp.where(bkt < 0, NEG, 0.0).astype(F32)
    for n in range(N_BUCKETS):
        val = jnp.where(bkt == n, (rb_ref[n, h] - shift) * LOG2E, val)
    return val


def _lam_value(lv_ref, lam_init):
    lv = lv_ref[...]
    a = jnp.sum(lv[0:1] * lv[1:2], axis=-1, keepdims=True)
    b = jnp.sum(lv[2:3] * lv[3:4], axis=-1, keepdims=True)
    return jnp.exp(a) - jnp.exp(b) + lam_init


def _block_diag_q(q):
    lane = lax.broadcasted_iota(jnp.int32, q.shape, 1)
    zero = jnp.zeros_like(q)
    return jnp.concatenate([jnp.where(lane < DA, q, zero), jnp.where(lane >= DA, q, zero)], axis=0)


ATTN_TAIL = 4


def _attn_kernel(rb_ref, lv_ref, q_ref, k_ref, vt_ref, bkt_ref, o_ref,
                 bias_sc, m_sc, acc_sc, va_sc, *, tq, tb, nsub, lam_init):
    h = pl.program_id(1)
    i = pl.program_id(2)

    @pl.when(i == 0)
    def _():
        far = rb_ref[N_BUCKETS // 2 - 1, h]
        for t in range(2):
            bias_sc[t] = _bias_from_buckets(bkt_ref[t], rb_ref, h, far)
        va_sc[0:DVA, :] = vt_ref[...]
        va_sc[DVA:, :] = jnp.ones((BF16_ROWS, va_sc.shape[1]), BF16)

    qt = q_ref[...].astype(F32).T
    drow = lax.broadcasted_iota(jnp.int32, qt.shape, 0)
    qbd_t = jnp.concatenate([jnp.where(drow < DA, qt, 0.0), jnp.where(drow >= DA, qt, 0.0)],
                            axis=1).astype(BF16)
    m_sc[...] = jnp.full(m_sc.shape, NEG, F32)
    acc_sc[...] = jnp.zeros(acc_sc.shape, F32)

    def scores(off, tk, bias=None):
        s = _dot(k_ref[pl.ds(off, tk), :], qbd_t)
        if bias is not None:
            s = s + jnp.concatenate([bias, bias], axis=1)
        return s

    def absorb(s, off, tk):
        m_prev = m_sc[...]
        m_new = jnp.maximum(m_prev, jnp.max(s, axis=0, keepdims=True))
        alpha = jnp.exp2(m_prev - m_new)
        p = jnp.exp2(s - m_new).astype(BF16)
        acc_sc[...] = alpha * acc_sc[...] + _dot(va_sc[:, pl.ds(off, tk)], p)
        m_sc[...] = m_new

    def tiles(specs):
        ss = [scores(*sp) for sp in specs]
        for s, sp in zip(ss, specs):
            absorb(s, sp[0], sp[1])

    assert tb == tq
    ntile = jnp.maximum(i - 1, 0)
    nfull = ntile // nsub

    def far(first_tile, n):
        return [(pl.multiple_of((first_tile + j) * tb, tb), tb) for j in range(n)]

    def far_body(t, carry):
        tiles(far(t * nsub, nsub))
        return carry

    lax.fori_loop(0, nfull, far_body, 0)
    done = nfull * nsub
    n = nsub // 2
    while n >= ATTN_TAIL:
        take = ((ntile - done) // n) > 0

        @pl.when(take)
        def _(done=done, n=n):
            tiles(far(done, n))

        done = done + jnp.where(take, n, 0)
        n //= 2

    rest = ntile - done
    for r in range(ATTN_TAIL):
        @pl.when((i >= 1) & (rest == r))
        def _(r=r):
            off = pl.multiple_of((i - 1) * tq, tq)
            tiles(far(done, r) + [(off, tq, bias_sc[1]),
                                  (pl.multiple_of(off + tq, tq), tq, bias_sc[0])])

    @pl.when(i == 0)
    def _():
        tiles([(0, tq, bias_sc[0])])

    lam = _lam_value(lv_ref, lam_init)
    o = acc_sc[0:DVA, :] / acc_sc[DVA:DVA + 1, :]
    o_ref[...] = (o[:, 0:tq] - lam * o[:, tq:2 * tq]).T


def attn_prompt(qab, kab, vt, rel_bias, lamv, lam_init, tq=256, nsub=16):
    B, S, _ = qab.shape
    tb = tq
    assert S % tq == 0
    c = jnp.arange(tq, dtype=jnp.int32)[:, None]
    r = jnp.arange(tq, dtype=jnp.int32)[None, :]
    diag = jnp.where((c // CHUNK) <= (r // CHUNK), _t5_bucket(c - r), -1)
    prev = _t5_bucket(c - r - tq)
    bkt = jnp.stack([diag, prev]).astype(jnp.int32)
    smem = pl.BlockSpec(memory_space=pltpu.SMEM)
    return pl.pallas_call(
        functools.partial(_attn_kernel, tq=tq, tb=tb, nsub=nsub, lam_init=lam_init),
        grid=(B, HA, S // tq),
        in_specs=[smem,
                  pl.BlockSpec((4, DA), lambda b, h, i: (0, 0)),
                  pl.BlockSpec((None, tq, DVA), lambda b, h, i: (b, i, h)),
                  pl.BlockSpec((None, S, DVA), lambda b, h, i: (b, 0, h)),
                  pl.BlockSpec((None, DVA, S), lambda b, h, i: (b, h, 0)),
                  pl.BlockSpec((2, tq, tq), lambda b, h, i: (0, 0, 0))],
        out_specs=pl.BlockSpec((None, tq, DVA), lambda b, h, i: (b, i, h)),
        out_shape=jax.ShapeDtypeStruct((B, S, WA), F32),
        scratch_shapes=[pltpu.VMEM((2, tq, tq), F32),
                        pltpu.VMEM((1, 2 * tq), F32),
                        pltpu.VMEM((DVA + BF16_ROWS, 2 * tq), F32),
                        pltpu.VMEM((DVA + BF16_ROWS, S), BF16)],
        compiler_params=_params(("arbitrary", "arbitrary", "arbitrary")),
        name="attn_prompt",
    )(rel_bias, lamv, qab, kab, vt, bkt)


def _attn_small_kernel(rb_ref, lv_ref, q_ref, k_ref, v_ref, bkt_ref, o_ref, *, lq, lam_init):
    h = pl.program_id(1)
    bias = _bias_from_buckets(bkt_ref[...], rb_ref, h, 0.0)
    qbd = _block_diag_q(q_ref[...])
    s = _dot_nt(qbd, k_ref[...]) + jnp.concatenate([bias, bias], axis=0)
    m = jnp.max(s, axis=-1, keepdims=True)
    p = jnp.exp2(s - m)
    l = jnp.sum(p, axis=-1, keepdims=True)
    o = _dot(p.astype(BF16), v_ref[...]) / l
    lam = _lam_value(lv_ref, lam_init)
    o_ref[...] = o[0:lq] - lam * o[lq:2 * lq]


def attn_small(qab, k_all, v_all, q_pos, k_pos, k_valid, rel_bias, lamv, lam_init):
    B, Lq, _ = qab.shape
    Lk = k_all.shape[1]
    visible = ((k_pos[None, :] // CHUNK) <= (q_pos[:, None] // CHUNK)) & k_valid[None, :]
    bkt = jnp.where(visible, _t5_bucket(k_pos[None, :] - q_pos[:, None]), -1).astype(jnp.int32)
    smem = pl.BlockSpec(memory_space=pltpu.SMEM)
    return pl.pallas_call(
        functools.partial(_attn_small_kernel, lq=Lq, lam_init=lam_init),
        grid=(B, HA),
        in_specs=[smem,
                  pl.BlockSpec((4, DA), lambda b, h: (0, 0)),
                  pl.BlockSpec((None, Lq, DVA), lambda b, h: (b, 0, h)),
                  pl.BlockSpec((None, Lk, DVA), lambda b, h: (b, 0, h)),
                  pl.BlockSpec((None, Lk, DVA), lambda b, h: (b, 0, h)),
                  pl.BlockSpec((Lq, Lk), lambda b, h: (0, 0))],
        out_specs=pl.BlockSpec((None, Lq, DVA), lambda b, h: (b, 0, h)),
        out_shape=jax.ShapeDtypeStruct((B, Lq, WA), F32),
        compiler_params=_params(("arbitrary", "arbitrary")),
        name="attn_sample",
    )(rel_bias, lamv, qab, k_all, v_all, bkt)


def _ret_kernel(lg_ref, q_ref, k_ref, v_ref, s0_ref, o_ref, so_ref, state_sc, decay_sc, *, C):
    c = pl.program_id(1)

    @pl.when(c == 0)
    def _():
        state_sc[...] = s0_ref[...]
        r = lax.broadcasted_iota(jnp.int32, (C, C), 0)
        cc = lax.broadcasted_iota(jnp.int32, (C, C), 1)
        diff = (r - cc).astype(F32)
        for h in range(HR):
            decay_sc[h] = jnp.where(diff >= 0, jnp.exp(jnp.maximum(diff, 0.0) * lg_ref[h]), 0.0)

    n = lax.broadcasted_iota(jnp.int32, (C, 1), 0).astype(F32)
    for h in range(HR):
        lg = lg_ref[h]
        cols = slice(h * DKR, (h + 1) * DKR)
        xi = jnp.exp((n + 1.0) * lg)
        zeta = jnp.exp((C - 1.0 - n) * lg)
        q = q_ref[:, cols]
        k = k_ref[:, cols]
        v = v_ref[:, cols]
        state = state_sc[h]
        scores = _dot_nt(q, k.astype(BF16)) * decay_sc[h]
        intra = _dot(scores.astype(BF16), v)
        cross = _dot(q, state.astype(BF16)) * xi
        o_ref[:, cols] = intra + cross
        kz = (k * zeta).astype(BF16)
        state_sc[h] = jnp.exp(C * lg) * state + _dot_tn(kz, v)

    @pl.when(c == pl.num_programs(1) - 1)
    def _():
        so_ref[...] = state_sc[...]


def retention(qrb, kr, vrb, state0, C):
    B, L, _ = qrb.shape
    lg = jnp.log(1.0 - 2.0 ** (-5.0 - jnp.arange(HR, dtype=F32)))
    blk = pl.BlockSpec((None, C, WR), lambda b, c: (b, c, 0))
    st = pl.BlockSpec((None, HR, DKR, DVR), lambda b, c: (b, 0, 0, 0))
    return pl.pallas_call(
        functools.partial(_ret_kernel, C=C),
        grid=(B, L // C),
        in_specs=[pl.BlockSpec(memory_space=pltpu.SMEM), blk, blk, blk, st],
        out_specs=[blk, st],
        out_shape=[jax.ShapeDtypeStruct((B, L, WR), F32),
                   jax.ShapeDtypeStruct((B, HR, DKR, DVR), F32)],
        scratch_shapes=[pltpu.VMEM((HR, DKR, DVR), F32), pltpu.VMEM((HR, C, C), F32)],
        compiler_params=_params(("arbitrary", "arbitrary")),
        name="retention",
    )(lg, qrb, kr, vrb, state0)


def _outmix_kernel(x_ref, oa_ref, or_ref, gr_ref, ga_ref, gb_ref, g1_ref, sh2_ref, sc2_ref, n2_ref,
                   sa_ref, sr_ref, wba_ref, wbr_ref, wo_ref, wpq_ref,
                   x1_ref, h2_ref, qp_ref, *, lam_init, h2_transposed):
    sa = sa_ref[...] * 1.0
    sr = sr_ref[...]
    gr = gr_ref[...]
    silu_gr = gr * jax.nn.sigmoid(gr)
    ya_parts, yr_parts = [], []
    for hh in range(HA):
        sl = slice(hh * DVA, (hh + 1) * DVA)
        ya_parts.append(_rms_rows(oa_ref[:, sl]) * sa * (1.0 - lam_init))
        yr_parts.append(silu_gr[:, sl] * (_rms_rows(or_ref[:, sl]) * sr))
    ya = jnp.concatenate(ya_parts, axis=-1).astype(BF16)
    yr = jnp.concatenate(yr_parts, axis=-1).astype(BF16)
    y = (jax.nn.sigmoid(ga_ref[...]) * _dot(ya, wba_ref[...])
         + jax.nn.sigmoid(gb_ref[...]) * _dot(yr, wbr_ref[...]))
    out = _dot(y.astype(BF16), wo_ref[...])
    x1 = x_ref[...] + g1_ref[...] * out
    x1_ref[...] = x1
    h2f = _rms_rows(x1) * n2_ref[...] * (1.0 + sc2_ref[...]) + sh2_ref[...]
    h2 = h2f.astype(BF16)
    h2_ref[...] = h2f.T.astype(BF16) if h2_transposed else h2
    qp_ref[...] = _dot(h2, wpq_ref[...]).astype(BF16)


def out_mix(x, oa, orr, gr, ga, gb, g1, sh2, sc2, norm2, subln_a, subln_r,
            w_ba_b, w_br_b, w_o_b, w_pq_b, lam_init, tm):
    B, L, D = x.shape
    nq = w_pq_b.shape[1]
    row = lambda n: pl.BlockSpec((None, tm, n), lambda b, i: (b, i, 0))
    mod = pl.BlockSpec((None, 1, D), lambda b, i: (b, 0, 0))
    full = lambda a: pl.BlockSpec(a.shape, lambda b, i: (0,) * a.ndim)
    n2 = norm2.reshape(1, D)
    sa = subln_a.reshape(1, DVA)
    sr = subln_r.reshape(1, DVR)
    h2_transposed = tm % LANES == 0
    nt = L // tm
    if h2_transposed:
        h2_spec = pl.BlockSpec((D, tm), lambda b, i: (0, b * nt + i))
        h2_shape = jax.ShapeDtypeStruct((D, B * L), BF16)
    else:
        h2_spec = row(D)
        h2_shape = jax.ShapeDtypeStruct((B, L, D), BF16)
    x1, h2, qp = pl.pallas_call(
        functools.partial(_outmix_kernel, lam_init=lam_init, h2_transposed=h2_transposed),
        grid=(B, nt),
        in_specs=[row(D), row(WA), row(WR), row(WR), row(D), row(D), mod, mod, mod,
                  full(n2), full(sa), full(sr), full(w_ba_b), full(w_br_b), full(w_o_b), full(w_pq_b)],
        out_specs=[row(D), h2_spec, row(nq)],
        out_shape=[jax.ShapeDtypeStruct((B, L, D), F32), h2_shape,
                   jax.ShapeDtypeStruct((B, L, nq), BF16)],
        compiler_params=_params(("arbitrary", "arbitrary")),
        name="out_mix",
    )(x, oa, orr, gr, ga, gb, g1.reshape(B, 1, D), sh2.reshape(B, 1, D), sc2.reshape(B, 1, D),
      n2, sa, sr, w_ba_b, w_br_b, w_o_b, w_pq_b)
    h2t = h2 if h2_transposed else h2.reshape(B * L, D).T
    return x1, h2t, qp


UNRANKED = float(PEER_TOPK * PEER_TOPK)


def _topk_rows(s, k, break_ties):
    n = s.shape[0]
    iota = lax.broadcasted_iota(jnp.int32, s.shape, 0).astype(F32)
    work = s
    rank = jnp.full(s.shape, UNRANKED, F32)
    vals = []
    for r in range(k):
        m = jnp.max(work, axis=0, keepdims=True)
        sel = work == m
        if break_ties:
            idx = jnp.min(jnp.where(sel, iota, float(n)), axis=0, keepdims=True)
            sel = iota == idx
        rank = jnp.where(sel, float(r), rank)
        work = jnp.where(sel, -jnp.inf, work)
        vals.append(m)
    return vals, rank


def _ranked_count(rank):
    return jnp.sum((rank < UNRANKED).astype(F32), axis=0, keepdims=True)


SUBLANES = 8
_HEAD_A = SUBLANES
_CAND_NB = [PEER_TOPK] + [SUBLANES] * (_HEAD_A - 1)


def _route_chunk(s1, s2, break_ties):
    v1, rank1 = _topk_rows(s1, PEER_TOPK, break_ties)
    v2, rank2 = _topk_rows(s2, PEER_TOPK, break_ties)
    v2m = jnp.concatenate(v2, axis=0)
    v1t = jnp.concatenate(v1[_HEAD_A:], axis=0)
    blocks = [v1[a] + v2m[0:nb] for a, nb in enumerate(_CAND_NB)] + [v1t + v2[0]]
    cand = jnp.concatenate(blocks, axis=0)
    _, crank = _topk_rows(cand, PEER_TOPK, break_ties)
    k = float(PEER_TOPK)
    bad = ((_ranked_count(rank1) != k) | (_ranked_count(rank2) != k) | (_ranked_count(crank) != k))
    nbad = jnp.sum(bad.astype(jnp.int32))
    sel = crank < UNRANKED
    cmax = v1[0] + v2[0]
    z = jnp.sum(jnp.where(sel, jnp.exp(cand - cmax), 0.0), axis=0, keepdims=True)
    self32 = sel.astype(F32)
    cidx = jnp.zeros(s1.shape, F32)
    lo = 0
    for a, nb in enumerate(_CAND_NB):
        cnt = jnp.sum(self32[lo:lo + nb], axis=0, keepdims=True)
        cidx = jnp.where(rank1 == float(a), cnt, cidx)
        lo += nb
    for a in range(_HEAD_A, PEER_TOPK):
        cidx = jnp.where(rank1 == float(a), self32[lo + a - _HEAD_A:lo + a - _HEAD_A + 1], cidx)
    e1 = jnp.exp(s1 - v1[0]) / z
    e2 = jnp.exp(s2 - v2[0])
    return (e1, e2, rank2, cidx), nbad


def _route_kernel(q_ref, keys_ref, e1_ref, e2_ref, r2_ref, c_ref, s1_sc, s2_sc):
    q = q_ref[...]
    s1_sc[...] = _dot_nt(keys_ref[0], q[:, 0:KEY_DIM])
    s2_sc[...] = _dot_nt(keys_ref[1], q[:, KEY_DIM:2 * KEY_DIM])

    def store(sl, res):
        e1, e2, rank2, cidx = res
        e1_ref[:, sl] = e1
        e2_ref[:, sl] = e2.astype(BF16)
        r2_ref[:, sl] = rank2.astype(BF16)
        c_ref[:, sl] = cidx

    group = 2
    assert s1_sc.shape[1] % (group * LANES) == 0

    def chunks(ci, carry):
        sls = [pl.ds(pl.multiple_of((ci * group + g) * LANES, LANES), LANES) for g in range(group)]
        fast = [_route_chunk(s1_sc[:, sl], s2_sc[:, sl], break_ties=False) for sl in sls]
        for sl, (res, _) in zip(sls, fast):
            store(sl, res)
        for sl, (_, nbad) in zip(sls, fast):
            @pl.when(nbad > 0)
            def _():
                store(sl, _route_chunk(s1_sc[:, sl], s2_sc[:, sl], break_ties=True)[0])
        return carry

    lax.fori_loop(0, s1_sc.shape[1] // (group * LANES), chunks, 0)


def peer_route(qp, keys_b, tm):
    T = qp.shape[0]
    out = pl.BlockSpec((None, N_KEYS, tm), lambda t, h: (h, 0, t))
    shp = lambda dt: jax.ShapeDtypeStruct((PEER_HEADS, N_KEYS, T), dt)
    return pl.pallas_call(
        _route_kernel,
        grid=(T // tm, PEER_HEADS),
        in_specs=[pl.BlockSpec((tm, 2 * KEY_DIM), lambda t, h: (t, h)),
                  pl.BlockSpec((2, N_KEYS, KEY_DIM), lambda t, h: (h, 0, 0))],
        out_specs=[out, out, out, out],
        out_shape=[shp(F32), shp(BF16), shp(BF16), shp(F32)],
        scratch_shapes=[pltpu.VMEM((N_KEYS, tm), F32), pltpu.VMEM((N_KEYS, tm), F32)],
        compiler_params=_params(("arbitrary", "arbitrary")),
        name="peer_route",
    )(qp, keys_b)


BF16_ROWS = 16


def _bcast_rows_bf16(row, n):
    t = row.shape[1]
    tile = jnp.broadcast_to(row, (BF16_ROWS, t)).astype(BF16)
    return jnp.broadcast_to(tile[None], (n // BF16_ROWS, BF16_ROWS, t)).reshape(n, t)


def _peer_kernel(h2_ref, u_ref, vt_ref, e1_ref, e2_ref, r2_ref, c_ref, x1_ref, g2_ref, fn_ref,
                 y_ref, acc_sc, *, ni):
    e = pl.program_id(1)

    @pl.when(e == 0)
    def _():
        acc_sc[...] = jnp.zeros(acc_sc.shape, F32)

    h2 = h2_ref[...]
    pair = 2 * N_KEYS
    a_pairs = [_dot(u_ref[p * pair:(p + 1) * pair, :], h2) for p in range(ni // 2)]
    for p in range(ni // 2):
        ws = []
        for jj in range(2):
            i = e * ni + 2 * p + jj
            a = a_pairs[p][jj * N_KEYS:(jj + 1) * N_KEYS]
            g = jnp.zeros(a.shape, BF16)
            for hh in range(PEER_HEADS):
                c_row = _bcast_rows_bf16(c_ref[hh, pl.ds(i, 1), :], N_KEYS)
                e1_row = _bcast_rows_bf16(e1_ref[hh, pl.ds(i, 1), :], N_KEYS)
                g = g + jnp.where(r2_ref[hh] < c_row, e2_ref[hh] * e1_row, jnp.zeros_like(g))
            act = 0.5 * a * (1.0 + lax.erf(a * (2.0 ** -0.5)))
            ws.append(g * act.astype(BF16))
        w = jnp.concatenate(ws, axis=0)
        acc_sc[...] += _dot(vt_ref[:, p * pair:(p + 1) * pair], w)

    @pl.when(e == pl.num_programs(1) - 1)
    def _():
        x2 = x1_ref[...] + g2_ref[...] * acc_sc[...].T
        y_ref[...] = _rms_rows(x2) * fn_ref[...]


def peer_experts(h2t, u_b, vt_b, e1, e2, r2, cc, x1, g2, final_norm, tm, ni=16):
    D, T = h2t.shape
    nb = ni * N_KEYS
    tok = pl.BlockSpec((tm, D), lambda t, e: (t, 0))
    gate = pl.BlockSpec((PEER_HEADS, N_KEYS, tm), lambda t, e: (0, 0, t))
    L = T // g2.shape[0]
    if L % tm == 0:
        g2_arg = g2.reshape(g2.shape[0], 1, D)
        g2_spec = pl.BlockSpec((None, 1, D), lambda t, e: (t // (L // tm), 0, 0))
    else:
        g2_arg = jnp.repeat(g2, L, axis=0)
        g2_spec = tok
    return pl.pallas_call(
        functools.partial(_peer_kernel, ni=ni),
        grid=(T // tm, u_b.shape[0] // nb),
        in_specs=[pl.BlockSpec((D, tm), lambda t, e: (0, t)),
                  pl.BlockSpec((nb, D), lambda t, e: (e, 0)),
                  pl.BlockSpec((D, nb), lambda t, e: (0, e)),
                  gate, gate, gate, gate, tok, g2_spec,
                  pl.BlockSpec((1, D), lambda t, e: (0, 0))],
        out_specs=tok,
        out_shape=jax.ShapeDtypeStruct((T, D), F32),
        scratch_shapes=[pltpu.VMEM((D, tm), F32)],
        compiler_params=_params(("arbitrary", "arbitrary")),
        name="peer_experts",
    )(h2t, u_b, vt_b, e1, e2, r2, cc, x1, g2_arg, final_norm.reshape(1, D))


def _rot_tables(pos):
    inv = 1.0 / (10000.0 ** jnp.linspace(0.0, 1.0, DKR // 2, dtype=F32))
    ang = pos[:, None].astype(F32) * inv[None, :]
    cos, sin = jnp.cos(ang), jnp.sin(ang)
    return jnp.concatenate([cos, cos], axis=-1), jnp.concatenate([-sin, sin], axis=-1)


def _pick_tile(n, pref):
    t = min(n, pref)
    assert n % t == 0, (n, t)
    return t


def _trunk(x, mods, pos, lam_init, lamv, rel_bias, w, attend, values_transposed, state0, ret_chunk,
           final_norm):
    (norm1, norm2, w_in_b, subln_a, subln_r, w_ba_b, w_br_b, w_o_b, w_pq_b, keys_b, u_b, vt_b) = w
    sh1, sc1, g1, sh2, sc2, g2 = mods
    B, L, D = x.shape
    T = B * L
    cos2, sin2 = _rot_tables(pos)
    tm = _pick_tile(L, 256)
    ka, va, qab, kab, vab, qrb, kr, vrb, gr, ga, gb, *vt = in_proj(
        x, sh1, sc1, norm1, w_in_b, cos2, sin2, tm, emit_vt=values_transposed)
    oa = attend(qab, kab, vt[0] if values_transposed else vab)
    orr, st = retention(qrb, kr, vrb, state0, ret_chunk)
    x1, h2t, qp = out_mix(x, oa, orr, gr, ga, gb, g1, sh2, sc2, norm2, subln_a, subln_r,
                          w_ba_b, w_br_b, w_o_b, w_pq_b, lam_init, tm)
    e1, e2, r2, cc = peer_route(qp.reshape(T, -1), keys_b, _pick_tile(T, 1024))
    y = peer_experts(h2t, u_b, vt_b, e1, e2, r2, cc, x1.reshape(T, D), g2,
                     final_norm, _pick_tile(T, 512))
    return y.reshape(B, L, D), ka, va, st


def kernel(x_prompt, x_sample, cache_k, cache_v, state_ret, c_prompt, c_sample, w_ada, b_ada, norm1,
           norm2, w_in, lam_q1, lam_k1, lam_q2, lam_k2, subln_a, subln_r, w_ba, w_br, w_o, rel_bias,
           w_pq, peer_keys, peer_u, peer_v, final_norm):
    depth = w_ada.shape[0]
    assert depth == 1, "the fused final norm assumes a single layer"
    Bp, Lp, D = x_prompt.shape
    Bs, Ls, _ = x_sample.shape
    past = cache_k.shape[2]
    pos_p = jnp.arange(Lp, dtype=jnp.int32)
    pos_s = past + jnp.arange(Ls, dtype=jnp.int32)
    lk = past + Ls
    lk_pad = -(-lk // LANES) * LANES
    k_pos_s = jnp.arange(lk_pad, dtype=jnp.int32)
    k_valid_s = k_pos_s < lk

    l = 0
    lam_init = 0.8 - 0.6 * math.exp(-0.3 * l)
    lamv = jnp.stack([lam_q1[l], lam_k1[l], lam_q2[l], lam_k2[l]]).astype(F32)
    mod = ada_mod(jnp.concatenate([c_prompt, c_sample], axis=0), w_ada[l], b_ada[l])
    mods = jnp.split(mod, 6, axis=-1)
    mods_p = [m[:Bp] for m in mods]
    mods_s = [m[Bp:] for m in mods]
    w = (norm1[l], norm2[l], w_in[l].astype(BF16), subln_a[l], subln_r[l], w_ba[l].astype(BF16),
         w_br[l].astype(BF16), w_o[l].astype(BF16), w_pq[l].astype(BF16),
         peer_keys[l].reshape(PEER_HEADS * 2, N_KEYS, KEY_DIM).astype(BF16),
         peer_u[l].astype(BF16), peer_v[l].T.astype(BF16))

    attend_p = lambda q, k, v: attn_prompt(q, k, v, rel_bias, lamv, lam_init)
    zero_state = jnp.zeros((Bp, HR, DKR, DVR), F32)
    yp, kp, vp, sp = _trunk(x_prompt, mods_p, pos_p, lam_init, lamv, rel_bias, w, attend_p, True,
                            zero_state, _pick_tile(Lp, 256), final_norm)

    def attend_s(q, k, v):
        padk = jnp.zeros((Bs, lk_pad - lk, WA), BF16)
        kc = cache_k[l].reshape(Bs, past, HA * 2 * DA).astype(BF16)
        vc = cache_v[l].reshape(Bs, past, WA).astype(BF16)
        k_all = jnp.concatenate([kc, k, padk], axis=1)
        v_all = jnp.concatenate([vc, v, padk], axis=1)
        return attn_small(q, k_all, v_all, pos_s, k_pos_s, k_valid_s, rel_bias, lamv, lam_init)

    ys, ks, vs, ss = _trunk(x_sample, mods_s, pos_s, lam_init, lamv, rel_bias, w, attend_s, False,
                            state_ret[l].astype(F32), Ls, final_norm)
    return (yp, ys, kp[None], vp[None], sp[None], ks[None], vs[None], ss[None])
```

```python
import functools
import math

import jax
import jax.numpy as jnp
from jax import lax
from jax.experimental import pallas as pl
from jax.experimental.pallas import tpu as pltpu

F32 = jnp.float32
BF16 = jnp.bfloat16

CHUNK = 64
HA = 4
DA = 64
DVA = 2 * DA
HR = 4
DKR = 128
DVR = 128
N_BUCKETS = 32
MAX_DIST = 128
PEER_HEADS = 8
N_KEYS = 128
KEY_DIM = 128
PEER_TOPK = 16
EPS = 1e-6
WA = HA * DVA
WR = HR * DVR
NEG = -1e30
LOG2E = math.log2(math.e)
LANES = 128
SUBLANES = 8
BF16_ROWS = 2 * SUBLANES
VMEM_LIMIT = 56 * 1024 * 1024

TOKEN_TILE = 256
ATTN_TQ = 256
ATTN_GROUP = 16
ATTN_TAIL = 4
ROUTE_TILE = 1024
EXPERT_TILE = 512
EXPERT_ROWS = 16
PEER_SUB = 2


def _params(sem, vmem=VMEM_LIMIT):
    return pltpu.CompilerParams(dimension_semantics=sem, vmem_limit_bytes=vmem)


def _dot(a, b):
    return jnp.dot(a, b, preferred_element_type=F32)


def _dot_nt(a, b):
    return lax.dot_general(a, b, (((1,), (1,)), ((), ())), preferred_element_type=F32)


def _dot_tn(a, b):
    return lax.dot_general(a, b, (((0,), (0,)), ((), ())), preferred_element_type=F32)


def _rms_rows(x):
    return x * lax.rsqrt(jnp.mean(x * x, axis=-1, keepdims=True) + EPS)


def _ada_kernel(c_ref, w_ref, b_ref, o_ref):
    c = c_ref[...]
    s = (c * jax.nn.sigmoid(c)).astype(BF16)
    o_ref[...] = _dot(s, w_ref[...].astype(BF16)) + b_ref[...]


def ada_mod(c, w_ada, b_ada, tn=1024):
    nb, d = c.shape
    n = w_ada.shape[1]
    return pl.pallas_call(
        _ada_kernel,
        grid=(n // tn,),
        in_specs=[pl.BlockSpec((nb, d), lambda j: (0, 0)),
                  pl.BlockSpec((d, tn), lambda j: (0, j)),
                  pl.BlockSpec((1, tn), lambda j: (0, j))],
        out_specs=pl.BlockSpec((nb, tn), lambda j: (0, j)),
        out_shape=jax.ShapeDtypeStruct((nb, n), F32),
        compiler_params=_params(("arbitrary",)),
        name="ada_mod",
    )(c, w_ada, b_ada.reshape(1, n))


def _inproj_kernel(x_ref, sh_ref, sc_ref, n1_ref, w_ref, cos_ref, sin_ref,
                   ka_ref, va_ref, qab_ref, kab_ref, vab_ref, qrb_ref, kr_ref, vrb_ref,
                   gr_ref, ga_ref, gb_ref, vt_ref=None):
    x = x_ref[...]
    h = _rms_rows(x) * n1_ref[...] * (1.0 + sc_ref[...]) + sh_ref[...]
    hb = h.astype(BF16)

    d_model = x.shape[1]
    sizes = (WA, WA, WA, WR, WR, WR, WR, d_model, d_model)
    starts = [sum(sizes[:g]) for g in range(len(sizes))]

    def proj(g):
        return _dot(hb, w_ref[:, starts[g]:starts[g] + sizes[g]])

    qa = proj(0)
    qab_ref[...] = (qa * (DA ** -0.5 * LOG2E)).astype(BF16)
    ka = proj(1)
    va = proj(2)
    for hh in range(HA):
        ka_ref[:, hh, :] = ka[:, hh * DVA:(hh + 1) * DVA]
        va_ref[:, hh, :] = va[:, hh * DVA:(hh + 1) * DVA]
    kab_ref[...] = ka.astype(BF16)
    vab_ref[...] = va.astype(BF16)
    if vt_ref is not None:
        vt_ref[...] = va.T.astype(BF16)
    cos2 = cos_ref[...]
    sin2 = sin_ref[...]

    def rot(z):
        parts = []
        for hh in range(HR):
            zh = z[:, hh * DKR:(hh + 1) * DKR]
            parts.append(zh * cos2 + pltpu.roll(zh, DKR // 2, 1) * sin2)
        return jnp.concatenate(parts, axis=-1)

    qrb_ref[...] = rot(proj(3)).astype(BF16)
    kr_ref[...] = rot(proj(4)) * (DKR ** -0.5)
    vrb_ref[...] = proj(5).astype(BF16)
    gr_ref[...] = proj(6)
    ga_ref[...] = proj(7)
    gb_ref[...] = proj(8)


def in_proj(x, sh1, sc1, norm1, w_in_b, cos2, sin2, tm, emit_vt):
    B, L, D = x.shape
    d_in = w_in_b.shape[1]
    row = lambda n: pl.BlockSpec((None, tm, n), lambda b, i: (b, i, 0))
    mod = pl.BlockSpec((None, 1, D), lambda b, i: (b, 0, 0))
    f = lambda n, dt: jax.ShapeDtypeStruct((B, L, n), dt)
    cache = pl.BlockSpec((None, tm, HA, DVA), lambda b, i: (b, i, 0, 0))
    cache_shape = jax.ShapeDtypeStruct((B, L, HA, DVA), F32)
    vt_spec = [pl.BlockSpec((None, WA, tm), lambda b, i: (b, 0, i))] if emit_vt else []
    vt_shape = [jax.ShapeDtypeStruct((B, WA, L), BF16)] if emit_vt else []
    return pl.pallas_call(
        _inproj_kernel,
        grid=(B, L // tm),
        in_specs=[row(D), mod, mod,
                  pl.BlockSpec((1, D), lambda b, i: (0, 0)),
                  pl.BlockSpec((D, d_in), lambda b, i: (0, 0)),
                  pl.BlockSpec((tm, DKR), lambda b, i: (i, 0)),
                  pl.BlockSpec((tm, DKR), lambda b, i: (i, 0))],
        out_specs=[cache, cache, row(WA), row(WA), row(WA), row(WR), row(WR), row(WR),
                   row(WR), row(D), row(D)] + vt_spec,
        out_shape=[cache_shape, cache_shape, f(WA, BF16), f(WA, BF16), f(WA, BF16),
                   f(WR, BF16), f(WR, F32), f(WR, BF16), f(WR, F32), f(D, F32), f(D, F32)]
        + vt_shape,
        compiler_params=_params(("arbitrary", "arbitrary")),
        name="in_proj",
    )(x, sh1.reshape(B, 1, D), sc1.reshape(B, 1, D), norm1.reshape(1, D), w_in_b, cos2, sin2)


def _t5_bucket(rel):
    nb = N_BUCKETS // 2
    ret = jnp.where(rel > 0, nb, 0)
    n = jnp.abs(rel)
    max_exact = nb // 2
    nf = jnp.maximum(n, max_exact).astype(F32)
    large = max_exact + (jnp.log(nf / max_exact) / math.log(MAX_DIST / max_exact)
                         * (nb - max_exact)).astype(jnp.int32)
    large = jnp.minimum(large, nb - 1)
    return ret + jnp.where(n < max_exact, n, large)


def _bias_from_buckets(bkt, rb_ref, h, shift):
    val = jnp.where(bkt < 0, NEG, 0.0).astype(F32)
    for n in range(N_BUCKETS):
        val = jnp.where(bkt == n, (rb_ref[n, h] - shift) * LOG2E, val)
    return val


def _lam_value(lv_ref, lam_init):
    lv = lv_ref[...]
    a = jnp.sum(lv[0:1] * lv[1:2], axis=-1, keepdims=True)
    b = jnp.sum(lv[2:3] * lv[3:4], axis=-1, keepdims=True)
    return jnp.exp(a) - jnp.exp(b) + lam_init


def _block_diag_q(q):
    lane = lax.broadcasted_iota(jnp.int32, q.shape, 1)
    zero = jnp.zeros_like(q)
    return jnp.concatenate([jnp.where(lane < DA, q, zero), jnp.where(lane >= DA, q, zero)], axis=0)


def _attn_kernel(rb_ref, lv_ref, q_ref, k_ref, vt_ref, bkt_ref, o_ref,
                 bias_sc, m_sc, acc_sc, va_sc, *, tq, nsub, lam_init):
    h = pl.program_id(1)
    i = pl.program_id(2)

    @pl.when(i == 0)
    def _():
        far = rb_ref[N_BUCKETS // 2 - 1, h]
        for t in range(2):
            bias_sc[t] = _bias_from_buckets(bkt_ref[t], rb_ref, h, far)
        va_sc[0:DVA, :] = vt_ref[...]
        va_sc[DVA:, :] = jnp.ones((BF16_ROWS, va_sc.shape[1]), BF16)

    qt = q_ref[...].astype(F32).T
    drow = lax.broadcasted_iota(jnp.int32, qt.shape, 0)
    qbd_t = jnp.concatenate([jnp.where(drow < DA, qt, 0.0), jnp.where(drow >= DA, qt, 0.0)],
                            axis=1).astype(BF16)
    m_sc[...] = jnp.full(m_sc.shape, NEG, F32)
    acc_sc[...] = jnp.zeros(acc_sc.shape, F32)

    def scores(off, tk, bias=None):
        s = _dot(k_ref[pl.ds(off, tk), :], qbd_t)
        if bias is not None:
            s = s + jnp.concatenate([bias, bias], axis=1)
        return s

    def absorb(s, off, tk):
        m_prev = m_sc[...]
        m_new = jnp.maximum(m_prev, jnp.max(s, axis=0, keepdims=True))
        alpha = jnp.exp2(m_prev - m_new)
        p = jnp.exp2(s - m_new).astype(BF16)
        acc_sc[...] = alpha * acc_sc[...] + _dot(va_sc[:, pl.ds(off, tk)], p)
        m_sc[...] = m_new

    def tiles(specs):
        ss = [scores(*sp) for sp in specs]
        for s, sp in zip(ss, specs):
            absorb(s, sp[0], sp[1])

    ntile = jnp.maximum(i - 1, 0)
    nfull = ntile // nsub

    def far(first_tile, n):
        return [(pl.multiple_of((first_tile + j) * tq, tq), tq) for j in range(n)]

    def far_body(t, carry):
        tiles(far(t * nsub, nsub))
        return carry

    lax.fori_loop(0, nfull, far_body, 0)
    done = nfull * nsub
    n = nsub // 2
    while n >= ATTN_TAIL:
        take = ((ntile - done) // n) > 0

        @pl.when(take)
        def _(done=done, n=n):
            tiles(far(done, n))

        done = done + jnp.where(take, n, 0)
        n //= 2

    rest = ntile - done
    for r in range(ATTN_TAIL):
        @pl.when((i >= 1) & (rest == r))
        def _(r=r):
            off = pl.multiple_of((i - 1) * tq, tq)
            tiles(far(done, r) + [(off, tq, bias_sc[1]),
                                  (pl.multiple_of(off + tq, tq), tq, bias_sc[0])])

    @pl.when(i == 0)
    def _():
        tiles([(0, tq, bias_sc[0])])

    lam = _lam_value(lv_ref, lam_init)
    o = acc_sc[0:DVA, :] / acc_sc[DVA:DVA + 1, :]
    o_ref[...] = (o[:, 0:tq] - lam * o[:, tq:2 * tq]).T


def attn_prompt(qab, kab, vt, rel_bias, lamv, lam_init, tq=ATTN_TQ, nsub=ATTN_GROUP):
    B, S, _ = qab.shape
    assert S % tq == 0
    c = jnp.arange(tq, dtype=jnp.int32)[:, None]
    r = jnp.arange(tq, dtype=jnp.int32)[None, :]
    diag = jnp.where((c // CHUNK) <= (r // CHUNK), _t5_bucket(c - r), -1)
    prev = _t5_bucket(c - r - tq)
    bkt = jnp.stack([diag, prev]).astype(jnp.int32)
    smem = pl.BlockSpec(memory_space=pltpu.SMEM)
    return pl.pallas_call(
        functools.partial(_attn_kernel, tq=tq, nsub=nsub, lam_init=lam_init),
        grid=(B, HA, S // tq),
        in_specs=[smem,
                  pl.BlockSpec((4, DA), lambda b, h, i: (0, 0)),
                  pl.BlockSpec((None, tq, DVA), lambda b, h, i: (b, i, h)),
                  pl.BlockSpec((None, S, DVA), lambda b, h, i: (b, 0, h)),
                  pl.BlockSpec((None, DVA, S), lambda b, h, i: (b, h, 0)),
                  pl.BlockSpec((2, tq, tq), lambda b, h, i: (0, 0, 0))],
        out_specs=pl.BlockSpec((None, tq, DVA), lambda b, h, i: (b, i, h)),
        out_shape=jax.ShapeDtypeStruct((B, S, WA), F32),
        scratch_shapes=[pltpu.VMEM((2, tq, tq), F32),
                        pltpu.VMEM((1, 2 * tq), F32),
                        pltpu.VMEM((DVA + BF16_ROWS, 2 * tq), F32),
                        pltpu.VMEM((DVA + BF16_ROWS, S), BF16)],
        compiler_params=_params(("arbitrary", "arbitrary", "arbitrary")),
        name="attn_prompt",
    )(rel_bias, lamv, qab, kab, vt, bkt)


def _attn_small_kernel(rb_ref, lv_ref, q_ref, k_ref, v_ref, bkt_ref, o_ref, *, lq, lam_init):
    h = pl.program_id(1)
    bias = _bias_from_buckets(bkt_ref[...], rb_ref, h, 0.0)
    qbd = _block_diag_q(q_ref[...])
    s = _dot_nt(qbd, k_ref[...]) + jnp.concatenate([bias, bias], axis=0)
    m = jnp.max(s, axis=-1, keepdims=True)
    p = jnp.exp2(s - m)
    l = jnp.sum(p, axis=-1, keepdims=True)
    o = _dot(p.astype(BF16), v_ref[...]) / l
    lam = _lam_value(lv_ref, lam_init)
    o_ref[...] = o[0:lq] - lam * o[lq:2 * lq]


def attn_small(qab, k_all, v_all, q_pos, k_pos, k_valid, rel_bias, lamv, lam_init):
    B, Lq, _ = qab.shape
    Lk = k_all.shape[1]
    visible = ((k_pos[None, :] // CHUNK) <= (q_pos[:, None] // CHUNK)) & k_valid[None, :]
    bkt = jnp.where(visible, _t5_bucket(k_pos[None, :] - q_pos[:, None]), -1).astype(jnp.int32)
    smem = pl.BlockSpec(memory_space=pltpu.SMEM)
    return pl.pallas_call(
        functools.partial(_attn_small_kernel, lq=Lq, lam_init=lam_init),
        grid=(B, HA),
        in_specs=[smem,
                  pl.BlockSpec((4, DA), lambda b, h: (0, 0)),
                  pl.BlockSpec((None, Lq, DVA), lambda b, h: (b, 0, h)),
                  pl.BlockSpec((None, Lk, DVA), lambda b, h: (b, 0, h)),
                  pl.BlockSpec((None, Lk, DVA), lambda b, h: (b, 0, h)),
                  pl.BlockSpec((Lq, Lk), lambda b, h: (0, 0))],
        out_specs=pl.BlockSpec((None, Lq, DVA), lambda b, h: (b, 0, h)),
        out_shape=jax.ShapeDtypeStruct((B, Lq, WA), F32),
        compiler_params=_params(("arbitrary", "arbitrary")),
        name="attn_sample",
    )(rel_bias, lamv, qab, k_all, v_all, bkt)


def _ret_kernel(lg_ref, q_ref, k_ref, v_ref, s0_ref, o_ref, so_ref, state_sc, decay_sc, *, C):
    c = pl.program_id(1)

    @pl.when(c == 0)
    def _():
        state_sc[...] = s0_ref[...]
        r = lax.broadcasted_iota(jnp.int32, (C, C), 0)
        cc = lax.broadcasted_iota(jnp.int32, (C, C), 1)
        diff = (r - cc).astype(F32)
        for h in range(HR):
            decay_sc[h] = jnp.where(diff >= 0, jnp.exp(jnp.maximum(diff, 0.0) * lg_ref[h]), 0.0)

    n = lax.broadcasted_iota(jnp.int32, (C, 1), 0).astype(F32)
    for h in range(HR):
        lg = lg_ref[h]
        cols = slice(h * DKR, (h + 1) * DKR)
        xi = jnp.exp((n + 1.0) * lg)
        zeta = jnp.exp((C - 1.0 - n) * lg)
        q = q_ref[:, cols]
        k = k_ref[:, cols]
        v = v_ref[:, cols]
        state = state_sc[h]
        scores = _dot_nt(q, k.astype(BF16)) * decay_sc[h]
        intra = _dot(scores.astype(BF16), v)
        cross = _dot(q, state.astype(BF16)) * xi
        o_ref[:, cols] = intra + cross
        kz = (k * zeta).astype(BF16)
        state_sc[h] = jnp.exp(C * lg) * state + _dot_tn(kz, v)

    @pl.when(c == pl.num_programs(1) - 1)
    def _():
        so_ref[...] = state_sc[...]


def retention(qrb, kr, vrb, state0, C):
    B, L, _ = qrb.shape
    lg = jnp.log(1.0 - 2.0 ** (-5.0 - jnp.arange(HR, dtype=F32)))
    blk = pl.BlockSpec((None, C, WR), lambda b, c: (b, c, 0))
    st = pl.BlockSpec((None, HR, DKR, DVR), lambda b, c: (b, 0, 0, 0))
    return pl.pallas_call(
        functools.partial(_ret_kernel, C=C),
        grid=(B, L // C),
        in_specs=[pl.BlockSpec(memory_space=pltpu.SMEM), blk, blk, blk, st],
        out_specs=[blk, st],
        out_shape=[jax.ShapeDtypeStruct((B, L, WR), F32),
                   jax.ShapeDtypeStruct((B, HR, DKR, DVR), F32)],
        scratch_shapes=[pltpu.VMEM((HR, DKR, DVR), F32), pltpu.VMEM((HR, C, C), F32)],
        compiler_params=_params(("arbitrary", "arbitrary")),
        name="retention",
    )(lg, qrb, kr, vrb, state0)


def _outmix_kernel(x_ref, oa_ref, or_ref, gr_ref, ga_ref, gb_ref, g1_ref, sh2_ref, sc2_ref, n2_ref,
                   sa_ref, sr_ref, wba_ref, wbr_ref, wo_ref, wpq_ref,
                   x1_ref, h2_ref, qp_ref, *, lam_init, h2_transposed):
    sa = sa_ref[...]
    sr = sr_ref[...]
    gr = gr_ref[...]
    silu_gr = gr * jax.nn.sigmoid(gr)
    ya_parts, yr_parts = [], []
    for hh in range(HA):
        sl = slice(hh * DVA, (hh + 1) * DVA)
        ya_parts.append(_rms_rows(oa_ref[:, sl]) * sa * (1.0 - lam_init))
        yr_parts.append(silu_gr[:, sl] * (_rms_rows(or_ref[:, sl]) * sr))
    ya = jnp.concatenate(ya_parts, axis=-1).astype(BF16)
    yr = jnp.concatenate(yr_parts, axis=-1).astype(BF16)
    y = (jax.nn.sigmoid(ga_ref[...]) * _dot(ya, wba_ref[...])
         + jax.nn.sigmoid(gb_ref[...]) * _dot(yr, wbr_ref[...]))
    out = _dot(y.astype(BF16), wo_ref[...])
    x1 = x_ref[...] + g1_ref[...] * out
    x1_ref[...] = x1
    h2f = _rms_rows(x1) * n2_ref[...] * (1.0 + sc2_ref[...]) + sh2_ref[...]
    h2 = h2f.astype(BF16)
    h2_ref[...] = h2f.T.astype(BF16) if h2_transposed else h2
    qp_ref[...] = _dot(h2, wpq_ref[...]).astype(BF16)


def out_mix(x, oa, orr, gr, ga, gb, g1, sh2, sc2, norm2, subln_a, subln_r,
            w_ba_b, w_br_b, w_o_b, w_pq_b, lam_init, tm):
    B, L, D = x.shape
    nq = w_pq_b.shape[1]
    row = lambda n: pl.BlockSpec((None, tm, n), lambda b, i: (b, i, 0))
    mod = pl.BlockSpec((None, 1, D), lambda b, i: (b, 0, 0))
    full = lambda a: pl.BlockSpec(a.shape, lambda b, i: (0,) * a.ndim)
    n2 = norm2.reshape(1, D)
    sa = subln_a.reshape(1, DVA)
    sr = subln_r.reshape(1, DVR)
    h2_transposed = tm % LANES == 0
    nt = L // tm
    if h2_transposed:
        h2_spec = pl.BlockSpec((D, tm), lambda b, i: (0, b * nt + i))
        h2_shape = jax.ShapeDtypeStruct((D, B * L), BF16)
    else:
        h2_spec = row(D)
        h2_shape = jax.ShapeDtypeStruct((B, L, D), BF16)
    x1, h2, qp = pl.pallas_call(
        functools.partial(_outmix_kernel, lam_init=lam_init, h2_transposed=h2_transposed),
        grid=(B, nt),
        in_specs=[row(D), row(WA), row(WR), row(WR), row(D), row(D), mod, mod, mod,
                  full(n2), full(sa), full(sr), full(w_ba_b), full(w_br_b), full(w_o_b), full(w_pq_b)],
        out_specs=[row(D), h2_spec, row(nq)],
        out_shape=[jax.ShapeDtypeStruct((B, L, D), F32), h2_shape,
                   jax.ShapeDtypeStruct((B, L, nq), BF16)],
        compiler_params=_params(("arbitrary", "arbitrary")),
        name="out_mix",
    )(x, oa, orr, gr, ga, gb, g1.reshape(B, 1, D), sh2.reshape(B, 1, D), sc2.reshape(B, 1, D),
      n2, sa, sr, w_ba_b, w_br_b, w_o_b, w_pq_b)
    h2t = h2 if h2_transposed else h2.reshape(B * L, D).T
    return x1, h2t, qp


UNRANKED = float(PEER_TOPK * PEER_TOPK)


def _topk_rows(s, k, break_ties):
    n = s.shape[0]
    iota = lax.broadcasted_iota(jnp.int32, s.shape, 0).astype(F32)
    work = s
    rank = jnp.full(s.shape, UNRANKED, F32)
    vals = []
    for r in range(k):
        m = jnp.max(work, axis=0, keepdims=True)
        sel = work == m
        if break_ties:
            idx = jnp.min(jnp.where(sel, iota, float(n)), axis=0, keepdims=True)
            sel = iota == idx
        rank = jnp.where(sel, float(r), rank)
        work = jnp.where(sel, -jnp.inf, work)
        vals.append(m)
    return vals, rank


def _ranked_count(rank):
    return jnp.sum((rank < UNRANKED).astype(F32), axis=0, keepdims=True)


_HEAD_A = SUBLANES
_CAND_NB = [PEER_TOPK] + [SUBLANES] * (_HEAD_A - 1)


def _route_chunk(s1, s2, break_ties):
    v1, rank1 = _topk_rows(s1, PEER_TOPK, break_ties)
    v2, rank2 = _topk_rows(s2, PEER_TOPK, break_ties)
    v2m = jnp.concatenate(v2, axis=0)
    v1t = jnp.concatenate(v1[_HEAD_A:], axis=0)
    blocks = [v1[a] + v2m[0:nb] for a, nb in enumerate(_CAND_NB)] + [v1t + v2[0]]
    cand = jnp.concatenate(blocks, axis=0)
    _, crank = _topk_rows(cand, PEER_TOPK, break_ties)
    k = float(PEER_TOPK)
    bad = ((_ranked_count(rank1) != k) | (_ranked_count(rank2) != k) | (_ranked_count(crank) != k))
    nbad = jnp.sum(bad.astype(jnp.int32))
    sel = crank < UNRANKED
    cmax = v1[0] + v2[0]
    z = jnp.sum(jnp.where(sel, jnp.exp(cand - cmax), 0.0), axis=0, keepdims=True)
    self32 = sel.astype(F32)
    cidx = jnp.zeros(s1.shape, F32)
    lo = 0
    for a, nb in enumerate(_CAND_NB):
        cnt = jnp.sum(self32[lo:lo + nb], axis=0, keepdims=True)
        cidx = jnp.where(rank1 == float(a), cnt, cidx)
        lo += nb
    for a in range(_HEAD_A, PEER_TOPK):
        cidx = jnp.where(rank1 == float(a), self32[lo + a - _HEAD_A:lo + a - _HEAD_A + 1], cidx)
    e1 = jnp.exp(s1 - v1[0]) / z
    e2 = jnp.exp(s2 - v2[0])
    return (e1, e2, rank2, cidx), nbad


def _route_kernel(q_ref, keys_ref, e1_ref, e2_ref, r2_ref, c_ref, s1_sc, s2_sc):
    q = q_ref[...]
    s1_sc[...] = _dot_nt(keys_ref[0], q[:, 0:KEY_DIM])
    s2_sc[...] = _dot_nt(keys_ref[1], q[:, KEY_DIM:2 * KEY_DIM])

    def store(sl, res):
        e1, e2, rank2, cidx = res
        e1_ref[:, sl] = e1
        e2_ref[:, sl] = e2.astype(BF16)
        r2_ref[:, sl] = rank2.astype(BF16)
        c_ref[:, sl] = cidx

    group = 2
    assert s1_sc.shape[1] % (group * LANES) == 0

    def chunks(ci, carry):
        sls = [pl.ds(pl.multiple_of((ci * group + g) * LANES, LANES), LANES) for g in range(group)]
        fast = [_route_chunk(s1_sc[:, sl], s2_sc[:, sl], break_ties=False) for sl in sls]
        for sl, (res, _) in zip(sls, fast):
            store(sl, res)
        for sl, (_, nbad) in zip(sls, fast):
            @pl.when(nbad > 0)
            def _():
                store(sl, _route_chunk(s1_sc[:, sl], s2_sc[:, sl], break_ties=True)[0])
        return carry

    lax.fori_loop(0, s1_sc.shape[1] // (group * LANES), chunks, 0)


def peer_route(qp, keys_b, tm):
    T = qp.shape[0]
    out = pl.BlockSpec((None, N_KEYS, tm), lambda t, h: (h, 0, t))
    shp = lambda dt: jax.ShapeDtypeStruct((PEER_HEADS, N_KEYS, T), dt)
    return pl.pallas_call(
        _route_kernel,
        grid=(T // tm, PEER_HEADS),
        in_specs=[pl.BlockSpec((tm, 2 * KEY_DIM), lambda t, h: (t, h)),
                  pl.BlockSpec((2, N_KEYS, KEY_DIM), lambda t, h: (h, 0, 0))],
        out_specs=[out, out, out, out],
        out_shape=[shp(F32), shp(BF16), shp(BF16), shp(F32)],
        scratch_shapes=[pltpu.VMEM((N_KEYS, tm), F32), pltpu.VMEM((N_KEYS, tm), F32)],
        compiler_params=_params(("arbitrary", "arbitrary")),
        name="peer_route",
    )(qp, keys_b)


def _bcast_rows_bf16(row, n):
    t = row.shape[1]
    tile = jnp.broadcast_to(row, (BF16_ROWS, t)).astype(BF16)
    return jnp.broadcast_to(tile[None], (n // BF16_ROWS, BF16_ROWS, t)).reshape(n, t)


def _peer_kernel(h2_ref, u_ref, vt_ref, e1_ref, e2_ref, r2_ref, c_ref, x1_ref, g2_ref, fn_ref,
                 y_ref, acc_sc, *, ni):
    e = pl.program_id(1)

    @pl.when(e == 0)
    def _():
        acc_sc[...] = jnp.zeros(acc_sc.shape, F32)

    h2 = h2_ref[...]
    pair = PEER_SUB * N_KEYS
    a_pairs = [_dot(u_ref[p * pair:(p + 1) * pair, :], h2) for p in range(ni // PEER_SUB)]
    for p in range(ni // PEER_SUB):
        gs = [jnp.zeros((N_KEYS, h2.shape[1]), BF16) for _ in range(PEER_SUB)]
        for hh in range(PEER_HEADS):
            r2h = r2_ref[hh]
            e2h = e2_ref[hh]
            for jj in range(PEER_SUB):
                i = e * ni + PEER_SUB * p + jj
                c_row = _bcast_rows_bf16(c_ref[hh, pl.ds(i, 1), :], N_KEYS)
                e1_row = _bcast_rows_bf16(e1_ref[hh, pl.ds(i, 1), :], N_KEYS)
                gs[jj] = gs[jj] + jnp.where(r2h < c_row, e2h * e1_row, jnp.zeros_like(e2h))
        ws = []
        for jj in range(PEER_SUB):
            a = a_pairs[p][jj * N_KEYS:(jj + 1) * N_KEYS]
            act = 0.5 * a * (1.0 + lax.erf(a * (2.0 ** -0.5)))
            ws.append(gs[jj] * act.astype(BF16))
        w = jnp.concatenate(ws, axis=0)
        acc_sc[...] += _dot(vt_ref[:, p * pair:(p + 1) * pair], w)

    @pl.when(e == pl.num_programs(1) - 1)
    def _():
        x2 = x1_ref[...] + g2_ref[...] * acc_sc[...].T
        y_ref[...] = _rms_rows(x2) * fn_ref[...]


def peer_experts(h2t, u_b, vt_b, e1, e2, r2, cc, x1, g2, final_norm, tm, ni=EXPERT_ROWS):
    D, T = h2t.shape
    nb = ni * N_KEYS
    tok = pl.BlockSpec((tm, D), lambda t, e: (t, 0))
    gate = pl.BlockSpec((PEER_HEADS, N_KEYS, tm), lambda t, e: (0, 0, t))
    L = T // g2.shape[0]
    if L % tm == 0:
        g2_arg = g2.reshape(g2.shape[0], 1, D)
        g2_spec = pl.BlockSpec((None, 1, D), lambda t, e: (t // (L // tm), 0, 0))
    else:
        g2_arg = jnp.repeat(g2, L, axis=0)
        g2_spec = tok
    return pl.pallas_call(
        functools.partial(_peer_kernel, ni=ni),
        grid=(T // tm, u_b.shape[0] // nb),
        in_specs=[pl.BlockSpec((D, tm), lambda t, e: (0, t)),
                  pl.BlockSpec((nb, D), lambda t, e: (e, 0)),
                  pl.BlockSpec((D, nb), lambda t, e: (0, e)),
                  gate, gate, gate, gate, tok, g2_spec,
                  pl.BlockSpec((1, D), lambda t, e: (0, 0))],
        out_specs=tok,
        out_shape=jax.ShapeDtypeStruct((T, D), F32),
        scratch_shapes=[pltpu.VMEM((D, tm), F32)],
        compiler_params=_params(("arbitrary", "arbitrary")),
        name="peer_experts",
    )(h2t, u_b, vt_b, e1, e2, r2, cc, x1, g2_arg, final_norm.reshape(1, D))


def _rot_tables(pos):
    inv = 1.0 / (10000.0 ** jnp.linspace(0.0, 1.0, DKR // 2, dtype=F32))
    ang = pos[:, None].astype(F32) * inv[None, :]
    cos, sin = jnp.cos(ang), jnp.sin(ang)
    return jnp.concatenate([cos, cos], axis=-1), jnp.concatenate([-sin, sin], axis=-1)


def _pick_tile(n, pref):
    t = min(n, pref)
    assert n % t == 0, (n, t)
    return t


def _trunk(x, mods, pos, lam_init, w, attend, values_transposed, state0, final_norm):
    (norm1, norm2, w_in_b, subln_a, subln_r, w_ba_b, w_br_b, w_o_b, w_pq_b, keys_b, u_b, vt_b) = w
    sh1, sc1, g1, sh2, sc2, g2 = mods
    B, L, D = x.shape
    T = B * L
    cos2, sin2 = _rot_tables(pos)
    tm = _pick_tile(L, TOKEN_TILE)
    ka, va, qab, kab, vab, qrb, kr, vrb, gr, ga, gb, *vt = in_proj(
        x, sh1, sc1, norm1, w_in_b, cos2, sin2, tm, emit_vt=values_transposed)
    oa = attend(qab, kab, vt[0] if values_transposed else vab)
    orr, st = retention(qrb, kr, vrb, state0, tm)
    x1, h2t, qp = out_mix(x, oa, orr, gr, ga, gb, g1, sh2, sc2, norm2, subln_a, subln_r,
                          w_ba_b, w_br_b, w_o_b, w_pq_b, lam_init, tm)
    e1, e2, r2, cc = peer_route(qp.reshape(T, -1), keys_b, _pick_tile(T, ROUTE_TILE))
    y = peer_experts(h2t, u_b, vt_b, e1, e2, r2, cc, x1.reshape(T, D), g2,
                     final_norm, _pick_tile(T, EXPERT_TILE))
    return y.reshape(B, L, D), ka, va, st


def kernel(x_prompt, x_sample, cache_k, cache_v, state_ret, c_prompt, c_sample, w_ada, b_ada, norm1,
           norm2, w_in, lam_q1, lam_k1, lam_q2, lam_k2, subln_a, subln_r, w_ba, w_br, w_o, rel_bias,
           w_pq, peer_keys, peer_u, peer_v, final_norm):
    depth = w_ada.shape[0]
    assert depth == 1, "the fused final norm assumes a single layer"
    Bp, Lp, D = x_prompt.shape
    Bs, Ls, _ = x_sample.shape
    past = cache_k.shape[2]
    pos_p = jnp.arange(Lp, dtype=jnp.int32)
    pos_s = past + jnp.arange(Ls, dtype=jnp.int32)
    lk = past + Ls
    lk_pad = -(-lk // LANES) * LANES
    k_pos_s = jnp.arange(lk_pad, dtype=jnp.int32)
    k_valid_s = k_pos_s < lk

    l = 0
    lam_init = 0.8 - 0.6 * math.exp(-0.3 * l)
    lamv = jnp.stack([lam_q1[l], lam_k1[l], lam_q2[l], lam_k2[l]]).astype(F32)
    mod = ada_mod(jnp.concatenate([c_prompt, c_sample], axis=0), w_ada[l], b_ada[l])
    mods = jnp.split(mod, 6, axis=-1)
    mods_p = [m[:Bp] for m in mods]
    mods_s = [m[Bp:] for m in mods]
    w = (norm1[l], norm2[l], w_in[l].astype(BF16), subln_a[l], subln_r[l], w_ba[l].astype(BF16),
         w_br[l].astype(BF16), w_o[l].astype(BF16), w_pq[l].astype(BF16),
         peer_keys[l].reshape(PEER_HEADS * 2, N_KEYS, KEY_DIM).astype(BF16),
         peer_u[l].astype(BF16), peer_v[l].T.astype(BF16))

    attend_p = lambda q, k, v: attn_prompt(q, k, v, rel_bias, lamv, lam_init)
    zero_state = jnp.zeros((Bp, HR, DKR, DVR), F32)
    yp, kp, vp, sp = _trunk(x_prompt, mods_p, pos_p, lam_init, w, attend_p, True, zero_state,
                            final_norm)

    def attend_s(q, k, v):
        padk = jnp.zeros((Bs, lk_pad - lk, WA), BF16)
        kc = cache_k[l].reshape(Bs, past, HA * 2 * DA).astype(BF16)
        vc = cache_v[l].reshape(Bs, past, WA).astype(BF16)
        k_all = jnp.concatenate([kc, k, padk], axis=1)
        v_all = jnp.concatenate([vc, v, padk], axis=1)
        return attn_small(q, k_all, v_all, pos_s, k_pos_s, k_valid_s, rel_bias, lamv, lam_init)

    ys, ks, vs, ss = _trunk(x_sample, mods_s, pos_s, lam_init, w, attend_s, False,
                            state_ret[l].astype(F32), final_norm)
    return (yp, ys, kp[None], vp[None], sp[None], ks[None], vs[None], ss[None])
```

```python
import functools
import math

import jax
import jax.numpy as jnp
from jax import lax
from jax.experimental import pallas as pl
from jax.experimental.pallas import tpu as pltpu

F32 = jnp.float32
BF16 = jnp.bfloat16

CHUNK = 64
HA = 4
DA = 64
DVA = 2 * DA
HR = 4
DKR = 128
DVR = 128
N_BUCKETS = 32
MAX_DIST = 128
PEER_HEADS = 8
N_KEYS = 128
KEY_DIM = 128
PEER_TOPK = 16
EPS = 1e-6
WA = HA * DVA
WR = HR * DVR
NEG = -1e30
LOG2E = math.log2(math.e)
LANES = 128
SUBLANES = 8
BF16_ROWS = 2 * SUBLANES
VMEM_LIMIT = 56 * 1024 * 1024

TOKEN_TILE = 256
ATTN_TQ = 256
ATTN_GROUP = 16
ATTN_TAIL = 4
ROUTE_TILE = 1024
EXPERT_TILE = 512
EXPERT_ROWS = 16
PEER_SUB = 2


def _params(sem, vmem=VMEM_LIMIT):
    return pltpu.CompilerParams(dimension_semantics=sem, vmem_limit_bytes=vmem)


def _dot(a, b):
    return jnp.dot(a, b, preferred_element_type=F32)


def _dot_nt(a, b):
    return lax.dot_general(a, b, (((1,), (1,)), ((), ())), preferred_element_type=F32)


def _dot_tn(a, b):
    return lax.dot_general(a, b, (((0,), (0,)), ((), ())), preferred_element_type=F32)


def _rms_rows(x):
    return x * lax.rsqrt(jnp.mean(x * x, axis=-1, keepdims=True) + EPS)


def _ada_kernel(c_ref, w_ref, b_ref, o_ref):
    c = c_ref[...]
    s = (c * jax.nn.sigmoid(c)).astype(BF16)
    o_ref[...] = _dot(s, w_ref[...].astype(BF16)) + b_ref[...]


def ada_mod(c, w_ada, b_ada, tn=1024):
    nb, d = c.shape
    n = w_ada.shape[1]
    return pl.pallas_call(
        _ada_kernel,
        grid=(n // tn,),
        in_specs=[pl.BlockSpec((nb, d), lambda j: (0, 0)),
                  pl.BlockSpec((d, tn), lambda j: (0, j)),
                  pl.BlockSpec((1, tn), lambda j: (0, j))],
        out_specs=pl.BlockSpec((nb, tn), lambda j: (0, j)),
        out_shape=jax.ShapeDtypeStruct((nb, n), F32),
        compiler_params=_params(("arbitrary",)),
        name="ada_mod",
    )(c, w_ada, b_ada.reshape(1, n))


def _inproj_kernel(x_ref, sh_ref, sc_ref, n1_ref, w_ref, cos_ref, sin_ref,
                   ka_ref, va_ref, qab_ref, kab_ref, vab_ref, qrb_ref, kr_ref, vrb_ref,
                   gr_ref, ga_ref, gb_ref, vt_ref=None):
    x = x_ref[...]
    h = _rms_rows(x) * n1_ref[...] * (1.0 + sc_ref[...]) + sh_ref[...]
    hb = h.astype(BF16)

    d_model = x.shape[1]
    sizes = (WA, WA, WA, WR, WR, WR, WR, d_model, d_model)
    starts = [sum(sizes[:g]) for g in range(len(sizes))]

    def proj(g):
        return _dot(hb, w_ref[:, starts[g]:starts[g] + sizes[g]])

    qa = proj(0)
    qab_ref[...] = (qa * (DA ** -0.5 * LOG2E)).astype(BF16)
    ka = proj(1)
    va = proj(2)
    for hh in range(HA):
        ka_ref[:, hh, :] = ka[:, hh * DVA:(hh + 1) * DVA]
        va_ref[:, hh, :] = va[:, hh * DVA:(hh + 1) * DVA]
    kab_ref[...] = ka.astype(BF16)
    vab_ref[...] = va.astype(BF16)
    if vt_ref is not None:
        vt_ref[...] = va.T.astype(BF16)
    cos2 = cos_ref[...]
    sin2 = sin_ref[...]

    def rot(z):
        parts = []
        for hh in range(HR):
            zh = z[:, hh * DKR:(hh + 1) * DKR]
            parts.append(zh * cos2 + pltpu.roll(zh, DKR // 2, 1) * sin2)
        return jnp.concatenate(parts, axis=-1)

    qrb_ref[...] = rot(proj(3)).astype(BF16)
    kr_ref[...] = rot(proj(4)) * (DKR ** -0.5)
    vrb_ref[...] = proj(5).astype(BF16)
    gr_ref[...] = proj(6)
    ga_ref[...] = proj(7)
    gb_ref[...] = proj(8)


def in_proj(x, sh1, sc1, norm1, w_in_b, cos2, sin2, tm, emit_vt):
    B, L, D = x.shape
    d_in = w_in_b.shape[1]
    row = lambda n: pl.BlockSpec((None, tm, n), lambda b, i: (b, i, 0))
    mod = pl.BlockSpec((None, 1, D), lambda b, i: (b, 0, 0))
    f = lambda n, dt: jax.ShapeDtypeStruct((B, L, n), dt)
    cache = pl.BlockSpec((None, tm, HA, DVA), lambda b, i: (b, i, 0, 0))
    cache_shape = jax.ShapeDtypeStruct((B, L, HA, DVA), F32)
    vt_spec = [pl.BlockSpec((None, WA, tm), lambda b, i: (b, 0, i))] if emit_vt else []
    vt_shape = [jax.ShapeDtypeStruct((B, WA, L), BF16)] if emit_vt else []
    return pl.pallas_call(
        _inproj_kernel,
        grid=(B, L // tm),
        in_specs=[row(D), mod, mod,
                  pl.BlockSpec((1, D), lambda b, i: (0, 0)),
                  pl.BlockSpec((D, d_in), lambda b, i: (0, 0)),
                  pl.BlockSpec((tm, DKR), lambda b, i: (i, 0)),
                  pl.BlockSpec((tm, DKR), lambda b, i: (i, 0))],
        out_specs=[cache, cache, row(WA), row(WA), row(WA), row(WR), row(WR), row(WR),
                   row(WR), row(D), row(D)] + vt_spec,
        out_shape=[cache_shape, cache_shape, f(WA, BF16), f(WA, BF16), f(WA, BF16),
                   f(WR, BF16), f(WR, F32), f(WR, BF16), f(WR, F32), f(D, F32), f(D, F32)]
        + vt_shape,
        compiler_params=_params(("arbitrary", "arbitrary")),
        name="in_proj",
    )(x, sh1.reshape(B, 1, D), sc1.reshape(B, 1, D), norm1.reshape(1, D), w_in_b, cos2, sin2)


def _t5_bucket(rel):
    nb = N_BUCKETS // 2
    ret = jnp.where(rel > 0, nb, 0)
    n = jnp.abs(rel)
    max_exact = nb // 2
    nf = jnp.maximum(n, max_exact).astype(F32)
    large = max_exact + (jnp.log(nf / max_exact) / math.log(MAX_DIST / max_exact)
                         * (nb - max_exact)).astype(jnp.int32)
    large = jnp.minimum(large, nb - 1)
    return ret + jnp.where(n < max_exact, n, large)


def _bias_from_buckets(bkt, rb_ref, h, shift):
    val = jnp.where(bkt < 0, NEG, 0.0).astype(F32)
    for n in range(N_BUCKETS):
        val = jnp.where(bkt == n, (rb_ref[n, h] - shift) * LOG2E, val)
    return val


def _lam_value(lv_ref, lam_init):
    lv = lv_ref[...]
    a = jnp.sum(lv[0:1] * lv[1:2], axis=-1, keepdims=True)
    b = jnp.sum(lv[2:3] * lv[3:4], axis=-1, keepdims=True)
    return jnp.exp(a) - jnp.exp(b) + lam_init


def _block_diag_q(q):
    lane = lax.broadcasted_iota(jnp.int32, q.shape, 1)
    zero = jnp.zeros_like(q)
    return jnp.concatenate([jnp.where(lane < DA, q, zero), jnp.where(lane >= DA, q, zero)], axis=0)


def _attn_kernel(rb_ref, lv_ref, q_ref, k_ref, vt_ref, bkt_ref, o_ref,
                 bias_sc, m_sc, acc_sc, va_sc, *, tq, nsub, lam_init):
    h = pl.program_id(1)
    i = pl.program_id(2)

    @pl.when(i == 0)
    def _():
        far = rb_ref[N_BUCKETS // 2 - 1, h]
        for t in range(2):
            bias_sc[t] = _bias_from_buckets(bkt_ref[t], rb_ref, h, far)
        va_sc[0:DVA, :] = vt_ref[...]
        va_sc[DVA:, :] = jnp.ones((BF16_ROWS, va_sc.shape[1]), BF16)

    qt = q_ref[...].astype(F32).T
    drow = lax.broadcasted_iota(jnp.int32, qt.shape, 0)
    qbd_t = jnp.concatenate([jnp.where(drow < DA, qt, 0.0), jnp.where(drow >= DA, qt, 0.0)],
                            axis=1).astype(BF16)
    m_sc[...] = jnp.full(m_sc.shape, NEG, F32)
    acc_sc[...] = jnp.zeros(acc_sc.shape, F32)

    def scores(off, tk, bias=None):
        s = _dot(k_ref[pl.ds(off, tk), :], qbd_t)
        if bias is not None:
            s = s + jnp.concatenate([bias, bias], axis=1)
        return s

    def absorb(s, off, tk):
        m_prev = m_sc[...]
        m_new = jnp.maximum(m_prev, jnp.max(s, axis=0, keepdims=True))
        alpha = jnp.exp2(m_prev - m_new)
        p = jnp.exp2(s - m_new).astype(BF16)
        acc_sc[...] = alpha * acc_sc[...] + _dot(va_sc[:, pl.ds(off, tk)], p)
        m_sc[...] = m_new

    def tiles(specs):
        ss = [scores(*sp) for sp in specs]
        for s, sp in zip(ss, specs):
            absorb(s, sp[0], sp[1])

    ntile = jnp.maximum(i - 1, 0)
    nfull = ntile // nsub

    def far(first_tile, n):
        return [(pl.multiple_of((first_tile + j) * tq, tq), tq) for j in range(n)]

    def far_body(t, carry):
        tiles(far(t * nsub, nsub))
        return carry

    lax.fori_loop(0, nfull, far_body, 0)
    done = nfull * nsub
    n = nsub // 2
    while n >= ATTN_TAIL:
        take = ((ntile - done) // n) > 0

        @pl.when(take)
        def _(done=done, n=n):
            tiles(far(done, n))

        done = done + jnp.where(take, n, 0)
        n //= 2

    rest = ntile - done
    for r in range(ATTN_TAIL):
        @pl.when((i >= 1) & (rest == r))
        def _(r=r):
            off = pl.multiple_of((i - 1) * tq, tq)
            tiles(far(done, r) + [(off, tq, bias_sc[1]),
                                  (pl.multiple_of(off + tq, tq), tq, bias_sc[0])])

    @pl.when(i == 0)
    def _():
        tiles([(0, tq, bias_sc[0])])

    lam = _lam_value(lv_ref, lam_init)
    o = acc_sc[0:DVA, :] / acc_sc[DVA:DVA + 1, :]
    o_ref[...] = (o[:, 0:tq] - lam * o[:, tq:2 * tq]).T


def attn_prompt(qab, kab, vt, rel_bias, lamv, lam_init, tq=ATTN_TQ, nsub=ATTN_GROUP):
    B, S, _ = qab.shape
    assert S % tq == 0
    c = jnp.arange(tq, dtype=jnp.int32)[:, None]
    r = jnp.arange(tq, dtype=jnp.int32)[None, :]
    diag = jnp.where((c // CHUNK) <= (r // CHUNK), _t5_bucket(c - r), -1)
    prev = _t5_bucket(c - r - tq)
    bkt = jnp.stack([diag, prev]).astype(jnp.int32)
    smem = pl.BlockSpec(memory_space=pltpu.SMEM)
    return pl.pallas_call(
        functools.partial(_attn_kernel, tq=tq, nsub=nsub, lam_init=lam_init),
        grid=(B, HA, S // tq),
        in_specs=[smem,
                  pl.BlockSpec((4, DA), lambda b, h, i: (0, 0)),
                  pl.BlockSpec((None, tq, DVA), lambda b, h, i: (b, i, h)),
                  pl.BlockSpec((None, S, DVA), lambda b, h, i: (b, 0, h)),
                  pl.BlockSpec((None, DVA, S), lambda b, h, i: (b, h, 0)),
                  pl.BlockSpec((2, tq, tq), lambda b, h, i: (0, 0, 0))],
        out_specs=pl.BlockSpec((None, tq, DVA), lambda b, h, i: (b, i, h)),
        out_shape=jax.ShapeDtypeStruct((B, S, WA), F32),
        scratch_shapes=[pltpu.VMEM((2, tq, tq), F32),
                        pltpu.VMEM((1, 2 * tq), F32),
                        pltpu.VMEM((DVA + BF16_ROWS, 2 * tq), F32),
                        pltpu.VMEM((DVA + BF16_ROWS, S), BF16)],
        compiler_params=_params(("arbitrary", "arbitrary", "arbitrary")),
        name="attn_prompt",
    )(rel_bias, lamv, qab, kab, vt, bkt)


def _attn_small_kernel(rb_ref, lv_ref, q_ref, k_ref, v_ref, bkt_ref, o_ref, *, lq, lam_init):
    h = pl.program_id(1)
    bias = _bias_from_buckets(bkt_ref[...], rb_ref, h, 0.0)
    qbd = _block_diag_q(q_ref[...])
    s = _dot_nt(qbd, k_ref[...]) + jnp.concatenate([bias, bias], axis=0)
    m = jnp.max(s, axis=-1, keepdims=True)
    p = jnp.exp2(s - m)
    l = jnp.sum(p, axis=-1, keepdims=True)
    o = _dot(p.astype(BF16), v_ref[...]) / l
    lam = _lam_value(lv_ref, lam_init)
    o_ref[...] = o[0:lq] - lam * o[lq:2 * lq]


def attn_small(qab, k_all, v_all, q_pos, k_pos, k_valid, rel_bias, lamv, lam_init):
    B, Lq, _ = qab.shape
    Lk = k_all.shape[1]
    visible = ((k_pos[None, :] // CHUNK) <= (q_pos[:, None] // CHUNK)) & k_valid[None, :]
    bkt = jnp.where(visible, _t5_bucket(k_pos[None, :] - q_pos[:, None]), -1).astype(jnp.int32)
    smem = pl.BlockSpec(memory_space=pltpu.SMEM)
    return pl.pallas_call(
        functools.partial(_attn_small_kernel, lq=Lq, lam_init=lam_init),
        grid=(B, HA),
        in_specs=[smem,
                  pl.BlockSpec((4, DA), lambda b, h: (0, 0)),
                  pl.BlockSpec((None, Lq, DVA), lambda b, h: (b, 0, h)),
                  pl.BlockSpec((None, Lk, DVA), lambda b, h: (b, 0, h)),
                  pl.BlockSpec((None, Lk, DVA), lambda b, h: (b, 0, h)),
                  pl.BlockSpec((Lq, Lk), lambda b, h: (0, 0))],
        out_specs=pl.BlockSpec((None, Lq, DVA), lambda b, h: (b, 0, h)),
        out_shape=jax.ShapeDtypeStruct((B, Lq, WA), F32),
        compiler_params=_params(("arbitrary", "arbitrary")),
        name="attn_sample",
    )(rel_bias, lamv, qab, k_all, v_all, bkt)


def _ret_kernel(lg_ref, q_ref, k_ref, v_ref, s0_ref, o_ref, so_ref, state_sc, decay_sc, *, C):
    c = pl.program_id(1)

    @pl.when(c == 0)
    def _():
        state_sc[...] = s0_ref[...]
        r = lax.broadcasted_iota(jnp.int32, (C, C), 0)
        cc = lax.broadcasted_iota(jnp.int32, (C, C), 1)
        diff = (r - cc).astype(F32)
        for h in range(HR):
            decay_sc[h] = jnp.where(diff >= 0, jnp.exp(jnp.maximum(diff, 0.0) * lg_ref[h]), 0.0)

    n = lax.broadcasted_iota(jnp.int32, (C, 1), 0).astype(F32)
    for h in range(HR):
        lg = lg_ref[h]
        cols = slice(h * DKR, (h + 1) * DKR)
        xi = jnp.exp((n + 1.0) * lg)
        zeta = jnp.exp((C - 1.0 - n) * lg)
        q = q_ref[:, cols]
        k = k_ref[:, cols]
        v = v_ref[:, cols]
        state = state_sc[h]
        scores = _dot_nt(q, k.astype(BF16)) * decay_sc[h]
        intra = _dot(scores.astype(BF16), v)
        cross = _dot(q, state.astype(BF16)) * xi
        o_ref[:, cols] = intra + cross
        kz = (k * zeta).astype(BF16)
        state_sc[h] = jnp.exp(C * lg) * state + _dot_tn(kz, v)

    @pl.when(c == pl.num_programs(1) - 1)
    def _():
        so_ref[...] = state_sc[...]


def retention(qrb, kr, vrb, state0, C):
    B, L, _ = qrb.shape
    lg = jnp.log(1.0 - 2.0 ** (-5.0 - jnp.arange(HR, dtype=F32)))
    blk = pl.BlockSpec((None, C, WR), lambda b, c: (b, c, 0))
    st = pl.BlockSpec((None, HR, DKR, DVR), lambda b, c: (b, 0, 0, 0))
    return pl.pallas_call(
        functools.partial(_ret_kernel, C=C),
        grid=(B, L // C),
        in_specs=[pl.BlockSpec(memory_space=pltpu.SMEM), blk, blk, blk, st],
        out_specs=[blk, st],
        out_shape=[jax.ShapeDtypeStruct((B, L, WR), F32),
                   jax.ShapeDtypeStruct((B, HR, DKR, DVR), F32)],
        scratch_shapes=[pltpu.VMEM((HR, DKR, DVR), F32), pltpu.VMEM((HR, C, C), F32)],
        compiler_params=_params(("arbitrary", "arbitrary")),
        name="retention",
    )(lg, qrb, kr, vrb, state0)


def _outmix_kernel(x_ref, oa_ref, or_ref, gr_ref, ga_ref, gb_ref, g1_ref, sh2_ref, sc2_ref, n2_ref,
                   sa_ref, sr_ref, wba_ref, wbr_ref, wo_ref, wpq_ref,
                   x1_ref, h2_ref, qp_ref, *, lam_init, h2_transposed):
    sa = sa_ref[...]
    sr = sr_ref[...]
    gr = gr_ref[...]
    silu_gr = gr * jax.nn.sigmoid(gr)
    ya_parts, yr_parts = [], []
    for hh in range(HA):
        sl = slice(hh * DVA, (hh + 1) * DVA)
        ya_parts.append(_rms_rows(oa_ref[:, sl]) * sa * (1.0 - lam_init))
        yr_parts.append(silu_gr[:, sl] * (_rms_rows(or_ref[:, sl]) * sr))
    ya = jnp.concatenate(ya_parts, axis=-1).astype(BF16)
    yr = jnp.concatenate(yr_parts, axis=-1).astype(BF16)
    y = (jax.nn.sigmoid(ga_ref[...]) * _dot(ya, wba_ref[...])
         + jax.nn.sigmoid(gb_ref[...]) * _dot(yr, wbr_ref[...]))
    out = _dot(y.astype(BF16), wo_ref[...])
    x1 = x_ref[...] + g1_ref[...] * out
    x1_ref[...] = x1
    h2f = _rms_rows(x1) * n2_ref[...] * (1.0 + sc2_ref[...]) + sh2_ref[...]
    h2 = h2f.astype(BF16)
    h2_ref[...] = h2f.T.astype(BF16) if h2_transposed else h2
    qp_ref[...] = _dot(h2, wpq_ref[...]).astype(BF16)


def out_mix(x, oa, orr, gr, ga, gb, g1, sh2, sc2, norm2, subln_a, subln_r,
            w_ba_b, w_br_b, w_o_b, w_pq_b, lam_init, tm):
    B, L, D = x.shape
    nq = w_pq_b.shape[1]
    row = lambda n: pl.BlockSpec((None, tm, n), lambda b, i: (b, i, 0))
    mod = pl.BlockSpec((None, 1, D), lambda b, i: (b, 0, 0))
    full = lambda a: pl.BlockSpec(a.shape, lambda b, i: (0,) * a.ndim)
    n2 = norm2.reshape(1, D)
    sa = subln_a.reshape(1, DVA)
    sr = subln_r.reshape(1, DVR)
    h2_transposed = tm % LANES == 0
    nt = L // tm
    if h2_transposed:
        h2_spec = pl.BlockSpec((D, tm), lambda b, i: (0, b * nt + i))
        h2_shape = jax.ShapeDtypeStruct((D, B * L), BF16)
    else:
        h2_spec = row(D)
        h2_shape = jax.ShapeDtypeStruct((B, L, D), BF16)
    x1, h2, qp = pl.pallas_call(
        functools.partial(_outmix_kernel, lam_init=lam_init, h2_transposed=h2_transposed),
        grid=(B, nt),
        in_specs=[row(D), row(WA), row(WR), row(WR), row(D), row(D), mod, mod, mod,
                  full(n2), full(sa), full(sr), full(w_ba_b), full(w_br_b), full(w_o_b), full(w_pq_b)],
        out_specs=[row(D), h2_spec, row(nq)],
        out_shape=[jax.ShapeDtypeStruct((B, L, D), F32), h2_shape,
                   jax.ShapeDtypeStruct((B, L, nq), BF16)],
        compiler_params=_params(("arbitrary", "arbitrary")),
        name="out_mix",
    )(x, oa, orr, gr, ga, gb, g1.reshape(B, 1, D), sh2.reshape(B, 1, D), sc2.reshape(B, 1, D),
      n2, sa, sr, w_ba_b, w_br_b, w_o_b, w_pq_b)
    h2t = h2 if h2_transposed else h2.reshape(B * L, D).T
    return x1, h2t, qp


UNRANKED = float(PEER_TOPK * PEER_TOPK)


def _topk_rows(s, k, break_ties, want_rank=True):
    n = s.shape[0]
    iota = lax.broadcasted_iota(jnp.int32, s.shape, 0).astype(F32)
    work = s
    rank = jnp.full(s.shape, UNRANKED, F32) if want_rank else None
    vals = []
    for r in range(k):
        m = jnp.max(work, axis=0, keepdims=True)
        sel = work == m
        if break_ties:
            idx = jnp.min(jnp.where(sel, iota, float(n)), axis=0, keepdims=True)
            sel = iota == idx
        if want_rank:
            rank = jnp.where(sel, float(r), rank)
        work = jnp.where(sel, -jnp.inf, work)
        vals.append(m)
    return vals, rank


def _count_rows(mask):
    return jnp.sum(mask.astype(F32), axis=0, keepdims=True)


_HEAD_A = SUBLANES
_CAND_NB = [PEER_TOPK] + [SUBLANES] * (_HEAD_A - 1)


def _route_chunk(s1, s2, break_ties):
    k = PEER_TOPK
    v1, rank1 = _topk_rows(s1, k, break_ties, want_rank=break_ties)
    v2, rank2 = _topk_rows(s2, k, break_ties)
    v2m = jnp.concatenate(v2, axis=0)
    v1t = jnp.concatenate(v1[_HEAD_A:], axis=0)
    blocks = [v1[a] + v2m[0:nb] for a, nb in enumerate(_CAND_NB)] + [v1t + v2[0]]
    cand = jnp.concatenate(blocks, axis=0)
    vc, crank = _topk_rows(cand, k, break_ties, want_rank=break_ties)
    if break_ties:
        sel = crank < UNRANKED
        row_is = lambda a: rank1 == float(a)
        nbad = jnp.zeros((), jnp.int32)
    else:
        sel = cand >= vc[k - 1]
        row_is = lambda a: s1 == v1[a]
        bad = ((_count_rows(s1 >= v1[k - 1]) != float(k))
               | (_count_rows(rank2 < UNRANKED) != float(k)) | (_count_rows(sel) != float(k)))
        nbad = jnp.sum(bad.astype(jnp.int32))
    cmax = v1[0] + v2[0]
    z = jnp.sum(jnp.where(sel, jnp.exp(cand - cmax), 0.0), axis=0, keepdims=True)
    self32 = sel.astype(F32)
    cidx = jnp.zeros(s1.shape, F32)
    lo = 0
    for a, nb in enumerate(_CAND_NB):
        cnt = jnp.sum(self32[lo:lo + nb], axis=0, keepdims=True)
        cidx = jnp.where(row_is(a), cnt, cidx)
        lo += nb
    for a in range(_HEAD_A, k):
        cidx = jnp.where(row_is(a), self32[lo + a - _HEAD_A:lo + a - _HEAD_A + 1], cidx)
    e1 = jnp.exp(s1 - v1[0]) / z
    e2 = jnp.exp(s2 - v2[0])
    return (e1, e2, rank2, cidx), nbad


def _route_kernel(q_ref, keys_ref, e1_ref, e2_ref, r2_ref, c_ref, s1_sc, s2_sc):
    q = q_ref[...]
    s1_sc[...] = _dot_nt(keys_ref[0], q[:, 0:KEY_DIM])
    s2_sc[...] = _dot_nt(keys_ref[1], q[:, KEY_DIM:2 * KEY_DIM])

    def store(sl, res):
        e1, e2, rank2, cidx = res
        e1_ref[:, sl] = e1
        e2_ref[:, sl] = e2.astype(BF16)
        r2_ref[:, sl] = rank2.astype(BF16)
        c_ref[:, sl] = cidx

    group = 2
    assert s1_sc.shape[1] % (group * LANES) == 0

    def chunks(ci, carry):
        sls = [pl.ds(pl.multiple_of((ci * group + g) * LANES, LANES), LANES) for g in range(group)]
        fast = [_route_chunk(s1_sc[:, sl], s2_sc[:, sl], break_ties=False) for sl in sls]
        for sl, (res, _) in zip(sls, fast):
            store(sl, res)
        for sl, (_, nbad) in zip(sls, fast):
            @pl.when(nbad > 0)
            def _():
                store(sl, _route_chunk(s1_sc[:, sl], s2_sc[:, sl], break_ties=True)[0])
        return carry

    lax.fori_loop(0, s1_sc.shape[1] // (group * LANES), chunks, 0)


def peer_route(qp, keys_b, tm):
    T = qp.shape[0]
    out = pl.BlockSpec((None, N_KEYS, tm), lambda t, h: (h, 0, t))
    shp = lambda dt: jax.ShapeDtypeStruct((PEER_HEADS, N_KEYS, T), dt)
    return pl.pallas_call(
        _route_kernel,
        grid=(T // tm, PEER_HEADS),
        in_specs=[pl.BlockSpec((tm, 2 * KEY_DIM), lambda t, h: (t, h)),
                  pl.BlockSpec((2, N_KEYS, KEY_DIM), lambda t, h: (h, 0, 0))],
        out_specs=[out, out, out, out],
        out_shape=[shp(F32), shp(BF16), shp(BF16), shp(F32)],
        scratch_shapes=[pltpu.VMEM((N_KEYS, tm), F32), pltpu.VMEM((N_KEYS, tm), F32)],
        compiler_params=_params(("arbitrary", "arbitrary")),
        name="peer_route",
    )(qp, keys_b)


def _bcast_rows_bf16(row, n):
    t = row.shape[1]
    tile = jnp.broadcast_to(row, (BF16_ROWS, t)).astype(BF16)
    return jnp.broadcast_to(tile[None], (n // BF16_ROWS, BF16_ROWS, t)).reshape(n, t)


def _peer_kernel(h2_ref, u_ref, vt_ref, e1_ref, e2_ref, r2_ref, c_ref, x1_ref, g2_ref, fn_ref,
                 y_ref, acc_sc, *, ni):
    e = pl.program_id(1)

    @pl.when(e == 0)
    def _():
        acc_sc[...] = jnp.zeros(acc_sc.shape, F32)

    h2 = h2_ref[...]
    pair = PEER_SUB * N_KEYS
    a_pairs = [_dot(u_ref[p * pair:(p + 1) * pair, :], h2) for p in range(ni // PEER_SUB)]
    for p in range(ni // PEER_SUB):
        gs = [jnp.zeros((N_KEYS, h2.shape[1]), BF16) for _ in range(PEER_SUB)]
        for hh in range(PEER_HEADS):
            r2h = r2_ref[hh]
            e2h = e2_ref[hh]
            for jj in range(PEER_SUB):
                i = e * ni + PEER_SUB * p + jj
                c_row = _bcast_rows_bf16(c_ref[hh, pl.ds(i, 1), :], N_KEYS)
                e1_row = _bcast_rows_bf16(e1_ref[hh, pl.ds(i, 1), :], N_KEYS)
                gs[jj] = gs[jj] + jnp.where(r2h < c_row, e2h * e1_row, jnp.zeros_like(e2h))
        ws = []
        for jj in range(PEER_SUB):
            a = a_pairs[p][jj * N_KEYS:(jj + 1) * N_KEYS]
            act = 0.5 * a * (1.0 + lax.erf(a * (2.0 ** -0.5)))
            ws.append(gs[jj] * act.astype(BF16))
        w = jnp.concatenate(ws, axis=0)
        acc_sc[...] += _dot(vt_ref[:, p * pair:(p + 1) * pair], w)

    @pl.when(e == pl.num_programs(1) - 1)
    def _():
        x2 = x1_ref[...] + g2_ref[...] * acc_sc[...].T
        y_ref[...] = _rms_rows(x2) * fn_ref[...]


def peer_experts(h2t, u_b, vt_b, e1, e2, r2, cc, x1, g2, final_norm, tm, ni=EXPERT_ROWS):
    D, T = h2t.shape
    nb = ni * N_KEYS
    tok = pl.BlockSpec((tm, D), lambda t, e: (t, 0))
    gate = pl.BlockSpec((PEER_HEADS, N_KEYS, tm), lambda t, e: (0, 0, t))
    L = T // g2.shape[0]
    if L % tm == 0:
        g2_arg = g2.reshape(g2.shape[0], 1, D)
        g2_spec = pl.BlockSpec((None, 1, D), lambda t, e: (t // (L // tm), 0, 0))
    else:
        g2_arg = jnp.repeat(g2, L, axis=0)
        g2_spec = tok
    return pl.pallas_call(
        functools.partial(_peer_kernel, ni=ni),
        grid=(T // tm, u_b.shape[0] // nb),
        in_specs=[pl.BlockSpec((D, tm), lambda t, e: (0, t)),
                  pl.BlockSpec((nb, D), lambda t, e: (e, 0)),
                  pl.BlockSpec((D, nb), lambda t, e: (0, e)),
                  gate, gate, gate, gate, tok, g2_spec,
                  pl.BlockSpec((1, D), lambda t, e: (0, 0))],
        out_specs=tok,
        out_shape=jax.ShapeDtypeStruct((T, D), F32),
        scratch_shapes=[pltpu.VMEM((D, tm), F32)],
        compiler_params=_params(("arbitrary", "arbitrary")),
        name="peer_experts",
    )(h2t, u_b, vt_b, e1, e2, r2, cc, x1, g2_arg, final_norm.reshape(1, D))


def _rot_tables(pos):
    inv = 1.0 / (10000.0 ** jnp.linspace(0.0, 1.0, DKR // 2, dtype=F32))
    ang = pos[:, None].astype(F32) * inv[None, :]
    cos, sin = jnp.cos(ang), jnp.sin(ang)
    return jnp.concatenate([cos, cos], axis=-1), jnp.concatenate([-sin, sin], axis=-1)


def _pick_tile(n, pref):
    t = min(n, pref)
    assert n % t == 0, (n, t)
    return t


def _trunk(x, mods, pos, lam_init, w, attend, values_transposed, state0, final_norm):
    (norm1, norm2, w_in_b, subln_a, subln_r, w_ba_b, w_br_b, w_o_b, w_pq_b, keys_b, u_b, vt_b) = w
    sh1, sc1, g1, sh2, sc2, g2 = mods
    B, L, D = x.shape
    T = B * L
    cos2, sin2 = _rot_tables(pos)
    tm = _pick_tile(L, TOKEN_TILE)
    ka, va, qab, kab, vab, qrb, kr, vrb, gr, ga, gb, *vt = in_proj(
        x, sh1, sc1, norm1, w_in_b, cos2, sin2, tm, emit_vt=values_transposed)
    oa = attend(qab, kab, vt[0] if values_transposed else vab)
    orr, st = retention(qrb, kr, vrb, state0, tm)
    x1, h2t, qp = out_mix(x, oa, orr, gr, ga, gb, g1, sh2, sc2, norm2, subln_a, subln_r,
                          w_ba_b, w_br_b, w_o_b, w_pq_b, lam_init, tm)
    e1, e2, r2, cc = peer_route(qp.reshape(T, -1), keys_b, _pick_tile(T, ROUTE_TILE))
    y = peer_experts(h2t, u_b, vt_b, e1, e2, r2, cc, x1.reshape(T, D), g2,
                     final_norm, _pick_tile(T, EXPERT_TILE))
    return y.reshape(B, L, D), ka, va, st


def kernel(x_prompt, x_sample, cache_k, cache_v, state_ret, c_prompt, c_sample, w_ada, b_ada, norm1,
           norm2, w_in, lam_q1, lam_k1, lam_q2, lam_k2, subln_a, subln_r, w_ba, w_br, w_o, rel_bias,
           w_pq, peer_keys, peer_u, peer_v, final_norm):
    depth = w_ada.shape[0]
    assert depth == 1, "the fused final norm assumes a single layer"
    Bp, Lp, D = x_prompt.shape
    Bs, Ls, _ = x_sample.shape
    past = cache_k.shape[2]
    pos_p = jnp.arange(Lp, dtype=jnp.int32)
    pos_s = past + jnp.arange(Ls, dtype=jnp.int32)
    lk = past + Ls
    lk_pad = -(-lk // LANES) * LANES
    k_pos_s = jnp.arange(lk_pad, dtype=jnp.int32)
    k_valid_s = k_pos_s < lk

    l = 0
    lam_init = 0.8 - 0.6 * math.exp(-0.3 * l)
    lamv = jnp.stack([lam_q1[l], lam_k1[l], lam_q2[l], lam_k2[l]]).astype(F32)
    mod = ada_mod(jnp.concatenate([c_prompt, c_sample], axis=0), w_ada[l], b_ada[l])
    mods = jnp.split(mod, 6, axis=-1)
    mods_p = [m[:Bp] for m in mods]
    mods_s = [m[Bp:] for m in mods]
    w = (norm1[l], norm2[l], w_in[l].astype(BF16), subln_a[l], subln_r[l], w_ba[l].astype(BF16),
         w_br[l].astype(BF16), w_o[l].astype(BF16), w_pq[l].astype(BF16),
         peer_keys[l].reshape(PEER_HEADS * 2, N_KEYS, KEY_DIM).astype(BF16),
         peer_u[l].astype(BF16), peer_v[l].T.astype(BF16))

    attend_p = lambda q, k, v: attn_prompt(q, k, v, rel_bias, lamv, lam_init)
    zero_state = jnp.zeros((Bp, HR, DKR, DVR), F32)
    yp, kp, vp, sp = _trunk(x_prompt, mods_p, pos_p, lam_init, w, attend_p, True, zero_state,
                            final_norm)

    def attend_s(q, k, v):
        padk = jnp.zeros((Bs, lk_pad - lk, WA), BF16)
        kc = cache_k[l].reshape(Bs, past, HA * 2 * DA).astype(BF16)
        vc = cache_v[l].reshape(Bs, past, WA).astype(BF16)
        k_all = jnp.concatenate([kc, k, padk], axis=1)
        v_all = jnp.concatenate([vc, v, padk], axis=1)
        return attn_small(q, k_all, v_all, pos_s, k_pos_s, k_valid_s, rel_bias, lamv, lam_init)

    ys, ks, vs, ss = _trunk(x_sample, mods_s, pos_s, lam_init, w, attend_s, False,
                            state_ret[l].astype(F32), final_norm)
    return (yp, ys, kp[None], vp[None], sp[None], ks[None], vs[None], ss[None])
```

```python
import functools
import math

import jax
import jax.numpy as jnp
from jax import lax
from jax.experimental import pallas as pl
from jax.experimental.pallas import tpu as pltpu

F32 = jnp.float32
BF16 = jnp.bfloat16

CHUNK = 64
HA = 4
DA = 64
DVA = 2 * DA
HR = 4
DKR = 128
DVR = 128
N_BUCKETS = 32
MAX_DIST = 128
PEER_HEADS = 8
N_KEYS = 128
KEY_DIM = 128
PEER_TOPK = 16
EPS = 1e-6
WA = HA * DVA
WR = HR * DVR
NEG = -1e30
LOG2E = math.log2(math.e)
LANES = 128
SUBLANES = 8
BF16_ROWS = 2 * SUBLANES
VMEM_LIMIT = 56 * 1024 * 1024

TOKEN_TILE = 256
ATTN_TQ = 256
ATTN_GROUP = 32
ATTN_TAIL = 4
ROUTE_TILE = 1024
EXPERT_TILE = 512
EXPERT_ROWS = 16
PEER_SUB = 2


def _params(sem, vmem=VMEM_LIMIT):
    return pltpu.CompilerParams(dimension_semantics=sem, vmem_limit_bytes=vmem)


def _dot(a, b):
    return jnp.dot(a, b, preferred_element_type=F32)


def _dot_nt(a, b):
    return lax.dot_general(a, b, (((1,), (1,)), ((), ())), preferred_element_type=F32)


def _dot_tn(a, b):
    return lax.dot_general(a, b, (((0,), (0,)), ((), ())), preferred_element_type=F32)


def _rms_rows(x):
    return x * lax.rsqrt(jnp.mean(x * x, axis=-1, keepdims=True) + EPS)


def _ada_kernel(c_ref, w_ref, b_ref, o_ref):
    c = c_ref[...]
    s = (c * jax.nn.sigmoid(c)).astype(BF16)
    o_ref[...] = _dot(s, w_ref[...].astype(BF16)) + b_ref[...]


def ada_mod(c, w_ada, b_ada, tn=1024):
    nb, d = c.shape
    n = w_ada.shape[1]
    return pl.pallas_call(
        _ada_kernel,
        grid=(n // tn,),
        in_specs=[pl.BlockSpec((nb, d), lambda j: (0, 0)),
                  pl.BlockSpec((d, tn), lambda j: (0, j)),
                  pl.BlockSpec((1, tn), lambda j: (0, j))],
        out_specs=pl.BlockSpec((nb, tn), lambda j: (0, j)),
        out_shape=jax.ShapeDtypeStruct((nb, n), F32),
        compiler_params=_params(("arbitrary",)),
        name="ada_mod",
    )(c, w_ada, b_ada.reshape(1, n))


def _inproj_kernel(x_ref, sh_ref, sc_ref, n1_ref, w_ref, cos_ref, sin_ref,
                   ka_ref, va_ref, qab_ref, kab_ref, vab_ref, qrb_ref, kr_ref, vrb_ref,
                   gr_ref, ga_ref, gb_ref, vt_ref=None):
    x = x_ref[...]
    h = _rms_rows(x) * n1_ref[...] * (1.0 + sc_ref[...]) + sh_ref[...]
    hb = h.astype(BF16)

    d_model = x.shape[1]
    sizes = (WA, WA, WA, WR, WR, WR, WR, d_model, d_model)
    starts = [sum(sizes[:g]) for g in range(len(sizes))]

    def proj(g):
        return _dot(hb, w_ref[:, starts[g]:starts[g] + sizes[g]])

    qa = proj(0)
    qab_ref[...] = (qa * (DA ** -0.5 * LOG2E)).astype(BF16)
    ka = proj(1)
    va = proj(2)
    for hh in range(HA):
        ka_ref[:, hh, :] = ka[:, hh * DVA:(hh + 1) * DVA]
        va_ref[:, hh, :] = va[:, hh * DVA:(hh + 1) * DVA]
    kab_ref[...] = ka.astype(BF16)
    vab_ref[...] = va.astype(BF16)
    if vt_ref is not None:
        vt_ref[...] = va.T.astype(BF16)
    cos2 = cos_ref[...]
    sin2 = sin_ref[...]

    def rot(z):
        parts = []
        for hh in range(HR):
            zh = z[:, hh * DKR:(hh + 1) * DKR]
            parts.append(zh * cos2 + pltpu.roll(zh, DKR // 2, 1) * sin2)
        return jnp.concatenate(parts, axis=-1)

    qrb_ref[...] = rot(proj(3)).astype(BF16)
    kr_ref[...] = rot(proj(4)) * (DKR ** -0.5)
    vrb_ref[...] = proj(5).astype(BF16)
    gr_ref[...] = proj(6)
    ga_ref[...] = proj(7)
    gb_ref[...] = proj(8)


def in_proj(x, sh1, sc1, norm1, w_in_b, cos2, sin2, tm, emit_vt):
    B, L, D = x.shape
    d_in = w_in_b.shape[1]
    row = lambda n: pl.BlockSpec((None, tm, n), lambda b, i: (b, i, 0))
    mod = pl.BlockSpec((None, 1, D), lambda b, i: (b, 0, 0))
    f = lambda n, dt: jax.ShapeDtypeStruct((B, L, n), dt)
    cache = pl.BlockSpec((None, tm, HA, DVA), lambda b, i: (b, i, 0, 0))
    cache_shape = jax.ShapeDtypeStruct((B, L, HA, DVA), F32)
    vt_spec = [pl.BlockSpec((None, WA, tm), lambda b, i: (b, 0, i))] if emit_vt else []
    vt_shape = [jax.ShapeDtypeStruct((B, WA, L), BF16)] if emit_vt else []
    return pl.pallas_call(
        _inproj_kernel,
        grid=(B, L // tm),
        in_specs=[row(D), mod, mod,
                  pl.BlockSpec((1, D), lambda b, i: (0, 0)),
                  pl.BlockSpec((D, d_in), lambda b, i: (0, 0)),
                  pl.BlockSpec((tm, DKR), lambda b, i: (i, 0)),
                  pl.BlockSpec((tm, DKR), lambda b, i: (i, 0))],
        out_specs=[cache, cache, row(WA), row(WA), row(WA), row(WR), row(WR), row(WR),
                   row(WR), row(D), row(D)] + vt_spec,
        out_shape=[cache_shape, cache_shape, f(WA, BF16), f(WA, BF16), f(WA, BF16),
                   f(WR, BF16), f(WR, F32), f(WR, BF16), f(WR, F32), f(D, F32), f(D, F32)]
        + vt_shape,
        compiler_params=_params(("arbitrary", "arbitrary")),
        name="in_proj",
    )(x, sh1.reshape(B, 1, D), sc1.reshape(B, 1, D), norm1.reshape(1, D), w_in_b, cos2, sin2)


def _t5_bucket(rel):
    nb = N_BUCKETS // 2
    ret = jnp.where(rel > 0, nb, 0)
    n = jnp.abs(rel)
    max_exact = nb // 2
    nf = jnp.maximum(n, max_exact).astype(F32)
    large = max_exact + (jnp.log(nf / max_exact) / math.log(MAX_DIST / max_exact)
                         * (nb - max_exact)).astype(jnp.int32)
    large = jnp.minimum(large, nb - 1)
    return ret + jnp.where(n < max_exact, n, large)


def _bias_from_buckets(bkt, rb_ref, h, shift):
    val = jnp.where(bkt < 0, NEG, 0.0).astype(F32)
    for n in range(N_BUCKETS):
        val = jnp.where(bkt == n, (rb_ref[n, h] - shift) * LOG2E, val)
    return val


def _lam_value(lv_ref, lam_init):
    lv = lv_ref[...]
    a = jnp.sum(lv[0:1] * lv[1:2], axis=-1, keepdims=True)
    b = jnp.sum(lv[2:3] * lv[3:4], axis=-1, keepdims=True)
    return jnp.exp(a) - jnp.exp(b) + lam_init


def _block_diag_q(q):
    lane = lax.broadcasted_iota(jnp.int32, q.shape, 1)
    zero = jnp.zeros_like(q)
    return jnp.concatenate([jnp.where(lane < DA, q, zero), jnp.where(lane >= DA, q, zero)], axis=0)


def _attn_kernel(rb_ref, lv_ref, q_ref, k_ref, vt_ref, bkt_ref, o_ref,
                 bias_sc, m_sc, acc_sc, va_sc, *, tq, nsub, lam_init):
    h = pl.program_id(1)
    i = pl.program_id(2)

    @pl.when(i == 0)
    def _():
        far = rb_ref[N_BUCKETS // 2 - 1, h]
        for t in range(2):
            bias_sc[t] = _bias_from_buckets(bkt_ref[t], rb_ref, h, far)
        va_sc[0:DVA, :] = vt_ref[...]
        va_sc[DVA:, :] = jnp.ones((BF16_ROWS, va_sc.shape[1]), BF16)

    qt = q_ref[...].astype(F32).T
    drow = lax.broadcasted_iota(jnp.int32, qt.shape, 0)
    qbd_t = jnp.concatenate([jnp.where(drow < DA, qt, 0.0), jnp.where(drow >= DA, qt, 0.0)],
                            axis=1).astype(BF16)
    m_sc[...] = jnp.full(m_sc.shape, NEG, F32)
    acc_sc[...] = jnp.zeros(acc_sc.shape, F32)

    def scores(off, tk, bias=None):
        s = _dot(k_ref[pl.ds(off, tk), :], qbd_t)
        if bias is not None:
            s = s + jnp.concatenate([bias, bias], axis=1)
        return s

    def absorb(s, off, tk):
        m_prev = m_sc[...]
        m_new = jnp.maximum(m_prev, jnp.max(s, axis=0, keepdims=True))
        alpha = jnp.exp2(m_prev - m_new)
        p = jnp.exp2(s - m_new).astype(BF16)
        acc_sc[...] = alpha * acc_sc[...] + _dot(va_sc[:, pl.ds(off, tk)], p)
        m_sc[...] = m_new

    def tiles(specs):
        ss = [scores(*sp) for sp in specs]
        for s, sp in zip(ss, specs):
            absorb(s, sp[0], sp[1])

    ntile = jnp.maximum(i - 1, 0)
    nfull = ntile // nsub

    def far(first_tile, n):
        return [(pl.multiple_of((first_tile + j) * tq, tq), tq) for j in range(n)]

    def far_body(t, carry):
        tiles(far(t * nsub, nsub))
        return carry

    lax.fori_loop(0, nfull, far_body, 0)
    done = nfull * nsub
    n = nsub // 2
    while n >= ATTN_TAIL:
        take = ((ntile - done) // n) > 0

        @pl.when(take)
        def _(done=done, n=n):
            tiles(far(done, n))

        done = done + jnp.where(take, n, 0)
        n //= 2

    rest = ntile - done
    for r in range(ATTN_TAIL):
        @pl.when((i >= 1) & (rest == r))
        def _(r=r):
            off = pl.multiple_of((i - 1) * tq, tq)
            tiles(far(done, r) + [(off, tq, bias_sc[1]),
                                  (pl.multiple_of(off + tq, tq), tq, bias_sc[0])])

    @pl.when(i == 0)
    def _():
        tiles([(0, tq, bias_sc[0])])

    lam = _lam_value(lv_ref, lam_init)
    o = acc_sc[0:DVA, :] / acc_sc[DVA:DVA + 1, :]
    o_ref[...] = (o[:, 0:tq] - lam * o[:, tq:2 * tq]).T


def attn_prompt(qab, kab, vt, rel_bias, lamv, lam_init, tq=ATTN_TQ, nsub=ATTN_GROUP):
    B, S, _ = qab.shape
    assert S % tq == 0
    c = jnp.arange(tq, dtype=jnp.int32)[:, None]
    r = jnp.arange(tq, dtype=jnp.int32)[None, :]
    diag = jnp.where((c // CHUNK) <= (r // CHUNK), _t5_bucket(c - r), -1)
    prev = _t5_bucket(c - r - tq)
    bkt = jnp.stack([diag, prev]).astype(jnp.int32)
    smem = pl.BlockSpec(memory_space=pltpu.SMEM)
    return pl.pallas_call(
        functools.partial(_attn_kernel, tq=tq, nsub=nsub, lam_init=lam_init),
        grid=(B, HA, S // tq),
        in_specs=[smem,
                  pl.BlockSpec((4, DA), lambda b, h, i: (0, 0)),
                  pl.BlockSpec((None, tq, DVA), lambda b, h, i: (b, i, h)),
                  pl.BlockSpec((None, S, DVA), lambda b, h, i: (b, 0, h)),
                  pl.BlockSpec((None, DVA, S), lambda b, h, i: (b, h, 0)),
                  pl.BlockSpec((2, tq, tq), lambda b, h, i: (0, 0, 0))],
        out_specs=pl.BlockSpec((None, tq, DVA), lambda b, h, i: (b, i, h)),
        out_shape=jax.ShapeDtypeStruct((B, S, WA), F32),
        scratch_shapes=[pltpu.VMEM((2, tq, tq), F32),
                        pltpu.VMEM((1, 2 * tq), F32),
                        pltpu.VMEM((DVA + BF16_ROWS, 2 * tq), F32),
                        pltpu.VMEM((DVA + BF16_ROWS, S), BF16)],
        compiler_params=_params(("arbitrary", "arbitrary", "arbitrary")),
        name="attn_prompt",
    )(rel_bias, lamv, qab, kab, vt, bkt)


def _attn_small_kernel(rb_ref, lv_ref, q_ref, k_ref, v_ref, bkt_ref, o_ref, *, lq, lam_init):
    h = pl.program_id(1)
    bias = _bias_from_buckets(bkt_ref[...], rb_ref, h, 0.0)
    qbd = _block_diag_q(q_ref[...])
    s = _dot_nt(qbd, k_ref[...]) + jnp.concatenate([bias, bias], axis=0)
    m = jnp.max(s, axis=-1, keepdims=True)
    p = jnp.exp2(s - m)
    l = jnp.sum(p, axis=-1, keepdims=True)
    o = _dot(p.astype(BF16), v_ref[...]) / l
    lam = _lam_value(lv_ref, lam_init)
    o_ref[...] = o[0:lq] - lam * o[lq:2 * lq]


def attn_small(qab, k_all, v_all, q_pos, k_pos, k_valid, rel_bias, lamv, lam_init):
    B, Lq, _ = qab.shape
    Lk = k_all.shape[1]
    visible = ((k_pos[None, :] // CHUNK) <= (q_pos[:, None] // CHUNK)) & k_valid[None, :]
    bkt = jnp.where(visible, _t5_bucket(k_pos[None, :] - q_pos[:, None]), -1).astype(jnp.int32)
    smem = pl.BlockSpec(memory_space=pltpu.SMEM)
    return pl.pallas_call(
        functools.partial(_attn_small_kernel, lq=Lq, lam_init=lam_init),
        grid=(B, HA),
        in_specs=[smem,
                  pl.BlockSpec((4, DA), lambda b, h: (0, 0)),
                  pl.BlockSpec((None, Lq, DVA), lambda b, h: (b, 0, h)),
                  pl.BlockSpec((None, Lk, DVA), lambda b, h: (b, 0, h)),
                  pl.BlockSpec((None, Lk, DVA), lambda b, h: (b, 0, h)),
                  pl.BlockSpec((Lq, Lk), lambda b, h: (0, 0))],
        out_specs=pl.BlockSpec((None, Lq, DVA), lambda b, h: (b, 0, h)),
        out_shape=jax.ShapeDtypeStruct((B, Lq, WA), F32),
        compiler_params=_params(("arbitrary", "arbitrary")),
        name="attn_sample",
    )(rel_bias, lamv, qab, k_all, v_all, bkt)


def _ret_kernel(lg_ref, q_ref, k_ref, v_ref, s0_ref, o_ref, so_ref, state_sc, decay_sc, *, C):
    c = pl.program_id(1)

    @pl.when(c == 0)
    def _():
        state_sc[...] = s0_ref[...]
        r = lax.broadcasted_iota(jnp.int32, (C, C), 0)
        cc = lax.broadcasted_iota(jnp.int32, (C, C), 1)
        diff = (r - cc).astype(F32)
        for h in range(HR):
            decay_sc[h] = jnp.where(diff >= 0, jnp.exp(jnp.maximum(diff, 0.0) * lg_ref[h]), 0.0)

    n = lax.broadcasted_iota(jnp.int32, (C, 1), 0).astype(F32)
    for h in range(HR):
        lg = lg_ref[h]
        cols = slice(h * DKR, (h + 1) * DKR)
        xi = jnp.exp((n + 1.0) * lg)
        zeta = jnp.exp((C - 1.0 - n) * lg)
        q = q_ref[:, cols]
        k = k_ref[:, cols]
        v = v_ref[:, cols]
        state = state_sc[h]
        scores = _dot_nt(q, k.astype(BF16)) * decay_sc[h]
        intra = _dot(scores.astype(BF16), v)
        cross = _dot(q, state.astype(BF16)) * xi
        o_ref[:, cols] = intra + cross
        kz = (k * zeta).astype(BF16)
        state_sc[h] = jnp.exp(C * lg) * state + _dot_tn(kz, v)

    @pl.when(c == pl.num_programs(1) - 1)
    def _():
        so_ref[...] = state_sc[...]


def retention(qrb, kr, vrb, state0, C):
    B, L, _ = qrb.shape
    lg = jnp.log(1.0 - 2.0 ** (-5.0 - jnp.arange(HR, dtype=F32)))
    blk = pl.BlockSpec((None, C, WR), lambda b, c: (b, c, 0))
    st = pl.BlockSpec((None, HR, DKR, DVR), lambda b, c: (b, 0, 0, 0))
    return pl.pallas_call(
        functools.partial(_ret_kernel, C=C),
        grid=(B, L // C),
        in_specs=[pl.BlockSpec(memory_space=pltpu.SMEM), blk, blk, blk, st],
        out_specs=[blk, st],
        out_shape=[jax.ShapeDtypeStruct((B, L, WR), F32),
                   jax.ShapeDtypeStruct((B, HR, DKR, DVR), F32)],
        scratch_shapes=[pltpu.VMEM((HR, DKR, DVR), F32), pltpu.VMEM((HR, C, C), F32)],
        compiler_params=_params(("arbitrary", "arbitrary")),
        name="retention",
    )(lg, qrb, kr, vrb, state0)


def _outmix_kernel(x_ref, oa_ref, or_ref, gr_ref, ga_ref, gb_ref, g1_ref, sh2_ref, sc2_ref, n2_ref,
                   sa_ref, sr_ref, wba_ref, wbr_ref, wo_ref, wpq_ref,
                   x1_ref, h2_ref, qp_ref, *, lam_init, h2_transposed):
    sa = sa_ref[...]
    sr = sr_ref[...]
    gr = gr_ref[...]
    silu_gr = gr * jax.nn.sigmoid(gr)
    ya_parts, yr_parts = [], []
    for hh in range(HA):
        sl = slice(hh * DVA, (hh + 1) * DVA)
        ya_parts.append(_rms_rows(oa_ref[:, sl]) * sa * (1.0 - lam_init))
        yr_parts.append(silu_gr[:, sl] * (_rms_rows(or_ref[:, sl]) * sr))
    ya = jnp.concatenate(ya_parts, axis=-1).astype(BF16)
    yr = jnp.concatenate(yr_parts, axis=-1).astype(BF16)
    y = (jax.nn.sigmoid(ga_ref[...]) * _dot(ya, wba_ref[...])
         + jax.nn.sigmoid(gb_ref[...]) * _dot(yr, wbr_ref[...]))
    out = _dot(y.astype(BF16), wo_ref[...])
    x1 = x_ref[...] + g1_ref[...] * out
    x1_ref[...] = x1
    h2f = _rms_rows(x1) * n2_ref[...] * (1.0 + sc2_ref[...]) + sh2_ref[...]
    h2 = h2f.astype(BF16)
    h2_ref[...] = h2f.T.astype(BF16) if h2_transposed else h2
    qp_ref[...] = _dot(h2, wpq_ref[...]).astype(BF16)


def out_mix(x, oa, orr, gr, ga, gb, g1, sh2, sc2, norm2, subln_a, subln_r,
            w_ba_b, w_br_b, w_o_b, w_pq_b, lam_init, tm):
    B, L, D = x.shape
    nq = w_pq_b.shape[1]
    row = lambda n: pl.BlockSpec((None, tm, n), lambda b, i: (b, i, 0))
    mod = pl.BlockSpec((None, 1, D), lambda b, i: (b, 0, 0))
    full = lambda a: pl.BlockSpec(a.shape, lambda b, i: (0,) * a.ndim)
    n2 = norm2.reshape(1, D)
    sa = subln_a.reshape(1, DVA)
    sr = subln_r.reshape(1, DVR)
    h2_transposed = tm % LANES == 0
    nt = L // tm
    if h2_transposed:
        h2_spec = pl.BlockSpec((D, tm), lambda b, i: (0, b * nt + i))
        h2_shape = jax.ShapeDtypeStruct((D, B * L), BF16)
    else:
        h2_spec = row(D)
        h2_shape = jax.ShapeDtypeStruct((B, L, D), BF16)
    x1, h2, qp = pl.pallas_call(
        functools.partial(_outmix_kernel, lam_init=lam_init, h2_transposed=h2_transposed),
        grid=(B, nt),
        in_specs=[row(D), row(WA), row(WR), row(WR), row(D), row(D), mod, mod, mod,
                  full(n2), full(sa), full(sr), full(w_ba_b), full(w_br_b), full(w_o_b), full(w_pq_b)],
        out_specs=[row(D), h2_spec, row(nq)],
        out_shape=[jax.ShapeDtypeStruct((B, L, D), F32), h2_shape,
                   jax.ShapeDtypeStruct((B, L, nq), BF16)],
        compiler_params=_params(("arbitrary", "arbitrary")),
        name="out_mix",
    )(x, oa, orr, gr, ga, gb, g1.reshape(B, 1, D), sh2.reshape(B, 1, D), sc2.reshape(B, 1, D),
      n2, sa, sr, w_ba_b, w_br_b, w_o_b, w_pq_b)
    h2t = h2 if h2_transposed else h2.reshape(B * L, D).T
    return x1, h2t, qp


UNRANKED = float(PEER_TOPK * PEER_TOPK)


def _topk_rows(s, k, break_ties, want_rank=True):
    n = s.shape[0]
    iota = lax.broadcasted_iota(jnp.int32, s.shape, 0).astype(F32)
    work = s
    rank = jnp.full(s.shape, UNRANKED, F32) if want_rank else None
    vals = []
    for r in range(k):
        m = jnp.max(work, axis=0, keepdims=True)
        sel = work == m
        if break_ties:
            idx = jnp.min(jnp.where(sel, iota, float(n)), axis=0, keepdims=True)
            sel = iota == idx
        if want_rank:
            rank = jnp.where(sel, float(r), rank)
        work = jnp.where(sel, -jnp.inf, work)
        vals.append(m)
    return vals, rank


def _count_rows(mask):
    return jnp.sum(mask.astype(F32), axis=0, keepdims=True)


_HEAD_A = SUBLANES
_CAND_NB = [PEER_TOPK] + [SUBLANES] * (_HEAD_A - 1)


def _route_chunk(s1, s2, break_ties):
    k = PEER_TOPK
    v1, rank1 = _topk_rows(s1, k, break_ties, want_rank=break_ties)
    v2, rank2 = _topk_rows(s2, k, break_ties)
    v2m = jnp.concatenate(v2, axis=0)
    v1t = jnp.concatenate(v1[_HEAD_A:], axis=0)
    blocks = [v1[a] + v2m[0:nb] for a, nb in enumerate(_CAND_NB)] + [v1t + v2[0]]
    cand = jnp.concatenate(blocks, axis=0)
    vc, crank = _topk_rows(cand, k, break_ties, want_rank=break_ties)
    if break_ties:
        sel = crank < UNRANKED
        row_is = lambda a: rank1 == float(a)
        nbad = jnp.zeros((), jnp.int32)
    else:
        sel = cand >= vc[k - 1]
        row_is = lambda a: s1 == v1[a]
        bad = ((_count_rows(s1 >= v1[k - 1]) != float(k))
               | (_count_rows(rank2 < UNRANKED) != float(k)) | (_count_rows(sel) != float(k)))
        nbad = jnp.sum(bad.astype(jnp.int32))
    cmax = v1[0] + v2[0]
    z = jnp.sum(jnp.where(sel, jnp.exp(cand - cmax), 0.0), axis=0, keepdims=True)
    self32 = sel.astype(F32)
    cidx = jnp.zeros(s1.shape, F32)
    lo = 0
    for a, nb in enumerate(_CAND_NB):
        cnt = jnp.sum(self32[lo:lo + nb], axis=0, keepdims=True)
        cidx = jnp.where(row_is(a), cnt, cidx)
        lo += nb
    for a in range(_HEAD_A, k):
        cidx = jnp.where(row_is(a), self32[lo + a - _HEAD_A:lo + a - _HEAD_A + 1], cidx)
    e1 = jnp.exp(s1 - v1[0]) / z
    e2 = jnp.exp(s2 - v2[0])
    return (e1, e2, rank2, cidx), nbad


def _route_kernel(q_ref, keys_ref, e1_ref, e2_ref, r2_ref, c_ref, s1_sc, s2_sc):
    q = q_ref[...]
    s1_sc[...] = _dot_nt(keys_ref[0], q[:, 0:KEY_DIM])
    s2_sc[...] = _dot_nt(keys_ref[1], q[:, KEY_DIM:2 * KEY_DIM])

    def store(sl, res):
        e1, e2, rank2, cidx = res
        e1_ref[:, sl] = e1
        e2_ref[:, sl] = e2.astype(BF16)
        r2_ref[:, sl] = rank2.astype(BF16)
        c_ref[:, sl] = cidx

    group = 2
    assert s1_sc.shape[1] % (group * LANES) == 0

    def chunks(ci, carry):
        sls = [pl.ds(pl.multiple_of((ci * group + g) * LANES, LANES), LANES) for g in range(group)]
        fast = [_route_chunk(s1_sc[:, sl], s2_sc[:, sl], break_ties=False) for sl in sls]
        for sl, (res, _) in zip(sls, fast):
            store(sl, res)
        for sl, (_, nbad) in zip(sls, fast):
            @pl.when(nbad > 0)
            def _():
                store(sl, _route_chunk(s1_sc[:, sl], s2_sc[:, sl], break_ties=True)[0])
        return carry

    lax.fori_loop(0, s1_sc.shape[1] // (group * LANES), chunks, 0)


def peer_route(qp, keys_b, tm):
    T = qp.shape[0]
    out = pl.BlockSpec((None, N_KEYS, tm), lambda t, h: (h, 0, t))
    shp = lambda dt: jax.ShapeDtypeStruct((PEER_HEADS, N_KEYS, T), dt)
    return pl.pallas_call(
        _route_kernel,
        grid=(T // tm, PEER_HEADS),
        in_specs=[pl.BlockSpec((tm, 2 * KEY_DIM), lambda t, h: (t, h)),
                  pl.BlockSpec((2, N_KEYS, KEY_DIM), lambda t, h: (h, 0, 0))],
        out_specs=[out, out, out, out],
        out_shape=[shp(F32), shp(BF16), shp(BF16), shp(F32)],
        scratch_shapes=[pltpu.VMEM((N_KEYS, tm), F32), pltpu.VMEM((N_KEYS, tm), F32)],
        compiler_params=_params(("arbitrary", "arbitrary")),
        name="peer_route",
    )(qp, keys_b)


def _bcast_rows_bf16(row, n):
    t = row.shape[1]
    tile = jnp.broadcast_to(row, (BF16_ROWS, t)).astype(BF16)
    return jnp.broadcast_to(tile[None], (n // BF16_ROWS, BF16_ROWS, t)).reshape(n, t)


def _peer_kernel(h2_ref, u_ref, vt_ref, e1_ref, e2_ref, r2_ref, c_ref, x1_ref, g2_ref, fn_ref,
                 y_ref, acc_sc, *, ni):
    e = pl.program_id(1)

    @pl.when(e == 0)
    def _():
        acc_sc[...] = jnp.zeros(acc_sc.shape, F32)

    h2 = h2_ref[...]
    pair = PEER_SUB * N_KEYS
    a_pairs = [_dot(u_ref[p * pair:(p + 1) * pair, :], h2) for p in range(ni // PEER_SUB)]
    for p in range(ni // PEER_SUB):
        gs = [jnp.zeros((N_KEYS, h2.shape[1]), BF16) for _ in range(PEER_SUB)]
        for hh in range(PEER_HEADS):
            r2h = r2_ref[hh]
            e2h = e2_ref[hh]
            for jj in range(PEER_SUB):
                i = e * ni + PEER_SUB * p + jj
                c_row = _bcast_rows_bf16(c_ref[hh, pl.ds(i, 1), :], N_KEYS)
                e1_row = _bcast_rows_bf16(e1_ref[hh, pl.ds(i, 1), :], N_KEYS)
                gs[jj] = gs[jj] + jnp.where(r2h < c_row, e2h * e1_row, jnp.zeros_like(e2h))
        ws = []
        for jj in range(PEER_SUB):
            a = a_pairs[p][jj * N_KEYS:(jj + 1) * N_KEYS]
            act = 0.5 * a * (1.0 + lax.erf(a * (2.0 ** -0.5)))
            ws.append(gs[jj] * act.astype(BF16))
        w = jnp.concatenate(ws, axis=0)
        acc_sc[...] += _dot(vt_ref[:, p * pair:(p + 1) * pair], w)

    @pl.when(e == pl.num_programs(1) - 1)
    def _():
        x2 = x1_ref[...] + g2_ref[...] * acc_sc[...].T
        y_ref[...] = _rms_rows(x2) * fn_ref[...]


def peer_experts(h2t, u_b, vt_b, e1, e2, r2, cc, x1, g2, final_norm, tm, ni=EXPERT_ROWS):
    D, T = h2t.shape
    nb = ni * N_KEYS
    tok = pl.BlockSpec((tm, D), lambda t, e: (t, 0))
    gate = pl.BlockSpec((PEER_HEADS, N_KEYS, tm), lambda t, e: (0, 0, t))
    L = T // g2.shape[0]
    if L % tm == 0:
        g2_arg = g2.reshape(g2.shape[0], 1, D)
        g2_spec = pl.BlockSpec((None, 1, D), lambda t, e: (t // (L // tm), 0, 0))
    else:
        g2_arg = jnp.repeat(g2, L, axis=0)
        g2_spec = tok
    return pl.pallas_call(
        functools.partial(_peer_kernel, ni=ni),
        grid=(T // tm, u_b.shape[0] // nb),
        in_specs=[pl.BlockSpec((D, tm), lambda t, e: (0, t)),
                  pl.BlockSpec((nb, D), lambda t, e: (e, 0)),
                  pl.BlockSpec((D, nb), lambda t, e: (0, e)),
                  gate, gate, gate, gate, tok, g2_spec,
                  pl.BlockSpec((1, D), lambda t, e: (0, 0))],
        out_specs=tok,
        out_shape=jax.ShapeDtypeStruct((T, D), F32),
        scratch_shapes=[pltpu.VMEM((D, tm), F32)],
        compiler_params=_params(("arbitrary", "arbitrary")),
        name="peer_experts",
    )(h2t, u_b, vt_b, e1, e2, r2, cc, x1, g2_arg, final_norm.reshape(1, D))


def _rot_tables(pos):
    inv = 1.0 / (10000.0 ** jnp.linspace(0.0, 1.0, DKR // 2, dtype=F32))
    ang = pos[:, None].astype(F32) * inv[None, :]
    cos, sin = jnp.cos(ang), jnp.sin(ang)
    return jnp.concatenate([cos, cos], axis=-1), jnp.concatenate([-sin, sin], axis=-1)


def _pick_tile(n, pref):
    t = min(n, pref)
    assert n % t == 0, (n, t)
    return t


def _trunk(x, mods, pos, lam_init, w, attend, values_transposed, state0, final_norm):
    (norm1, norm2, w_in_b, subln_a, subln_r, w_ba_b, w_br_b, w_o_b, w_pq_b, keys_b, u_b, vt_b) = w
    sh1, sc1, g1, sh2, sc2, g2 = mods
    B, L, D = x.shape
    T = B * L
    cos2, sin2 = _rot_tables(pos)
    tm = _pick_tile(L, TOKEN_TILE)
    ka, va, qab, kab, vab, qrb, kr, vrb, gr, ga, gb, *vt = in_proj(
        x, sh1, sc1, norm1, w_in_b, cos2, sin2, tm, emit_vt=values_transposed)
    oa = attend(qab, kab, vt[0] if values_transposed else vab)
    orr, st = retention(qrb, kr, vrb, state0, tm)
    x1, h2t, qp = out_mix(x, oa, orr, gr, ga, gb, g1, sh2, sc2, norm2, subln_a, subln_r,
                          w_ba_b, w_br_b, w_o_b, w_pq_b, lam_init, tm)
    e1, e2, r2, cc = peer_route(qp.reshape(T, -1), keys_b, _pick_tile(T, ROUTE_TILE))
    y = peer_experts(h2t, u_b, vt_b, e1, e2, r2, cc, x1.reshape(T, D), g2,
                     final_norm, _pick_tile(T, EXPERT_TILE))
    return y.reshape(B, L, D), ka, va, st


def kernel(x_prompt, x_sample, cache_k, cache_v, state_ret, c_prompt, c_sample, w_ada, b_ada, norm1,
           norm2, w_in, lam_q1, lam_k1, lam_q2, lam_k2, subln_a, subln_r, w_ba, w_br, w_o, rel_bias,
           w_pq, peer_keys, peer_u, peer_v, final_norm):
    depth = w_ada.shape[0]
    assert depth == 1, "the fused final norm assumes a single layer"
    Bp, Lp, D = x_prompt.shape
    Bs, Ls, _ = x_sample.shape
    past = cache_k.shape[2]
    pos_p = jnp.arange(Lp, dtype=jnp.int32)
    pos_s = past + jnp.arange(Ls, dtype=jnp.int32)
    lk = past + Ls
    lk_pad = -(-lk // LANES) * LANES
    k_pos_s = jnp.arange(lk_pad, dtype=jnp.int32)
    k_valid_s = k_pos_s < lk

    l = 0
    lam_init = 0.8 - 0.6 * math.exp(-0.3 * l)
    lamv = jnp.stack([lam_q1[l], lam_k1[l], lam_q2[l], lam_k2[l]]).astype(F32)
    mod = ada_mod(jnp.concatenate([c_prompt, c_sample], axis=0), w_ada[l], b_ada[l])
    mods = jnp.split(mod, 6, axis=-1)
    mods_p = [m[:Bp] for m in mods]
    mods_s = [m[Bp:] for m in mods]
    w = (norm1[l], norm2[l], w_in[l].astype(BF16), subln_a[l], subln_r[l], w_ba[l].astype(BF16),
         w_br[l].astype(BF16), w_o[l].astype(BF16), w_pq[l].astype(BF16),
         peer_keys[l].reshape(PEER_HEADS * 2, N_KEYS, KEY_DIM).astype(BF16),
         peer_u[l].astype(BF16), peer_v[l].T.astype(BF16))

    attend_p = lambda q, k, v: attn_prompt(q, k, v, rel_bias, lamv, lam_init)
    zero_state = jnp.zeros((Bp, HR, DKR, DVR), F32)
    yp, kp, vp, sp = _trunk(x_prompt, mods_p, pos_p, lam_init, w, attend_p, True, zero_state,
                            final_norm)

    def attend_s(q, k, v):
        padk = jnp.zeros((Bs, lk_pad - lk, WA), BF16)
        kc = cache_k[l].reshape(Bs, past, HA * 2 * DA).astype(BF16)
        vc = cache_v[l].reshape(Bs, past, WA).astype(BF16)
        k_all = jnp.concatenate([kc, k, padk], axis=1)
        v_all = jnp.concatenate([vc, v, padk], axis=1)
        return attn_small(q, k_all, v_all, pos_s, k_pos_s, k_valid_s, rel_bias, lamv, lam_init)

    ys, ks, vs, ss = _trunk(x_sample, mods_s, pos_s, lam_init, w, attend_s, False,
                            state_ret[l].astype(F32), final_norm)
    return (yp, ys, kp[None], vp[None], sp[None], ks[None], vs[None], ss[None])
```

```python
import functools
import math

import jax
import jax.numpy as jnp
from jax import lax
from jax.experimental import pallas as pl
from jax.experimental.pallas import tpu as pltpu

F32 = jnp.float32
BF16 = jnp.bfloat16

CHUNK = 64
HA = 4
DA = 64
DVA = 2 * DA
HR = 4
DKR = 128
DVR = 128
N_BUCKETS = 32
MAX_DIST = 128
PEER_HEADS = 8
N_KEYS = 128
KEY_DIM = 128
PEER_TOPK = 16
EPS = 1e-6
WA = HA * DVA
WR = HR * DVR
NEG = -1e30
LOG2E = math.log2(math.e)
LANES = 128
SUBLANES = 8
BF16_ROWS = 2 * SUBLANES
VMEM_LIMIT = 56 * 1024 * 1024

TOKEN_TILE = 512
RET_CHUNK = 256
ATTN_TQ = 256
ATTN_GROUP = 32
ATTN_TAIL = 4
ROUTE_TILE = 1024
EXPERT_TILE = 512
EXPERT_ROWS = 16
PEER_SUB = 2


def _params(sem, vmem=VMEM_LIMIT):
    return pltpu.CompilerParams(dimension_semantics=sem, vmem_limit_bytes=vmem)


def _dot(a, b):
    return jnp.dot(a, b, preferred_element_type=F32)


def _dot_nt(a, b):
    return lax.dot_general(a, b, (((1,), (1,)), ((), ())), preferred_element_type=F32)


def _dot_tn(a, b):
    return lax.dot_general(a, b, (((0,), (0,)), ((), ())), preferred_element_type=F32)


def _rms_rows(x):
    return x * lax.rsqrt(jnp.mean(x * x, axis=-1, keepdims=True) + EPS)


def _ada_kernel(c_ref, w_ref, b_ref, o_ref):
    c = c_ref[...]
    s = (c * jax.nn.sigmoid(c)).astype(BF16)
    o_ref[...] = _dot(s, w_ref[...].astype(BF16)) + b_ref[...]


def ada_mod(c, w_ada, b_ada, tn=1024):
    nb, d = c.shape
    n = w_ada.shape[1]
    return pl.pallas_call(
        _ada_kernel,
        grid=(n // tn,),
        in_specs=[pl.BlockSpec((nb, d), lambda j: (0, 0)),
                  pl.BlockSpec((d, tn), lambda j: (0, j)),
                  pl.BlockSpec((1, tn), lambda j: (0, j))],
        out_specs=pl.BlockSpec((nb, tn), lambda j: (0, j)),
        out_shape=jax.ShapeDtypeStruct((nb, n), F32),
        compiler_params=_params(("arbitrary",)),
        name="ada_mod",
    )(c, w_ada, b_ada.reshape(1, n))


def _inproj_kernel(x_ref, sh_ref, sc_ref, n1_ref, w_ref, cos_ref, sin_ref,
                   ka_ref, va_ref, qab_ref, kab_ref, vab_ref, qrb_ref, kr_ref, vrb_ref,
                   gr_ref, ga_ref, gb_ref, vt_ref=None):
    x = x_ref[...]
    h = _rms_rows(x) * n1_ref[...] * (1.0 + sc_ref[...]) + sh_ref[...]
    hb = h.astype(BF16)

    d_model = x.shape[1]
    sizes = (WA, WA, WA, WR, WR, WR, WR, d_model, d_model)
    starts = [sum(sizes[:g]) for g in range(len(sizes))]

    def proj(g):
        return _dot(hb, w_ref[:, starts[g]:starts[g] + sizes[g]])

    qa = proj(0)
    qab_ref[...] = (qa * (DA ** -0.5 * LOG2E)).astype(BF16)
    ka = proj(1)
    va = proj(2)
    for hh in range(HA):
        ka_ref[:, hh, :] = ka[:, hh * DVA:(hh + 1) * DVA]
        va_ref[:, hh, :] = va[:, hh * DVA:(hh + 1) * DVA]
    kab_ref[...] = ka.astype(BF16)
    vab_ref[...] = va.astype(BF16)
    if vt_ref is not None:
        vt_ref[...] = va.T.astype(BF16)
    cos2 = cos_ref[...]
    sin2 = sin_ref[...]

    def rot(z):
        parts = []
        for hh in range(HR):
            zh = z[:, hh * DKR:(hh + 1) * DKR]
            parts.append(zh * cos2 + pltpu.roll(zh, DKR // 2, 1) * sin2)
        return jnp.concatenate(parts, axis=-1)

    qrb_ref[...] = rot(proj(3)).astype(BF16)
    kr_ref[...] = rot(proj(4)) * (DKR ** -0.5)
    vrb_ref[...] = proj(5).astype(BF16)
    gr_ref[...] = proj(6)
    ga_ref[...] = proj(7)
    gb_ref[...] = proj(8)


def in_proj(x, sh1, sc1, norm1, w_in_b, cos2, sin2, tm, emit_vt):
    B, L, D = x.shape
    d_in = w_in_b.shape[1]
    row = lambda n: pl.BlockSpec((None, tm, n), lambda b, i: (b, i, 0))
    mod = pl.BlockSpec((None, 1, D), lambda b, i: (b, 0, 0))
    f = lambda n, dt: jax.ShapeDtypeStruct((B, L, n), dt)
    cache = pl.BlockSpec((None, tm, HA, DVA), lambda b, i: (b, i, 0, 0))
    cache_shape = jax.ShapeDtypeStruct((B, L, HA, DVA), F32)
    vt_spec = [pl.BlockSpec((None, WA, tm), lambda b, i: (b, 0, i))] if emit_vt else []
    vt_shape = [jax.ShapeDtypeStruct((B, WA, L), BF16)] if emit_vt else []
    return pl.pallas_call(
        _inproj_kernel,
        grid=(B, L // tm),
        in_specs=[row(D), mod, mod,
                  pl.BlockSpec((1, D), lambda b, i: (0, 0)),
                  pl.BlockSpec((D, d_in), lambda b, i: (0, 0), pipeline_mode=pl.Buffered(1)),
                  pl.BlockSpec((tm, DKR), lambda b, i: (i, 0)),
                  pl.BlockSpec((tm, DKR), lambda b, i: (i, 0))],
        out_specs=[cache, cache, row(WA), row(WA), row(WA), row(WR), row(WR), row(WR),
                   row(WR), row(D), row(D)] + vt_spec,
        out_shape=[cache_shape, cache_shape, f(WA, BF16), f(WA, BF16), f(WA, BF16),
                   f(WR, BF16), f(WR, F32), f(WR, BF16), f(WR, F32), f(D, F32), f(D, F32)]
        + vt_shape,
        compiler_params=_params(("arbitrary", "arbitrary")),
        name="in_proj",
    )(x, sh1.reshape(B, 1, D), sc1.reshape(B, 1, D), norm1.reshape(1, D), w_in_b, cos2, sin2)


def _t5_bucket(rel):
    nb = N_BUCKETS // 2
    ret = jnp.where(rel > 0, nb, 0)
    n = jnp.abs(rel)
    max_exact = nb // 2
    nf = jnp.maximum(n, max_exact).astype(F32)
    large = max_exact + (jnp.log(nf / max_exact) / math.log(MAX_DIST / max_exact)
                         * (nb - max_exact)).astype(jnp.int32)
    large = jnp.minimum(large, nb - 1)
    return ret + jnp.where(n < max_exact, n, large)


def _bias_from_buckets(bkt, rb_ref, h, shift):
    val = jnp.where(bkt < 0, NEG, 0.0).astype(F32)
    for n in range(N_BUCKETS):
        val = jnp.where(bkt == n, (rb_ref[n, h] - shift) * LOG2E, val)
    return val


def _lam_value(lv_ref, lam_init):
    lv = lv_ref[...]
    a = jnp.sum(lv[0:1] * lv[1:2], axis=-1, keepdims=True)
    b = jnp.sum(lv[2:3] * lv[3:4], axis=-1, keepdims=True)
    return jnp.exp(a) - jnp.exp(b) + lam_init


def _block_diag_q(q):
    lane = lax.broadcasted_iota(jnp.int32, q.shape, 1)
    zero = jnp.zeros_like(q)
    return jnp.concatenate([jnp.where(lane < DA, q, zero), jnp.where(lane >= DA, q, zero)], axis=0)


def _attn_kernel(rb_ref, lv_ref, q_ref, k_ref, vt_ref, bkt_ref, o_ref,
                 bias_sc, m_sc, acc_sc, va_sc, *, tq, nsub, lam_init):
    h = pl.program_id(1)
    i = pl.program_id(2)

    @pl.when(i == 0)
    def _():
        far = rb_ref[N_BUCKETS // 2 - 1, h]
        for t in range(2):
            bias_sc[t] = _bias_from_buckets(bkt_ref[t], rb_ref, h, far)
        va_sc[0:DVA, :] = vt_ref[...]
        va_sc[DVA:, :] = jnp.ones((BF16_ROWS, va_sc.shape[1]), BF16)

    qt = q_ref[...].astype(F32).T
    drow = lax.broadcasted_iota(jnp.int32, qt.shape, 0)
    qbd_t = jnp.concatenate([jnp.where(drow < DA, qt, 0.0), jnp.where(drow >= DA, qt, 0.0)],
                            axis=1).astype(BF16)
    m_sc[...] = jnp.full(m_sc.shape, NEG, F32)
    acc_sc[...] = jnp.zeros(acc_sc.shape, F32)

    def scores(off, tk, bias=None):
        s = _dot(k_ref[pl.ds(off, tk), :], qbd_t)
        if bias is not None:
            s = s + jnp.concatenate([bias, bias], axis=1)
        return s

    def absorb(s, off, tk):
        m_prev = m_sc[...]
        m_new = jnp.maximum(m_prev, jnp.max(s, axis=0, keepdims=True))
        alpha = jnp.exp2(m_prev - m_new)
        p = jnp.exp2(s - m_new).astype(BF16)
        acc_sc[...] = alpha * acc_sc[...] + _dot(va_sc[:, pl.ds(off, tk)], p)
        m_sc[...] = m_new

    def tiles(specs):
        ss = [scores(*sp) for sp in specs]
        for s, sp in zip(ss, specs):
            absorb(s, sp[0], sp[1])

    ntile = jnp.maximum(i - 1, 0)
    nfull = ntile // nsub

    def far(first_tile, n):
        return [(pl.multiple_of((first_tile + j) * tq, tq), tq) for j in range(n)]

    def far_body(t, carry):
        tiles(far(t * nsub, nsub))
        return carry

    lax.fori_loop(0, nfull, far_body, 0)
    done = nfull * nsub
    n = nsub // 2
    while n >= ATTN_TAIL:
        take = ((ntile - done) // n) > 0

        @pl.when(take)
        def _(done=done, n=n):
            tiles(far(done, n))

        done = done + jnp.where(take, n, 0)
        n //= 2

    rest = ntile - done
    for r in range(ATTN_TAIL):
        @pl.when((i >= 1) & (rest == r))
        def _(r=r):
            off = pl.multiple_of((i - 1) * tq, tq)
            tiles(far(done, r) + [(off, tq, bias_sc[1]),
                                  (pl.multiple_of(off + tq, tq), tq, bias_sc[0])])

    @pl.when(i == 0)
    def _():
        tiles([(0, tq, bias_sc[0])])

    lam = _lam_value(lv_ref, lam_init)
    o = acc_sc[0:DVA, :] / acc_sc[DVA:DVA + 1, :]
    o_ref[...] = (o[:, 0:tq] - lam * o[:, tq:2 * tq]).T


def attn_prompt(qab, kab, vt, rel_bias, lamv, lam_init, tq=ATTN_TQ, nsub=ATTN_GROUP):
    B, S, _ = qab.shape
    assert S % tq == 0
    c = jnp.arange(tq, dtype=jnp.int32)[:, None]
    r = jnp.arange(tq, dtype=jnp.int32)[None, :]
    diag = jnp.where((c // CHUNK) <= (r // CHUNK), _t5_bucket(c - r), -1)
    prev = _t5_bucket(c - r - tq)
    bkt = jnp.stack([diag, prev]).astype(jnp.int32)
    smem = pl.BlockSpec(memory_space=pltpu.SMEM)
    return pl.pallas_call(
        functools.partial(_attn_kernel, tq=tq, nsub=nsub, lam_init=lam_init),
        grid=(B, HA, S // tq),
        in_specs=[smem,
                  pl.BlockSpec((4, DA), lambda b, h, i: (0, 0)),
                  pl.BlockSpec((None, tq, DVA), lambda b, h, i: (b, i, h)),
                  pl.BlockSpec((None, S, DVA), lambda b, h, i: (b, 0, h)),
                  pl.BlockSpec((None, DVA, S), lambda b, h, i: (b, h, 0)),
                  pl.BlockSpec((2, tq, tq), lambda b, h, i: (0, 0, 0))],
        out_specs=pl.BlockSpec((None, tq, DVA), lambda b, h, i: (b, i, h)),
        out_shape=jax.ShapeDtypeStruct((B, S, WA), F32),
        scratch_shapes=[pltpu.VMEM((2, tq, tq), F32),
                        pltpu.VMEM((1, 2 * tq), F32),
                        pltpu.VMEM((DVA + BF16_ROWS, 2 * tq), F32),
                        pltpu.VMEM((DVA + BF16_ROWS, S), BF16)],
        compiler_params=_params(("arbitrary", "arbitrary", "arbitrary")),
        name="attn_prompt",
    )(rel_bias, lamv, qab, kab, vt, bkt)


def _attn_small_kernel(rb_ref, lv_ref, q_ref, k_ref, v_ref, bkt_ref, o_ref, *, lq, lam_init):
    h = pl.program_id(1)
    bias = _bias_from_buckets(bkt_ref[...], rb_ref, h, 0.0)
    qbd = _block_diag_q(q_ref[...])
    s = _dot_nt(qbd, k_ref[...]) + jnp.concatenate([bias, bias], axis=0)
    m = jnp.max(s, axis=-1, keepdims=True)
    p = jnp.exp2(s - m)
    l = jnp.sum(p, axis=-1, keepdims=True)
    o = _dot(p.astype(BF16), v_ref[...]) / l
    lam = _lam_value(lv_ref, lam_init)
    o_ref[...] = o[0:lq] - lam * o[lq:2 * lq]


def attn_small(qab, k_all, v_all, q_pos, k_pos, k_valid, rel_bias, lamv, lam_init):
    B, Lq, _ = qab.shape
    Lk = k_all.shape[1]
    visible = ((k_pos[None, :] // CHUNK) <= (q_pos[:, None] // CHUNK)) & k_valid[None, :]
    bkt = jnp.where(visible, _t5_bucket(k_pos[None, :] - q_pos[:, None]), -1).astype(jnp.int32)
    smem = pl.BlockSpec(memory_space=pltpu.SMEM)
    return pl.pallas_call(
        functools.partial(_attn_small_kernel, lq=Lq, lam_init=lam_init),
        grid=(B, HA),
        in_specs=[smem,
                  pl.BlockSpec((4, DA), lambda b, h: (0, 0)),
                  pl.BlockSpec((None, Lq, DVA), lambda b, h: (b, 0, h)),
                  pl.BlockSpec((None, Lk, DVA), lambda b, h: (b, 0, h)),
                  pl.BlockSpec((None, Lk, DVA), lambda b, h: (b, 0, h)),
                  pl.BlockSpec((Lq, Lk), lambda b, h: (0, 0))],
        out_specs=pl.BlockSpec((None, Lq, DVA), lambda b, h: (b, 0, h)),
        out_shape=jax.ShapeDtypeStruct((B, Lq, WA), F32),
        compiler_params=_params(("arbitrary", "arbitrary")),
        name="attn_sample",
    )(rel_bias, lamv, qab, k_all, v_all, bkt)


def _ret_kernel(lg_ref, q_ref, k_ref, v_ref, s0_ref, o_ref, so_ref, state_sc, decay_sc, *, C):
    c = pl.program_id(1)

    @pl.when(c == 0)
    def _():
        state_sc[...] = s0_ref[...]
        r = lax.broadcasted_iota(jnp.int32, (C, C), 0)
        cc = lax.broadcasted_iota(jnp.int32, (C, C), 1)
        diff = (r - cc).astype(F32)
        for h in range(HR):
            decay_sc[h] = jnp.where(diff >= 0, jnp.exp(jnp.maximum(diff, 0.0) * lg_ref[h]), 0.0)

    n = lax.broadcasted_iota(jnp.int32, (C, 1), 0).astype(F32)
    for h in range(HR):
        lg = lg_ref[h]
        cols = slice(h * DKR, (h + 1) * DKR)
        xi = jnp.exp((n + 1.0) * lg)
        zeta = jnp.exp((C - 1.0 - n) * lg)
        q = q_ref[:, cols]
        k = k_ref[:, cols]
        v = v_ref[:, cols]
        state = state_sc[h]
        scores = _dot_nt(q, k.astype(BF16)) * decay_sc[h]
        intra = _dot(scores.astype(BF16), v)
        cross = _dot(q, state.astype(BF16)) * xi
        o_ref[:, cols] = intra + cross
        kz = (k * zeta).astype(BF16)
        state_sc[h] = jnp.exp(C * lg) * state + _dot_tn(kz, v)

    @pl.when(c == pl.num_programs(1) - 1)
    def _():
        so_ref[...] = state_sc[...]


def retention(qrb, kr, vrb, state0, C):
    B, L, _ = qrb.shape
    lg = jnp.log(1.0 - 2.0 ** (-5.0 - jnp.arange(HR, dtype=F32)))
    blk = pl.BlockSpec((None, C, WR), lambda b, c: (b, c, 0))
    st = pl.BlockSpec((None, HR, DKR, DVR), lambda b, c: (b, 0, 0, 0))
    return pl.pallas_call(
        functools.partial(_ret_kernel, C=C),
        grid=(B, L // C),
        in_specs=[pl.BlockSpec(memory_space=pltpu.SMEM), blk, blk, blk, st],
        out_specs=[blk, st],
        out_shape=[jax.ShapeDtypeStruct((B, L, WR), F32),
                   jax.ShapeDtypeStruct((B, HR, DKR, DVR), F32)],
        scratch_shapes=[pltpu.VMEM((HR, DKR, DVR), F32), pltpu.VMEM((HR, C, C), F32)],
        compiler_params=_params(("arbitrary", "arbitrary")),
        name="retention",
    )(lg, qrb, kr, vrb, state0)


def _outmix_kernel(x_ref, oa_ref, or_ref, gr_ref, ga_ref, gb_ref, g1_ref, sh2_ref, sc2_ref, n2_ref,
                   sa_ref, sr_ref, wba_ref, wbr_ref, wo_ref, wpq_ref,
                   x1_ref, h2_ref, qp_ref, *, lam_init, h2_transposed):
    sa = sa_ref[...]
    sr = sr_ref[...]
    gr = gr_ref[...]
    silu_gr = gr * jax.nn.sigmoid(gr)
    ya_parts, yr_parts = [], []
    for hh in range(HA):
        sl = slice(hh * DVA, (hh + 1) * DVA)
        ya_parts.append(_rms_rows(oa_ref[:, sl]) * sa * (1.0 - lam_init))
        yr_parts.append(silu_gr[:, sl] * (_rms_rows(or_ref[:, sl]) * sr))
    ya = jnp.concatenate(ya_parts, axis=-1).astype(BF16)
    yr = jnp.concatenate(yr_parts, axis=-1).astype(BF16)
    y = (jax.nn.sigmoid(ga_ref[...]) * _dot(ya, wba_ref[...])
         + jax.nn.sigmoid(gb_ref[...]) * _dot(yr, wbr_ref[...]))
    out = _dot(y.astype(BF16), wo_ref[...])
    x1 = x_ref[...] + g1_ref[...] * out
    x1_ref[...] = x1
    h2f = _rms_rows(x1) * n2_ref[...] * (1.0 + sc2_ref[...]) + sh2_ref[...]
    h2 = h2f.astype(BF16)
    h2_ref[...] = h2f.T.astype(BF16) if h2_transposed else h2
    qp_ref[...] = _dot(h2, wpq_ref[...]).astype(BF16)


def out_mix(x, oa, orr, gr, ga, gb, g1, sh2, sc2, norm2, subln_a, subln_r,
            w_ba_b, w_br_b, w_o_b, w_pq_b, lam_init, tm):
    B, L, D = x.shape
    nq = w_pq_b.shape[1]
    row = lambda n: pl.BlockSpec((None, tm, n), lambda b, i: (b, i, 0))
    mod = pl.BlockSpec((None, 1, D), lambda b, i: (b, 0, 0))
    full = lambda a: pl.BlockSpec(a.shape, lambda b, i: (0,) * a.ndim, pipeline_mode=pl.Buffered(1))
    n2 = norm2.reshape(1, D)
    sa = subln_a.reshape(1, DVA)
    sr = subln_r.reshape(1, DVR)
    h2_transposed = tm % LANES == 0
    nt = L // tm
    if h2_transposed:
        h2_spec = pl.BlockSpec((D, tm), lambda b, i: (0, b * nt + i))
        h2_shape = jax.ShapeDtypeStruct((D, B * L), BF16)
    else:
        h2_spec = row(D)
        h2_shape = jax.ShapeDtypeStruct((B, L, D), BF16)
    x1, h2, qp = pl.pallas_call(
        functools.partial(_outmix_kernel, lam_init=lam_init, h2_transposed=h2_transposed),
        grid=(B, nt),
        in_specs=[row(D), row(WA), row(WR), row(WR), row(D), row(D), mod, mod, mod,
                  full(n2), full(sa), full(sr), full(w_ba_b), full(w_br_b), full(w_o_b), full(w_pq_b)],
        out_specs=[row(D), h2_spec, row(nq)],
        out_shape=[jax.ShapeDtypeStruct((B, L, D), F32), h2_shape,
                   jax.ShapeDtypeStruct((B, L, nq), BF16)],
        compiler_params=_params(("arbitrary", "arbitrary")),
        name="out_mix",
    )(x, oa, orr, gr, ga, gb, g1.reshape(B, 1, D), sh2.reshape(B, 1, D), sc2.reshape(B, 1, D),
      n2, sa, sr, w_ba_b, w_br_b, w_o_b, w_pq_b)
    h2t = h2 if h2_transposed else h2.reshape(B * L, D).T
    return x1, h2t, qp


UNRANKED = float(PEER_TOPK * PEER_TOPK)


def _topk_rows(s, k, break_ties, want_rank=True):
    n = s.shape[0]
    iota = lax.broadcasted_iota(jnp.int32, s.shape, 0).astype(F32)
    work = s
    rank = jnp.full(s.shape, UNRANKED, F32) if want_rank else None
    vals = []
    for r in range(k):
        m = jnp.max(work, axis=0, keepdims=True)
        sel = work == m
        if break_ties:
            idx = jnp.min(jnp.where(sel, iota, float(n)), axis=0, keepdims=True)
            sel = iota == idx
        if want_rank:
            rank = jnp.where(sel, float(r), rank)
        work = jnp.where(sel, -jnp.inf, work)
        vals.append(m)
    return vals, rank


def _count_rows(mask):
    return jnp.sum(mask.astype(F32), axis=0, keepdims=True)


_HEAD_A = SUBLANES
_CAND_NB = [PEER_TOPK] + [SUBLANES] * (_HEAD_A - 1)


def _route_chunk(s1, s2, break_ties):
    k = PEER_TOPK
    v1, rank1 = _topk_rows(s1, k, break_ties, want_rank=break_ties)
    v2, rank2 = _topk_rows(s2, k, break_ties)
    v2m = jnp.concatenate(v2, axis=0)
    v1t = jnp.concatenate(v1[_HEAD_A:], axis=0)
    blocks = [v1[a] + v2m[0:nb] for a, nb in enumerate(_CAND_NB)] + [v1t + v2[0]]
    cand = jnp.concatenate(blocks, axis=0)
    vc, crank = _topk_rows(cand, k, break_ties, want_rank=break_ties)
    if break_ties:
        sel = crank < UNRANKED
        row_is = lambda a: rank1 == float(a)
        nbad = jnp.zeros((), jnp.int32)
    else:
        sel = cand >= vc[k - 1]
        row_is = lambda a: s1 == v1[a]
        bad = ((_count_rows(s1 >= v1[k - 1]) != float(k))
               | (_count_rows(rank2 < UNRANKED) != float(k)) | (_count_rows(sel) != float(k)))
        nbad = jnp.sum(bad.astype(jnp.int32))
    cmax = v1[0] + v2[0]
    z = jnp.sum(jnp.where(sel, jnp.exp(cand - cmax), 0.0), axis=0, keepdims=True)
    self32 = sel.astype(F32)
    cidx = jnp.zeros(s1.shape, F32)
    lo = 0
    for a, nb in enumerate(_CAND_NB):
        cnt = jnp.sum(self32[lo:lo + nb], axis=0, keepdims=True)
        cidx = jnp.where(row_is(a), cnt, cidx)
        lo += nb
    for a in range(_HEAD_A, k):
        cidx = jnp.where(row_is(a), self32[lo + a - _HEAD_A:lo + a - _HEAD_A + 1], cidx)
    e1 = jnp.exp(s1 - v1[0]) / z
    e2 = jnp.exp(s2 - v2[0])
    return (e1, e2, rank2, cidx), nbad


def _route_kernel(q_ref, keys_ref, e1_ref, e2_ref, r2_ref, c_ref, s1_sc, s2_sc):
    q = q_ref[...]
    s1_sc[...] = _dot_nt(keys_ref[0], q[:, 0:KEY_DIM])
    s2_sc[...] = _dot_nt(keys_ref[1], q[:, KEY_DIM:2 * KEY_DIM])

    def store(sl, res):
        e1, e2, rank2, cidx = res
        e1_ref[:, sl] = e1
        e2_ref[:, sl] = e2.astype(BF16)
        r2_ref[:, sl] = rank2.astype(BF16)
        c_ref[:, sl] = cidx

    group = 2
    assert s1_sc.shape[1] % (group * LANES) == 0

    def chunks(ci, carry):
        sls = [pl.ds(pl.multiple_of((ci * group + g) * LANES, LANES), LANES) for g in range(group)]
        fast = [_route_chunk(s1_sc[:, sl], s2_sc[:, sl], break_ties=False) for sl in sls]
        for sl, (res, _) in zip(sls, fast):
            store(sl, res)
        for sl, (_, nbad) in zip(sls, fast):
            @pl.when(nbad > 0)
            def _():
                store(sl, _route_chunk(s1_sc[:, sl], s2_sc[:, sl], break_ties=True)[0])
        return carry

    lax.fori_loop(0, s1_sc.shape[1] // (group * LANES), chunks, 0)


def peer_route(qp, keys_b, tm):
    T = qp.shape[0]
    out = pl.BlockSpec((None, N_KEYS, tm), lambda t, h: (h, 0, t))
    shp = lambda dt: jax.ShapeDtypeStruct((PEER_HEADS, N_KEYS, T), dt)
    return pl.pallas_call(
        _route_kernel,
        grid=(T // tm, PEER_HEADS),
        in_specs=[pl.BlockSpec((tm, 2 * KEY_DIM), lambda t, h: (t, h)),
                  pl.BlockSpec((2, N_KEYS, KEY_DIM), lambda t, h: (h, 0, 0))],
        out_specs=[out, out, out, out],
        out_shape=[shp(F32), shp(BF16), shp(BF16), shp(F32)],
        scratch_shapes=[pltpu.VMEM((N_KEYS, tm), F32), pltpu.VMEM((N_KEYS, tm), F32)],
        compiler_params=_params(("arbitrary", "arbitrary")),
        name="peer_route",
    )(qp, keys_b)


def _bcast_rows_bf16(row, n):
    t = row.shape[1]
    tile = jnp.broadcast_to(row, (BF16_ROWS, t)).astype(BF16)
    return jnp.broadcast_to(tile[None], (n // BF16_ROWS, BF16_ROWS, t)).reshape(n, t)


def _peer_kernel(h2_ref, u_ref, vt_ref, e1_ref, e2_ref, r2_ref, c_ref, x1_ref, g2_ref, fn_ref,
                 y_ref, acc_sc, *, ni):
    e = pl.program_id(1)

    @pl.when(e == 0)
    def _():
        acc_sc[...] = jnp.zeros(acc_sc.shape, F32)

    h2 = h2_ref[...]
    pair = PEER_SUB * N_KEYS
    a_pairs = [_dot(u_ref[p * pair:(p + 1) * pair, :], h2) for p in range(ni // PEER_SUB)]
    for p in range(ni // PEER_SUB):
        gs = [jnp.zeros((N_KEYS, h2.shape[1]), BF16) for _ in range(PEER_SUB)]
        for hh in range(PEER_HEADS):
            r2h = r2_ref[hh]
            e2h = e2_ref[hh]
            for jj in range(PEER_SUB):
                i = e * ni + PEER_SUB * p + jj
                c_row = _bcast_rows_bf16(c_ref[hh, pl.ds(i, 1), :], N_KEYS)
                e1_row = _bcast_rows_bf16(e1_ref[hh, pl.ds(i, 1), :], N_KEYS)
                gs[jj] = gs[jj] + jnp.where(r2h < c_row, e2h * e1_row, jnp.zeros_like(e2h))
        ws = []
        for jj in range(PEER_SUB):
            a = a_pairs[p][jj * N_KEYS:(jj + 1) * N_KEYS]
            act = 0.5 * a * (1.0 + lax.erf(a * (2.0 ** -0.5)))
            ws.append(gs[jj] * act.astype(BF16))
        w = jnp.concatenate(ws, axis=0)
        acc_sc[...] += _dot(vt_ref[:, p * pair:(p + 1) * pair], w)

    @pl.when(e == pl.num_programs(1) - 1)
    def _():
        x2 = x1_ref[...] + g2_ref[...] * acc_sc[...].T
        y_ref[...] = _rms_rows(x2) * fn_ref[...]


def peer_experts(h2t, u_b, vt_b, e1, e2, r2, cc, x1, g2, final_norm, tm, ni=EXPERT_ROWS):
    D, T = h2t.shape
    nb = ni * N_KEYS
    tok = pl.BlockSpec((tm, D), lambda t, e: (t, 0))
    gate = pl.BlockSpec((PEER_HEADS, N_KEYS, tm), lambda t, e: (0, 0, t))
    L = T // g2.shape[0]
    if L % tm == 0:
        g2_arg = g2.reshape(g2.shape[0], 1, D)
        g2_spec = pl.BlockSpec((None, 1, D), lambda t, e: (t // (L // tm), 0, 0))
    else:
        g2_arg = jnp.repeat(g2, L, axis=0)
        g2_spec = tok
    return pl.pallas_call(
        functools.partial(_peer_kernel, ni=ni),
        grid=(T // tm, u_b.shape[0] // nb),
        in_specs=[pl.BlockSpec((D, tm), lambda t, e: (0, t)),
                  pl.BlockSpec((nb, D), lambda t, e: (e, 0)),
                  pl.BlockSpec((D, nb), lambda t, e: (0, e)),
                  gate, gate, gate, gate, tok, g2_spec,
                  pl.BlockSpec((1, D), lambda t, e: (0, 0))],
        out_specs=tok,
        out_shape=jax.ShapeDtypeStruct((T, D), F32),
        scratch_shapes=[pltpu.VMEM((D, tm), F32)],
        compiler_params=_params(("arbitrary", "arbitrary")),
        name="peer_experts",
    )(h2t, u_b, vt_b, e1, e2, r2, cc, x1, g2_arg, final_norm.reshape(1, D))


def _rot_tables(pos):
    inv = 1.0 / (10000.0 ** jnp.linspace(0.0, 1.0, DKR // 2, dtype=F32))
    ang = pos[:, None].astype(F32) * inv[None, :]
    cos, sin = jnp.cos(ang), jnp.sin(ang)
    return jnp.concatenate([cos, cos], axis=-1), jnp.concatenate([-sin, sin], axis=-1)


def _pick_tile(n, pref):
    t = min(n, pref)
    assert n % t == 0, (n, t)
    return t


def _trunk(x, mods, pos, lam_init, w, attend, values_transposed, state0, final_norm):
    (norm1, norm2, w_in_b, subln_a, subln_r, w_ba_b, w_br_b, w_o_b, w_pq_b, keys_b, u_b, vt_b) = w
    sh1, sc1, g1, sh2, sc2, g2 = mods
    B, L, D = x.shape
    T = B * L
    cos2, sin2 = _rot_tables(pos)
    tm = _pick_tile(L, TOKEN_TILE)
    ka, va, qab, kab, vab, qrb, kr, vrb, gr, ga, gb, *vt = in_proj(
        x, sh1, sc1, norm1, w_in_b, cos2, sin2, tm, emit_vt=values_transposed)
    oa = attend(qab, kab, vt[0] if values_transposed else vab)
    orr, st = retention(qrb, kr, vrb, state0, _pick_tile(L, RET_CHUNK))
    x1, h2t, qp = out_mix(x, oa, orr, gr, ga, gb, g1, sh2, sc2, norm2, subln_a, subln_r,
                          w_ba_b, w_br_b, w_o_b, w_pq_b, lam_init, tm)
    e1, e2, r2, cc = peer_route(qp.reshape(T, -1), keys_b, _pick_tile(T, ROUTE_TILE))
    y = peer_experts(h2t, u_b, vt_b, e1, e2, r2, cc, x1.reshape(T, D), g2,
                     final_norm, _pick_tile(T, EXPERT_TILE))
    return y.reshape(B, L, D), ka, va, st


def kernel(x_prompt, x_sample, cache_k, cache_v, state_ret, c_prompt, c_sample, w_ada, b_ada, norm1,
           norm2, w_in, lam_q1, lam_k1, lam_q2, lam_k2, subln_a, subln_r, w_ba, w_br, w_o, rel_bias,
           w_pq, peer_keys, peer_u, peer_v, final_norm):
    depth = w_ada.shape[0]
    assert depth == 1, "the fused final norm assumes a single layer"
    Bp, Lp, D = x_prompt.shape
    Bs, Ls, _ = x_sample.shape
    past = cache_k.shape[2]
    pos_p = jnp.arange(Lp, dtype=jnp.int32)
    pos_s = past + jnp.arange(Ls, dtype=jnp.int32)
    lk = past + Ls
    lk_pad = -(-lk // LANES) * LANES
    k_pos_s = jnp.arange(lk_pad, dtype=jnp.int32)
    k_valid_s = k_pos_s < lk

    l = 0
    lam_init = 0.8 - 0.6 * math.exp(-0.3 * l)
    lamv = jnp.stack([lam_q1[l], lam_k1[l], lam_q2[l], lam_k2[l]]).astype(F32)
    mod = ada_mod(jnp.concatenate([c_prompt, c_sample], axis=0), w_ada[l], b_ada[l])
    mods = jnp.split(mod, 6, axis=-1)
    mods_p = [m[:Bp] for m in mods]
    mods_s = [m[Bp:] for m in mods]
    w = (norm1[l], norm2[l], w_in[l].astype(BF16), subln_a[l], subln_r[l], w_ba[l].astype(BF16),
         w_br[l].astype(BF16), w_o[l].astype(BF16), w_pq[l].astype(BF16),
         peer_keys[l].reshape(PEER_HEADS * 2, N_KEYS, KEY_DIM).astype(BF16),
         peer_u[l].astype(BF16), peer_v[l].T.astype(BF16))

    attend_p = lambda q, k, v: attn_prompt(q, k, v, rel_bias, lamv, lam_init)
    zero_state = jnp.zeros((Bp, HR, DKR, DVR), F32)
    yp, kp, vp, sp = _trunk(x_prompt, mods_p, pos_p, lam_init, w, attend_p, True, zero_state,
                            final_norm)

    def attend_s(q, k, v):
        padk = jnp.zeros((Bs, lk_pad - lk, WA), BF16)
        kc = cache_k[l].reshape(Bs, past, HA * 2 * DA).astype(BF16)
        vc = cache_v[l].reshape(Bs, past, WA).astype(BF16)
        k_all = jnp.concatenate([kc, k, padk], axis=1)
        v_all = jnp.concatenate([vc, v, padk], axis=1)
        return attn_small(q, k_all, v_all, pos_s, k_pos_s, k_valid_s, rel_bias, lamv, lam_init)

    ys, ks, vs, ss = _trunk(x_sample, mods_s, pos_s, lam_init, w, attend_s, False,
                            state_ret[l].astype(F32), final_norm)
    return (yp, ys, kp[None], vp[None], sp[None], ks[None], vs[None], ss[None])
```

```python
import functools
import math

import jax
import jax.numpy as jnp
from jax import lax
from jax.experimental import pallas as pl
from jax.experimental.pallas import tpu as pltpu

F32 = jnp.float32
BF16 = jnp.bfloat16

CHUNK = 64
HA = 4
DA = 64
DVA = 2 * DA
HR = 4
DKR = 128
DVR = 128
N_BUCKETS = 32
MAX_DIST = 128
PEER_HEADS = 8
N_KEYS = 128
KEY_DIM = 128
PEER_TOPK = 16
EPS = 1e-6
WA = HA * DVA
WR = HR * DVR
NEG = -1e30
LOG2E = math.log2(math.e)
LANES = 128
SUBLANES = 8
BF16_ROWS = 2 * SUBLANES
VMEM_LIMIT = 56 * 1024 * 1024

TOKEN_TILE = 512
RET_CHUNK = 256
ATTN_TQ = 256
ATTN_GROUP = 32
ATTN_TAIL = 4
ROUTE_TILE = 1024
ROUTE_CHUNKS = 8
EXPERT_TILE = 512
EXPERT_ROWS = 16
PEER_SUB = 2


def _params(sem, vmem=VMEM_LIMIT):
    return pltpu.CompilerParams(dimension_semantics=sem, vmem_limit_bytes=vmem)


def _dot(a, b):
    return jnp.dot(a, b, preferred_element_type=F32)


def _dot_nt(a, b):
    return lax.dot_general(a, b, (((1,), (1,)), ((), ())), preferred_element_type=F32)


def _dot_tn(a, b):
    return lax.dot_general(a, b, (((0,), (0,)), ((), ())), preferred_element_type=F32)


def _rms_rows(x):
    return x * lax.rsqrt(jnp.mean(x * x, axis=-1, keepdims=True) + EPS)


def _ada_kernel(c_ref, w_ref, b_ref, o_ref):
    c = c_ref[...]
    s = (c * jax.nn.sigmoid(c)).astype(BF16)
    o_ref[...] = _dot(s, w_ref[...].astype(BF16)) + b_ref[...]


def ada_mod(c, w_ada, b_ada, tn=1024):
    nb, d = c.shape
    n = w_ada.shape[1]
    return pl.pallas_call(
        _ada_kernel,
        grid=(n // tn,),
        in_specs=[pl.BlockSpec((nb, d), lambda j: (0, 0)),
                  pl.BlockSpec((d, tn), lambda j: (0, j)),
                  pl.BlockSpec((1, tn), lambda j: (0, j))],
        out_specs=pl.BlockSpec((nb, tn), lambda j: (0, j)),
        out_shape=jax.ShapeDtypeStruct((nb, n), F32),
        compiler_params=_params(("arbitrary",)),
        name="ada_mod",
    )(c, w_ada, b_ada.reshape(1, n))


def _inproj_kernel(x_ref, sh_ref, sc_ref, n1_ref, w_ref, cos_ref, sin_ref,
                   ka_ref, va_ref, qab_ref, kab_ref, vab_ref, qrb_ref, kr_ref, vrb_ref,
                   gr_ref, ga_ref, gb_ref, vt_ref=None):
    x = x_ref[...]
    h = _rms_rows(x) * n1_ref[...] * (1.0 + sc_ref[...]) + sh_ref[...]
    hb = h.astype(BF16)

    d_model = x.shape[1]
    sizes = (WA, WA, WA, WR, WR, WR, WR, d_model, d_model)
    starts = [sum(sizes[:g]) for g in range(len(sizes))]

    def proj(g):
        return _dot(hb, w_ref[:, starts[g]:starts[g] + sizes[g]])

    qa = proj(0)
    qab_ref[...] = (qa * (DA ** -0.5 * LOG2E)).astype(BF16)
    ka = proj(1)
    va = proj(2)
    for hh in range(HA):
        ka_ref[:, hh, :] = ka[:, hh * DVA:(hh + 1) * DVA]
        va_ref[:, hh, :] = va[:, hh * DVA:(hh + 1) * DVA]
    kab_ref[...] = ka.astype(BF16)
    vab_ref[...] = va.astype(BF16)
    if vt_ref is not None:
        vt_ref[...] = va.T.astype(BF16)
    cos2 = cos_ref[...]
    sin2 = sin_ref[...]

    def rot(z):
        parts = []
        for hh in range(HR):
            zh = z[:, hh * DKR:(hh + 1) * DKR]
            parts.append(zh * cos2 + pltpu.roll(zh, DKR // 2, 1) * sin2)
        return jnp.concatenate(parts, axis=-1)

    qrb_ref[...] = rot(proj(3)).astype(BF16)
    kr_ref[...] = rot(proj(4)) * (DKR ** -0.5)
    vrb_ref[...] = proj(5).astype(BF16)
    gr_ref[...] = proj(6)
    ga_ref[...] = proj(7)
    gb_ref[...] = proj(8)


def in_proj(x, sh1, sc1, norm1, w_in_b, cos2, sin2, tm, emit_vt):
    B, L, D = x.shape
    d_in = w_in_b.shape[1]
    row = lambda n: pl.BlockSpec((None, tm, n), lambda b, i: (b, i, 0))
    mod = pl.BlockSpec((None, 1, D), lambda b, i: (b, 0, 0))
    f = lambda n, dt: jax.ShapeDtypeStruct((B, L, n), dt)
    cache = pl.BlockSpec((None, tm, HA, DVA), lambda b, i: (b, i, 0, 0))
    cache_shape = jax.ShapeDtypeStruct((B, L, HA, DVA), F32)
    vt_spec = [pl.BlockSpec((None, WA, tm), lambda b, i: (b, 0, i))] if emit_vt else []
    vt_shape = [jax.ShapeDtypeStruct((B, WA, L), BF16)] if emit_vt else []
    return pl.pallas_call(
        _inproj_kernel,
        grid=(B, L // tm),
        in_specs=[row(D), mod, mod,
                  pl.BlockSpec((1, D), lambda b, i: (0, 0)),
                  pl.BlockSpec((D, d_in), lambda b, i: (0, 0), pipeline_mode=pl.Buffered(1)),
                  pl.BlockSpec((tm, DKR), lambda b, i: (i, 0)),
                  pl.BlockSpec((tm, DKR), lambda b, i: (i, 0))],
        out_specs=[cache, cache, row(WA), row(WA), row(WA), row(WR), row(WR), row(WR),
                   row(WR), row(D), row(D)] + vt_spec,
        out_shape=[cache_shape, cache_shape, f(WA, BF16), f(WA, BF16), f(WA, BF16),
                   f(WR, BF16), f(WR, F32), f(WR, BF16), f(WR, F32), f(D, F32), f(D, F32)]
        + vt_shape,
        compiler_params=_params(("arbitrary", "arbitrary")),
        name="in_proj",
    )(x, sh1.reshape(B, 1, D), sc1.reshape(B, 1, D), norm1.reshape(1, D), w_in_b, cos2, sin2)


def _t5_bucket(rel):
    nb = N_BUCKETS // 2
    ret = jnp.where(rel > 0, nb, 0)
    n = jnp.abs(rel)
    max_exact = nb // 2
    nf = jnp.maximum(n, max_exact).astype(F32)
    large = max_exact + (jnp.log(nf / max_exact) / math.log(MAX_DIST / max_exact)
                         * (nb - max_exact)).astype(jnp.int32)
    large = jnp.minimum(large, nb - 1)
    return ret + jnp.where(n < max_exact, n, large)


def _bias_from_buckets(bkt, rb_ref, h, shift):
    val = jnp.where(bkt < 0, NEG, 0.0).astype(F32)
    for n in range(N_BUCKETS):
        val = jnp.where(bkt == n, (rb_ref[n, h] - shift) * LOG2E, val)
    return val


def _lam_value(lv_ref, lam_init):
    lv = lv_ref[...]
    a = jnp.sum(lv[0:1] * lv[1:2], axis=-1, keepdims=True)
    b = jnp.sum(lv[2:3] * lv[3:4], axis=-1, keepdims=True)
    return jnp.exp(a) - jnp.exp(b) + lam_init


def _block_diag_q(q):
    lane = lax.broadcasted_iota(jnp.int32, q.shape, 1)
    zero = jnp.zeros_like(q)
    return jnp.concatenate([jnp.where(lane < DA, q, zero), jnp.where(lane >= DA, q, zero)], axis=0)


def _attn_kernel(rb_ref, lv_ref, q_ref, k_ref, vt_ref, bkt_ref, o_ref,
                 bias_sc, m_sc, acc_sc, va_sc, *, tq, nsub, lam_init):
    h = pl.program_id(1)
    i = pl.program_id(2)

    @pl.when(i == 0)
    def _():
        far = rb_ref[N_BUCKETS // 2 - 1, h]
        for t in range(2):
            bias_sc[t] = _bias_from_buckets(bkt_ref[t], rb_ref, h, far)
        va_sc[0:DVA, :] = vt_ref[...]
        va_sc[DVA:, :] = jnp.ones((BF16_ROWS, va_sc.shape[1]), BF16)

    qt = q_ref[...].astype(F32).T
    drow = lax.broadcasted_iota(jnp.int32, qt.shape, 0)
    qbd_t = jnp.concatenate([jnp.where(drow < DA, qt, 0.0), jnp.where(drow >= DA, qt, 0.0)],
                            axis=1).astype(BF16)
    m_sc[...] = jnp.full(m_sc.shape, NEG, F32)
    acc_sc[...] = jnp.zeros(acc_sc.shape, F32)

    def scores(off, tk, bias=None):
        s = _dot(k_ref[pl.ds(off, tk), :], qbd_t)
        if bias is not None:
            s = s + jnp.concatenate([bias, bias], axis=1)
        return s

    def absorb(s, off, tk):
        m_prev = m_sc[...]
        m_new = jnp.maximum(m_prev, jnp.max(s, axis=0, keepdims=True))
        alpha = jnp.exp2(m_prev - m_new)
        p = jnp.exp2(s - m_new).astype(BF16)
        acc_sc[...] = alpha * acc_sc[...] + _dot(va_sc[:, pl.ds(off, tk)], p)
        m_sc[...] = m_new

    def tiles(specs):
        ss = [scores(*sp) for sp in specs]
        for s, sp in zip(ss, specs):
            absorb(s, sp[0], sp[1])

    ntile = jnp.maximum(i - 1, 0)
    nfull = ntile // nsub

    def far(first_tile, n):
        return [(pl.multiple_of((first_tile + j) * tq, tq), tq) for j in range(n)]

    def far_body(t, carry):
        tiles(far(t * nsub, nsub))
        return carry

    lax.fori_loop(0, nfull, far_body, 0)
    done = nfull * nsub
    n = nsub // 2
    while n >= ATTN_TAIL:
        take = ((ntile - done) // n) > 0

        @pl.when(take)
        def _(done=done, n=n):
            tiles(far(done, n))

        done = done + jnp.where(take, n, 0)
        n //= 2

    rest = ntile - done
    for r in range(ATTN_TAIL):
        @pl.when((i >= 1) & (rest == r))
        def _(r=r):
            off = pl.multiple_of((i - 1) * tq, tq)
            tiles(far(done, r) + [(off, tq, bias_sc[1]),
                                  (pl.multiple_of(off + tq, tq), tq, bias_sc[0])])

    @pl.when(i == 0)
    def _():
        tiles([(0, tq, bias_sc[0])])

    lam = _lam_value(lv_ref, lam_init)
    o = acc_sc[0:DVA, :] / acc_sc[DVA:DVA + 1, :]
    o_ref[...] = (o[:, 0:tq] - lam * o[:, tq:2 * tq]).T


def attn_prompt(qab, kab, vt, rel_bias, lamv, lam_init, tq=ATTN_TQ, nsub=ATTN_GROUP):
    B, S, _ = qab.shape
    assert S % tq == 0
    c = jnp.arange(tq, dtype=jnp.int32)[:, None]
    r = jnp.arange(tq, dtype=jnp.int32)[None, :]
    diag = jnp.where((c // CHUNK) <= (r // CHUNK), _t5_bucket(c - r), -1)
    prev = _t5_bucket(c - r - tq)
    bkt = jnp.stack([diag, prev]).astype(jnp.int32)
    smem = pl.BlockSpec(memory_space=pltpu.SMEM)
    return pl.pallas_call(
        functools.partial(_attn_kernel, tq=tq, nsub=nsub, lam_init=lam_init),
        grid=(B, HA, S // tq),
        in_specs=[smem,
                  pl.BlockSpec((4, DA), lambda b, h, i: (0, 0)),
                  pl.BlockSpec((None, tq, DVA), lambda b, h, i: (b, i, h)),
                  pl.BlockSpec((None, S, DVA), lambda b, h, i: (b, 0, h)),
                  pl.BlockSpec((None, DVA, S), lambda b, h, i: (b, h, 0)),
                  pl.BlockSpec((2, tq, tq), lambda b, h, i: (0, 0, 0))],
        out_specs=pl.BlockSpec((None, tq, DVA), lambda b, h, i: (b, i, h)),
        out_shape=jax.ShapeDtypeStruct((B, S, WA), F32),
        scratch_shapes=[pltpu.VMEM((2, tq, tq), F32),
                        pltpu.VMEM((1, 2 * tq), F32),
                        pltpu.VMEM((DVA + BF16_ROWS, 2 * tq), F32),
                        pltpu.VMEM((DVA + BF16_ROWS, S), BF16)],
        compiler_params=_params(("arbitrary", "arbitrary", "arbitrary")),
        name="attn_prompt",
    )(rel_bias, lamv, qab, kab, vt, bkt)


def _attn_small_kernel(rb_ref, lv_ref, q_ref, k_ref, v_ref, bkt_ref, o_ref, *, lq, lam_init):
    h = pl.program_id(1)
    bias = _bias_from_buckets(bkt_ref[...], rb_ref, h, 0.0)
    qbd = _block_diag_q(q_ref[...])
    s = _dot_nt(qbd, k_ref[...]) + jnp.concatenate([bias, bias], axis=0)
    m = jnp.max(s, axis=-1, keepdims=True)
    p = jnp.exp2(s - m)
    l = jnp.sum(p, axis=-1, keepdims=True)
    o = _dot(p.astype(BF16), v_ref[...]) / l
    lam = _lam_value(lv_ref, lam_init)
    o_ref[...] = o[0:lq] - lam * o[lq:2 * lq]


def attn_small(qab, k_all, v_all, q_pos, k_pos, k_valid, rel_bias, lamv, lam_init):
    B, Lq, _ = qab.shape
    Lk = k_all.shape[1]
    visible = ((k_pos[None, :] // CHUNK) <= (q_pos[:, None] // CHUNK)) & k_valid[None, :]
    bkt = jnp.where(visible, _t5_bucket(k_pos[None, :] - q_pos[:, None]), -1).astype(jnp.int32)
    smem = pl.BlockSpec(memory_space=pltpu.SMEM)
    return pl.pallas_call(
        functools.partial(_attn_small_kernel, lq=Lq, lam_init=lam_init),
        grid=(B, HA),
        in_specs=[smem,
                  pl.BlockSpec((4, DA), lambda b, h: (0, 0)),
                  pl.BlockSpec((None, Lq, DVA), lambda b, h: (b, 0, h)),
                  pl.BlockSpec((None, Lk, DVA), lambda b, h: (b, 0, h)),
                  pl.BlockSpec((None, Lk, DVA), lambda b, h: (b, 0, h)),
                  pl.BlockSpec((Lq, Lk), lambda b, h: (0, 0))],
        out_specs=pl.BlockSpec((None, Lq, DVA), lambda b, h: (b, 0, h)),
        out_shape=jax.ShapeDtypeStruct((B, Lq, WA), F32),
        compiler_params=_params(("arbitrary", "arbitrary")),
        name="attn_sample",
    )(rel_bias, lamv, qab, k_all, v_all, bkt)


def _ret_kernel(lg_ref, q_ref, k_ref, v_ref, s0_ref, o_ref, so_ref, state_sc, decay_sc, *, C):
    c = pl.program_id(1)

    @pl.when(c == 0)
    def _():
        state_sc[...] = s0_ref[...]
        r = lax.broadcasted_iota(jnp.int32, (C, C), 0)
        cc = lax.broadcasted_iota(jnp.int32, (C, C), 1)
        diff = (r - cc).astype(F32)
        for h in range(HR):
            decay_sc[h] = jnp.where(diff >= 0, jnp.exp(jnp.maximum(diff, 0.0) * lg_ref[h]), 0.0)

    n = lax.broadcasted_iota(jnp.int32, (C, 1), 0).astype(F32)
    for h in range(HR):
        lg = lg_ref[h]
        cols = slice(h * DKR, (h + 1) * DKR)
        xi = jnp.exp((n + 1.0) * lg)
        zeta = jnp.exp((C - 1.0 - n) * lg)
        q = q_ref[:, cols]
        k = k_ref[:, cols]
        v = v_ref[:, cols]
        state = state_sc[h]
        scores = _dot_nt(q, k.astype(BF16)) * decay_sc[h]
        intra = _dot(scores.astype(BF16), v)
        cross = _dot(q, state.astype(BF16)) * xi
        o_ref[:, cols] = intra + cross
        kz = (k * zeta).astype(BF16)
        state_sc[h] = jnp.exp(C * lg) * state + _dot_tn(kz, v)

    @pl.when(c == pl.num_programs(1) - 1)
    def _():
        so_ref[...] = state_sc[...]


def retention(qrb, kr, vrb, state0, C):
    B, L, _ = qrb.shape
    lg = jnp.log(1.0 - 2.0 ** (-5.0 - jnp.arange(HR, dtype=F32)))
    blk = pl.BlockSpec((None, C, WR), lambda b, c: (b, c, 0))
    st = pl.BlockSpec((None, HR, DKR, DVR), lambda b, c: (b, 0, 0, 0))
    return pl.pallas_call(
        functools.partial(_ret_kernel, C=C),
        grid=(B, L // C),
        in_specs=[pl.BlockSpec(memory_space=pltpu.SMEM), blk, blk, blk, st],
        out_specs=[blk, st],
        out_shape=[jax.ShapeDtypeStruct((B, L, WR), F32),
                   jax.ShapeDtypeStruct((B, HR, DKR, DVR), F32)],
        scratch_shapes=[pltpu.VMEM((HR, DKR, DVR), F32), pltpu.VMEM((HR, C, C), F32)],
        compiler_params=_params(("arbitrary", "arbitrary")),
        name="retention",
    )(lg, qrb, kr, vrb, state0)


def _outmix_kernel(x_ref, oa_ref, or_ref, gr_ref, ga_ref, gb_ref, g1_ref, sh2_ref, sc2_ref, n2_ref,
                   sa_ref, sr_ref, wba_ref, wbr_ref, wo_ref, wpq_ref,
                   x1_ref, h2_ref, qp_ref, *, lam_init, h2_transposed):
    sa = sa_ref[...]
    sr = sr_ref[...]
    gr = gr_ref[...]
    silu_gr = gr * jax.nn.sigmoid(gr)
    ya_parts, yr_parts = [], []
    for hh in range(HA):
        sl = slice(hh * DVA, (hh + 1) * DVA)
        ya_parts.append(_rms_rows(oa_ref[:, sl]) * sa * (1.0 - lam_init))
        yr_parts.append(silu_gr[:, sl] * (_rms_rows(or_ref[:, sl]) * sr))
    ya = jnp.concatenate(ya_parts, axis=-1).astype(BF16)
    yr = jnp.concatenate(yr_parts, axis=-1).astype(BF16)
    y = (jax.nn.sigmoid(ga_ref[...]) * _dot(ya, wba_ref[...])
         + jax.nn.sigmoid(gb_ref[...]) * _dot(yr, wbr_ref[...]))
    out = _dot(y.astype(BF16), wo_ref[...])
    x1 = x_ref[...] + g1_ref[...] * out
    x1_ref[...] = x1
    h2f = _rms_rows(x1) * n2_ref[...] * (1.0 + sc2_ref[...]) + sh2_ref[...]
    h2 = h2f.astype(BF16)
    h2_ref[...] = h2f.T.astype(BF16) if h2_transposed else h2
    qp_ref[...] = _dot(h2, wpq_ref[...]).astype(BF16)


def out_mix(x, oa, orr, gr, ga, gb, g1, sh2, sc2, norm2, subln_a, subln_r,
            w_ba_b, w_br_b, w_o_b, w_pq_b, lam_init, tm):
    B, L, D = x.shape
    nq = w_pq_b.shape[1]
    row = lambda n: pl.BlockSpec((None, tm, n), lambda b, i: (b, i, 0))
    mod = pl.BlockSpec((None, 1, D), lambda b, i: (b, 0, 0))
    full = lambda a: pl.BlockSpec(a.shape, lambda b, i: (0,) * a.ndim, pipeline_mode=pl.Buffered(1))
    n2 = norm2.reshape(1, D)
    sa = subln_a.reshape(1, DVA)
    sr = subln_r.reshape(1, DVR)
    h2_transposed = tm % LANES == 0
    nt = L // tm
    if h2_transposed:
        h2_spec = pl.BlockSpec((D, tm), lambda b, i: (0, b * nt + i))
        h2_shape = jax.ShapeDtypeStruct((D, B * L), BF16)
    else:
        h2_spec = row(D)
        h2_shape = jax.ShapeDtypeStruct((B, L, D), BF16)
    x1, h2, qp = pl.pallas_call(
        functools.partial(_outmix_kernel, lam_init=lam_init, h2_transposed=h2_transposed),
        grid=(B, nt),
        in_specs=[row(D), row(WA), row(WR), row(WR), row(D), row(D), mod, mod, mod,
                  full(n2), full(sa), full(sr), full(w_ba_b), full(w_br_b), full(w_o_b), full(w_pq_b)],
        out_specs=[row(D), h2_spec, row(nq)],
        out_shape=[jax.ShapeDtypeStruct((B, L, D), F32), h2_shape,
                   jax.ShapeDtypeStruct((B, L, nq), BF16)],
        compiler_params=_params(("arbitrary", "arbitrary")),
        name="out_mix",
    )(x, oa, orr, gr, ga, gb, g1.reshape(B, 1, D), sh2.reshape(B, 1, D), sc2.reshape(B, 1, D),
      n2, sa, sr, w_ba_b, w_br_b, w_o_b, w_pq_b)
    h2t = h2 if h2_transposed else h2.reshape(B * L, D).T
    return x1, h2t, qp


UNRANKED = float(PEER_TOPK * PEER_TOPK)


def _topk_rows(s, k, break_ties, want_rank=True):
    n = s.shape[0]
    iota = lax.broadcasted_iota(jnp.int32, s.shape, 0).astype(F32)
    work = s
    rank = jnp.full(s.shape, UNRANKED, F32) if want_rank else None
    vals = []
    for r in range(k):
        m = jnp.max(work, axis=0, keepdims=True)
        sel = work == m
        if break_ties:
            idx = jnp.min(jnp.where(sel, iota, float(n)), axis=0, keepdims=True)
            sel = iota == idx
        if want_rank:
            rank = jnp.where(sel, float(r), rank)
        work = jnp.where(sel, -jnp.inf, work)
        vals.append(m)
    return vals, rank


def _count_rows(mask):
    return jnp.sum(mask.astype(F32), axis=0, keepdims=True)


_HEAD_A = SUBLANES
_CAND_NB = [PEER_TOPK] + [SUBLANES] * (_HEAD_A - 1)


def _route_chunk(s1, s2, break_ties):
    k = PEER_TOPK
    v1, rank1 = _topk_rows(s1, k, break_ties, want_rank=break_ties)
    v2, rank2 = _topk_rows(s2, k, break_ties)
    v2m = jnp.concatenate(v2, axis=0)
    v1t = jnp.concatenate(v1[_HEAD_A:], axis=0)
    blocks = [v1[a] + v2m[0:nb] for a, nb in enumerate(_CAND_NB)] + [v1t + v2[0]]
    cand = jnp.concatenate(blocks, axis=0)
    vc, crank = _topk_rows(cand, k, break_ties, want_rank=break_ties)
    if break_ties:
        sel = crank < UNRANKED
        row_is = lambda a: rank1 == float(a)
        nbad = jnp.zeros((), jnp.int32)
    else:
        sel = cand >= vc[k - 1]
        row_is = lambda a: s1 == v1[a]
        bad = ((_count_rows(s1 >= v1[k - 1]) != float(k))
               | (_count_rows(rank2 < UNRANKED) != float(k)) | (_count_rows(sel) != float(k)))
        nbad = jnp.sum(bad.astype(jnp.int32))
    cmax = v1[0] + v2[0]
    z = jnp.sum(jnp.where(sel, jnp.exp(cand - cmax), 0.0), axis=0, keepdims=True)
    self32 = sel.astype(F32)
    cidx = jnp.zeros(s1.shape, F32)
    lo = 0
    for a, nb in enumerate(_CAND_NB):
        cnt = jnp.sum(self32[lo:lo + nb], axis=0, keepdims=True)
        cidx = jnp.where(row_is(a), cnt, cidx)
        lo += nb
    for a in range(_HEAD_A, k):
        cidx = jnp.where(row_is(a), self32[lo + a - _HEAD_A:lo + a - _HEAD_A + 1], cidx)
    e1 = jnp.exp(s1 - v1[0]) / z
    e2 = jnp.exp(s2 - v2[0])
    return (e1, e2, rank2, cidx), nbad


def _route_kernel(q_ref, keys_ref, e1_ref, e2_ref, r2_ref, c_ref, s1_sc, s2_sc):
    q = q_ref[...]
    s1_sc[...] = _dot_nt(keys_ref[0], q[:, 0:KEY_DIM])
    s2_sc[...] = _dot_nt(keys_ref[1], q[:, KEY_DIM:2 * KEY_DIM])

    def store(sl, res):
        e1, e2, rank2, cidx = res
        e1_ref[:, sl] = e1
        e2_ref[:, sl] = e2.astype(BF16)
        r2_ref[:, sl] = rank2.astype(BF16)
        c_ref[:, sl] = cidx

    group = math.gcd(ROUTE_CHUNKS, s1_sc.shape[1] // LANES)

    def chunks(ci, carry):
        sls = [pl.ds(pl.multiple_of((ci * group + g) * LANES, LANES), LANES) for g in range(group)]
        fast = [_route_chunk(s1_sc[:, sl], s2_sc[:, sl], break_ties=False) for sl in sls]
        for sl, (res, _) in zip(sls, fast):
            store(sl, res)
        for sl, (_, nbad) in zip(sls, fast):
            @pl.when(nbad > 0)
            def _():
                store(sl, _route_chunk(s1_sc[:, sl], s2_sc[:, sl], break_ties=True)[0])
        return carry

    lax.fori_loop(0, s1_sc.shape[1] // (group * LANES), chunks, 0)


def peer_route(qp, keys_b, tm):
    T = qp.shape[0]
    out = pl.BlockSpec((None, N_KEYS, tm), lambda t, h: (h, 0, t))
    shp = lambda dt: jax.ShapeDtypeStruct((PEER_HEADS, N_KEYS, T), dt)
    return pl.pallas_call(
        _route_kernel,
        grid=(T // tm, PEER_HEADS),
        in_specs=[pl.BlockSpec((tm, 2 * KEY_DIM), lambda t, h: (t, h)),
                  pl.BlockSpec((2, N_KEYS, KEY_DIM), lambda t, h: (h, 0, 0))],
        out_specs=[out, out, out, out],
        out_shape=[shp(F32), shp(BF16), shp(BF16), shp(F32)],
        scratch_shapes=[pltpu.VMEM((N_KEYS, tm), F32), pltpu.VMEM((N_KEYS, tm), F32)],
        compiler_params=_params(("arbitrary", "arbitrary")),
        name="peer_route",
    )(qp, keys_b)


def _bcast_rows_bf16(row, n):
    t = row.shape[1]
    tile = jnp.broadcast_to(row, (BF16_ROWS, t)).astype(BF16)
    return jnp.broadcast_to(tile[None], (n // BF16_ROWS, BF16_ROWS, t)).reshape(n, t)


def _peer_kernel(h2_ref, u_ref, vt_ref, e1_ref, e2_ref, r2_ref, c_ref, x1_ref, g2_ref, fn_ref,
                 y_ref, acc_sc, *, ni):
    e = pl.program_id(1)

    @pl.when(e == 0)
    def _():
        acc_sc[...] = jnp.zeros(acc_sc.shape, F32)

    h2 = h2_ref[...]
    pair = PEER_SUB * N_KEYS
    a_pairs = [_dot(u_ref[p * pair:(p + 1) * pair, :], h2) for p in range(ni // PEER_SUB)]
    for p in range(ni // PEER_SUB):
        gs = [jnp.zeros((N_KEYS, h2.shape[1]), BF16) for _ in range(PEER_SUB)]
        for hh in range(PEER_HEADS):
            r2h = r2_ref[hh]
            e2h = e2_ref[hh]
            for jj in range(PEER_SUB):
                i = e * ni + PEER_SUB * p + jj
                c_row = _bcast_rows_bf16(c_ref[hh, pl.ds(i, 1), :], N_KEYS)
                e1_row = _bcast_rows_bf16(e1_ref[hh, pl.ds(i, 1), :], N_KEYS)
                gs[jj] = gs[jj] + jnp.where(r2h < c_row, e2h * e1_row, jnp.zeros_like(e2h))
        ws = []
        for jj in range(PEER_SUB):
            a = a_pairs[p][jj * N_KEYS:(jj + 1) * N_KEYS]
            act = 0.5 * a * (1.0 + lax.erf(a * (2.0 ** -0.5)))
            ws.append(gs[jj] * act.astype(BF16))
        w = jnp.concatenate(ws, axis=0)
        acc_sc[...] += _dot(vt_ref[:, p * pair:(p + 1) * pair], w)

    @pl.when(e == pl.num_programs(1) - 1)
    def _():
        x2 = x1_ref[...] + g2_ref[...] * acc_sc[...].T
        y_ref[...] = _rms_rows(x2) * fn_ref[...]


def peer_experts(h2t, u_b, vt_b, e1, e2, r2, cc, x1, g2, final_norm, tm, ni=EXPERT_ROWS):
    D, T = h2t.shape
    nb = ni * N_KEYS
    tok = pl.BlockSpec((tm, D), lambda t, e: (t, 0))
    gate = pl.BlockSpec((PEER_HEADS, N_KEYS, tm), lambda t, e: (0, 0, t))
    L = T // g2.shape[0]
    if L % tm == 0:
        g2_arg = g2.reshape(g2.shape[0], 1, D)
        g2_spec = pl.BlockSpec((None, 1, D), lambda t, e: (t // (L // tm), 0, 0))
    else:
        g2_arg = jnp.repeat(g2, L, axis=0)
        g2_spec = tok
    return pl.pallas_call(
        functools.partial(_peer_kernel, ni=ni),
        grid=(T // tm, u_b.shape[0] // nb),
        in_specs=[pl.BlockSpec((D, tm), lambda t, e: (0, t)),
                  pl.BlockSpec((nb, D), lambda t, e: (e, 0)),
                  pl.BlockSpec((D, nb), lambda t, e: (0, e)),
                  gate, gate, gate, gate, tok, g2_spec,
                  pl.BlockSpec((1, D), lambda t, e: (0, 0))],
        out_specs=tok,
        out_shape=jax.ShapeDtypeStruct((T, D), F32),
        scratch_shapes=[pltpu.VMEM((D, tm), F32)],
        compiler_params=_params(("arbitrary", "arbitrary")),
        name="peer_experts",
    )(h2t, u_b, vt_b, e1, e2, r2, cc, x1, g2_arg, final_norm.reshape(1, D))


def _rot_tables(pos):
    inv = 1.0 / (10000.0 ** jnp.linspace(0.0, 1.0, DKR // 2, dtype=F32))
    ang = pos[:, None].astype(F32) * inv[None, :]
    cos, sin = jnp.cos(ang), jnp.sin(ang)
    return jnp.concatenate([cos, cos], axis=-1), jnp.concatenate([-sin, sin], axis=-1)


def _pick_tile(n, pref):
    t = min(n, pref)
    assert n % t == 0, (n, t)
    return t


def _trunk(x, mods, pos, lam_init, w, attend, values_transposed, state0, final_norm):
    (norm1, norm2, w_in_b, subln_a, subln_r, w_ba_b, w_br_b, w_o_b, w_pq_b, keys_b, u_b, vt_b) = w
    sh1, sc1, g1, sh2, sc2, g2 = mods
    B, L, D = x.shape
    T = B * L
    cos2, sin2 = _rot_tables(pos)
    tm = _pick_tile(L, TOKEN_TILE)
    ka, va, qab, kab, vab, qrb, kr, vrb, gr, ga, gb, *vt = in_proj(
        x, sh1, sc1, norm1, w_in_b, cos2, sin2, tm, emit_vt=values_transposed)
    oa = attend(qab, kab, vt[0] if values_transposed else vab)
    orr, st = retention(qrb, kr, vrb, state0, _pick_tile(L, RET_CHUNK))
    x1, h2t, qp = out_mix(x, oa, orr, gr, ga, gb, g1, sh2, sc2, norm2, subln_a, subln_r,
                          w_ba_b, w_br_b, w_o_b, w_pq_b, lam_init, tm)
    e1, e2, r2, cc = peer_route(qp.reshape(T, -1), keys_b, _pick_tile(T, ROUTE_TILE))
    y = peer_experts(h2t, u_b, vt_b, e1, e2, r2, cc, x1.reshape(T, D), g2,
                     final_norm, _pick_tile(T, EXPERT_TILE))
    return y.reshape(B, L, D), ka, va, st


def kernel(x_prompt, x_sample, cache_k, cache_v, state_ret, c_prompt, c_sample, w_ada, b_ada, norm1,
           norm2, w_in, lam_q1, lam_k1, lam_q2, lam_k2, subln_a, subln_r, w_ba, w_br, w_o, rel_bias,
           w_pq, peer_keys, peer_u, peer_v, final_norm):
    depth = w_ada.shape[0]
    assert depth == 1, "the fused final norm assumes a single layer"
    Bp, Lp, D = x_prompt.shape
    Bs, Ls, _ = x_sample.shape
    past = cache_k.shape[2]
    pos_p = jnp.arange(Lp, dtype=jnp.int32)
    pos_s = past + jnp.arange(Ls, dtype=jnp.int32)
    lk = past + Ls
    lk_pad = -(-lk // LANES) * LANES
    k_pos_s = jnp.arange(lk_pad, dtype=jnp.int32)
    k_valid_s = k_pos_s < lk

    l = 0
    lam_init = 0.8 - 0.6 * math.exp(-0.3 * l)
    lamv = jnp.stack([lam_q1[l], lam_k1[l], lam_q2[l], lam_k2[l]]).astype(F32)
    mod = ada_mod(jnp.concatenate([c_prompt, c_sample], axis=0), w_ada[l], b_ada[l])
    mods = jnp.split(mod, 6, axis=-1)
    mods_p = [m[:Bp] for m in mods]
    mods_s = [m[Bp:] for m in mods]
    w = (norm1[l], norm2[l], w_in[l].astype(BF16), subln_a[l], subln_r[l], w_ba[l].astype(BF16),
         w_br[l].astype(BF16), w_o[l].astype(BF16), w_pq[l].astype(BF16),
         peer_keys[l].reshape(PEER_HEADS * 2, N_KEYS, KEY_DIM).astype(BF16),
         peer_u[l].astype(BF16), peer_v[l].T.astype(BF16))

    attend_p = lambda q, k, v: attn_prompt(q, k, v, rel_bias, lamv, lam_init)
    zero_state = jnp.zeros((Bp, HR, DKR, DVR), F32)
    yp, kp, vp, sp = _trunk(x_prompt, mods_p, pos_p, lam_init, w, attend_p, True, zero_state,
                            final_norm)

    def attend_s(q, k, v):
        padk = jnp.zeros((Bs, lk_pad - lk, WA), BF16)
        kc = cache_k[l].reshape(Bs, past, HA * 2 * DA).astype(BF16)
        vc = cache_v[l].reshape(Bs, past, WA).astype(BF16)
        k_all = jnp.concatenate([kc, k, padk], axis=1)
        v_all = jnp.concatenate([vc, v, padk], axis=1)
        return attn_small(q, k_all, v_all, pos_s, k_pos_s, k_valid_s, rel_bias, lamv, lam_init)

    ys, ks, vs, ss = _trunk(x_sample, mods_s, pos_s, lam_init, w, attend_s, False,
                            state_ret[l].astype(F32), final_norm)
    return (yp, ys, kp[None], vp[None], sp[None], ks[None], vs[None], ss[None])
```

```python
import functools
import math

import jax
import jax.numpy as jnp
from jax import lax
from jax.experimental import pallas as pl
from jax.experimental.pallas import tpu as pltpu

F32 = jnp.float32
BF16 = jnp.bfloat16

CHUNK = 64
HA = 4
DA = 64
DVA = 2 * DA
HR = 4
DKR = 128
DVR = 128
N_BUCKETS = 32
MAX_DIST = 128
PEER_HEADS = 8
N_KEYS = 128
KEY_DIM = 128
PEER_TOPK = 16
EPS = 1e-6
WA = HA * DVA
WR = HR * DVR
NEG = -1e30
LOG2E = math.log2(math.e)
LANES = 128
SUBLANES = 8
BF16_ROWS = 2 * SUBLANES
VMEM_LIMIT = 60 * 1024 * 1024

TOKEN_TILE = 512
RET_CHUNK = 256
ATTN_TQ = 256
ATTN_GROUP = 32
ATTN_TAIL = 4
ROUTE_TILE = 1024
ROUTE_CHUNKS = 8
EXPERT_TILE = 512
EXPERT_ROWS = 32
PEER_SUB = 2


def _params(sem, vmem=VMEM_LIMIT):
    return pltpu.CompilerParams(dimension_semantics=sem, vmem_limit_bytes=vmem)


def _dot(a, b):
    return jnp.dot(a, b, preferred_element_type=F32)


def _dot_nt(a, b):
    return lax.dot_general(a, b, (((1,), (1,)), ((), ())), preferred_element_type=F32)


def _dot_tn(a, b):
    return lax.dot_general(a, b, (((0,), (0,)), ((), ())), preferred_element_type=F32)


def _rms_rows(x):
    return x * lax.rsqrt(jnp.mean(x * x, axis=-1, keepdims=True) + EPS)


def _ada_kernel(c_ref, w_ref, b_ref, o_ref):
    c = c_ref[...]
    s = (c * jax.nn.sigmoid(c)).astype(BF16)
    o_ref[...] = _dot(s, w_ref[...].astype(BF16)) + b_ref[...]


def ada_mod(c, w_ada, b_ada, tn=1024):
    nb, d = c.shape
    n = w_ada.shape[1]
    return pl.pallas_call(
        _ada_kernel,
        grid=(n // tn,),
        in_specs=[pl.BlockSpec((nb, d), lambda j: (0, 0)),
                  pl.BlockSpec((d, tn), lambda j: (0, j)),
                  pl.BlockSpec((1, tn), lambda j: (0, j))],
        out_specs=pl.BlockSpec((nb, tn), lambda j: (0, j)),
        out_shape=jax.ShapeDtypeStruct((nb, n), F32),
        compiler_params=_params(("arbitrary",)),
        name="ada_mod",
    )(c, w_ada, b_ada.reshape(1, n))


def _inproj_kernel(x_ref, sh_ref, sc_ref, n1_ref, w_ref, cos_ref, sin_ref,
                   ka_ref, va_ref, qab_ref, kab_ref, vab_ref, qrb_ref, kr_ref, vrb_ref,
                   gr_ref, ga_ref, gb_ref, vt_ref=None):
    x = x_ref[...]
    h = _rms_rows(x) * n1_ref[...] * (1.0 + sc_ref[...]) + sh_ref[...]
    hb = h.astype(BF16)

    d_model = x.shape[1]
    sizes = (WA, WA, WA, WR, WR, WR, WR, d_model, d_model)
    starts = [sum(sizes[:g]) for g in range(len(sizes))]

    def proj(g):
        return _dot(hb, w_ref[:, starts[g]:starts[g] + sizes[g]])

    qa = proj(0)
    qab_ref[...] = (qa * (DA ** -0.5 * LOG2E)).astype(BF16)
    ka = proj(1)
    va = proj(2)
    for hh in range(HA):
        ka_ref[:, hh, :] = ka[:, hh * DVA:(hh + 1) * DVA]
        va_ref[:, hh, :] = va[:, hh * DVA:(hh + 1) * DVA]
    kab_ref[...] = ka.astype(BF16)
    vab_ref[...] = va.astype(BF16)
    if vt_ref is not None:
        vt_ref[...] = va.T.astype(BF16)
    cos2 = cos_ref[...]
    sin2 = sin_ref[...]

    def rot(z):
        parts = []
        for hh in range(HR):
            zh = z[:, hh * DKR:(hh + 1) * DKR]
            parts.append(zh * cos2 + pltpu.roll(zh, DKR // 2, 1) * sin2)
        return jnp.concatenate(parts, axis=-1)

    qrb_ref[...] = rot(proj(3)).astype(BF16)
    kr_ref[...] = rot(proj(4)) * (DKR ** -0.5)
    vrb_ref[...] = proj(5).astype(BF16)
    gr_ref[...] = proj(6)
    ga_ref[...] = proj(7)
    gb_ref[...] = proj(8)


def in_proj(x, sh1, sc1, norm1, w_in_b, cos2, sin2, tm, emit_vt):
    B, L, D = x.shape
    d_in = w_in_b.shape[1]
    row = lambda n: pl.BlockSpec((None, tm, n), lambda b, i: (b, i, 0))
    mod = pl.BlockSpec((None, 1, D), lambda b, i: (b, 0, 0))
    f = lambda n, dt: jax.ShapeDtypeStruct((B, L, n), dt)
    cache = pl.BlockSpec((None, tm, HA, DVA), lambda b, i: (b, i, 0, 0))
    cache_shape = jax.ShapeDtypeStruct((B, L, HA, DVA), F32)
    vt_spec = [pl.BlockSpec((None, WA, tm), lambda b, i: (b, 0, i))] if emit_vt else []
    vt_shape = [jax.ShapeDtypeStruct((B, WA, L), BF16)] if emit_vt else []
    return pl.pallas_call(
        _inproj_kernel,
        grid=(B, L // tm),
        in_specs=[row(D), mod, mod,
                  pl.BlockSpec((1, D), lambda b, i: (0, 0)),
                  pl.BlockSpec((D, d_in), lambda b, i: (0, 0), pipeline_mode=pl.Buffered(1)),
                  pl.BlockSpec((tm, DKR), lambda b, i: (i, 0)),
                  pl.BlockSpec((tm, DKR), lambda b, i: (i, 0))],
        out_specs=[cache, cache, row(WA), row(WA), row(WA), row(WR), row(WR), row(WR),
                   row(WR), row(D), row(D)] + vt_spec,
        out_shape=[cache_shape, cache_shape, f(WA, BF16), f(WA, BF16), f(WA, BF16),
                   f(WR, BF16), f(WR, F32), f(WR, BF16), f(WR, F32), f(D, F32), f(D, F32)]
        + vt_shape,
        compiler_params=_params(("arbitrary", "arbitrary")),
        name="in_proj",
    )(x, sh1.reshape(B, 1, D), sc1.reshape(B, 1, D), norm1.reshape(1, D), w_in_b, cos2, sin2)


def _t5_bucket(rel):
    nb = N_BUCKETS // 2
    ret = jnp.where(rel > 0, nb, 0)
    n = jnp.abs(rel)
    max_exact = nb // 2
    nf = jnp.maximum(n, max_exact).astype(F32)
    large = max_exact + (jnp.log(nf / max_exact) / math.log(MAX_DIST / max_exact)
                         * (nb - max_exact)).astype(jnp.int32)
    large = jnp.minimum(large, nb - 1)
    return ret + jnp.where(n < max_exact, n, large)


def _bias_from_buckets(bkt, rb_ref, h, shift):
    val = jnp.where(bkt < 0, NEG, 0.0).astype(F32)
    for n in range(N_BUCKETS):
        val = jnp.where(bkt == n, (rb_ref[n, h] - shift) * LOG2E, val)
    return val


def _lam_value(lv_ref, lam_init):
    lv = lv_ref[...]
    a = jnp.sum(lv[0:1] * lv[1:2], axis=-1, keepdims=True)
    b = jnp.sum(lv[2:3] * lv[3:4], axis=-1, keepdims=True)
    return jnp.exp(a) - jnp.exp(b) + lam_init


def _block_diag_q(q):
    lane = lax.broadcasted_iota(jnp.int32, q.shape, 1)
    zero = jnp.zeros_like(q)
    return jnp.concatenate([jnp.where(lane < DA, q, zero), jnp.where(lane >= DA, q, zero)], axis=0)


def _attn_kernel(rb_ref, lv_ref, q_ref, k_ref, vt_ref, bkt_ref, o_ref,
                 bias_sc, m_sc, acc_sc, va_sc, *, tq, nsub, lam_init):
    h = pl.program_id(1)
    i = pl.program_id(2)

    @pl.when(i == 0)
    def _():
        far = rb_ref[N_BUCKETS // 2 - 1, h]
        for t in range(2):
            bias_sc[t] = _bias_from_buckets(bkt_ref[t], rb_ref, h, far)
        va_sc[0:DVA, :] = vt_ref[...]
        va_sc[DVA:, :] = jnp.ones((BF16_ROWS, va_sc.shape[1]), BF16)

    qt = q_ref[...].astype(F32).T
    drow = lax.broadcasted_iota(jnp.int32, qt.shape, 0)
    qbd_t = jnp.concatenate([jnp.where(drow < DA, qt, 0.0), jnp.where(drow >= DA, qt, 0.0)],
                            axis=1).astype(BF16)
    m_sc[...] = jnp.full(m_sc.shape, NEG, F32)
    acc_sc[...] = jnp.zeros(acc_sc.shape, F32)

    def scores(off, tk, bias=None):
        s = _dot(k_ref[pl.ds(off, tk), :], qbd_t)
        if bias is not None:
            s = s + jnp.concatenate([bias, bias], axis=1)
        return s

    def absorb(s, off, tk):
        m_prev = m_sc[...]
        m_new = jnp.maximum(m_prev, jnp.max(s, axis=0, keepdims=True))
        alpha = jnp.exp2(m_prev - m_new)
        p = jnp.exp2(s - m_new).astype(BF16)
        acc_sc[...] = alpha * acc_sc[...] + _dot(va_sc[:, pl.ds(off, tk)], p)
        m_sc[...] = m_new

    def tiles(specs):
        ss = [scores(*sp) for sp in specs]
        for s, sp in zip(ss, specs):
            absorb(s, sp[0], sp[1])

    ntile = jnp.maximum(i - 1, 0)
    nfull = ntile // nsub

    def far(first_tile, n):
        return [(pl.multiple_of((first_tile + j) * tq, tq), tq) for j in range(n)]

    def far_body(t, carry):
        tiles(far(t * nsub, nsub))
        return carry

    lax.fori_loop(0, nfull, far_body, 0)
    done = nfull * nsub
    n = nsub // 2
    while n >= ATTN_TAIL:
        take = ((ntile - done) // n) > 0

        @pl.when(take)
        def _(done=done, n=n):
            tiles(far(done, n))

        done = done + jnp.where(take, n, 0)
        n //= 2

    rest = ntile - done
    for r in range(ATTN_TAIL):
        @pl.when((i >= 1) & (rest == r))
        def _(r=r):
            off = pl.multiple_of((i - 1) * tq, tq)
            tiles(far(done, r) + [(off, tq, bias_sc[1]),
                                  (pl.multiple_of(off + tq, tq), tq, bias_sc[0])])

    @pl.when(i == 0)
    def _():
        tiles([(0, tq, bias_sc[0])])

    lam = _lam_value(lv_ref, lam_init)
    o = acc_sc[0:DVA, :] / acc_sc[DVA:DVA + 1, :]
    o_ref[...] = (o[:, 0:tq] - lam * o[:, tq:2 * tq]).T


def attn_prompt(qab, kab, vt, rel_bias, lamv, lam_init, tq=ATTN_TQ, nsub=ATTN_GROUP):
    B, S, _ = qab.shape
    assert S % tq == 0
    c = jnp.arange(tq, dtype=jnp.int32)[:, None]
    r = jnp.arange(tq, dtype=jnp.int32)[None, :]
    diag = jnp.where((c // CHUNK) <= (r // CHUNK), _t5_bucket(c - r), -1)
    prev = _t5_bucket(c - r - tq)
    bkt = jnp.stack([diag, prev]).astype(jnp.int32)
    smem = pl.BlockSpec(memory_space=pltpu.SMEM)
    return pl.pallas_call(
        functools.partial(_attn_kernel, tq=tq, nsub=nsub, lam_init=lam_init),
        grid=(B, HA, S // tq),
        in_specs=[smem,
                  pl.BlockSpec((4, DA), lambda b, h, i: (0, 0)),
                  pl.BlockSpec((None, tq, DVA), lambda b, h, i: (b, i, h)),
                  pl.BlockSpec((None, S, DVA), lambda b, h, i: (b, 0, h)),
                  pl.BlockSpec((None, DVA, S), lambda b, h, i: (b, h, 0)),
                  pl.BlockSpec((2, tq, tq), lambda b, h, i: (0, 0, 0))],
        out_specs=pl.BlockSpec((None, tq, DVA), lambda b, h, i: (b, i, h)),
        out_shape=jax.ShapeDtypeStruct((B, S, WA), F32),
        scratch_shapes=[pltpu.VMEM((2, tq, tq), F32),
                        pltpu.VMEM((1, 2 * tq), F32),
                        pltpu.VMEM((DVA + BF16_ROWS, 2 * tq), F32),
                        pltpu.VMEM((DVA + BF16_ROWS, S), BF16)],
        compiler_params=_params(("arbitrary", "arbitrary", "arbitrary")),
        name="attn_prompt",
    )(rel_bias, lamv, qab, kab, vt, bkt)


def _attn_small_kernel(rb_ref, lv_ref, q_ref, k_ref, v_ref, bkt_ref, o_ref, *, lq, lam_init):
    h = pl.program_id(1)
    bias = _bias_from_buckets(bkt_ref[...], rb_ref, h, 0.0)
    qbd = _block_diag_q(q_ref[...])
    s = _dot_nt(qbd, k_ref[...]) + jnp.concatenate([bias, bias], axis=0)
    m = jnp.max(s, axis=-1, keepdims=True)
    p = jnp.exp2(s - m)
    l = jnp.sum(p, axis=-1, keepdims=True)
    o = _dot(p.astype(BF16), v_ref[...]) / l
    lam = _lam_value(lv_ref, lam_init)
    o_ref[...] = o[0:lq] - lam * o[lq:2 * lq]


def attn_small(qab, k_all, v_all, q_pos, k_pos, k_valid, rel_bias, lamv, lam_init):
    B, Lq, _ = qab.shape
    Lk = k_all.shape[1]
    visible = ((k_pos[None, :] // CHUNK) <= (q_pos[:, None] // CHUNK)) & k_valid[None, :]
    bkt = jnp.where(visible, _t5_bucket(k_pos[None, :] - q_pos[:, None]), -1).astype(jnp.int32)
    smem = pl.BlockSpec(memory_space=pltpu.SMEM)
    return pl.pallas_call(
        functools.partial(_attn_small_kernel, lq=Lq, lam_init=lam_init),
        grid=(B, HA),
        in_specs=[smem,
                  pl.BlockSpec((4, DA), lambda b, h: (0, 0)),
                  pl.BlockSpec((None, Lq, DVA), lambda b, h: (b, 0, h)),
                  pl.BlockSpec((None, Lk, DVA), lambda b, h: (b, 0, h)),
                  pl.BlockSpec((None, Lk, DVA), lambda b, h: (b, 0, h)),
                  pl.BlockSpec((Lq, Lk), lambda b, h: (0, 0))],
        out_specs=pl.BlockSpec((None, Lq, DVA), lambda b, h: (b, 0, h)),
        out_shape=jax.ShapeDtypeStruct((B, Lq, WA), F32),
        compiler_params=_params(("arbitrary", "arbitrary")),
        name="attn_sample",
    )(rel_bias, lamv, qab, k_all, v_all, bkt)


def _ret_kernel(lg_ref, q_ref, k_ref, v_ref, s0_ref, o_ref, so_ref, state_sc, decay_sc, *, C):
    c = pl.program_id(1)

    @pl.when(c == 0)
    def _():
        state_sc[...] = s0_ref[...]
        r = lax.broadcasted_iota(jnp.int32, (C, C), 0)
        cc = lax.broadcasted_iota(jnp.int32, (C, C), 1)
        diff = (r - cc).astype(F32)
        for h in range(HR):
            decay_sc[h] = jnp.where(diff >= 0, jnp.exp(jnp.maximum(diff, 0.0) * lg_ref[h]), 0.0)

    n = lax.broadcasted_iota(jnp.int32, (C, 1), 0).astype(F32)
    for h in range(HR):
        lg = lg_ref[h]
        cols = slice(h * DKR, (h + 1) * DKR)
        xi = jnp.exp((n + 1.0) * lg)
        zeta = jnp.exp((C - 1.0 - n) * lg)
        q = q_ref[:, cols]
        k = k_ref[:, cols]
        v = v_ref[:, cols]
        state = state_sc[h]
        scores = _dot_nt(q, k.astype(BF16)) * decay_sc[h]
        intra = _dot(scores.astype(BF16), v)
        cross = _dot(q, state.astype(BF16)) * xi
        o_ref[:, cols] = intra + cross
        kz = (k * zeta).astype(BF16)
        state_sc[h] = jnp.exp(C * lg) * state + _dot_tn(kz, v)

    @pl.when(c == pl.num_programs(1) - 1)
    def _():
        so_ref[...] = state_sc[...]


def retention(qrb, kr, vrb, state0, C):
    B, L, _ = qrb.shape
    lg = jnp.log(1.0 - 2.0 ** (-5.0 - jnp.arange(HR, dtype=F32)))
    blk = pl.BlockSpec((None, C, WR), lambda b, c: (b, c, 0))
    st = pl.BlockSpec((None, HR, DKR, DVR), lambda b, c: (b, 0, 0, 0))
    return pl.pallas_call(
        functools.partial(_ret_kernel, C=C),
        grid=(B, L // C),
        in_specs=[pl.BlockSpec(memory_space=pltpu.SMEM), blk, blk, blk, st],
        out_specs=[blk, st],
        out_shape=[jax.ShapeDtypeStruct((B, L, WR), F32),
                   jax.ShapeDtypeStruct((B, HR, DKR, DVR), F32)],
        scratch_shapes=[pltpu.VMEM((HR, DKR, DVR), F32), pltpu.VMEM((HR, C, C), F32)],
        compiler_params=_params(("arbitrary", "arbitrary")),
        name="retention",
    )(lg, qrb, kr, vrb, state0)


def _outmix_kernel(x_ref, oa_ref, or_ref, gr_ref, ga_ref, gb_ref, g1_ref, sh2_ref, sc2_ref, n2_ref,
                   sa_ref, sr_ref, wba_ref, wbr_ref, wo_ref, wpq_ref,
                   x1_ref, h2_ref, qp_ref, *, lam_init, h2_transposed):
    sa = sa_ref[...]
    sr = sr_ref[...]
    gr = gr_ref[...]
    silu_gr = gr * jax.nn.sigmoid(gr)
    ya_parts, yr_parts = [], []
    for hh in range(HA):
        sl = slice(hh * DVA, (hh + 1) * DVA)
        ya_parts.append(_rms_rows(oa_ref[:, sl]) * sa * (1.0 - lam_init))
        yr_parts.append(silu_gr[:, sl] * (_rms_rows(or_ref[:, sl]) * sr))
    ya = jnp.concatenate(ya_parts, axis=-1).astype(BF16)
    yr = jnp.concatenate(yr_parts, axis=-1).astype(BF16)
    y = (jax.nn.sigmoid(ga_ref[...]) * _dot(ya, wba_ref[...])
         + jax.nn.sigmoid(gb_ref[...]) * _dot(yr, wbr_ref[...]))
    out = _dot(y.astype(BF16), wo_ref[...])
    x1 = x_ref[...] + g1_ref[...] * out
    x1_ref[...] = x1
    h2f = _rms_rows(x1) * n2_ref[...] * (1.0 + sc2_ref[...]) + sh2_ref[...]
    h2 = h2f.astype(BF16)
    h2_ref[...] = h2f.T.astype(BF16) if h2_transposed else h2
    qp_ref[...] = _dot(h2, wpq_ref[...]).astype(BF16)


def out_mix(x, oa, orr, gr, ga, gb, g1, sh2, sc2, norm2, subln_a, subln_r,
            w_ba_b, w_br_b, w_o_b, w_pq_b, lam_init, tm):
    B, L, D = x.shape
    nq = w_pq_b.shape[1]
    row = lambda n: pl.BlockSpec((None, tm, n), lambda b, i: (b, i, 0))
    mod = pl.BlockSpec((None, 1, D), lambda b, i: (b, 0, 0))
    full = lambda a: pl.BlockSpec(a.shape, lambda b, i: (0,) * a.ndim, pipeline_mode=pl.Buffered(1))
    n2 = norm2.reshape(1, D)
    sa = subln_a.reshape(1, DVA)
    sr = subln_r.reshape(1, DVR)
    h2_transposed = tm % LANES == 0
    nt = L // tm
    if h2_transposed:
        h2_spec = pl.BlockSpec((D, tm), lambda b, i: (0, b * nt + i))
        h2_shape = jax.ShapeDtypeStruct((D, B * L), BF16)
    else:
        h2_spec = row(D)
        h2_shape = jax.ShapeDtypeStruct((B, L, D), BF16)
    x1, h2, qp = pl.pallas_call(
        functools.partial(_outmix_kernel, lam_init=lam_init, h2_transposed=h2_transposed),
        grid=(B, nt),
        in_specs=[row(D), row(WA), row(WR), row(WR), row(D), row(D), mod, mod, mod,
                  full(n2), full(sa), full(sr), full(w_ba_b), full(w_br_b), full(w_o_b), full(w_pq_b)],
        out_specs=[row(D), h2_spec, row(nq)],
        out_shape=[jax.ShapeDtypeStruct((B, L, D), F32), h2_shape,
                   jax.ShapeDtypeStruct((B, L, nq), BF16)],
        compiler_params=_params(("arbitrary", "arbitrary")),
        name="out_mix",
    )(x, oa, orr, gr, ga, gb, g1.reshape(B, 1, D), sh2.reshape(B, 1, D), sc2.reshape(B, 1, D),
      n2, sa, sr, w_ba_b, w_br_b, w_o_b, w_pq_b)
    h2t = h2 if h2_transposed else h2.reshape(B * L, D).T
    return x1, h2t, qp


UNRANKED = float(PEER_TOPK * PEER_TOPK)


def _topk_rows(s, k, break_ties, want_rank=True):
    n = s.shape[0]
    iota = lax.broadcasted_iota(jnp.int32, s.shape, 0).astype(F32)
    work = s
    rank = jnp.full(s.shape, UNRANKED, F32) if want_rank else None
    vals = []
    for r in range(k):
        m = jnp.max(work, axis=0, keepdims=True)
        sel = work == m
        if break_ties:
            idx = jnp.min(jnp.where(sel, iota, float(n)), axis=0, keepdims=True)
            sel = iota == idx
        if want_rank:
            rank = jnp.where(sel, float(r), rank)
        work = jnp.where(sel, -jnp.inf, work)
        vals.append(m)
    return vals, rank


def _count_rows(mask):
    return jnp.sum(mask.astype(F32), axis=0, keepdims=True)


_HEAD_A = SUBLANES
_CAND_NB = [PEER_TOPK] + [SUBLANES] * (_HEAD_A - 1)


def _route_chunk(s1, s2, break_ties):
    k = PEER_TOPK
    v1, rank1 = _topk_rows(s1, k, break_ties, want_rank=break_ties)
    v2, rank2 = _topk_rows(s2, k, break_ties)
    v2m = jnp.concatenate(v2, axis=0)
    v1t = jnp.concatenate(v1[_HEAD_A:], axis=0)
    blocks = [v1[a] + v2m[0:nb] for a, nb in enumerate(_CAND_NB)] + [v1t + v2[0]]
    cand = jnp.concatenate(blocks, axis=0)
    vc, crank = _topk_rows(cand, k, break_ties, want_rank=break_ties)
    if break_ties:
        sel = crank < UNRANKED
        row_is = lambda a: rank1 == float(a)
        nbad = jnp.zeros((), jnp.int32)
    else:
        sel = cand >= vc[k - 1]
        row_is = lambda a: s1 == v1[a]
        bad = ((_count_rows(s1 >= v1[k - 1]) != float(k))
               | (_count_rows(rank2 < UNRANKED) != float(k)) | (_count_rows(sel) != float(k)))
        nbad = jnp.sum(bad.astype(jnp.int32))
    cmax = v1[0] + v2[0]
    z = jnp.sum(jnp.where(sel, jnp.exp(cand - cmax), 0.0), axis=0, keepdims=True)
    self32 = sel.astype(F32)
    cidx = jnp.zeros(s1.shape, F32)
    lo = 0
    for a, nb in enumerate(_CAND_NB):
        cnt = jnp.sum(self32[lo:lo + nb], axis=0, keepdims=True)
        cidx = jnp.where(row_is(a), cnt, cidx)
        lo += nb
    for a in range(_HEAD_A, k):
        cidx = jnp.where(row_is(a), self32[lo + a - _HEAD_A:lo + a - _HEAD_A + 1], cidx)
    e1 = jnp.exp(s1 - v1[0]) / z
    e2 = jnp.exp(s2 - v2[0])
    return (e1, e2, rank2, cidx), nbad


def _route_kernel(q_ref, keys_ref, e1_ref, e2_ref, r2_ref, c_ref, s1_sc, s2_sc):
    q = q_ref[...]
    s1_sc[...] = _dot_nt(keys_ref[0], q[:, 0:KEY_DIM])
    s2_sc[...] = _dot_nt(keys_ref[1], q[:, KEY_DIM:2 * KEY_DIM])

    def store(sl, res):
        e1, e2, rank2, cidx = res
        e1_ref[:, sl] = e1
        e2_ref[:, sl] = e2.astype(BF16)
        r2_ref[:, sl] = rank2.astype(BF16)
        c_ref[:, sl] = cidx

    group = math.gcd(ROUTE_CHUNKS, s1_sc.shape[1] // LANES)

    def chunks(ci, carry):
        sls = [pl.ds(pl.multiple_of((ci * group + g) * LANES, LANES), LANES) for g in range(group)]
        fast = [_route_chunk(s1_sc[:, sl], s2_sc[:, sl], break_ties=False) for sl in sls]
        for sl, (res, _) in zip(sls, fast):
            store(sl, res)
        for sl, (_, nbad) in zip(sls, fast):
            @pl.when(nbad > 0)
            def _():
                store(sl, _route_chunk(s1_sc[:, sl], s2_sc[:, sl], break_ties=True)[0])
        return carry

    lax.fori_loop(0, s1_sc.shape[1] // (group * LANES), chunks, 0)


def peer_route(qp, keys_b, tm):
    T = qp.shape[0]
    out = pl.BlockSpec((None, N_KEYS, tm), lambda t, h: (h, 0, t))
    shp = lambda dt: jax.ShapeDtypeStruct((PEER_HEADS, N_KEYS, T), dt)
    return pl.pallas_call(
        _route_kernel,
        grid=(T // tm, PEER_HEADS),
        in_specs=[pl.BlockSpec((tm, 2 * KEY_DIM), lambda t, h: (t, h)),
                  pl.BlockSpec((2, N_KEYS, KEY_DIM), lambda t, h: (h, 0, 0))],
        out_specs=[out, out, out, out],
        out_shape=[shp(F32), shp(BF16), shp(BF16), shp(F32)],
        scratch_shapes=[pltpu.VMEM((N_KEYS, tm), F32), pltpu.VMEM((N_KEYS, tm), F32)],
        compiler_params=_params(("arbitrary", "arbitrary")),
        name="peer_route",
    )(qp, keys_b)


def _bcast_rows_bf16(row, n):
    t = row.shape[1]
    tile = jnp.broadcast_to(row, (BF16_ROWS, t)).astype(BF16)
    return jnp.broadcast_to(tile[None], (n // BF16_ROWS, BF16_ROWS, t)).reshape(n, t)


def _peer_kernel(h2_ref, u_ref, vt_ref, e1_ref, e2_ref, r2_ref, c_ref, x1_ref, g2_ref, fn_ref,
                 y_ref, acc_sc, *, ni):
    e = pl.program_id(1)

    @pl.when(e == 0)
    def _():
        acc_sc[...] = jnp.zeros(acc_sc.shape, F32)

    h2 = h2_ref[...]
    pair = PEER_SUB * N_KEYS
    a_pairs = [_dot(u_ref[p * pair:(p + 1) * pair, :], h2) for p in range(ni // PEER_SUB)]
    for p in range(ni // PEER_SUB):
        gs = [jnp.zeros((N_KEYS, h2.shape[1]), BF16) for _ in range(PEER_SUB)]
        for hh in range(PEER_HEADS):
            r2h = r2_ref[hh]
            e2h = e2_ref[hh]
            for jj in range(PEER_SUB):
                i = PEER_SUB * p + jj
                c_row = _bcast_rows_bf16(c_ref[hh, pl.ds(i, 1), :], N_KEYS)
                e1_row = _bcast_rows_bf16(e1_ref[hh, pl.ds(i, 1), :], N_KEYS)
                gs[jj] = gs[jj] + jnp.where(r2h < c_row, e2h * e1_row, jnp.zeros_like(e2h))
        ws = []
        for jj in range(PEER_SUB):
            a = a_pairs[p][jj * N_KEYS:(jj + 1) * N_KEYS]
            act = 0.5 * a * (1.0 + lax.erf(a * (2.0 ** -0.5)))
            ws.append(gs[jj] * act.astype(BF16))
        w = jnp.concatenate(ws, axis=0)
        acc_sc[...] += _dot(vt_ref[:, p * pair:(p + 1) * pair], w)

    @pl.when(e == pl.num_programs(1) - 1)
    def _():
        x2 = x1_ref[...] + g2_ref[...] * acc_sc[...].T
        y_ref[...] = _rms_rows(x2) * fn_ref[...]


def peer_experts(h2t, u_b, vt_b, e1, e2, r2, cc, x1, g2, final_norm, tm, ni=EXPERT_ROWS):
    D, T = h2t.shape
    nb = ni * N_KEYS
    tok = pl.BlockSpec((tm, D), lambda t, e: (t, 0))
    gate = pl.BlockSpec((PEER_HEADS, N_KEYS, tm), lambda t, e: (0, 0, t))
    rows = pl.BlockSpec((PEER_HEADS, ni, tm), lambda t, e: (0, e, t))
    L = T // g2.shape[0]
    if L % tm == 0:
        g2_arg = g2.reshape(g2.shape[0], 1, D)
        g2_spec = pl.BlockSpec((None, 1, D), lambda t, e: (t // (L // tm), 0, 0))
    else:
        g2_arg = jnp.repeat(g2, L, axis=0)
        g2_spec = tok
    return pl.pallas_call(
        functools.partial(_peer_kernel, ni=ni),
        grid=(T // tm, u_b.shape[0] // nb),
        in_specs=[pl.BlockSpec((D, tm), lambda t, e: (0, t)),
                  pl.BlockSpec((nb, D), lambda t, e: (e, 0)),
                  pl.BlockSpec((D, nb), lambda t, e: (0, e)),
                  rows, gate, gate, rows, tok, g2_spec,
                  pl.BlockSpec((1, D), lambda t, e: (0, 0))],
        out_specs=tok,
        out_shape=jax.ShapeDtypeStruct((T, D), F32),
        scratch_shapes=[pltpu.VMEM((D, tm), F32)],
        compiler_params=_params(("arbitrary", "arbitrary")),
        name="peer_experts",
    )(h2t, u_b, vt_b, e1, e2, r2, cc, x1, g2_arg, final_norm.reshape(1, D))


def _rot_tables(pos):
    inv = 1.0 / (10000.0 ** jnp.linspace(0.0, 1.0, DKR // 2, dtype=F32))
    ang = pos[:, None].astype(F32) * inv[None, :]
    cos, sin = jnp.cos(ang), jnp.sin(ang)
    return jnp.concatenate([cos, cos], axis=-1), jnp.concatenate([-sin, sin], axis=-1)


def _pick_tile(n, pref):
    t = min(n, pref)
    assert n % t == 0, (n, t)
    return t


def _trunk(x, mods, pos, lam_init, w, attend, values_transposed, state0, final_norm):
    (norm1, norm2, w_in_b, subln_a, subln_r, w_ba_b, w_br_b, w_o_b, w_pq_b, keys_b, u_b, vt_b) = w
    sh1, sc1, g1, sh2, sc2, g2 = mods
    B, L, D = x.shape
    T = B * L
    cos2, sin2 = _rot_tables(pos)
    tm = _pick_tile(L, TOKEN_TILE)
    ka, va, qab, kab, vab, qrb, kr, vrb, gr, ga, gb, *vt = in_proj(
        x, sh1, sc1, norm1, w_in_b, cos2, sin2, tm, emit_vt=values_transposed)
    oa = attend(qab, kab, vt[0] if values_transposed else vab)
    orr, st = retention(qrb, kr, vrb, state0, _pick_tile(L, RET_CHUNK))
    x1, h2t, qp = out_mix(x, oa, orr, gr, ga, gb, g1, sh2, sc2, norm2, subln_a, subln_r,
                          w_ba_b, w_br_b, w_o_b, w_pq_b, lam_init, tm)
    e1, e2, r2, cc = peer_route(qp.reshape(T, -1), keys_b, _pick_tile(T, ROUTE_TILE))
    y = peer_experts(h2t, u_b, vt_b, e1, e2, r2, cc, x1.reshape(T, D), g2,
                     final_norm, _pick_tile(T, EXPERT_TILE))
    return y.reshape(B, L, D), ka, va, st


def kernel(x_prompt, x_sample, cache_k, cache_v, state_ret, c_prompt, c_sample, w_ada, b_ada, norm1,
           norm2, w_in, lam_q1, lam_k1, lam_q2, lam_k2, subln_a, subln_r, w_ba, w_br, w_o, rel_bias,
           w_pq, peer_keys, peer_u, peer_v, final_norm):
    depth = w_ada.shape[0]
    assert depth == 1, "the fused final norm assumes a single layer"
    Bp, Lp, D = x_prompt.shape
    Bs, Ls, _ = x_sample.shape
    past = cache_k.shape[2]
    pos_p = jnp.arange(Lp, dtype=jnp.int32)
    pos_s = past + jnp.arange(Ls, dtype=jnp.int32)
    lk = past + Ls
    lk_pad = -(-lk // LANES) * LANES
    k_pos_s = jnp.arange(lk_pad, dtype=jnp.int32)
    k_valid_s = k_pos_s < lk

    l = 0
    lam_init = 0.8 - 0.6 * math.exp(-0.3 * l)
    lamv = jnp.stack([lam_q1[l], lam_k1[l], lam_q2[l], lam_k2[l]]).astype(F32)
    mod = ada_mod(jnp.concatenate([c_prompt, c_sample], axis=0), w_ada[l], b_ada[l])
    mods = jnp.split(mod, 6, axis=-1)
    mods_p = [m[:Bp] for m in mods]
    mods_s = [m[Bp:] for m in mods]
    w = (norm1[l], norm2[l], w_in[l].astype(BF16), subln_a[l], subln_r[l], w_ba[l].astype(BF16),
         w_br[l].astype(BF16), w_o[l].astype(BF16), w_pq[l].astype(BF16),
         peer_keys[l].reshape(PEER_HEADS * 2, N_KEYS, KEY_DIM).astype(BF16),
         peer_u[l].astype(BF16), peer_v[l].T.astype(BF16))

    attend_p = lambda q, k, v: attn_prompt(q, k, v, rel_bias, lamv, lam_init)
    zero_state = jnp.zeros((Bp, HR, DKR, DVR), F32)
    yp, kp, vp, sp = _trunk(x_prompt, mods_p, pos_p, lam_init, w, attend_p, True, zero_state,
                            final_norm)

    def attend_s(q, k, v):
        padk = jnp.zeros((Bs, lk_pad - lk, WA), BF16)
        kc = cache_k[l].reshape(Bs, past, HA * 2 * DA).astype(BF16)
        vc = cache_v[l].reshape(Bs, past, WA).astype(BF16)
        k_all = jnp.concatenate([kc, k, padk], axis=1)
        v_all = jnp.concatenate([vc, v, padk], axis=1)
        return attn_small(q, k_all, v_all, pos_s, k_pos_s, k_valid_s, rel_bias, lamv, lam_init)

    ys, ks, vs, ss = _trunk(x_sample, mods_s, pos_s, lam_init, w, attend_s, False,
                            state_ret[l].astype(F32), final_norm)
    return (yp, ys, kp[None], vp[None], sp[None], ks[None], vs[None], ss[None])
```

```python
import functools
import math

import jax
import jax.numpy as jnp
from jax import lax
from jax.experimental import pallas as pl
from jax.experimental.pallas import tpu as pltpu

F32 = jnp.float32
BF16 = jnp.bfloat16

CHUNK = 64
HA = 4
DA = 64
DVA = 2 * DA
HR = 4
DKR = 128
DVR = 128
N_BUCKETS = 32
MAX_DIST = 128
PEER_HEADS = 8
N_KEYS = 128
KEY_DIM = 128
PEER_TOPK = 16
EPS = 1e-6
WA = HA * DVA
WR = HR * DVR
NEG = -1e30
LOG2E = math.log2(math.e)
LANES = 128
SUBLANES = 8
BF16_ROWS = 2 * SUBLANES
VMEM_LIMIT = 60 * 1024 * 1024

TOKEN_TILE = 512
RET_CHUNK = 256
ATTN_TQ = 256
ATTN_GROUP = 32
ATTN_TAIL = 8
ROUTE_TILE = 1024
ROUTE_CHUNKS = 8
EXPERT_TILE = 512
EXPERT_ROWS = 32
PEER_SUB = 2


def _params(sem, vmem=VMEM_LIMIT):
    return pltpu.CompilerParams(dimension_semantics=sem, vmem_limit_bytes=vmem)


def _dot(a, b):
    return jnp.dot(a, b, preferred_element_type=F32)


def _dot_nt(a, b):
    return lax.dot_general(a, b, (((1,), (1,)), ((), ())), preferred_element_type=F32)


def _dot_tn(a, b):
    return lax.dot_general(a, b, (((0,), (0,)), ((), ())), preferred_element_type=F32)


def _rms_rows(x):
    return x * lax.rsqrt(jnp.mean(x * x, axis=-1, keepdims=True) + EPS)


def _ada_kernel(c_ref, w_ref, b_ref, o_ref):
    c = c_ref[...]
    s = (c * jax.nn.sigmoid(c)).astype(BF16)
    o_ref[...] = _dot(s, w_ref[...].astype(BF16)) + b_ref[...]


def ada_mod(c, w_ada, b_ada, tn=1024):
    nb, d = c.shape
    n = w_ada.shape[1]
    return pl.pallas_call(
        _ada_kernel,
        grid=(n // tn,),
        in_specs=[pl.BlockSpec((nb, d), lambda j: (0, 0)),
                  pl.BlockSpec((d, tn), lambda j: (0, j)),
                  pl.BlockSpec((1, tn), lambda j: (0, j))],
        out_specs=pl.BlockSpec((nb, tn), lambda j: (0, j)),
        out_shape=jax.ShapeDtypeStruct((nb, n), F32),
        compiler_params=_params(("arbitrary",)),
        name="ada_mod",
    )(c, w_ada, b_ada.reshape(1, n))


def _inproj_kernel(x_ref, sh_ref, sc_ref, n1_ref, w_ref, cos_ref, sin_ref,
                   ka_ref, va_ref, qab_ref, kab_ref, vab_ref, qrb_ref, kr_ref, vrb_ref,
                   gr_ref, ga_ref, gb_ref, vt_ref=None):
    x = x_ref[...]
    h = _rms_rows(x) * n1_ref[...] * (1.0 + sc_ref[...]) + sh_ref[...]
    hb = h.astype(BF16)

    d_model = x.shape[1]
    sizes = (WA, WA, WA, WR, WR, WR, WR, d_model, d_model)
    starts = [sum(sizes[:g]) for g in range(len(sizes))]

    def proj(g):
        return _dot(hb, w_ref[:, starts[g]:starts[g] + sizes[g]])

    qa = proj(0)
    qab_ref[...] = (qa * (DA ** -0.5 * LOG2E)).astype(BF16)
    ka = proj(1)
    va = proj(2)
    for hh in range(HA):
        ka_ref[:, hh, :] = ka[:, hh * DVA:(hh + 1) * DVA]
        va_ref[:, hh, :] = va[:, hh * DVA:(hh + 1) * DVA]
    kab_ref[...] = ka.astype(BF16)
    vab_ref[...] = va.astype(BF16)
    if vt_ref is not None:
        vt_ref[...] = va.T.astype(BF16)
    cos2 = cos_ref[...]
    sin2 = sin_ref[...]

    def rot(z):
        parts = []
        for hh in range(HR):
            zh = z[:, hh * DKR:(hh + 1) * DKR]
            parts.append(zh * cos2 + pltpu.roll(zh, DKR // 2, 1) * sin2)
        return jnp.concatenate(parts, axis=-1)

    qrb_ref[...] = rot(proj(3)).astype(BF16)
    kr_ref[...] = rot(proj(4)) * (DKR ** -0.5)
    vrb_ref[...] = proj(5).astype(BF16)
    gr_ref[...] = proj(6)
    ga_ref[...] = proj(7)
    gb_ref[...] = proj(8)


def in_proj(x, sh1, sc1, norm1, w_in_b, cos2, sin2, tm, emit_vt):
    B, L, D = x.shape
    d_in = w_in_b.shape[1]
    row = lambda n: pl.BlockSpec((None, tm, n), lambda b, i: (b, i, 0))
    mod = pl.BlockSpec((None, 1, D), lambda b, i: (b, 0, 0))
    f = lambda n, dt: jax.ShapeDtypeStruct((B, L, n), dt)
    cache = pl.BlockSpec((None, tm, HA, DVA), lambda b, i: (b, i, 0, 0))
    cache_shape = jax.ShapeDtypeStruct((B, L, HA, DVA), F32)
    vt_spec = [pl.BlockSpec((None, WA, tm), lambda b, i: (b, 0, i))] if emit_vt else []
    vt_shape = [jax.ShapeDtypeStruct((B, WA, L), BF16)] if emit_vt else []
    return pl.pallas_call(
        _inproj_kernel,
        grid=(B, L // tm),
        in_specs=[row(D), mod, mod,
                  pl.BlockSpec((1, D), lambda b, i: (0, 0)),
                  pl.BlockSpec((D, d_in), lambda b, i: (0, 0), pipeline_mode=pl.Buffered(1)),
                  pl.BlockSpec((tm, DKR), lambda b, i: (i, 0)),
                  pl.BlockSpec((tm, DKR), lambda b, i: (i, 0))],
        out_specs=[cache, cache, row(WA), row(WA), row(WA), row(WR), row(WR), row(WR),
                   row(WR), row(D), row(D)] + vt_spec,
        out_shape=[cache_shape, cache_shape, f(WA, BF16), f(WA, BF16), f(WA, BF16),
                   f(WR, BF16), f(WR, F32), f(WR, BF16), f(WR, F32), f(D, F32), f(D, F32)]
        + vt_shape,
        compiler_params=_params(("arbitrary", "arbitrary")),
        name="in_proj",
    )(x, sh1.reshape(B, 1, D), sc1.reshape(B, 1, D), norm1.reshape(1, D), w_in_b, cos2, sin2)


def _t5_bucket(rel):
    nb = N_BUCKETS // 2
    ret = jnp.where(rel > 0, nb, 0)
    n = jnp.abs(rel)
    max_exact = nb // 2
    nf = jnp.maximum(n, max_exact).astype(F32)
    large = max_exact + (jnp.log(nf / max_exact) / math.log(MAX_DIST / max_exact)
                         * (nb - max_exact)).astype(jnp.int32)
    large = jnp.minimum(large, nb - 1)
    return ret + jnp.where(n < max_exact, n, large)


def _bias_from_buckets(bkt, rb_ref, h, shift):
    val = jnp.where(bkt < 0, NEG, 0.0).astype(F32)
    for n in range(N_BUCKETS):
        val = jnp.where(bkt == n, (rb_ref[n, h] - shift) * LOG2E, val)
    return val


def _lam_value(lv_ref, lam_init):
    lv = lv_ref[...]
    a = jnp.sum(lv[0:1] * lv[1:2], axis=-1, keepdims=True)
    b = jnp.sum(lv[2:3] * lv[3:4], axis=-1, keepdims=True)
    return jnp.exp(a) - jnp.exp(b) + lam_init


def _block_diag_q(q):
    lane = lax.broadcasted_iota(jnp.int32, q.shape, 1)
    zero = jnp.zeros_like(q)
    return jnp.concatenate([jnp.where(lane < DA, q, zero), jnp.where(lane >= DA, q, zero)], axis=0)


def _attn_kernel(rb_ref, lv_ref, q_ref, k_ref, vt_ref, bkt_ref, o_ref,
                 bias_sc, m_sc, acc_sc, va_sc, *, tq, nsub, lam_init):
    h = pl.program_id(1)
    i = pl.program_id(2)

    @pl.when(i == 0)
    def _():
        far = rb_ref[N_BUCKETS // 2 - 1, h]
        for t in range(2):
            bias_sc[t] = _bias_from_buckets(bkt_ref[t], rb_ref, h, far)
        va_sc[0:DVA, :] = vt_ref[...]
        va_sc[DVA:, :] = jnp.ones((BF16_ROWS, va_sc.shape[1]), BF16)

    qt = q_ref[...].astype(F32).T
    drow = lax.broadcasted_iota(jnp.int32, qt.shape, 0)
    qbd_t = jnp.concatenate([jnp.where(drow < DA, qt, 0.0), jnp.where(drow >= DA, qt, 0.0)],
                            axis=1).astype(BF16)
    m_sc[...] = jnp.full(m_sc.shape, NEG, F32)
    acc_sc[...] = jnp.zeros(acc_sc.shape, F32)

    def scores(off, tk, bias=None):
        s = _dot(k_ref[pl.ds(off, tk), :], qbd_t)
        if bias is not None:
            s = s + jnp.concatenate([bias, bias], axis=1)
        return s

    def absorb(s, off, tk):
        m_prev = m_sc[...]
        m_new = jnp.maximum(m_prev, jnp.max(s, axis=0, keepdims=True))
        alpha = jnp.exp2(m_prev - m_new)
        p = jnp.exp2(s - m_new).astype(BF16)
        acc_sc[...] = alpha * acc_sc[...] + _dot(va_sc[:, pl.ds(off, tk)], p)
        m_sc[...] = m_new

    def tiles(specs):
        ss = [scores(*sp) for sp in specs]
        for s, sp in zip(ss, specs):
            absorb(s, sp[0], sp[1])

    ntile = jnp.maximum(i - 1, 0)
    nfull = ntile // nsub

    def far(first_tile, n):
        return [(pl.multiple_of((first_tile + j) * tq, tq), tq) for j in range(n)]

    def far_body(t, carry):
        tiles(far(t * nsub, nsub))
        return carry

    lax.fori_loop(0, nfull, far_body, 0)
    done = nfull * nsub
    n = nsub // 2
    while n >= ATTN_TAIL:
        take = ((ntile - done) // n) > 0

        @pl.when(take)
        def _(done=done, n=n):
            tiles(far(done, n))

        done = done + jnp.where(take, n, 0)
        n //= 2

    rest = ntile - done
    for r in range(ATTN_TAIL):
        @pl.when((i >= 1) & (rest == r))
        def _(r=r):
            off = pl.multiple_of((i - 1) * tq, tq)
            tiles(far(done, r) + [(off, tq, bias_sc[1]),
                                  (pl.multiple_of(off + tq, tq), tq, bias_sc[0])])

    @pl.when(i == 0)
    def _():
        tiles([(0, tq, bias_sc[0])])

    lam = _lam_value(lv_ref, lam_init)
    o = acc_sc[0:DVA, :] / acc_sc[DVA:DVA + 1, :]
    o_ref[...] = (o[:, 0:tq] - lam * o[:, tq:2 * tq]).T


def attn_prompt(qab, kab, vt, rel_bias, lamv, lam_init, tq=ATTN_TQ, nsub=ATTN_GROUP):
    B, S, _ = qab.shape
    assert S % tq == 0
    c = jnp.arange(tq, dtype=jnp.int32)[:, None]
    r = jnp.arange(tq, dtype=jnp.int32)[None, :]
    diag = jnp.where((c // CHUNK) <= (r // CHUNK), _t5_bucket(c - r), -1)
    prev = _t5_bucket(c - r - tq)
    bkt = jnp.stack([diag, prev]).astype(jnp.int32)
    smem = pl.BlockSpec(memory_space=pltpu.SMEM)
    return pl.pallas_call(
        functools.partial(_attn_kernel, tq=tq, nsub=nsub, lam_init=lam_init),
        grid=(B, HA, S // tq),
        in_specs=[smem,
                  pl.BlockSpec((4, DA), lambda b, h, i: (0, 0)),
                  pl.BlockSpec((None, tq, DVA), lambda b, h, i: (b, i, h)),
                  pl.BlockSpec((None, S, DVA), lambda b, h, i: (b, 0, h)),
                  pl.BlockSpec((None, DVA, S), lambda b, h, i: (b, h, 0)),
                  pl.BlockSpec((2, tq, tq), lambda b, h, i: (0, 0, 0))],
        out_specs=pl.BlockSpec((None, tq, DVA), lambda b, h, i: (b, i, h)),
        out_shape=jax.ShapeDtypeStruct((B, S, WA), F32),
        scratch_shapes=[pltpu.VMEM((2, tq, tq), F32),
                        pltpu.VMEM((1, 2 * tq), F32),
                        pltpu.VMEM((DVA + BF16_ROWS, 2 * tq), F32),
                        pltpu.VMEM((DVA + BF16_ROWS, S), BF16)],
        compiler_params=_params(("arbitrary", "arbitrary", "arbitrary")),
        name="attn_prompt",
    )(rel_bias, lamv, qab, kab, vt, bkt)


def _attn_small_kernel(rb_ref, lv_ref, q_ref, k_ref, v_ref, bkt_ref, o_ref, *, lq, lam_init):
    h = pl.program_id(1)
    bias = _bias_from_buckets(bkt_ref[...], rb_ref, h, 0.0)
    qbd = _block_diag_q(q_ref[...])
    s = _dot_nt(qbd, k_ref[...]) + jnp.concatenate([bias, bias], axis=0)
    m = jnp.max(s, axis=-1, keepdims=True)
    p = jnp.exp2(s - m)
    l = jnp.sum(p, axis=-1, keepdims=True)
    o = _dot(p.astype(BF16), v_ref[...]) / l
    lam = _lam_value(lv_ref, lam_init)
    o_ref[...] = o[0:lq] - lam * o[lq:2 * lq]


def attn_small(qab, k_all, v_all, q_pos, k_pos, k_valid, rel_bias, lamv, lam_init):
    B, Lq, _ = qab.shape
    Lk = k_all.shape[1]
    visible = ((k_pos[None, :] // CHUNK) <= (q_pos[:, None] // CHUNK)) & k_valid[None, :]
    bkt = jnp.where(visible, _t5_bucket(k_pos[None, :] - q_pos[:, None]), -1).astype(jnp.int32)
    smem = pl.BlockSpec(memory_space=pltpu.SMEM)
    return pl.pallas_call(
        functools.partial(_attn_small_kernel, lq=Lq, lam_init=lam_init),
        grid=(B, HA),
        in_specs=[smem,
                  pl.BlockSpec((4, DA), lambda b, h: (0, 0)),
                  pl.BlockSpec((None, Lq, DVA), lambda b, h: (b, 0, h)),
                  pl.BlockSpec((None, Lk, DVA), lambda b, h: (b, 0, h)),
                  pl.BlockSpec((None, Lk, DVA), lambda b, h: (b, 0, h)),
                  pl.BlockSpec((Lq, Lk), lambda b, h: (0, 0))],
        out_specs=pl.BlockSpec((None, Lq, DVA), lambda b, h: (b, 0, h)),
        out_shape=jax.ShapeDtypeStruct((B, Lq, WA), F32),
        compiler_params=_params(("arbitrary", "arbitrary")),
        name="attn_sample",
    )(rel_bias, lamv, qab, k_all, v_all, bkt)


def _ret_kernel(lg_ref, q_ref, k_ref, v_ref, s0_ref, o_ref, so_ref, state_sc, decay_sc, *, C):
    c = pl.program_id(1)

    @pl.when(c == 0)
    def _():
        state_sc[...] = s0_ref[...]
        r = lax.broadcasted_iota(jnp.int32, (C, C), 0)
        cc = lax.broadcasted_iota(jnp.int32, (C, C), 1)
        diff = (r - cc).astype(F32)
        for h in range(HR):
            decay_sc[h] = jnp.where(diff >= 0, jnp.exp(jnp.maximum(diff, 0.0) * lg_ref[h]), 0.0)

    n = lax.broadcasted_iota(jnp.int32, (C, 1), 0).astype(F32)
    for h in range(HR):
        lg = lg_ref[h]
        cols = slice(h * DKR, (h + 1) * DKR)
        xi = jnp.exp((n + 1.0) * lg)
        zeta = jnp.exp((C - 1.0 - n) * lg)
        q = q_ref[:, cols]
        k = k_ref[:, cols]
        v = v_ref[:, cols]
        state = state_sc[h]
        scores = _dot_nt(q, k.astype(BF16)) * decay_sc[h]
        intra = _dot(scores.astype(BF16), v)
        cross = _dot(q, state.astype(BF16)) * xi
        o_ref[:, cols] = intra + cross
        kz = (k * zeta).astype(BF16)
        state_sc[h] = jnp.exp(C * lg) * state + _dot_tn(kz, v)

    @pl.when(c == pl.num_programs(1) - 1)
    def _():
        so_ref[...] = state_sc[...]


def retention(qrb, kr, vrb, state0, C):
    B, L, _ = qrb.shape
    lg = jnp.log(1.0 - 2.0 ** (-5.0 - jnp.arange(HR, dtype=F32)))
    blk = pl.BlockSpec((None, C, WR), lambda b, c: (b, c, 0))
    st = pl.BlockSpec((None, HR, DKR, DVR), lambda b, c: (b, 0, 0, 0))
    return pl.pallas_call(
        functools.partial(_ret_kernel, C=C),
        grid=(B, L // C),
        in_specs=[pl.BlockSpec(memory_space=pltpu.SMEM), blk, blk, blk, st],
        out_specs=[blk, st],
        out_shape=[jax.ShapeDtypeStruct((B, L, WR), F32),
                   jax.ShapeDtypeStruct((B, HR, DKR, DVR), F32)],
        scratch_shapes=[pltpu.VMEM((HR, DKR, DVR), F32), pltpu.VMEM((HR, C, C), F32)],
        compiler_params=_params(("arbitrary", "arbitrary")),
        name="retention",
    )(lg, qrb, kr, vrb, state0)


def _outmix_kernel(x_ref, oa_ref, or_ref, gr_ref, ga_ref, gb_ref, g1_ref, sh2_ref, sc2_ref, n2_ref,
                   sa_ref, sr_ref, wba_ref, wbr_ref, wo_ref, wpq_ref,
                   x1_ref, h2_ref, qp_ref, *, lam_init, h2_transposed):
    sa = sa_ref[...]
    sr = sr_ref[...]
    gr = gr_ref[...]
    silu_gr = gr * jax.nn.sigmoid(gr)
    ya_parts, yr_parts = [], []
    for hh in range(HA):
        sl = slice(hh * DVA, (hh + 1) * DVA)
        ya_parts.append(_rms_rows(oa_ref[:, sl]) * sa * (1.0 - lam_init))
        yr_parts.append(silu_gr[:, sl] * (_rms_rows(or_ref[:, sl]) * sr))
    ya = jnp.concatenate(ya_parts, axis=-1).astype(BF16)
    yr = jnp.concatenate(yr_parts, axis=-1).astype(BF16)
    y = (jax.nn.sigmoid(ga_ref[...]) * _dot(ya, wba_ref[...])
         + jax.nn.sigmoid(gb_ref[...]) * _dot(yr, wbr_ref[...]))
    out = _dot(y.astype(BF16), wo_ref[...])
    x1 = x_ref[...] + g1_ref[...] * out
    x1_ref[...] = x1
    h2f = _rms_rows(x1) * n2_ref[...] * (1.0 + sc2_ref[...]) + sh2_ref[...]
    h2 = h2f.astype(BF16)
    h2_ref[...] = h2f.T.astype(BF16) if h2_transposed else h2
    qp_ref[...] = _dot(h2, wpq_ref[...]).astype(BF16)


def out_mix(x, oa, orr, gr, ga, gb, g1, sh2, sc2, norm2, subln_a, subln_r,
            w_ba_b, w_br_b, w_o_b, w_pq_b, lam_init, tm):
    B, L, D = x.shape
    nq = w_pq_b.shape[1]
    row = lambda n: pl.BlockSpec((None, tm, n), lambda b, i: (b, i, 0))
    mod = pl.BlockSpec((None, 1, D), lambda b, i: (b, 0, 0))
    full = lambda a: pl.BlockSpec(a.shape, lambda b, i: (0,) * a.ndim, pipeline_mode=pl.Buffered(1))
    n2 = norm2.reshape(1, D)
    sa = subln_a.reshape(1, DVA)
    sr = subln_r.reshape(1, DVR)
    h2_transposed = tm % LANES == 0
    nt = L // tm
    if h2_transposed:
        h2_spec = pl.BlockSpec((D, tm), lambda b, i: (0, b * nt + i))
        h2_shape = jax.ShapeDtypeStruct((D, B * L), BF16)
    else:
        h2_spec = row(D)
        h2_shape = jax.ShapeDtypeStruct((B, L, D), BF16)
    x1, h2, qp = pl.pallas_call(
        functools.partial(_outmix_kernel, lam_init=lam_init, h2_transposed=h2_transposed),
        grid=(B, nt),
        in_specs=[row(D), row(WA), row(WR), row(WR), row(D), row(D), mod, mod, mod,
                  full(n2), full(sa), full(sr), full(w_ba_b), full(w_br_b), full(w_o_b), full(w_pq_b)],
        out_specs=[row(D), h2_spec, row(nq)],
        out_shape=[jax.ShapeDtypeStruct((B, L, D), F32), h2_shape,
                   jax.ShapeDtypeStruct((B, L, nq), BF16)],
        compiler_params=_params(("arbitrary", "arbitrary")),
        name="out_mix",
    )(x, oa, orr, gr, ga, gb, g1.reshape(B, 1, D), sh2.reshape(B, 1, D), sc2.reshape(B, 1, D),
      n2, sa, sr, w_ba_b, w_br_b, w_o_b, w_pq_b)
    h2t = h2 if h2_transposed else h2.reshape(B * L, D).T
    return x1, h2t, qp


UNRANKED = float(PEER_TOPK * PEER_TOPK)


def _topk_rows(s, k, break_ties, want_rank=True):
    n = s.shape[0]
    iota = lax.broadcasted_iota(jnp.int32, s.shape, 0).astype(F32)
    work = s
    rank = jnp.full(s.shape, UNRANKED, F32) if want_rank else None
    vals = []
    for r in range(k):
        m = jnp.max(work, axis=0, keepdims=True)
        sel = work == m
        if break_ties:
            idx = jnp.min(jnp.where(sel, iota, float(n)), axis=0, keepdims=True)
            sel = iota == idx
        if want_rank:
            rank = jnp.where(sel, float(r), rank)
        work = jnp.where(sel, -jnp.inf, work)
        vals.append(m)
    return vals, rank


def _count_rows(mask):
    return jnp.sum(mask.astype(F32), axis=0, keepdims=True)


_HEAD_A = SUBLANES
_CAND_NB = [PEER_TOPK] + [SUBLANES] * (_HEAD_A - 1)


def _route_chunk(s1, s2, break_ties):
    k = PEER_TOPK
    v1, rank1 = _topk_rows(s1, k, break_ties, want_rank=break_ties)
    v2, rank2 = _topk_rows(s2, k, break_ties)
    v2m = jnp.concatenate(v2, axis=0)
    v1t = jnp.concatenate(v1[_HEAD_A:], axis=0)
    blocks = [v1[a] + v2m[0:nb] for a, nb in enumerate(_CAND_NB)] + [v1t + v2[0]]
    cand = jnp.concatenate(blocks, axis=0)
    vc, crank = _topk_rows(cand, k, break_ties, want_rank=break_ties)
    if break_ties:
        sel = crank < UNRANKED
        row_is = lambda a: rank1 == float(a)
        nbad = jnp.zeros((), jnp.int32)
    else:
        sel = cand >= vc[k - 1]
        row_is = lambda a: s1 == v1[a]
        bad = ((_count_rows(s1 >= v1[k - 1]) != float(k))
               | (_count_rows(rank2 < UNRANKED) != float(k)) | (_count_rows(sel) != float(k)))
        nbad = jnp.sum(bad.astype(jnp.int32))
    cmax = v1[0] + v2[0]
    z = jnp.sum(jnp.where(sel, jnp.exp(cand - cmax), 0.0), axis=0, keepdims=True)
    self32 = sel.astype(F32)
    cidx = jnp.zeros(s1.shape, F32)
    lo = 0
    for a, nb in enumerate(_CAND_NB):
        cnt = jnp.sum(self32[lo:lo + nb], axis=0, keepdims=True)
        cidx = jnp.where(row_is(a), cnt, cidx)
        lo += nb
    for a in range(_HEAD_A, k):
        cidx = jnp.where(row_is(a), self32[lo + a - _HEAD_A:lo + a - _HEAD_A + 1], cidx)
    e1 = jnp.exp(s1 - v1[0]) / z
    e2 = jnp.exp(s2 - v2[0])
    return (e1, e2, rank2, cidx), nbad


def _route_kernel(q_ref, keys_ref, e1_ref, e2_ref, r2_ref, c_ref, s1_sc, s2_sc):
    q = q_ref[...]
    s1_sc[...] = _dot_nt(keys_ref[0], q[:, 0:KEY_DIM])
    s2_sc[...] = _dot_nt(keys_ref[1], q[:, KEY_DIM:2 * KEY_DIM])

    def store(sl, res):
        e1, e2, rank2, cidx = res
        e1_ref[:, sl] = e1
        e2_ref[:, sl] = e2.astype(BF16)
        r2_ref[:, sl] = rank2.astype(BF16)
        c_ref[:, sl] = cidx

    group = math.gcd(ROUTE_CHUNKS, s1_sc.shape[1] // LANES)

    def chunks(ci, carry):
        sls = [pl.ds(pl.multiple_of((ci * group + g) * LANES, LANES), LANES) for g in range(group)]
        fast = [_route_chunk(s1_sc[:, sl], s2_sc[:, sl], break_ties=False) for sl in sls]
        for sl, (res, _) in zip(sls, fast):
            store(sl, res)
        for sl, (_, nbad) in zip(sls, fast):
            @pl.when(nbad > 0)
            def _():
                store(sl, _route_chunk(s1_sc[:, sl], s2_sc[:, sl], break_ties=True)[0])
        return carry

    lax.fori_loop(0, s1_sc.shape[1] // (group * LANES), chunks, 0)


def peer_route(qp, keys_b, tm):
    T = qp.shape[0]
    out = pl.BlockSpec((None, N_KEYS, tm), lambda t, h: (h, 0, t))
    shp = lambda dt: jax.ShapeDtypeStruct((PEER_HEADS, N_KEYS, T), dt)
    return pl.pallas_call(
        _route_kernel,
        grid=(T // tm, PEER_HEADS),
        in_specs=[pl.BlockSpec((tm, 2 * KEY_DIM), lambda t, h: (t, h)),
                  pl.BlockSpec((2, N_KEYS, KEY_DIM), lambda t, h: (h, 0, 0))],
        out_specs=[out, out, out, out],
        out_shape=[shp(F32), shp(BF16), shp(BF16), shp(F32)],
        scratch_shapes=[pltpu.VMEM((N_KEYS, tm), F32), pltpu.VMEM((N_KEYS, tm), F32)],
        compiler_params=_params(("arbitrary", "arbitrary")),
        name="peer_route",
    )(qp, keys_b)


def _bcast_rows_bf16(row, n):
    t = row.shape[1]
    tile = jnp.broadcast_to(row, (BF16_ROWS, t)).astype(BF16)
    return jnp.broadcast_to(tile[None], (n // BF16_ROWS, BF16_ROWS, t)).reshape(n, t)


def _peer_kernel(h2_ref, u_ref, vt_ref, e1_ref, e2_ref, r2_ref, c_ref, x1_ref, g2_ref, fn_ref,
                 y_ref, acc_sc, *, ni):
    e = pl.program_id(1)

    @pl.when(e == 0)
    def _():
        acc_sc[...] = jnp.zeros(acc_sc.shape, F32)

    h2 = h2_ref[...]
    pair = PEER_SUB * N_KEYS
    a_pairs = [_dot(u_ref[p * pair:(p + 1) * pair, :], h2) for p in range(ni // PEER_SUB)]
    for p in range(ni // PEER_SUB):
        gs = [jnp.zeros((N_KEYS, h2.shape[1]), BF16) for _ in range(PEER_SUB)]
        for hh in range(PEER_HEADS):
            r2h = r2_ref[hh]
            e2h = e2_ref[hh]
            for jj in range(PEER_SUB):
                i = PEER_SUB * p + jj
                c_row = _bcast_rows_bf16(c_ref[hh, pl.ds(i, 1), :], N_KEYS)
                e1_row = _bcast_rows_bf16(e1_ref[hh, pl.ds(i, 1), :], N_KEYS)
                gs[jj] = gs[jj] + jnp.where(r2h < c_row, e2h * e1_row, jnp.zeros_like(e2h))
        ws = []
        for jj in range(PEER_SUB):
            a = a_pairs[p][jj * N_KEYS:(jj + 1) * N_KEYS]
            act = 0.5 * a * (1.0 + lax.erf(a * (2.0 ** -0.5)))
            ws.append(gs[jj] * act.astype(BF16))
        w = jnp.concatenate(ws, axis=0)
        acc_sc[...] += _dot(vt_ref[:, p * pair:(p + 1) * pair], w)

    @pl.when(e == pl.num_programs(1) - 1)
    def _():
        x2 = x1_ref[...] + g2_ref[...] * acc_sc[...].T
        y_ref[...] = _rms_rows(x2) * fn_ref[...]


def peer_experts(h2t, u_b, vt_b, e1, e2, r2, cc, x1, g2, final_norm, tm, ni=EXPERT_ROWS):
    D, T = h2t.shape
    nb = ni * N_KEYS
    tok = pl.BlockSpec((tm, D), lambda t, e: (t, 0))
    gate = pl.BlockSpec((PEER_HEADS, N_KEYS, tm), lambda t, e: (0, 0, t))
    rows = pl.BlockSpec((PEER_HEADS, ni, tm), lambda t, e: (0, e, t))
    L = T // g2.shape[0]
    if L % tm == 0:
        g2_arg = g2.reshape(g2.shape[0], 1, D)
        g2_spec = pl.BlockSpec((None, 1, D), lambda t, e: (t // (L // tm), 0, 0))
    else:
        g2_arg = jnp.repeat(g2, L, axis=0)
        g2_spec = tok
    return pl.pallas_call(
        functools.partial(_peer_kernel, ni=ni),
        grid=(T // tm, u_b.shape[0] // nb),
        in_specs=[pl.BlockSpec((D, tm), lambda t, e: (0, t)),
                  pl.BlockSpec((nb, D), lambda t, e: (e, 0)),
                  pl.BlockSpec((D, nb), lambda t, e: (0, e)),
                  rows, gate, gate, rows, tok, g2_spec,
                  pl.BlockSpec((1, D), lambda t, e: (0, 0))],
        out_specs=tok,
        out_shape=jax.ShapeDtypeStruct((T, D), F32),
        scratch_shapes=[pltpu.VMEM((D, tm), F32)],
        compiler_params=_params(("arbitrary", "arbitrary")),
        name="peer_experts",
    )(h2t, u_b, vt_b, e1, e2, r2, cc, x1, g2_arg, final_norm.reshape(1, D))


def _rot_tables(pos):
    inv = 1.0 / (10000.0 ** jnp.linspace(0.0, 1.0, DKR // 2, dtype=F32))
    ang = pos[:, None].astype(F32) * inv[None, :]
    cos, sin = jnp.cos(ang), jnp.sin(ang)
    return jnp.concatenate([cos, cos], axis=-1), jnp.concatenate([-sin, sin], axis=-1)


def _pick_tile(n, pref):
    t = min(n, pref)
    assert n % t == 0, (n, t)
    return t


def _trunk(x, mods, pos, lam_init, w, attend, values_transposed, state0, final_norm):
    (norm1, norm2, w_in_b, subln_a, subln_r, w_ba_b, w_br_b, w_o_b, w_pq_b, keys_b, u_b, vt_b) = w
    sh1, sc1, g1, sh2, sc2, g2 = mods
    B, L, D = x.shape
    T = B * L
    cos2, sin2 = _rot_tables(pos)
    tm = _pick_tile(L, TOKEN_TILE)
    ka, va, qab, kab, vab, qrb, kr, vrb, gr, ga, gb, *vt = in_proj(
        x, sh1, sc1, norm1, w_in_b, cos2, sin2, tm, emit_vt=values_transposed)
    oa = attend(qab, kab, vt[0] if values_transposed else vab)
    orr, st = retention(qrb, kr, vrb, state0, _pick_tile(L, RET_CHUNK))
    x1, h2t, qp = out_mix(x, oa, orr, gr, ga, gb, g1, sh2, sc2, norm2, subln_a, subln_r,
                          w_ba_b, w_br_b, w_o_b, w_pq_b, lam_init, tm)
    e1, e2, r2, cc = peer_route(qp.reshape(T, -1), keys_b, _pick_tile(T, ROUTE_TILE))
    y = peer_experts(h2t, u_b, vt_b, e1, e2, r2, cc, x1.reshape(T, D), g2,
                     final_norm, _pick_tile(T, EXPERT_TILE))
    return y.reshape(B, L, D), ka, va, st


def kernel(x_prompt, x_sample, cache_k, cache_v, state_ret, c_prompt, c_sample, w_ada, b_ada, norm1,
           norm2, w_in, lam_q1, lam_k1, lam_q2, lam_k2, subln_a, subln_r, w_ba, w_br, w_o, rel_bias,
           w_pq, peer_keys, peer_u, peer_v, final_norm):
    depth = w_ada.shape[0]
    assert depth == 1, "the fused final norm assumes a single layer"
    Bp, Lp, D = x_prompt.shape
    Bs, Ls, _ = x_sample.shape
    past = cache_k.shape[2]
    pos_p = jnp.arange(Lp, dtype=jnp.int32)
    pos_s = past + jnp.arange(Ls, dtype=jnp.int32)
    lk = past + Ls
    lk_pad = -(-lk // LANES) * LANES
    k_pos_s = jnp.arange(lk_pad, dtype=jnp.int32)
    k_valid_s = k_pos_s < lk

    l = 0
    lam_init = 0.8 - 0.6 * math.exp(-0.3 * l)
    lamv = jnp.stack([lam_q1[l], lam_k1[l], lam_q2[l], lam_k2[l]]).astype(F32)
    mod = ada_mod(jnp.concatenate([c_prompt, c_sample], axis=0), w_ada[l], b_ada[l])
    mods = jnp.split(mod, 6, axis=-1)
    mods_p = [m[:Bp] for m in mods]
    mods_s = [m[Bp:] for m in mods]
    w = (norm1[l], norm2[l], w_in[l].astype(BF16), subln_a[l], subln_r[l], w_ba[l].astype(BF16),
         w_br[l].astype(BF16), w_o[l].astype(BF16), w_pq[l].astype(BF16),
         peer_keys[l].reshape(PEER_HEADS * 2, N_KEYS, KEY_DIM).astype(BF16),
         peer_u[l].astype(BF16), peer_v[l].T.astype(BF16))

    attend_p = lambda q, k, v: attn_prompt(q, k, v, rel_bias, lamv, lam_init)
    zero_state = jnp.zeros((Bp, HR, DKR, DVR), F32)
    yp, kp, vp, sp = _trunk(x_prompt, mods_p, pos_p, lam_init, w, attend_p, True, zero_state,
                            final_norm)

    def attend_s(q, k, v):
        padk = jnp.zeros((Bs, lk_pad - lk, WA), BF16)
        kc = cache_k[l].reshape(Bs, past, HA * 2 * DA).astype(BF16)
        vc = cache_v[l].reshape(Bs, past, WA).astype(BF16)
        k_all = jnp.concatenate([kc, k, padk], axis=1)
        v_all = jnp.concatenate([vc, v, padk], axis=1)
        return attn_small(q, k_all, v_all, pos_s, k_pos_s, k_valid_s, rel_bias, lamv, lam_init)

    ys, ks, vs, ss = _trunk(x_sample, mods_s, pos_s, lam_init, w, attend_s, False,
                            state_ret[l].astype(F32), final_norm)
    return (yp, ys, kp[None], vp[None], sp[None], ks[None], vs[None], ss[None])
```

```python
import functools
import math

import jax
import jax.numpy as jnp
from jax import lax
from jax.experimental import pallas as pl
from jax.experimental.pallas import tpu as pltpu

F32 = jnp.float32
BF16 = jnp.bfloat16

CHUNK = 64
HA = 4
DA = 64
DVA = 2 * DA
HR = 4
DKR = 128
DVR = 128
N_BUCKETS = 32
MAX_DIST = 128
PEER_HEADS = 8
N_KEYS = 128
KEY_DIM = 128
PEER_TOPK = 16
EPS = 1e-6
WA = HA * DVA
WR = HR * DVR
NEG = -1e30
LOG2E = math.log2(math.e)
LANES = 128
SUBLANES = 8
BF16_ROWS = 2 * SUBLANES
VMEM_LIMIT = 60 * 1024 * 1024

TOKEN_TILE = 512
RET_CHUNK = 256
ATTN_TQ = 256
ATTN_GROUP = 32
ATTN_TAIL = 8
ROUTE_TILE = 1024
ROUTE_CHUNKS = 8
EXPERT_TILE = 512
EXPERT_ROWS = 32
PEER_SUB = 2


def _params(sem, vmem=VMEM_LIMIT):
    return pltpu.CompilerParams(dimension_semantics=sem, vmem_limit_bytes=vmem)


def _dot(a, b):
    return jnp.dot(a, b, preferred_element_type=F32)


def _dot_nt(a, b):
    return lax.dot_general(a, b, (((1,), (1,)), ((), ())), preferred_element_type=F32)


def _dot_tn(a, b):
    return lax.dot_general(a, b, (((0,), (0,)), ((), ())), preferred_element_type=F32)


def _rms_rows(x):
    return x * lax.rsqrt(jnp.mean(x * x, axis=-1, keepdims=True) + EPS)


def _ada_kernel(c_ref, w_ref, b_ref, o_ref):
    c = c_ref[...]
    s = (c * jax.nn.sigmoid(c)).astype(BF16)
    o_ref[...] = _dot(s, w_ref[...].astype(BF16)) + b_ref[...]


def ada_mod(c, w_ada, b_ada, tn=1024):
    nb, d = c.shape
    n = w_ada.shape[1]
    return pl.pallas_call(
        _ada_kernel,
        grid=(n // tn,),
        in_specs=[pl.BlockSpec((nb, d), lambda j: (0, 0)),
                  pl.BlockSpec((d, tn), lambda j: (0, j)),
                  pl.BlockSpec((1, tn), lambda j: (0, j))],
        out_specs=pl.BlockSpec((nb, tn), lambda j: (0, j)),
        out_shape=jax.ShapeDtypeStruct((nb, n), F32),
        compiler_params=_params(("arbitrary",)),
        name="ada_mod",
    )(c, w_ada, b_ada.reshape(1, n))


def _inproj_kernel(x_ref, sh_ref, sc_ref, n1_ref, w_ref, cos_ref, sin_ref,
                   ka_ref, va_ref, qab_ref, kab_ref, vab_ref, qrb_ref, kr_ref, vrb_ref,
                   gr_ref, ga_ref, gb_ref, vt_ref=None):
    x = x_ref[...]
    h = _rms_rows(x) * n1_ref[...] * (1.0 + sc_ref[...]) + sh_ref[...]
    hb = h.astype(BF16)

    d_model = x.shape[1]
    sizes = (WA, WA, WA, WR, WR, WR, WR, d_model, d_model)
    starts = [sum(sizes[:g]) for g in range(len(sizes))]

    def proj(g):
        return _dot(hb, w_ref[:, starts[g]:starts[g] + sizes[g]])

    qa = proj(0)
    qab_ref[...] = (qa * (DA ** -0.5 * LOG2E)).astype(BF16)
    ka = proj(1)
    va = proj(2)
    for hh in range(HA):
        ka_ref[:, hh, :] = ka[:, hh * DVA:(hh + 1) * DVA]
        va_ref[:, hh, :] = va[:, hh * DVA:(hh + 1) * DVA]
    kab_ref[...] = ka.astype(BF16)
    vab_ref[...] = va.astype(BF16)
    if vt_ref is not None:
        vt_ref[...] = va.T.astype(BF16)
    cos2 = cos_ref[...]
    sin2 = sin_ref[...]

    def rot(z):
        parts = []
        for hh in range(HR):
            zh = z[:, hh * DKR:(hh + 1) * DKR]
            parts.append(zh * cos2 + pltpu.roll(zh, DKR // 2, 1) * sin2)
        return jnp.concatenate(parts, axis=-1)

    qrb_ref[...] = rot(proj(3)).astype(BF16)
    kr_ref[...] = rot(proj(4)) * (DKR ** -0.5)
    vrb_ref[...] = proj(5).astype(BF16)
    gr_ref[...] = proj(6)
    ga_ref[...] = proj(7)
    gb_ref[...] = proj(8)


def in_proj(x, sh1, sc1, norm1, w_in_b, cos2, sin2, tm, emit_vt):
    B, L, D = x.shape
    d_in = w_in_b.shape[1]
    row = lambda n: pl.BlockSpec((None, tm, n), lambda b, i: (b, i, 0))
    mod = pl.BlockSpec((None, 1, D), lambda b, i: (b, 0, 0))
    f = lambda n, dt: jax.ShapeDtypeStruct((B, L, n), dt)
    cache = pl.BlockSpec((None, tm, HA, DVA), lambda b, i: (b, i, 0, 0))
    cache_shape = jax.ShapeDtypeStruct((B, L, HA, DVA), F32)
    vt_spec = [pl.BlockSpec((None, WA, tm), lambda b, i: (b, 0, i))] if emit_vt else []
    vt_shape = [jax.ShapeDtypeStruct((B, WA, L), BF16)] if emit_vt else []
    return pl.pallas_call(
        _inproj_kernel,
        grid=(B, L // tm),
        in_specs=[row(D), mod, mod,
                  pl.BlockSpec((1, D), lambda b, i: (0, 0)),
                  pl.BlockSpec((D, d_in), lambda b, i: (0, 0), pipeline_mode=pl.Buffered(1)),
                  pl.BlockSpec((tm, DKR), lambda b, i: (i, 0)),
                  pl.BlockSpec((tm, DKR), lambda b, i: (i, 0))],
        out_specs=[cache, cache, row(WA), row(WA), row(WA), row(WR), row(WR), row(WR),
                   row(WR), row(D), row(D)] + vt_spec,
        out_shape=[cache_shape, cache_shape, f(WA, BF16), f(WA, BF16), f(WA, BF16),
                   f(WR, BF16), f(WR, F32), f(WR, BF16), f(WR, F32), f(D, F32), f(D, F32)]
        + vt_shape,
        compiler_params=_params(("arbitrary", "arbitrary")),
        name="in_proj",
    )(x, sh1.reshape(B, 1, D), sc1.reshape(B, 1, D), norm1.reshape(1, D), w_in_b, cos2, sin2)


def _t5_bucket(rel):
    nb = N_BUCKETS // 2
    ret = jnp.where(rel > 0, nb, 0)
    n = jnp.abs(rel)
    max_exact = nb // 2
    nf = jnp.maximum(n, max_exact).astype(F32)
    large = max_exact + (jnp.log(nf / max_exact) / math.log(MAX_DIST / max_exact)
                         * (nb - max_exact)).astype(jnp.int32)
    large = jnp.minimum(large, nb - 1)
    return ret + jnp.where(n < max_exact, n, large)


def _bias_from_buckets(bkt, rb_ref, h, shift):
    val = jnp.where(bkt < 0, NEG, 0.0).astype(F32)
    for n in range(N_BUCKETS):
        val = jnp.where(bkt == n, (rb_ref[n, h] - shift) * LOG2E, val)
    return val


def _lam_value(lv_ref, lam_init):
    lv = lv_ref[...]
    a = jnp.sum(lv[0:1] * lv[1:2], axis=-1, keepdims=True)
    b = jnp.sum(lv[2:3] * lv[3:4], axis=-1, keepdims=True)
    return jnp.exp(a) - jnp.exp(b) + lam_init


def _block_diag_q(q):
    lane = lax.broadcasted_iota(jnp.int32, q.shape, 1)
    zero = jnp.zeros_like(q)
    return jnp.concatenate([jnp.where(lane < DA, q, zero), jnp.where(lane >= DA, q, zero)], axis=0)


def _attn_kernel(rb_ref, lv_ref, q_ref, k_ref, vt_ref, bkt_ref, o_ref,
                 bias_sc, m_sc, acc_sc, va_sc, *, tq, nsub, lam_init):
    h = pl.program_id(1)
    i = pl.program_id(2)

    @pl.when(i == 0)
    def _():
        far = rb_ref[N_BUCKETS // 2 - 1, h]
        for t in range(2):
            bias_sc[t] = _bias_from_buckets(bkt_ref[t], rb_ref, h, far)
        va_sc[0:DVA, :] = vt_ref[...]
        va_sc[DVA:, :] = jnp.ones((BF16_ROWS, va_sc.shape[1]), BF16)

    qt = q_ref[...].astype(F32).T
    drow = lax.broadcasted_iota(jnp.int32, qt.shape, 0)
    qbd_t = jnp.concatenate([jnp.where(drow < DA, qt, 0.0), jnp.where(drow >= DA, qt, 0.0)],
                            axis=1).astype(BF16)
    m_sc[...] = jnp.full(m_sc.shape, NEG, F32)
    acc_sc[...] = jnp.zeros(acc_sc.shape, F32)

    def scores(off, tk, bias=None):
        s = _dot(k_ref[pl.ds(off, tk), :], qbd_t)
        if bias is not None:
            s = s + jnp.concatenate([bias, bias], axis=1)
        return s

    def absorb(s, off, tk):
        m_prev = m_sc[...]
        m_new = jnp.maximum(m_prev, jnp.max(s, axis=0, keepdims=True))
        alpha = jnp.exp2(m_prev - m_new)
        p = jnp.exp2((s - m_new).astype(BF16))
        acc_sc[...] = alpha * acc_sc[...] + _dot(va_sc[:, pl.ds(off, tk)], p)
        m_sc[...] = m_new

    def tiles(specs):
        ss = [scores(*sp) for sp in specs]
        for s, sp in zip(ss, specs):
            absorb(s, sp[0], sp[1])

    ntile = jnp.maximum(i - 1, 0)
    nfull = ntile // nsub

    def far(first_tile, n):
        return [(pl.multiple_of((first_tile + j) * tq, tq), tq) for j in range(n)]

    def far_body(t, carry):
        tiles(far(t * nsub, nsub))
        return carry

    lax.fori_loop(0, nfull, far_body, 0)
    done = nfull * nsub
    n = nsub // 2
    while n >= ATTN_TAIL:
        take = ((ntile - done) // n) > 0

        @pl.when(take)
        def _(done=done, n=n):
            tiles(far(done, n))

        done = done + jnp.where(take, n, 0)
        n //= 2

    rest = ntile - done
    for r in range(ATTN_TAIL):
        @pl.when((i >= 1) & (rest == r))
        def _(r=r):
            off = pl.multiple_of((i - 1) * tq, tq)
            tiles(far(done, r) + [(off, tq, bias_sc[1]),
                                  (pl.multiple_of(off + tq, tq), tq, bias_sc[0])])

    @pl.when(i == 0)
    def _():
        tiles([(0, tq, bias_sc[0])])

    lam = _lam_value(lv_ref, lam_init)
    o = acc_sc[0:DVA, :] / acc_sc[DVA:DVA + 1, :]
    o_ref[...] = (o[:, 0:tq] - lam * o[:, tq:2 * tq]).T


def attn_prompt(qab, kab, vt, rel_bias, lamv, lam_init, tq=ATTN_TQ, nsub=ATTN_GROUP):
    B, S, _ = qab.shape
    assert S % tq == 0
    c = jnp.arange(tq, dtype=jnp.int32)[:, None]
    r = jnp.arange(tq, dtype=jnp.int32)[None, :]
    diag = jnp.where((c // CHUNK) <= (r // CHUNK), _t5_bucket(c - r), -1)
    prev = _t5_bucket(c - r - tq)
    bkt = jnp.stack([diag, prev]).astype(jnp.int32)
    smem = pl.BlockSpec(memory_space=pltpu.SMEM)
    return pl.pallas_call(
        functools.partial(_attn_kernel, tq=tq, nsub=nsub, lam_init=lam_init),
        grid=(B, HA, S // tq),
        in_specs=[smem,
                  pl.BlockSpec((4, DA), lambda b, h, i: (0, 0)),
                  pl.BlockSpec((None, tq, DVA), lambda b, h, i: (b, i, h)),
                  pl.BlockSpec((None, S, DVA), lambda b, h, i: (b, 0, h)),
                  pl.BlockSpec((None, DVA, S), lambda b, h, i: (b, h, 0)),
                  pl.BlockSpec((2, tq, tq), lambda b, h, i: (0, 0, 0))],
        out_specs=pl.BlockSpec((None, tq, DVA), lambda b, h, i: (b, i, h)),
        out_shape=jax.ShapeDtypeStruct((B, S, WA), F32),
        scratch_shapes=[pltpu.VMEM((2, tq, tq), F32),
                        pltpu.VMEM((1, 2 * tq), F32),
                        pltpu.VMEM((DVA + BF16_ROWS, 2 * tq), F32),
                        pltpu.VMEM((DVA + BF16_ROWS, S), BF16)],
        compiler_params=_params(("arbitrary", "arbitrary", "arbitrary")),
        name="attn_prompt",
    )(rel_bias, lamv, qab, kab, vt, bkt)


def _attn_small_kernel(rb_ref, lv_ref, q_ref, k_ref, v_ref, bkt_ref, o_ref, *, lq, lam_init):
    h = pl.program_id(1)
    bias = _bias_from_buckets(bkt_ref[...], rb_ref, h, 0.0)
    qbd = _block_diag_q(q_ref[...])
    s = _dot_nt(qbd, k_ref[...]) + jnp.concatenate([bias, bias], axis=0)
    m = jnp.max(s, axis=-1, keepdims=True)
    p = jnp.exp2(s - m)
    l = jnp.sum(p, axis=-1, keepdims=True)
    o = _dot(p.astype(BF16), v_ref[...]) / l
    lam = _lam_value(lv_ref, lam_init)
    o_ref[...] = o[0:lq] - lam * o[lq:2 * lq]


def attn_small(qab, k_all, v_all, q_pos, k_pos, k_valid, rel_bias, lamv, lam_init):
    B, Lq, _ = qab.shape
    Lk = k_all.shape[1]
    visible = ((k_pos[None, :] // CHUNK) <= (q_pos[:, None] // CHUNK)) & k_valid[None, :]
    bkt = jnp.where(visible, _t5_bucket(k_pos[None, :] - q_pos[:, None]), -1).astype(jnp.int32)
    smem = pl.BlockSpec(memory_space=pltpu.SMEM)
    return pl.pallas_call(
        functools.partial(_attn_small_kernel, lq=Lq, lam_init=lam_init),
        grid=(B, HA),
        in_specs=[smem,
                  pl.BlockSpec((4, DA), lambda b, h: (0, 0)),
                  pl.BlockSpec((None, Lq, DVA), lambda b, h: (b, 0, h)),
                  pl.BlockSpec((None, Lk, DVA), lambda b, h: (b, 0, h)),
                  pl.BlockSpec((None, Lk, DVA), lambda b, h: (b, 0, h)),
                  pl.BlockSpec((Lq, Lk), lambda b, h: (0, 0))],
        out_specs=pl.BlockSpec((None, Lq, DVA), lambda b, h: (b, 0, h)),
        out_shape=jax.ShapeDtypeStruct((B, Lq, WA), F32),
        compiler_params=_params(("arbitrary", "arbitrary")),
        name="attn_sample",
    )(rel_bias, lamv, qab, k_all, v_all, bkt)


def _ret_kernel(lg_ref, q_ref, k_ref, v_ref, s0_ref, o_ref, so_ref, state_sc, decay_sc, *, C):
    c = pl.program_id(1)

    @pl.when(c == 0)
    def _():
        state_sc[...] = s0_ref[...]
        r = lax.broadcasted_iota(jnp.int32, (C, C), 0)
        cc = lax.broadcasted_iota(jnp.int32, (C, C), 1)
        diff = (r - cc).astype(F32)
        for h in range(HR):
            decay_sc[h] = jnp.where(diff >= 0, jnp.exp(jnp.maximum(diff, 0.0) * lg_ref[h]), 0.0)

    n = lax.broadcasted_iota(jnp.int32, (C, 1), 0).astype(F32)
    for h in range(HR):
        lg = lg_ref[h]
        cols = slice(h * DKR, (h + 1) * DKR)
        xi = jnp.exp((n + 1.0) * lg)
        zeta = jnp.exp((C - 1.0 - n) * lg)
        q = q_ref[:, cols]
        k = k_ref[:, cols]
        v = v_ref[:, cols]
        state = state_sc[h]
        scores = _dot_nt(q, k.astype(BF16)) * decay_sc[h]
        intra = _dot(scores.astype(BF16), v)
        cross = _dot(q, state.astype(BF16)) * xi
        o_ref[:, cols] = intra + cross
        kz = (k * zeta).astype(BF16)
        state_sc[h] = jnp.exp(C * lg) * state + _dot_tn(kz, v)

    @pl.when(c == pl.num_programs(1) - 1)
    def _():
        so_ref[...] = state_sc[...]


def retention(qrb, kr, vrb, state0, C):
    B, L, _ = qrb.shape
    lg = jnp.log(1.0 - 2.0 ** (-5.0 - jnp.arange(HR, dtype=F32)))
    blk = pl.BlockSpec((None, C, WR), lambda b, c: (b, c, 0))
    st = pl.BlockSpec((None, HR, DKR, DVR), lambda b, c: (b, 0, 0, 0))
    return pl.pallas_call(
        functools.partial(_ret_kernel, C=C),
        grid=(B, L // C),
        in_specs=[pl.BlockSpec(memory_space=pltpu.SMEM), blk, blk, blk, st],
        out_specs=[blk, st],
        out_shape=[jax.ShapeDtypeStruct((B, L, WR), F32),
                   jax.ShapeDtypeStruct((B, HR, DKR, DVR), F32)],
        scratch_shapes=[pltpu.VMEM((HR, DKR, DVR), F32), pltpu.VMEM((HR, C, C), F32)],
        compiler_params=_params(("arbitrary", "arbitrary")),
        name="retention",
    )(lg, qrb, kr, vrb, state0)


def _outmix_kernel(x_ref, oa_ref, or_ref, gr_ref, ga_ref, gb_ref, g1_ref, sh2_ref, sc2_ref, n2_ref,
                   sa_ref, sr_ref, wba_ref, wbr_ref, wo_ref, wpq_ref,
                   x1_ref, h2_ref, qp_ref, *, lam_init, h2_transposed):
    sa = sa_ref[...]
    sr = sr_ref[...]
    gr = gr_ref[...]
    silu_gr = gr * jax.nn.sigmoid(gr)
    ya_parts, yr_parts = [], []
    for hh in range(HA):
        sl = slice(hh * DVA, (hh + 1) * DVA)
        ya_parts.append(_rms_rows(oa_ref[:, sl]) * sa * (1.0 - lam_init))
        yr_parts.append(silu_gr[:, sl] * (_rms_rows(or_ref[:, sl]) * sr))
    ya = jnp.concatenate(ya_parts, axis=-1).astype(BF16)
    yr = jnp.concatenate(yr_parts, axis=-1).astype(BF16)
    y = (jax.nn.sigmoid(ga_ref[...]) * _dot(ya, wba_ref[...])
         + jax.nn.sigmoid(gb_ref[...]) * _dot(yr, wbr_ref[...]))
    out = _dot(y.astype(BF16), wo_ref[...])
    x1 = x_ref[...] + g1_ref[...] * out
    x1_ref[...] = x1
    h2f = _rms_rows(x1) * n2_ref[...] * (1.0 + sc2_ref[...]) + sh2_ref[...]
    h2 = h2f.astype(BF16)
    h2_ref[...] = h2f.T.astype(BF16) if h2_transposed else h2
    qp_ref[...] = _dot(h2, wpq_ref[...]).astype(BF16)


def out_mix(x, oa, orr, gr, ga, gb, g1, sh2, sc2, norm2, subln_a, subln_r,
            w_ba_b, w_br_b, w_o_b, w_pq_b, lam_init, tm):
    B, L, D = x.shape
    nq = w_pq_b.shape[1]
    row = lambda n: pl.BlockSpec((None, tm, n), lambda b, i: (b, i, 0))
    mod = pl.BlockSpec((None, 1, D), lambda b, i: (b, 0, 0))
    full = lambda a: pl.BlockSpec(a.shape, lambda b, i: (0,) * a.ndim, pipeline_mode=pl.Buffered(1))
    n2 = norm2.reshape(1, D)
    sa = subln_a.reshape(1, DVA)
    sr = subln_r.reshape(1, DVR)
    h2_transposed = tm % LANES == 0
    nt = L // tm
    if h2_transposed:
        h2_spec = pl.BlockSpec((D, tm), lambda b, i: (0, b * nt + i))
        h2_shape = jax.ShapeDtypeStruct((D, B * L), BF16)
    else:
        h2_spec = row(D)
        h2_shape = jax.ShapeDtypeStruct((B, L, D), BF16)
    x1, h2, qp = pl.pallas_call(
        functools.partial(_outmix_kernel, lam_init=lam_init, h2_transposed=h2_transposed),
        grid=(B, nt),
        in_specs=[row(D), row(WA), row(WR), row(WR), row(D), row(D), mod, mod, mod,
                  full(n2), full(sa), full(sr), full(w_ba_b), full(w_br_b), full(w_o_b), full(w_pq_b)],
        out_specs=[row(D), h2_spec, row(nq)],
        out_shape=[jax.ShapeDtypeStruct((B, L, D), F32), h2_shape,
                   jax.ShapeDtypeStruct((B, L, nq), BF16)],
        compiler_params=_params(("arbitrary", "arbitrary")),
        name="out_mix",
    )(x, oa, orr, gr, ga, gb, g1.reshape(B, 1, D), sh2.reshape(B, 1, D), sc2.reshape(B, 1, D),
      n2, sa, sr, w_ba_b, w_br_b, w_o_b, w_pq_b)
    h2t = h2 if h2_transposed else h2.reshape(B * L, D).T
    return x1, h2t, qp


UNRANKED = float(PEER_TOPK * PEER_TOPK)


def _topk_rows(s, k, break_ties, want_rank=True):
    n = s.shape[0]
    iota = lax.broadcasted_iota(jnp.int32, s.shape, 0).astype(F32)
    work = s
    rank = jnp.full(s.shape, UNRANKED, F32) if want_rank else None
    vals = []
    for r in range(k):
        m = jnp.max(work, axis=0, keepdims=True)
        sel = work == m
        if break_ties:
            idx = jnp.min(jnp.where(sel, iota, float(n)), axis=0, keepdims=True)
            sel = iota == idx
        if want_rank:
            rank = jnp.where(sel, float(r), rank)
        work = jnp.where(sel, -jnp.inf, work)
        vals.append(m)
    return vals, rank


def _count_rows(mask):
    return jnp.sum(mask.astype(F32), axis=0, keepdims=True)


_HEAD_A = SUBLANES
_CAND_NB = [PEER_TOPK] + [SUBLANES] * (_HEAD_A - 1)


def _route_chunk(s1, s2, break_ties):
    k = PEER_TOPK
    v1, rank1 = _topk_rows(s1, k, break_ties, want_rank=break_ties)
    v2, rank2 = _topk_rows(s2, k, break_ties)
    v2m = jnp.concatenate(v2, axis=0)
    v1t = jnp.concatenate(v1[_HEAD_A:], axis=0)
    blocks = [v1[a] + v2m[0:nb] for a, nb in enumerate(_CAND_NB)] + [v1t + v2[0]]
    cand = jnp.concatenate(blocks, axis=0)
    vc, crank = _topk_rows(cand, k, break_ties, want_rank=break_ties)
    if break_ties:
        sel = crank < UNRANKED
        row_is = lambda a: rank1 == float(a)
        nbad = jnp.zeros((), jnp.int32)
    else:
        sel = cand >= vc[k - 1]
        row_is = lambda a: s1 == v1[a]
        bad = ((_count_rows(s1 >= v1[k - 1]) != float(k))
               | (_count_rows(rank2 < UNRANKED) != float(k)) | (_count_rows(sel) != float(k)))
        nbad = jnp.sum(bad.astype(jnp.int32))
    cmax = v1[0] + v2[0]
    z = jnp.sum(jnp.where(sel, jnp.exp(cand - cmax), 0.0), axis=0, keepdims=True)
    self32 = sel.astype(F32)
    cidx = jnp.zeros(s1.shape, F32)
    lo = 0
    for a, nb in enumerate(_CAND_NB):
        cnt = jnp.sum(self32[lo:lo + nb], axis=0, keepdims=True)
        cidx = jnp.where(row_is(a), cnt, cidx)
        lo += nb
    for a in range(_HEAD_A, k):
        cidx = jnp.where(row_is(a), self32[lo + a - _HEAD_A:lo + a - _HEAD_A + 1], cidx)
    e1 = jnp.exp(s1 - v1[0]) / z
    e2 = jnp.exp(s2 - v2[0])
    return (e1, e2, rank2, cidx), nbad


def _route_kernel(q_ref, keys_ref, e1_ref, e2_ref, r2_ref, c_ref, s1_sc, s2_sc):
    q = q_ref[...]
    s1_sc[...] = _dot_nt(keys_ref[0], q[:, 0:KEY_DIM])
    s2_sc[...] = _dot_nt(keys_ref[1], q[:, KEY_DIM:2 * KEY_DIM])

    def store(sl, res):
        e1, e2, rank2, cidx = res
        e1_ref[:, sl] = e1
        e2_ref[:, sl] = e2.astype(BF16)
        r2_ref[:, sl] = rank2.astype(BF16)
        c_ref[:, sl] = cidx

    group = math.gcd(ROUTE_CHUNKS, s1_sc.shape[1] // LANES)

    def chunks(ci, carry):
        sls = [pl.ds(pl.multiple_of((ci * group + g) * LANES, LANES), LANES) for g in range(group)]
        fast = [_route_chunk(s1_sc[:, sl], s2_sc[:, sl], break_ties=False) for sl in sls]
        for sl, (res, _) in zip(sls, fast):
            store(sl, res)
        for sl, (_, nbad) in zip(sls, fast):
            @pl.when(nbad > 0)
            def _():
                store(sl, _route_chunk(s1_sc[:, sl], s2_sc[:, sl], break_ties=True)[0])
        return carry

    lax.fori_loop(0, s1_sc.shape[1] // (group * LANES), chunks, 0)


def peer_route(qp, keys_b, tm):
    T = qp.shape[0]
    out = pl.BlockSpec((None, N_KEYS, tm), lambda t, h: (h, 0, t))
    shp = lambda dt: jax.ShapeDtypeStruct((PEER_HEADS, N_KEYS, T), dt)
    return pl.pallas_call(
        _route_kernel,
        grid=(T // tm, PEER_HEADS),
        in_specs=[pl.BlockSpec((tm, 2 * KEY_DIM), lambda t, h: (t, h)),
                  pl.BlockSpec((2, N_KEYS, KEY_DIM), lambda t, h: (h, 0, 0))],
        out_specs=[out, out, out, out],
        out_shape=[shp(F32), shp(BF16), shp(BF16), shp(F32)],
        scratch_shapes=[pltpu.VMEM((N_KEYS, tm), F32), pltpu.VMEM((N_KEYS, tm), F32)],
        compiler_params=_params(("arbitrary", "arbitrary")),
        name="peer_route",
    )(qp, keys_b)


def _bcast_rows_bf16(row, n):
    t = row.shape[1]
    tile = jnp.broadcast_to(row, (BF16_ROWS, t)).astype(BF16)
    return jnp.broadcast_to(tile[None], (n // BF16_ROWS, BF16_ROWS, t)).reshape(n, t)


def _peer_kernel(h2_ref, u_ref, vt_ref, e1_ref, e2_ref, r2_ref, c_ref, x1_ref, g2_ref, fn_ref,
                 y_ref, acc_sc, *, ni):
    e = pl.program_id(1)

    @pl.when(e == 0)
    def _():
        acc_sc[...] = jnp.zeros(acc_sc.shape, F32)

    h2 = h2_ref[...]
    pair = PEER_SUB * N_KEYS
    a_pairs = [_dot(u_ref[p * pair:(p + 1) * pair, :], h2) for p in range(ni // PEER_SUB)]
    for p in range(ni // PEER_SUB):
        gs = [jnp.zeros((N_KEYS, h2.shape[1]), BF16) for _ in range(PEER_SUB)]
        for hh in range(PEER_HEADS):
            r2h = r2_ref[hh]
            e2h = e2_ref[hh]
            for jj in range(PEER_SUB):
                i = PEER_SUB * p + jj
                c_row = _bcast_rows_bf16(c_ref[hh, pl.ds(i, 1), :], N_KEYS)
                e1_row = _bcast_rows_bf16(e1_ref[hh, pl.ds(i, 1), :], N_KEYS)
                gs[jj] = gs[jj] + jnp.where(r2h < c_row, e2h * e1_row, jnp.zeros_like(e2h))
        ws = []
        for jj in range(PEER_SUB):
            a = a_pairs[p][jj * N_KEYS:(jj + 1) * N_KEYS]
            act = 0.5 * a * (1.0 + lax.erf(a * (2.0 ** -0.5)))
            ws.append(gs[jj] * act.astype(BF16))
        w = jnp.concatenate(ws, axis=0)
        acc_sc[...] += _dot(vt_ref[:, p * pair:(p + 1) * pair], w)

    @pl.when(e == pl.num_programs(1) - 1)
    def _():
        x2 = x1_ref[...] + g2_ref[...] * acc_sc[...].T
        y_ref[...] = _rms_rows(x2) * fn_ref[...]


def peer_experts(h2t, u_b, vt_b, e1, e2, r2, cc, x1, g2, final_norm, tm, ni=EXPERT_ROWS):
    D, T = h2t.shape
    nb = ni * N_KEYS
    tok = pl.BlockSpec((tm, D), lambda t, e: (t, 0))
    gate = pl.BlockSpec((PEER_HEADS, N_KEYS, tm), lambda t, e: (0, 0, t))
    rows = pl.BlockSpec((PEER_HEADS, ni, tm), lambda t, e: (0, e, t))
    L = T // g2.shape[0]
    if L % tm == 0:
        g2_arg = g2.reshape(g2.shape[0], 1, D)
        g2_spec = pl.BlockSpec((None, 1, D), lambda t, e: (t // (L // tm), 0, 0))
    else:
        g2_arg = jnp.repeat(g2, L, axis=0)
        g2_spec = tok
    return pl.pallas_call(
        functools.partial(_peer_kernel, ni=ni),
        grid=(T // tm, u_b.shape[0] // nb),
        in_specs=[pl.BlockSpec((D, tm), lambda t, e: (0, t)),
                  pl.BlockSpec((nb, D), lambda t, e: (e, 0)),
                  pl.BlockSpec((D, nb), lambda t, e: (0, e)),
                  rows, gate, gate, rows, tok, g2_spec,
                  pl.BlockSpec((1, D), lambda t, e: (0, 0))],
        out_specs=tok,
        out_shape=jax.ShapeDtypeStruct((T, D), F32),
        scratch_shapes=[pltpu.VMEM((D, tm), F32)],
        compiler_params=_params(("arbitrary", "arbitrary")),
        name="peer_experts",
    )(h2t, u_b, vt_b, e1, e2, r2, cc, x1, g2_arg, final_norm.reshape(1, D))


def _rot_tables(pos):
    inv = 1.0 / (10000.0 ** jnp.linspace(0.0, 1.0, DKR // 2, dtype=F32))
    ang = pos[:, None].astype(F32) * inv[None, :]
    cos, sin = jnp.cos(ang), jnp.sin(ang)
    return jnp.concatenate([cos, cos], axis=-1), jnp.concatenate([-sin, sin], axis=-1)


def _pick_tile(n, pref):
    t = min(n, pref)
    assert n % t == 0, (n, t)
    return t


def _trunk(x, mods, pos, lam_init, w, attend, values_transposed, state0, final_norm):
    (norm1, norm2, w_in_b, subln_a, subln_r, w_ba_b, w_br_b, w_o_b, w_pq_b, keys_b, u_b, vt_b) = w
    sh1, sc1, g1, sh2, sc2, g2 = mods
    B, L, D = x.shape
    T = B * L
    cos2, sin2 = _rot_tables(pos)
    tm = _pick_tile(L, TOKEN_TILE)
    ka, va, qab, kab, vab, qrb, kr, vrb, gr, ga, gb, *vt = in_proj(
        x, sh1, sc1, norm1, w_in_b, cos2, sin2, tm, emit_vt=values_transposed)
    oa = attend(qab, kab, vt[0] if values_transposed else vab)
    orr, st = retention(qrb, kr, vrb, state0, _pick_tile(L, RET_CHUNK))
    x1, h2t, qp = out_mix(x, oa, orr, gr, ga, gb, g1, sh2, sc2, norm2, subln_a, subln_r,
                          w_ba_b, w_br_b, w_o_b, w_pq_b, lam_init, tm)
    e1, e2, r2, cc = peer_route(qp.reshape(T, -1), keys_b, _pick_tile(T, ROUTE_TILE))
    y = peer_experts(h2t, u_b, vt_b, e1, e2, r2, cc, x1.reshape(T, D), g2,
                     final_norm, _pick_tile(T, EXPERT_TILE))
    return y.reshape(B, L, D), ka, va, st


def kernel(x_prompt, x_sample, cache_k, cache_v, state_ret, c_prompt, c_sample, w_ada, b_ada, norm1,
           norm2, w_in, lam_q1, lam_k1, lam_q2, lam_k2, subln_a, subln_r, w_ba, w_br, w_o, rel_bias,
           w_pq, peer_keys, peer_u, peer_v, final_norm):
    depth = w_ada.shape[0]
    assert depth == 1, "the fused final norm assumes a single layer"
    Bp, Lp, D = x_prompt.shape
    Bs, Ls, _ = x_sample.shape
    past = cache_k.shape[2]
    pos_p = jnp.arange(Lp, dtype=jnp.int32)
    pos_s = past + jnp.arange(Ls, dtype=jnp.int32)
    lk = past + Ls
    lk_pad = -(-lk // LANES) * LANES
    k_pos_s = jnp.arange(lk_pad, dtype=jnp.int32)
    k_valid_s = k_pos_s < lk

    l = 0
    lam_init = 0.8 - 0.6 * math.exp(-0.3 * l)
    lamv = jnp.stack([lam_q1[l], lam_k1[l], lam_q2[l], lam_k2[l]]).astype(F32)
    mod = ada_mod(jnp.concatenate([c_prompt, c_sample], axis=0), w_ada[l], b_ada[l])
    mods = jnp.split(mod, 6, axis=-1)
    mods_p = [m[:Bp] for m in mods]
    mods_s = [m[Bp:] for m in mods]
    w = (norm1[l], norm2[l], w_in[l].astype(BF16), subln_a[l], subln_r[l], w_ba[l].astype(BF16),
         w_br[l].astype(BF16), w_o[l].astype(BF16), w_pq[l].astype(BF16),
         peer_keys[l].reshape(PEER_HEADS * 2, N_KEYS, KEY_DIM).astype(BF16),
         peer_u[l].astype(BF16), peer_v[l].T.astype(BF16))

    attend_p = lambda q, k, v: attn_prompt(q, k, v, rel_bias, lamv, lam_init)
    zero_state = jnp.zeros((Bp, HR, DKR, DVR), F32)
    yp, kp, vp, sp = _trunk(x_prompt, mods_p, pos_p, lam_init, w, attend_p, True, zero_state,
                            final_norm)

    def attend_s(q, k, v):
        padk = jnp.zeros((Bs, lk_pad - lk, WA), BF16)
        kc = cache_k[l].reshape(Bs, past, HA * 2 * DA).astype(BF16)
        vc = cache_v[l].reshape(Bs, past, WA).astype(BF16)
        k_all = jnp.concatenate([kc, k, padk], axis=1)
        v_all = jnp.concatenate([vc, v, padk], axis=1)
        return attn_small(q, k_all, v_all, pos_s, k_pos_s, k_valid_s, rel_bias, lamv, lam_init)

    ys, ks, vs, ss = _trunk(x_sample, mods_s, pos_s, lam_init, w, attend_s, False,
                            state_ret[l].astype(F32), final_norm)
    return (yp, ys, kp[None], vp[None], sp[None], ks[None], vs[None], ss[None])
```

```python
import functools
import math

import jax
import jax.numpy as jnp
from jax import lax
from jax.experimental import pallas as pl
from jax.experimental.pallas import tpu as pltpu

F32 = jnp.float32
BF16 = jnp.bfloat16

CHUNK = 64
HA = 4
DA = 64
DVA = 2 * DA
HR = 4
DKR = 128
DVR = 128
N_BUCKETS = 32
MAX_DIST = 128
PEER_HEADS = 8
N_KEYS = 128
KEY_DIM = 128
PEER_TOPK = 16
EPS = 1e-6
WA = HA * DVA
WR = HR * DVR
NEG = -1e30
LOG2E = math.log2(math.e)
LANES = 128
SUBLANES = 8
BF16_ROWS = 2 * SUBLANES
VMEM_LIMIT = 60 * 1024 * 1024

TOKEN_TILE = 512
RET_CHUNK = 256
ATTN_TQ = 256
ATTN_GROUP = 32
ATTN_TAIL = 8
ROUTE_TILE = 1024
ROUTE_CHUNKS = 8
EXPERT_TILE = 512
EXPERT_ROWS = 32
PEER_SUB = 2


def _params(sem, vmem=VMEM_LIMIT):
    return pltpu.CompilerParams(dimension_semantics=sem, vmem_limit_bytes=vmem)


def _dot(a, b):
    return jnp.dot(a, b, preferred_element_type=F32)


def _dot_nt(a, b):
    return lax.dot_general(a, b, (((1,), (1,)), ((), ())), preferred_element_type=F32)


def _dot_tn(a, b):
    return lax.dot_general(a, b, (((0,), (0,)), ((), ())), preferred_element_type=F32)


def _rms_rows(x):
    return x * lax.rsqrt(jnp.mean(x * x, axis=-1, keepdims=True) + EPS)


def _ada_kernel(c_ref, w_ref, b_ref, o_ref):
    c = c_ref[...]
    s = (c * jax.nn.sigmoid(c)).astype(BF16)
    o_ref[...] = _dot(s, w_ref[...].astype(BF16)) + b_ref[...]


def ada_mod(c, w_ada, b_ada, tn=1024):
    nb, d = c.shape
    n = w_ada.shape[1]
    return pl.pallas_call(
        _ada_kernel,
        grid=(n // tn,),
        in_specs=[pl.BlockSpec((nb, d), lambda j: (0, 0)),
                  pl.BlockSpec((d, tn), lambda j: (0, j)),
                  pl.BlockSpec((1, tn), lambda j: (0, j))],
        out_specs=pl.BlockSpec((nb, tn), lambda j: (0, j)),
        out_shape=jax.ShapeDtypeStruct((nb, n), F32),
        compiler_params=_params(("arbitrary",)),
        name="ada_mod",
    )(c, w_ada, b_ada.reshape(1, n))


def _inproj_kernel(x_ref, sh_ref, sc_ref, n1_ref, w_ref, cos_ref, sin_ref,
                   ka_ref, va_ref, qab_ref, kab_ref, vab_ref, qrb_ref, kr_ref, vrb_ref,
                   gr_ref, ga_ref, gb_ref, vt_ref=None):
    x = x_ref[...]
    h = _rms_rows(x) * n1_ref[...] * (1.0 + sc_ref[...]) + sh_ref[...]
    hb = h.astype(BF16)

    d_model = x.shape[1]
    sizes = (WA, WA, WA, WR, WR, WR, WR, d_model, d_model)
    starts = [sum(sizes[:g]) for g in range(len(sizes))]

    def proj(g):
        return _dot(hb, w_ref[:, starts[g]:starts[g] + sizes[g]])

    qa = proj(0)
    qab_ref[...] = (qa * (DA ** -0.5 * LOG2E)).astype(BF16)
    ka = proj(1)
    va = proj(2)
    for hh in range(HA):
        ka_ref[:, hh, :] = ka[:, hh * DVA:(hh + 1) * DVA]
        va_ref[:, hh, :] = va[:, hh * DVA:(hh + 1) * DVA]
    kab_ref[...] = ka.astype(BF16)
    vab_ref[...] = va.astype(BF16)
    if vt_ref is not None:
        vt_ref[...] = va.T.astype(BF16)
    cos2 = cos_ref[...]
    sin2 = sin_ref[...]

    def rot(z):
        parts = []
        for hh in range(HR):
            zh = z[:, hh * DKR:(hh + 1) * DKR]
            parts.append(zh * cos2 + pltpu.roll(zh, DKR // 2, 1) * sin2)
        return jnp.concatenate(parts, axis=-1)

    qrb_ref[...] = rot(proj(3)).astype(BF16)
    kr_ref[...] = rot(proj(4)) * (DKR ** -0.5)
    vrb_ref[...] = proj(5).astype(BF16)
    gr_ref[...] = proj(6)
    ga_ref[...] = proj(7)
    gb_ref[...] = proj(8)


def in_proj(x, sh1, sc1, norm1, w_in_b, cos2, sin2, tm, emit_vt):
    B, L, D = x.shape
    d_in = w_in_b.shape[1]
    row = lambda n: pl.BlockSpec((None, tm, n), lambda b, i: (b, i, 0))
    mod = pl.BlockSpec((None, 1, D), lambda b, i: (b, 0, 0))
    f = lambda n, dt: jax.ShapeDtypeStruct((B, L, n), dt)
    cache = pl.BlockSpec((None, tm, HA, DVA), lambda b, i: (b, i, 0, 0))
    cache_shape = jax.ShapeDtypeStruct((B, L, HA, DVA), F32)
    vt_spec = [pl.BlockSpec((None, WA, tm), lambda b, i: (b, 0, i))] if emit_vt else []
    vt_shape = [jax.ShapeDtypeStruct((B, WA, L), BF16)] if emit_vt else []
    return pl.pallas_call(
        _inproj_kernel,
        grid=(B, L // tm),
        in_specs=[row(D), mod, mod,
                  pl.BlockSpec((1, D), lambda b, i: (0, 0)),
                  pl.BlockSpec((D, d_in), lambda b, i: (0, 0), pipeline_mode=pl.Buffered(1)),
                  pl.BlockSpec((tm, DKR), lambda b, i: (i, 0)),
                  pl.BlockSpec((tm, DKR), lambda b, i: (i, 0))],
        out_specs=[cache, cache, row(WA), row(WA), row(WA), row(WR), row(WR), row(WR),
                   row(WR), row(D), row(D)] + vt_spec,
        out_shape=[cache_shape, cache_shape, f(WA, BF16), f(WA, BF16), f(WA, BF16),
                   f(WR, BF16), f(WR, F32), f(WR, BF16), f(WR, F32), f(D, F32), f(D, F32)]
        + vt_shape,
        compiler_params=_params(("arbitrary", "arbitrary")),
        name="in_proj",
    )(x, sh1.reshape(B, 1, D), sc1.reshape(B, 1, D), norm1.reshape(1, D), w_in_b, cos2, sin2)


def _t5_bucket(rel):
    nb = N_BUCKETS // 2
    ret = jnp.where(rel > 0, nb, 0)
    n = jnp.abs(rel)
    max_exact = nb // 2
    nf = jnp.maximum(n, max_exact).astype(F32)
    large = max_exact + (jnp.log(nf / max_exact) / math.log(MAX_DIST / max_exact)
                         * (nb - max_exact)).astype(jnp.int32)
    large = jnp.minimum(large, nb - 1)
    return ret + jnp.where(n < max_exact, n, large)


def _bias_from_buckets(bkt, rb_ref, h, shift):
    val = jnp.where(bkt < 0, NEG, 0.0).astype(F32)
    for n in range(N_BUCKETS):
        val = jnp.where(bkt == n, (rb_ref[n, h] - shift) * LOG2E, val)
    return val


def _lam_value(lv_ref, lam_init):
    lv = lv_ref[...]
    a = jnp.sum(lv[0:1] * lv[1:2], axis=-1, keepdims=True)
    b = jnp.sum(lv[2:3] * lv[3:4], axis=-1, keepdims=True)
    return jnp.exp(a) - jnp.exp(b) + lam_init


def _block_diag_q(q):
    lane = lax.broadcasted_iota(jnp.int32, q.shape, 1)
    zero = jnp.zeros_like(q)
    return jnp.concatenate([jnp.where(lane < DA, q, zero), jnp.where(lane >= DA, q, zero)], axis=0)


def _attn_kernel(rb_ref, lv_ref, q_ref, k_ref, vt_ref, bkt_ref, o_ref,
                 bias_sc, m_sc, acc_sc, va_sc, *, tq, nsub, lam_init):
    h = pl.program_id(1)
    i = pl.program_id(2)

    @pl.when(i == 0)
    def _():
        far = rb_ref[N_BUCKETS // 2 - 1, h]
        for t in range(2):
            bias_sc[t] = _bias_from_buckets(bkt_ref[t], rb_ref, h, far)
        va_sc[0:DVA, :] = vt_ref[...]
        va_sc[DVA:, :] = jnp.ones((BF16_ROWS, va_sc.shape[1]), BF16)

    qt = q_ref[...].astype(F32).T
    drow = lax.broadcasted_iota(jnp.int32, qt.shape, 0)
    qbd_t = jnp.concatenate([jnp.where(drow < DA, qt, 0.0), jnp.where(drow >= DA, qt, 0.0)],
                            axis=1).astype(BF16)
    m_sc[...] = jnp.full(m_sc.shape, NEG, F32)
    acc_sc[...] = jnp.zeros(acc_sc.shape, F32)

    def scores(off, tk, bias=None):
        s = _dot(k_ref[pl.ds(off, tk), :], qbd_t)
        if bias is not None:
            s = s + jnp.concatenate([bias, bias], axis=1)
        return s

    def absorb(s, off, tk):
        m_prev = m_sc[...]
        m_new = jnp.maximum(m_prev, jnp.max(s, axis=0, keepdims=True))
        alpha = jnp.exp2(m_prev - m_new)
        p = jnp.exp2(s - m_new)
        acc_sc[DVA:DVA + 1, :] = alpha * acc_sc[DVA:DVA + 1, :] + jnp.sum(p, axis=0, keepdims=True)
        acc_sc[0:DVA, :] = alpha * acc_sc[0:DVA, :] + _dot(vt_ref[:, pl.ds(off, tk)], p.astype(BF16))
        m_sc[...] = m_new

    def tiles(specs):
        ss = [scores(*sp) for sp in specs]
        for s, sp in zip(ss, specs):
            absorb(s, sp[0], sp[1])

    ntile = jnp.maximum(i - 1, 0)
    nfull = ntile // nsub

    def far(first_tile, n):
        return [(pl.multiple_of((first_tile + j) * tq, tq), tq) for j in range(n)]

    def far_body(t, carry):
        tiles(far(t * nsub, nsub))
        return carry

    lax.fori_loop(0, nfull, far_body, 0)
    done = nfull * nsub
    n = nsub // 2
    while n >= ATTN_TAIL:
        take = ((ntile - done) // n) > 0

        @pl.when(take)
        def _(done=done, n=n):
            tiles(far(done, n))

        done = done + jnp.where(take, n, 0)
        n //= 2

    rest = ntile - done
    for r in range(ATTN_TAIL):
        @pl.when((i >= 1) & (rest == r))
        def _(r=r):
            off = pl.multiple_of((i - 1) * tq, tq)
            tiles(far(done, r) + [(off, tq, bias_sc[1]),
                                  (pl.multiple_of(off + tq, tq), tq, bias_sc[0])])

    @pl.when(i == 0)
    def _():
        tiles([(0, tq, bias_sc[0])])

    lam = _lam_value(lv_ref, lam_init)
    o = acc_sc[0:DVA, :] / acc_sc[DVA:DVA + 1, :]
    o_ref[...] = (o[:, 0:tq] - lam * o[:, tq:2 * tq]).T


def attn_prompt(qab, kab, vt, rel_bias, lamv, lam_init, tq=ATTN_TQ, nsub=ATTN_GROUP):
    B, S, _ = qab.shape
    assert S % tq == 0
    c = jnp.arange(tq, dtype=jnp.int32)[:, None]
    r = jnp.arange(tq, dtype=jnp.int32)[None, :]
    diag = jnp.where((c // CHUNK) <= (r // CHUNK), _t5_bucket(c - r), -1)
    prev = _t5_bucket(c - r - tq)
    bkt = jnp.stack([diag, prev]).astype(jnp.int32)
    smem = pl.BlockSpec(memory_space=pltpu.SMEM)
    return pl.pallas_call(
        functools.partial(_attn_kernel, tq=tq, nsub=nsub, lam_init=lam_init),
        grid=(B, HA, S // tq),
        in_specs=[smem,
                  pl.BlockSpec((4, DA), lambda b, h, i: (0, 0)),
                  pl.BlockSpec((None, tq, DVA), lambda b, h, i: (b, i, h)),
                  pl.BlockSpec((None, S, DVA), lambda b, h, i: (b, 0, h)),
                  pl.BlockSpec((None, DVA, S), lambda b, h, i: (b, h, 0)),
                  pl.BlockSpec((2, tq, tq), lambda b, h, i: (0, 0, 0))],
        out_specs=pl.BlockSpec((None, tq, DVA), lambda b, h, i: (b, i, h)),
        out_shape=jax.ShapeDtypeStruct((B, S, WA), F32),
        scratch_shapes=[pltpu.VMEM((2, tq, tq), F32),
                        pltpu.VMEM((1, 2 * tq), F32),
                        pltpu.VMEM((DVA + BF16_ROWS, 2 * tq), F32),
                        pltpu.VMEM((DVA + BF16_ROWS, S), BF16)],
        compiler_params=_params(("arbitrary", "arbitrary", "arbitrary")),
        name="attn_prompt",
    )(rel_bias, lamv, qab, kab, vt, bkt)


def _attn_small_kernel(rb_ref, lv_ref, q_ref, k_ref, v_ref, bkt_ref, o_ref, *, lq, lam_init):
    h = pl.program_id(1)
    bias = _bias_from_buckets(bkt_ref[...], rb_ref, h, 0.0)
    qbd = _block_diag_q(q_ref[...])
    s = _dot_nt(qbd, k_ref[...]) + jnp.concatenate([bias, bias], axis=0)
    m = jnp.max(s, axis=-1, keepdims=True)
    p = jnp.exp2(s - m)
    l = jnp.sum(p, axis=-1, keepdims=True)
    o = _dot(p.astype(BF16), v_ref[...]) / l
    lam = _lam_value(lv_ref, lam_init)
    o_ref[...] = o[0:lq] - lam * o[lq:2 * lq]


def attn_small(qab, k_all, v_all, q_pos, k_pos, k_valid, rel_bias, lamv, lam_init):
    B, Lq, _ = qab.shape
    Lk = k_all.shape[1]
    visible = ((k_pos[None, :] // CHUNK) <= (q_pos[:, None] // CHUNK)) & k_valid[None, :]
    bkt = jnp.where(visible, _t5_bucket(k_pos[None, :] - q_pos[:, None]), -1).astype(jnp.int32)
    smem = pl.BlockSpec(memory_space=pltpu.SMEM)
    return pl.pallas_call(
        functools.partial(_attn_small_kernel, lq=Lq, lam_init=lam_init),
        grid=(B, HA),
        in_specs=[smem,
                  pl.BlockSpec((4, DA), lambda b, h: (0, 0)),
                  pl.BlockSpec((None, Lq, DVA), lambda b, h: (b, 0, h)),
                  pl.BlockSpec((None, Lk, DVA), lambda b, h: (b, 0, h)),
                  pl.BlockSpec((None, Lk, DVA), lambda b, h: (b, 0, h)),
                  pl.BlockSpec((Lq, Lk), lambda b, h: (0, 0))],
        out_specs=pl.BlockSpec((None, Lq, DVA), lambda b, h: (b, 0, h)),
        out_shape=jax.ShapeDtypeStruct((B, Lq, WA), F32),
        compiler_params=_params(("arbitrary", "arbitrary")),
        name="attn_sample",
    )(rel_bias, lamv, qab, k_all, v_all, bkt)


def _ret_kernel(lg_ref, q_ref, k_ref, v_ref, s0_ref, o_ref, so_ref, state_sc, decay_sc, *, C):
    c = pl.program_id(1)

    @pl.when(c == 0)
    def _():
        state_sc[...] = s0_ref[...]
        r = lax.broadcasted_iota(jnp.int32, (C, C), 0)
        cc = lax.broadcasted_iota(jnp.int32, (C, C), 1)
        diff = (r - cc).astype(F32)
        for h in range(HR):
            decay_sc[h] = jnp.where(diff >= 0, jnp.exp(jnp.maximum(diff, 0.0) * lg_ref[h]), 0.0)

    n = lax.broadcasted_iota(jnp.int32, (C, 1), 0).astype(F32)
    for h in range(HR):
        lg = lg_ref[h]
        cols = slice(h * DKR, (h + 1) * DKR)
        xi = jnp.exp((n + 1.0) * lg)
        zeta = jnp.exp((C - 1.0 - n) * lg)
        q = q_ref[:, cols]
        k = k_ref[:, cols]
        v = v_ref[:, cols]
        state = state_sc[h]
        scores = _dot_nt(q, k.astype(BF16)) * decay_sc[h]
        intra = _dot(scores.astype(BF16), v)
        cross = _dot(q, state.astype(BF16)) * xi
        o_ref[:, cols] = intra + cross
        kz = (k * zeta).astype(BF16)
        state_sc[h] = jnp.exp(C * lg) * state + _dot_tn(kz, v)

    @pl.when(c == pl.num_programs(1) - 1)
    def _():
        so_ref[...] = state_sc[...]


def retention(qrb, kr, vrb, state0, C):
    B, L, _ = qrb.shape
    lg = jnp.log(1.0 - 2.0 ** (-5.0 - jnp.arange(HR, dtype=F32)))
    blk = pl.BlockSpec((None, C, WR), lambda b, c: (b, c, 0))
    st = pl.BlockSpec((None, HR, DKR, DVR), lambda b, c: (b, 0, 0, 0))
    return pl.pallas_call(
        functools.partial(_ret_kernel, C=C),
        grid=(B, L // C),
        in_specs=[pl.BlockSpec(memory_space=pltpu.SMEM), blk, blk, blk, st],
        out_specs=[blk, st],
        out_shape=[jax.ShapeDtypeStruct((B, L, WR), F32),
                   jax.ShapeDtypeStruct((B, HR, DKR, DVR), F32)],
        scratch_shapes=[pltpu.VMEM((HR, DKR, DVR), F32), pltpu.VMEM((HR, C, C), F32)],
        compiler_params=_params(("arbitrary", "arbitrary")),
        name="retention",
    )(lg, qrb, kr, vrb, state0)


def _outmix_kernel(x_ref, oa_ref, or_ref, gr_ref, ga_ref, gb_ref, g1_ref, sh2_ref, sc2_ref, n2_ref,
                   sa_ref, sr_ref, wba_ref, wbr_ref, wo_ref, wpq_ref,
                   x1_ref, h2_ref, qp_ref, *, lam_init, h2_transposed):
    sa = sa_ref[...]
    sr = sr_ref[...]
    gr = gr_ref[...]
    silu_gr = gr * jax.nn.sigmoid(gr)
    ya_parts, yr_parts = [], []
    for hh in range(HA):
        sl = slice(hh * DVA, (hh + 1) * DVA)
        ya_parts.append(_rms_rows(oa_ref[:, sl]) * sa * (1.0 - lam_init))
        yr_parts.append(silu_gr[:, sl] * (_rms_rows(or_ref[:, sl]) * sr))
    ya = jnp.concatenate(ya_parts, axis=-1).astype(BF16)
    yr = jnp.concatenate(yr_parts, axis=-1).astype(BF16)
    y = (jax.nn.sigmoid(ga_ref[...]) * _dot(ya, wba_ref[...])
         + jax.nn.sigmoid(gb_ref[...]) * _dot(yr, wbr_ref[...]))
    out = _dot(y.astype(BF16), wo_ref[...])
    x1 = x_ref[...] + g1_ref[...] * out
    x1_ref[...] = x1
    h2f = _rms_rows(x1) * n2_ref[...] * (1.0 + sc2_ref[...]) + sh2_ref[...]
    h2 = h2f.astype(BF16)
    h2_ref[...] = h2f.T.astype(BF16) if h2_transposed else h2
    qp_ref[...] = _dot(h2, wpq_ref[...]).astype(BF16)


def out_mix(x, oa, orr, gr, ga, gb, g1, sh2, sc2, norm2, subln_a, subln_r,
            w_ba_b, w_br_b, w_o_b, w_pq_b, lam_init, tm):
    B, L, D = x.shape
    nq = w_pq_b.shape[1]
    row = lambda n: pl.BlockSpec((None, tm, n), lambda b, i: (b, i, 0))
    mod = pl.BlockSpec((None, 1, D), lambda b, i: (b, 0, 0))
    full = lambda a: pl.BlockSpec(a.shape, lambda b, i: (0,) * a.ndim, pipeline_mode=pl.Buffered(1))
    n2 = norm2.reshape(1, D)
    sa = subln_a.reshape(1, DVA)
    sr = subln_r.reshape(1, DVR)
    h2_transposed = tm % LANES == 0
    nt = L // tm
    if h2_transposed:
        h2_spec = pl.BlockSpec((D, tm), lambda b, i: (0, b * nt + i))
        h2_shape = jax.ShapeDtypeStruct((D, B * L), BF16)
    else:
        h2_spec = row(D)
        h2_shape = jax.ShapeDtypeStruct((B, L, D), BF16)
    x1, h2, qp = pl.pallas_call(
        functools.partial(_outmix_kernel, lam_init=lam_init, h2_transposed=h2_transposed),
        grid=(B, nt),
        in_specs=[row(D), row(WA), row(WR), row(WR), row(D), row(D), mod, mod, mod,
                  full(n2), full(sa), full(sr), full(w_ba_b), full(w_br_b), full(w_o_b), full(w_pq_b)],
        out_specs=[row(D), h2_spec, row(nq)],
        out_shape=[jax.ShapeDtypeStruct((B, L, D), F32), h2_shape,
                   jax.ShapeDtypeStruct((B, L, nq), BF16)],
        compiler_params=_params(("arbitrary", "arbitrary")),
        name="out_mix",
    )(x, oa, orr, gr, ga, gb, g1.reshape(B, 1, D), sh2.reshape(B, 1, D), sc2.reshape(B, 1, D),
      n2, sa, sr, w_ba_b, w_br_b, w_o_b, w_pq_b)
    h2t = h2 if h2_transposed else h2.reshape(B * L, D).T
    return x1, h2t, qp


UNRANKED = float(PEER_TOPK * PEER_TOPK)


def _topk_rows(s, k, break_ties, want_rank=True):
    n = s.shape[0]
    iota = lax.broadcasted_iota(jnp.int32, s.shape, 0).astype(F32)
    work = s
    rank = jnp.full(s.shape, UNRANKED, F32) if want_rank else None
    vals = []
    for r in range(k):
        m = jnp.max(work, axis=0, keepdims=True)
        sel = work == m
        if break_ties:
            idx = jnp.min(jnp.where(sel, iota, float(n)), axis=0, keepdims=True)
            sel = iota == idx
        if want_rank:
            rank = jnp.where(sel, float(r), rank)
        work = jnp.where(sel, -jnp.inf, work)
        vals.append(m)
    return vals, rank


def _count_rows(mask):
    return jnp.sum(mask.astype(F32), axis=0, keepdims=True)


_HEAD_A = SUBLANES
_CAND_NB = [PEER_TOPK] + [SUBLANES] * (_HEAD_A - 1)


def _route_chunk(s1, s2, break_ties):
    k = PEER_TOPK
    v1, rank1 = _topk_rows(s1, k, break_ties, want_rank=break_ties)
    v2, rank2 = _topk_rows(s2, k, break_ties)
    v2m = jnp.concatenate(v2, axis=0)
    v1t = jnp.concatenate(v1[_HEAD_A:], axis=0)
    blocks = [v1[a] + v2m[0:nb] for a, nb in enumerate(_CAND_NB)] + [v1t + v2[0]]
    cand = jnp.concatenate(blocks, axis=0)
    vc, crank = _topk_rows(cand, k, break_ties, want_rank=break_ties)
    if break_ties:
        sel = crank < UNRANKED
        row_is = lambda a: rank1 == float(a)
        nbad = jnp.zeros((), jnp.int32)
    else:
        sel = cand >= vc[k - 1]
        row_is = lambda a: s1 == v1[a]
        bad = ((_count_rows(s1 >= v1[k - 1]) != float(k))
               | (_count_rows(rank2 < UNRANKED) != float(k)) | (_count_rows(sel) != float(k)))
        nbad = jnp.sum(bad.astype(jnp.int32))
    cmax = v1[0] + v2[0]
    z = jnp.sum(jnp.where(sel, jnp.exp(cand - cmax), 0.0), axis=0, keepdims=True)
    self32 = sel.astype(F32)
    cidx = jnp.zeros(s1.shape, F32)
    lo = 0
    for a, nb in enumerate(_CAND_NB):
        cnt = jnp.sum(self32[lo:lo + nb], axis=0, keepdims=True)
        cidx = jnp.where(row_is(a), cnt, cidx)
        lo += nb
    for a in range(_HEAD_A, k):
        cidx = jnp.where(row_is(a), self32[lo + a - _HEAD_A:lo + a - _HEAD_A + 1], cidx)
    e1 = jnp.exp(s1 - v1[0]) / z
    e2 = jnp.exp(s2 - v2[0])
    return (e1, e2, rank2, cidx), nbad


def _route_kernel(q_ref, keys_ref, e1_ref, e2_ref, r2_ref, c_ref, s1_sc, s2_sc):
    q = q_ref[...]
    s1_sc[...] = _dot_nt(keys_ref[0], q[:, 0:KEY_DIM])
    s2_sc[...] = _dot_nt(keys_ref[1], q[:, KEY_DIM:2 * KEY_DIM])

    def store(sl, res):
        e1, e2, rank2, cidx = res
        e1_ref[:, sl] = e1
        e2_ref[:, sl] = e2.astype(BF16)
        r2_ref[:, sl] = rank2.astype(BF16)
        c_ref[:, sl] = cidx

    group = math.gcd(ROUTE_CHUNKS, s1_sc.shape[1] // LANES)

    def chunks(ci, carry):
        sls = [pl.ds(pl.multiple_of((ci * group + g) * LANES, LANES), LANES) for g in range(group)]
        fast = [_route_chunk(s1_sc[:, sl], s2_sc[:, sl], break_ties=False) for sl in sls]
        for sl, (res, _) in zip(sls, fast):
            store(sl, res)
        for sl, (_, nbad) in zip(sls, fast):
            @pl.when(nbad > 0)
            def _():
                store(sl, _route_chunk(s1_sc[:, sl], s2_sc[:, sl], break_ties=True)[0])
        return carry

    lax.fori_loop(0, s1_sc.shape[1] // (group * LANES), chunks, 0)


def peer_route(qp, keys_b, tm):
    T = qp.shape[0]
    out = pl.BlockSpec((None, N_KEYS, tm), lambda t, h: (h, 0, t))
    shp = lambda dt: jax.ShapeDtypeStruct((PEER_HEADS, N_KEYS, T), dt)
    return pl.pallas_call(
        _route_kernel,
        grid=(T // tm, PEER_HEADS),
        in_specs=[pl.BlockSpec((tm, 2 * KEY_DIM), lambda t, h: (t, h)),
                  pl.BlockSpec((2, N_KEYS, KEY_DIM), lambda t, h: (h, 0, 0))],
        out_specs=[out, out, out, out],
        out_shape=[shp(F32), shp(BF16), shp(BF16), shp(F32)],
        scratch_shapes=[pltpu.VMEM((N_KEYS, tm), F32), pltpu.VMEM((N_KEYS, tm), F32)],
        compiler_params=_params(("arbitrary", "arbitrary")),
        name="peer_route",
    )(qp, keys_b)


def _bcast_rows_bf16(row, n):
    t = row.shape[1]
    tile = jnp.broadcast_to(row, (BF16_ROWS, t)).astype(BF16)
    return jnp.broadcast_to(tile[None], (n // BF16_ROWS, BF16_ROWS, t)).reshape(n, t)


def _peer_kernel(h2_ref, u_ref, vt_ref, e1_ref, e2_ref, r2_ref, c_ref, x1_ref, g2_ref, fn_ref,
                 y_ref, acc_sc, *, ni):
    e = pl.program_id(1)

    @pl.when(e == 0)
    def _():
        acc_sc[...] = jnp.zeros(acc_sc.shape, F32)

    h2 = h2_ref[...]
    pair = PEER_SUB * N_KEYS
    a_pairs = [_dot(u_ref[p * pair:(p + 1) * pair, :], h2) for p in range(ni // PEER_SUB)]
    for p in range(ni // PEER_SUB):
        gs = [jnp.zeros((N_KEYS, h2.shape[1]), BF16) for _ in range(PEER_SUB)]
        for hh in range(PEER_HEADS):
            r2h = r2_ref[hh]
            e2h = e2_ref[hh]
            for jj in range(PEER_SUB):
                i = PEER_SUB * p + jj
                c_row = _bcast_rows_bf16(c_ref[hh, pl.ds(i, 1), :], N_KEYS)
                e1_row = _bcast_rows_bf16(e1_ref[hh, pl.ds(i, 1), :], N_KEYS)
                gs[jj] = gs[jj] + jnp.where(r2h < c_row, e2h * e1_row, jnp.zeros_like(e2h))
        ws = []
        for jj in range(PEER_SUB):
            a = a_pairs[p][jj * N_KEYS:(jj + 1) * N_KEYS]
            act = 0.5 * a * (1.0 + lax.erf(a * (2.0 ** -0.5)))
            ws.append(gs[jj] * act.astype(BF16))
        w = jnp.concatenate(ws, axis=0)
        acc_sc[...] += _dot(vt_ref[:, p * pair:(p + 1) * pair], w)

    @pl.when(e == pl.num_programs(1) - 1)
    def _():
        x2 = x1_ref[...] + g2_ref[...] * acc_sc[...].T
        y_ref[...] = _rms_rows(x2) * fn_ref[...]


def peer_experts(h2t, u_b, vt_b, e1, e2, r2, cc, x1, g2, final_norm, tm, ni=EXPERT_ROWS):
    D, T = h2t.shape
    nb = ni * N_KEYS
    tok = pl.BlockSpec((tm, D), lambda t, e: (t, 0))
    gate = pl.BlockSpec((PEER_HEADS, N_KEYS, tm), lambda t, e: (0, 0, t))
    rows = pl.BlockSpec((PEER_HEADS, ni, tm), lambda t, e: (0, e, t))
    L = T // g2.shape[0]
    if L % tm == 0:
        g2_arg = g2.reshape(g2.shape[0], 1, D)
        g2_spec = pl.BlockSpec((None, 1, D), lambda t, e: (t // (L // tm), 0, 0))
    else:
        g2_arg = jnp.repeat(g2, L, axis=0)
        g2_spec = tok
    return pl.pallas_call(
        functools.partial(_peer_kernel, ni=ni),
        grid=(T // tm, u_b.shape[0] // nb),
        in_specs=[pl.BlockSpec((D, tm), lambda t, e: (0, t)),
                  pl.BlockSpec((nb, D), lambda t, e: (e, 0)),
                  pl.BlockSpec((D, nb), lambda t, e: (0, e)),
                  rows, gate, gate, rows, tok, g2_spec,
                  pl.BlockSpec((1, D), lambda t, e: (0, 0))],
        out_specs=tok,
        out_shape=jax.ShapeDtypeStruct((T, D), F32),
        scratch_shapes=[pltpu.VMEM((D, tm), F32)],
        compiler_params=_params(("arbitrary", "arbitrary")),
        name="peer_experts",
    )(h2t, u_b, vt_b, e1, e2, r2, cc, x1, g2_arg, final_norm.reshape(1, D))


def _rot_tables(pos):
    inv = 1.0 / (10000.0 ** jnp.linspace(0.0, 1.0, DKR // 2, dtype=F32))
    ang = pos[:, None].astype(F32) * inv[None, :]
    cos, sin = jnp.cos(ang), jnp.sin(ang)
    return jnp.concatenate([cos, cos], axis=-1), jnp.concatenate([-sin, sin], axis=-1)


def _pick_tile(n, pref):
    t = min(n, pref)
    assert n % t == 0, (n, t)
    return t


def _trunk(x, mods, pos, lam_init, w, attend, values_transposed, state0, final_norm):
    (norm1, norm2, w_in_b, subln_a, subln_r, w_ba_b, w_br_b, w_o_b, w_pq_b, keys_b, u_b, vt_b) = w
    sh1, sc1, g1, sh2, sc2, g2 = mods
    B, L, D = x.shape
    T = B * L
    cos2, sin2 = _rot_tables(pos)
    tm = _pick_tile(L, TOKEN_TILE)
    ka, va, qab, kab, vab, qrb, kr, vrb, gr, ga, gb, *vt = in_proj(
        x, sh1, sc1, norm1, w_in_b, cos2, sin2, tm, emit_vt=values_transposed)
    oa = attend(qab, kab, vt[0] if values_transposed else vab)
    orr, st = retention(qrb, kr, vrb, state0, _pick_tile(L, RET_CHUNK))
    x1, h2t, qp = out_mix(x, oa, orr, gr, ga, gb, g1, sh2, sc2, norm2, subln_a, subln_r,
                          w_ba_b, w_br_b, w_o_b, w_pq_b, lam_init, tm)
    e1, e2, r2, cc = peer_route(qp.reshape(T, -1), keys_b, _pick_tile(T, ROUTE_TILE))
    y = peer_experts(h2t, u_b, vt_b, e1, e2, r2, cc, x1.reshape(T, D), g2,
                     final_norm, _pick_tile(T, EXPERT_TILE))
    return y.reshape(B, L, D), ka, va, st


def kernel(x_prompt, x_sample, cache_k, cache_v, state_ret, c_prompt, c_sample, w_ada, b_ada, norm1,
           norm2, w_in, lam_q1, lam_k1, lam_q2, lam_k2, subln_a, subln_r, w_ba, w_br, w_o, rel_bias,
           w_pq, peer_keys, peer_u, peer_v, final_norm):
    depth = w_ada.shape[0]
    assert depth == 1, "the fused final norm assumes a single layer"
    Bp, Lp, D = x_prompt.shape
    Bs, Ls, _ = x_sample.shape
    past = cache_k.shape[2]
    pos_p = jnp.arange(Lp, dtype=jnp.int32)
    pos_s = past + jnp.arange(Ls, dtype=jnp.int32)
    lk = past + Ls
    lk_pad = -(-lk // LANES) * LANES
    k_pos_s = jnp.arange(lk_pad, dtype=jnp.int32)
    k_valid_s = k_pos_s < lk

    l = 0
    lam_init = 0.8 - 0.6 * math.exp(-0.3 * l)
    lamv = jnp.stack([lam_q1[l], lam_k1[l], lam_q2[l], lam_k2[l]]).astype(F32)
    mod = ada_mod(jnp.concatenate([c_prompt, c_sample], axis=0), w_ada[l], b_ada[l])
    mods = jnp.split(mod, 6, axis=-1)
    mods_p = [m[:Bp] for m in mods]
    mods_s = [m[Bp:] for m in mods]
    w = (norm1[l], norm2[l], w_in[l].astype(BF16), subln_a[l], subln_r[l], w_ba[l].astype(BF16),
         w_br[l].astype(BF16), w_o[l].astype(BF16), w_pq[l].astype(BF16),
         peer_keys[l].reshape(PEER_HEADS * 2, N_KEYS, KEY_DIM).astype(BF16),
         peer_u[l].astype(BF16), peer_v[l].T.astype(BF16))

    attend_p = lambda q, k, v: attn_prompt(q, k, v, rel_bias, lamv, lam_init)
    zero_state = jnp.zeros((Bp, HR, DKR, DVR), F32)
    yp, kp, vp, sp = _trunk(x_prompt, mods_p, pos_p, lam_init, w, attend_p, True, zero_state,
                            final_norm)

    def attend_s(q, k, v):
        padk = jnp.zeros((Bs, lk_pad - lk, WA), BF16)
        kc = cache_k[l].reshape(Bs, past, HA * 2 * DA).astype(BF16)
        vc = cache_v[l].reshape(Bs, past, WA).astype(BF16)
        k_all = jnp.concatenate([kc, k, padk], axis=1)
        v_all = jnp.concatenate([vc, v, padk], axis=1)
        return attn_small(q, k_all, v_all, pos_s, k_pos_s, k_valid_s, rel_bias, lamv, lam_init)

    ys, ks, vs, ss = _trunk(x_sample, mods_s, pos_s, lam_init, w, attend_s, False,
                            state_ret[l].astype(F32), final_norm)
    return (yp, ys, kp[None], vp[None], sp[None], ks[None], vs[None], ss[None])
```

```python
import functools
import math

import jax
import jax.numpy as jnp
from jax import lax
from jax.experimental import pallas as pl
from jax.experimental.pallas import tpu as pltpu

F32 = jnp.float32
BF16 = jnp.bfloat16

CHUNK = 64
HA = 4
DA = 64
DVA = 2 * DA
HR = 4
DKR = 128
DVR = 128
N_BUCKETS = 32
MAX_DIST = 128
PEER_HEADS = 8
N_KEYS = 128
KEY_DIM = 128
PEER_TOPK = 16
EPS = 1e-6
WA = HA * DVA
WR = HR * DVR
NEG = -1e30
LOG2E = math.log2(math.e)
LANES = 128
SUBLANES = 8
BF16_ROWS = 2 * SUBLANES
VMEM_LIMIT = 60 * 1024 * 1024

TOKEN_TILE = 512
RET_CHUNK = 256
ATTN_TQ = 256
ATTN_GROUP = 32
ATTN_TAIL = 8
ATTN_AHEAD = 8
ROUTE_TILE = 1024
ROUTE_CHUNKS = 8
EXPERT_TILE = 512
EXPERT_ROWS = 32
PEER_SUB = 2


def _params(sem, vmem=VMEM_LIMIT):
    return pltpu.CompilerParams(dimension_semantics=sem, vmem_limit_bytes=vmem)


def _dot(a, b):
    return jnp.dot(a, b, preferred_element_type=F32)


def _dot_nt(a, b):
    return lax.dot_general(a, b, (((1,), (1,)), ((), ())), preferred_element_type=F32)


def _dot_tn(a, b):
    return lax.dot_general(a, b, (((0,), (0,)), ((), ())), preferred_element_type=F32)


def _rms_rows(x):
    return x * lax.rsqrt(jnp.mean(x * x, axis=-1, keepdims=True) + EPS)


def _ada_kernel(c_ref, w_ref, b_ref, o_ref):
    c = c_ref[...]
    s = (c * jax.nn.sigmoid(c)).astype(BF16)
    o_ref[...] = _dot(s, w_ref[...].astype(BF16)) + b_ref[...]


def ada_mod(c, w_ada, b_ada, tn=1024):
    nb, d = c.shape
    n = w_ada.shape[1]
    return pl.pallas_call(
        _ada_kernel,
        grid=(n // tn,),
        in_specs=[pl.BlockSpec((nb, d), lambda j: (0, 0)),
                  pl.BlockSpec((d, tn), lambda j: (0, j)),
                  pl.BlockSpec((1, tn), lambda j: (0, j))],
        out_specs=pl.BlockSpec((nb, tn), lambda j: (0, j)),
        out_shape=jax.ShapeDtypeStruct((nb, n), F32),
        compiler_params=_params(("arbitrary",)),
        name="ada_mod",
    )(c, w_ada, b_ada.reshape(1, n))


def _inproj_kernel(x_ref, sh_ref, sc_ref, n1_ref, w_ref, cos_ref, sin_ref,
                   ka_ref, va_ref, qab_ref, kab_ref, vab_ref, qrb_ref, kr_ref, vrb_ref,
                   gr_ref, ga_ref, gb_ref, vt_ref=None):
    x = x_ref[...]
    h = _rms_rows(x) * n1_ref[...] * (1.0 + sc_ref[...]) + sh_ref[...]
    hb = h.astype(BF16)

    d_model = x.shape[1]
    sizes = (WA, WA, WA, WR, WR, WR, WR, d_model, d_model)
    starts = [sum(sizes[:g]) for g in range(len(sizes))]

    def proj(g):
        return _dot(hb, w_ref[:, starts[g]:starts[g] + sizes[g]])

    qa = proj(0)
    qab_ref[...] = (qa * (DA ** -0.5 * LOG2E)).astype(BF16)
    ka = proj(1)
    va = proj(2)
    for hh in range(HA):
        ka_ref[:, hh, :] = ka[:, hh * DVA:(hh + 1) * DVA]
        va_ref[:, hh, :] = va[:, hh * DVA:(hh + 1) * DVA]
    kab_ref[...] = ka.astype(BF16)
    vab_ref[...] = va.astype(BF16)
    if vt_ref is not None:
        vt_ref[...] = va.T.astype(BF16)
    cos2 = cos_ref[...]
    sin2 = sin_ref[...]

    def rot(z):
        parts = []
        for hh in range(HR):
            zh = z[:, hh * DKR:(hh + 1) * DKR]
            parts.append(zh * cos2 + pltpu.roll(zh, DKR // 2, 1) * sin2)
        return jnp.concatenate(parts, axis=-1)

    qrb_ref[...] = rot(proj(3)).astype(BF16)
    kr_ref[...] = rot(proj(4)) * (DKR ** -0.5)
    vrb_ref[...] = proj(5).astype(BF16)
    gr_ref[...] = proj(6)
    ga_ref[...] = proj(7)
    gb_ref[...] = proj(8)


def in_proj(x, sh1, sc1, norm1, w_in_b, cos2, sin2, tm, emit_vt):
    B, L, D = x.shape
    d_in = w_in_b.shape[1]
    row = lambda n: pl.BlockSpec((None, tm, n), lambda b, i: (b, i, 0))
    mod = pl.BlockSpec((None, 1, D), lambda b, i: (b, 0, 0))
    f = lambda n, dt: jax.ShapeDtypeStruct((B, L, n), dt)
    cache = pl.BlockSpec((None, tm, HA, DVA), lambda b, i: (b, i, 0, 0))
    cache_shape = jax.ShapeDtypeStruct((B, L, HA, DVA), F32)
    vt_spec = [pl.BlockSpec((None, WA, tm), lambda b, i: (b, 0, i))] if emit_vt else []
    vt_shape = [jax.ShapeDtypeStruct((B, WA, L), BF16)] if emit_vt else []
    return pl.pallas_call(
        _inproj_kernel,
        grid=(B, L // tm),
        in_specs=[row(D), mod, mod,
                  pl.BlockSpec((1, D), lambda b, i: (0, 0)),
                  pl.BlockSpec((D, d_in), lambda b, i: (0, 0), pipeline_mode=pl.Buffered(1)),
                  pl.BlockSpec((tm, DKR), lambda b, i: (i, 0)),
                  pl.BlockSpec((tm, DKR), lambda b, i: (i, 0))],
        out_specs=[cache, cache, row(WA), row(WA), row(WA), row(WR), row(WR), row(WR),
                   row(WR), row(D), row(D)] + vt_spec,
        out_shape=[cache_shape, cache_shape, f(WA, BF16), f(WA, BF16), f(WA, BF16),
                   f(WR, BF16), f(WR, F32), f(WR, BF16), f(WR, F32), f(D, F32), f(D, F32)]
        + vt_shape,
        compiler_params=_params(("arbitrary", "arbitrary")),
        name="in_proj",
    )(x, sh1.reshape(B, 1, D), sc1.reshape(B, 1, D), norm1.reshape(1, D), w_in_b, cos2, sin2)


def _t5_bucket(rel):
    nb = N_BUCKETS // 2
    ret = jnp.where(rel > 0, nb, 0)
    n = jnp.abs(rel)
    max_exact = nb // 2
    nf = jnp.maximum(n, max_exact).astype(F32)
    large = max_exact + (jnp.log(nf / max_exact) / math.log(MAX_DIST / max_exact)
                         * (nb - max_exact)).astype(jnp.int32)
    large = jnp.minimum(large, nb - 1)
    return ret + jnp.where(n < max_exact, n, large)


def _bias_from_buckets(bkt, rb_ref, h, shift):
    val = jnp.where(bkt < 0, NEG, 0.0).astype(F32)
    for n in range(N_BUCKETS):
        val = jnp.where(bkt == n, (rb_ref[n, h] - shift) * LOG2E, val)
    return val


def _lam_value(lv_ref, lam_init):
    lv = lv_ref[...]
    a = jnp.sum(lv[0:1] * lv[1:2], axis=-1, keepdims=True)
    b = jnp.sum(lv[2:3] * lv[3:4], axis=-1, keepdims=True)
    return jnp.exp(a) - jnp.exp(b) + lam_init


def _block_diag_q(q):
    lane = lax.broadcasted_iota(jnp.int32, q.shape, 1)
    zero = jnp.zeros_like(q)
    return jnp.concatenate([jnp.where(lane < DA, q, zero), jnp.where(lane >= DA, q, zero)], axis=0)


def _attn_kernel(rb_ref, lv_ref, q_ref, k_ref, vt_ref, bkt_ref, o_ref,
                 bias_sc, m_sc, acc_sc, va_sc, *, tq, nsub, lam_init):
    h = pl.program_id(1)
    i = pl.program_id(2)

    @pl.when(i == 0)
    def _():
        far = rb_ref[N_BUCKETS // 2 - 1, h]
        for t in range(2):
            bias_sc[t] = _bias_from_buckets(bkt_ref[t], rb_ref, h, far)
        va_sc[0:DVA, :] = vt_ref[...]
        va_sc[DVA:, :] = jnp.ones((BF16_ROWS, va_sc.shape[1]), BF16)

    qt = q_ref[...].astype(F32).T
    drow = lax.broadcasted_iota(jnp.int32, qt.shape, 0)
    qbd_t = jnp.concatenate([jnp.where(drow < DA, qt, 0.0), jnp.where(drow >= DA, qt, 0.0)],
                            axis=1).astype(BF16)
    m_sc[...] = jnp.full(m_sc.shape, NEG, F32)
    acc_sc[...] = jnp.zeros(acc_sc.shape, F32)

    def scores(off, tk, bias=None):
        s = _dot(k_ref[pl.ds(off, tk), :], qbd_t)
        if bias is not None:
            s = s + jnp.concatenate([bias, bias], axis=1)
        return s

    def absorb(s, off, tk):
        m_prev = m_sc[...]
        m_new = jnp.maximum(m_prev, jnp.max(s, axis=0, keepdims=True))
        alpha = jnp.exp2(m_prev - m_new)
        p = jnp.exp2(s - m_new).astype(BF16)
        acc_sc[...] = alpha * acc_sc[...] + _dot(va_sc[:, pl.ds(off, tk)], p)
        m_sc[...] = m_new

    def tiles(specs):
        ss = [scores(*sp) for sp in specs[:ATTN_AHEAD]]
        for j, sp in enumerate(specs):
            if j + ATTN_AHEAD < len(specs):
                ss.append(scores(*specs[j + ATTN_AHEAD]))
            absorb(ss[j], sp[0], sp[1])

    ntile = jnp.maximum(i - 1, 0)
    nfull = ntile // nsub

    def far(first_tile, n):
        return [(pl.multiple_of((first_tile + j) * tq, tq), tq) for j in range(n)]

    def far_body(t, carry):
        tiles(far(t * nsub, nsub))
        return carry

    lax.fori_loop(0, nfull, far_body, 0)
    done = nfull * nsub
    n = nsub // 2
    while n >= ATTN_TAIL:
        take = ((ntile - done) // n) > 0

        @pl.when(take)
        def _(done=done, n=n):
            tiles(far(done, n))

        done = done + jnp.where(take, n, 0)
        n //= 2

    rest = ntile - done
    for r in range(ATTN_TAIL):
        @pl.when((i >= 1) & (rest == r))
        def _(r=r):
            off = pl.multiple_of((i - 1) * tq, tq)
            tiles(far(done, r) + [(off, tq, bias_sc[1]),
                                  (pl.multiple_of(off + tq, tq), tq, bias_sc[0])])

    @pl.when(i == 0)
    def _():
        tiles([(0, tq, bias_sc[0])])

    lam = _lam_value(lv_ref, lam_init)
    o = acc_sc[0:DVA, :] / acc_sc[DVA:DVA + 1, :]
    o_ref[...] = (o[:, 0:tq] - lam * o[:, tq:2 * tq]).T


def attn_prompt(qab, kab, vt, rel_bias, lamv, lam_init, tq=ATTN_TQ, nsub=ATTN_GROUP):
    B, S, _ = qab.shape
    assert S % tq == 0
    c = jnp.arange(tq, dtype=jnp.int32)[:, None]
    r = jnp.arange(tq, dtype=jnp.int32)[None, :]
    diag = jnp.where((c // CHUNK) <= (r // CHUNK), _t5_bucket(c - r), -1)
    prev = _t5_bucket(c - r - tq)
    bkt = jnp.stack([diag, prev]).astype(jnp.int32)
    smem = pl.BlockSpec(memory_space=pltpu.SMEM)
    return pl.pallas_call(
        functools.partial(_attn_kernel, tq=tq, nsub=nsub, lam_init=lam_init),
        grid=(B, HA, S // tq),
        in_specs=[smem,
                  pl.BlockSpec((4, DA), lambda b, h, i: (0, 0)),
                  pl.BlockSpec((None, tq, DVA), lambda b, h, i: (b, i, h)),
                  pl.BlockSpec((None, S, DVA), lambda b, h, i: (b, 0, h)),
                  pl.BlockSpec((None, DVA, S), lambda b, h, i: (b, h, 0)),
                  pl.BlockSpec((2, tq, tq), lambda b, h, i: (0, 0, 0))],
        out_specs=pl.BlockSpec((None, tq, DVA), lambda b, h, i: (b, i, h)),
        out_shape=jax.ShapeDtypeStruct((B, S, WA), F32),
        scratch_shapes=[pltpu.VMEM((2, tq, tq), F32),
                        pltpu.VMEM((1, 2 * tq), F32),
                        pltpu.VMEM((DVA + BF16_ROWS, 2 * tq), F32),
                        pltpu.VMEM((DVA + BF16_ROWS, S), BF16)],
        compiler_params=_params(("arbitrary", "arbitrary", "arbitrary")),
        name="attn_prompt",
    )(rel_bias, lamv, qab, kab, vt, bkt)


def _attn_small_kernel(rb_ref, lv_ref, q_ref, k_ref, v_ref, bkt_ref, o_ref, *, lq, lam_init):
    h = pl.program_id(1)
    bias = _bias_from_buckets(bkt_ref[...], rb_ref, h, 0.0)
    qbd = _block_diag_q(q_ref[...])
    s = _dot_nt(qbd, k_ref[...]) + jnp.concatenate([bias, bias], axis=0)
    m = jnp.max(s, axis=-1, keepdims=True)
    p = jnp.exp2(s - m)
    l = jnp.sum(p, axis=-1, keepdims=True)
    o = _dot(p.astype(BF16), v_ref[...]) / l
    lam = _lam_value(lv_ref, lam_init)
    o_ref[...] = o[0:lq] - lam * o[lq:2 * lq]


def attn_small(qab, k_all, v_all, q_pos, k_pos, k_valid, rel_bias, lamv, lam_init):
    B, Lq, _ = qab.shape
    Lk = k_all.shape[1]
    visible = ((k_pos[None, :] // CHUNK) <= (q_pos[:, None] // CHUNK)) & k_valid[None, :]
    bkt = jnp.where(visible, _t5_bucket(k_pos[None, :] - q_pos[:, None]), -1).astype(jnp.int32)
    smem = pl.BlockSpec(memory_space=pltpu.SMEM)
    return pl.pallas_call(
        functools.partial(_attn_small_kernel, lq=Lq, lam_init=lam_init),
        grid=(B, HA),
        in_specs=[smem,
                  pl.BlockSpec((4, DA), lambda b, h: (0, 0)),
                  pl.BlockSpec((None, Lq, DVA), lambda b, h: (b, 0, h)),
                  pl.BlockSpec((None, Lk, DVA), lambda b, h: (b, 0, h)),
                  pl.BlockSpec((None, Lk, DVA), lambda b, h: (b, 0, h)),
                  pl.BlockSpec((Lq, Lk), lambda b, h: (0, 0))],
        out_specs=pl.BlockSpec((None, Lq, DVA), lambda b, h: (b, 0, h)),
        out_shape=jax.ShapeDtypeStruct((B, Lq, WA), F32),
        compiler_params=_params(("arbitrary", "arbitrary")),
        name="attn_sample",
    )(rel_bias, lamv, qab, k_all, v_all, bkt)


def _ret_kernel(lg_ref, q_ref, k_ref, v_ref, s0_ref, o_ref, so_ref, state_sc, decay_sc, *, C):
    c = pl.program_id(1)

    @pl.when(c == 0)
    def _():
        state_sc[...] = s0_ref[...]
        r = lax.broadcasted_iota(jnp.int32, (C, C), 0)
        cc = lax.broadcasted_iota(jnp.int32, (C, C), 1)
        diff = (r - cc).astype(F32)
        for h in range(HR):
            decay_sc[h] = jnp.where(diff >= 0, jnp.exp(jnp.maximum(diff, 0.0) * lg_ref[h]), 0.0)

    n = lax.broadcasted_iota(jnp.int32, (C, 1), 0).astype(F32)
    for h in range(HR):
        lg = lg_ref[h]
        cols = slice(h * DKR, (h + 1) * DKR)
        xi = jnp.exp((n + 1.0) * lg)
        zeta = jnp.exp((C - 1.0 - n) * lg)
        q = q_ref[:, cols]
        k = k_ref[:, cols]
        v = v_ref[:, cols]
        state = state_sc[h]
        scores = _dot_nt(q, k.astype(BF16)) * decay_sc[h]
        intra = _dot(scores.astype(BF16), v)
        cross = _dot(q, state.astype(BF16)) * xi
        o_ref[:, cols] = intra + cross
        kz = (k * zeta).astype(BF16)
        state_sc[h] = jnp.exp(C * lg) * state + _dot_tn(kz, v)

    @pl.when(c == pl.num_programs(1) - 1)
    def _():
        so_ref[...] = state_sc[...]


def retention(qrb, kr, vrb, state0, C):
    B, L, _ = qrb.shape
    lg = jnp.log(1.0 - 2.0 ** (-5.0 - jnp.arange(HR, dtype=F32)))
    blk = pl.BlockSpec((None, C, WR), lambda b, c: (b, c, 0))
    st = pl.BlockSpec((None, HR, DKR, DVR), lambda b, c: (b, 0, 0, 0))
    return pl.pallas_call(
        functools.partial(_ret_kernel, C=C),
        grid=(B, L // C),
        in_specs=[pl.BlockSpec(memory_space=pltpu.SMEM), blk, blk, blk, st],
        out_specs=[blk, st],
        out_shape=[jax.ShapeDtypeStruct((B, L, WR), F32),
                   jax.ShapeDtypeStruct((B, HR, DKR, DVR), F32)],
        scratch_shapes=[pltpu.VMEM((HR, DKR, DVR), F32), pltpu.VMEM((HR, C, C), F32)],
        compiler_params=_params(("arbitrary", "arbitrary")),
        name="retention",
    )(lg, qrb, kr, vrb, state0)


def _outmix_kernel(x_ref, oa_ref, or_ref, gr_ref, ga_ref, gb_ref, g1_ref, sh2_ref, sc2_ref, n2_ref,
                   sa_ref, sr_ref, wba_ref, wbr_ref, wo_ref, wpq_ref,
                   x1_ref, h2_ref, qp_ref, *, lam_init, h2_transposed):
    sa = sa_ref[...]
    sr = sr_ref[...]
    gr = gr_ref[...]
    silu_gr = gr * jax.nn.sigmoid(gr)
    ya_parts, yr_parts = [], []
    for hh in range(HA):
        sl = slice(hh * DVA, (hh + 1) * DVA)
        ya_parts.append(_rms_rows(oa_ref[:, sl]) * sa * (1.0 - lam_init))
        yr_parts.append(silu_gr[:, sl] * (_rms_rows(or_ref[:, sl]) * sr))
    ya = jnp.concatenate(ya_parts, axis=-1).astype(BF16)
    yr = jnp.concatenate(yr_parts, axis=-1).astype(BF16)
    y = (jax.nn.sigmoid(ga_ref[...]) * _dot(ya, wba_ref[...])
         + jax.nn.sigmoid(gb_ref[...]) * _dot(yr, wbr_ref[...]))
    out = _dot(y.astype(BF16), wo_ref[...])
    x1 = x_ref[...] + g1_ref[...] * out
    x1_ref[...] = x1
    h2f = _rms_rows(x1) * n2_ref[...] * (1.0 + sc2_ref[...]) + sh2_ref[...]
    h2 = h2f.astype(BF16)
    h2_ref[...] = h2f.T.astype(BF16) if h2_transposed else h2
    qp_ref[...] = _dot(h2, wpq_ref[...]).astype(BF16)


def out_mix(x, oa, orr, gr, ga, gb, g1, sh2, sc2, norm2, subln_a, subln_r,
            w_ba_b, w_br_b, w_o_b, w_pq_b, lam_init, tm):
    B, L, D = x.shape
    nq = w_pq_b.shape[1]
    row = lambda n: pl.BlockSpec((None, tm, n), lambda b, i: (b, i, 0))
    mod = pl.BlockSpec((None, 1, D), lambda b, i: (b, 0, 0))
    full = lambda a: pl.BlockSpec(a.shape, lambda b, i: (0,) * a.ndim, pipeline_mode=pl.Buffered(1))
    n2 = norm2.reshape(1, D)
    sa = subln_a.reshape(1, DVA)
    sr = subln_r.reshape(1, DVR)
    h2_transposed = tm % LANES == 0
    nt = L // tm
    if h2_transposed:
        h2_spec = pl.BlockSpec((D, tm), lambda b, i: (0, b * nt + i))
        h2_shape = jax.ShapeDtypeStruct((D, B * L), BF16)
    else:
        h2_spec = row(D)
        h2_shape = jax.ShapeDtypeStruct((B, L, D), BF16)
    x1, h2, qp = pl.pallas_call(
        functools.partial(_outmix_kernel, lam_init=lam_init, h2_transposed=h2_transposed),
        grid=(B, nt),
        in_specs=[row(D), row(WA), row(WR), row(WR), row(D), row(D), mod, mod, mod,
                  full(n2), full(sa), full(sr), full(w_ba_b), full(w_br_b), full(w_o_b), full(w_pq_b)],
        out_specs=[row(D), h2_spec, row(nq)],
        out_shape=[jax.ShapeDtypeStruct((B, L, D), F32), h2_shape,
                   jax.ShapeDtypeStruct((B, L, nq), BF16)],
        compiler_params=_params(("arbitrary", "arbitrary")),
        name="out_mix",
    )(x, oa, orr, gr, ga, gb, g1.reshape(B, 1, D), sh2.reshape(B, 1, D), sc2.reshape(B, 1, D),
      n2, sa, sr, w_ba_b, w_br_b, w_o_b, w_pq_b)
    h2t = h2 if h2_transposed else h2.reshape(B * L, D).T
    return x1, h2t, qp


UNRANKED = float(PEER_TOPK * PEER_TOPK)


def _topk_rows(s, k, break_ties, want_rank=True):
    n = s.shape[0]
    iota = lax.broadcasted_iota(jnp.int32, s.shape, 0).astype(F32)
    work = s
    rank = jnp.full(s.shape, UNRANKED, F32) if want_rank else None
    vals = []
    for r in range(k):
        m = jnp.max(work, axis=0, keepdims=True)
        sel = work == m
        if break_ties:
            idx = jnp.min(jnp.where(sel, iota, float(n)), axis=0, keepdims=True)
            sel = iota == idx
        if want_rank:
            rank = jnp.where(sel, float(r), rank)
        work = jnp.where(sel, -jnp.inf, work)
        vals.append(m)
    return vals, rank


def _count_rows(mask):
    return jnp.sum(mask.astype(F32), axis=0, keepdims=True)


_HEAD_A = SUBLANES
_CAND_NB = [PEER_TOPK] + [SUBLANES] * (_HEAD_A - 1)


def _route_chunk(s1, s2, break_ties):
    k = PEER_TOPK
    v1, rank1 = _topk_rows(s1, k, break_ties, want_rank=break_ties)
    v2, rank2 = _topk_rows(s2, k, break_ties)
    v2m = jnp.concatenate(v2, axis=0)
    v1t = jnp.concatenate(v1[_HEAD_A:], axis=0)
    blocks = [v1[a] + v2m[0:nb] for a, nb in enumerate(_CAND_NB)] + [v1t + v2[0]]
    cand = jnp.concatenate(blocks, axis=0)
    vc, crank = _topk_rows(cand, k, break_ties, want_rank=break_ties)
    if break_ties:
        sel = crank < UNRANKED
        row_is = lambda a: rank1 == float(a)
        nbad = jnp.zeros((), jnp.int32)
    else:
        sel = cand >= vc[k - 1]
        row_is = lambda a: s1 == v1[a]
        bad = ((_count_rows(s1 >= v1[k - 1]) != float(k))
               | (_count_rows(rank2 < UNRANKED) != float(k)) | (_count_rows(sel) != float(k)))
        nbad = jnp.sum(bad.astype(jnp.int32))
    cmax = v1[0] + v2[0]
    z = jnp.sum(jnp.where(sel, jnp.exp(cand - cmax), 0.0), axis=0, keepdims=True)
    self32 = sel.astype(F32)
    cidx = jnp.zeros(s1.shape, F32)
    lo = 0
    for a, nb in enumerate(_CAND_NB):
        cnt = jnp.sum(self32[lo:lo + nb], axis=0, keepdims=True)
        cidx = jnp.where(row_is(a), cnt, cidx)
        lo += nb
    for a in range(_HEAD_A, k):
        cidx = jnp.where(row_is(a), self32[lo + a - _HEAD_A:lo + a - _HEAD_A + 1], cidx)
    e1 = jnp.exp(s1 - v1[0]) / z
    e2 = jnp.exp(s2 - v2[0])
    return (e1, e2, rank2, cidx), nbad


def _route_kernel(q_ref, keys_ref, e1_ref, e2_ref, r2_ref, c_ref, s1_sc, s2_sc):
    q = q_ref[...]
    s1_sc[...] = _dot_nt(keys_ref[0], q[:, 0:KEY_DIM])
    s2_sc[...] = _dot_nt(keys_ref[1], q[:, KEY_DIM:2 * KEY_DIM])

    def store(sl, res):
        e1, e2, rank2, cidx = res
        e1_ref[:, sl] = e1
        e2_ref[:, sl] = e2.astype(BF16)
        r2_ref[:, sl] = rank2.astype(BF16)
        c_ref[:, sl] = cidx

    group = math.gcd(ROUTE_CHUNKS, s1_sc.shape[1] // LANES)

    def chunks(ci, carry):
        sls = [pl.ds(pl.multiple_of((ci * group + g) * LANES, LANES), LANES) for g in range(group)]
        fast = [_route_chunk(s1_sc[:, sl], s2_sc[:, sl], break_ties=False) for sl in sls]
        for sl, (res, _) in zip(sls, fast):
            store(sl, res)
        for sl, (_, nbad) in zip(sls, fast):
            @pl.when(nbad > 0)
            def _():
                store(sl, _route_chunk(s1_sc[:, sl], s2_sc[:, sl], break_ties=True)[0])
        return carry

    lax.fori_loop(0, s1_sc.shape[1] // (group * LANES), chunks, 0)


def peer_route(qp, keys_b, tm):
    T = qp.shape[0]
    out = pl.BlockSpec((None, N_KEYS, tm), lambda t, h: (h, 0, t))
    shp = lambda dt: jax.ShapeDtypeStruct((PEER_HEADS, N_KEYS, T), dt)
    return pl.pallas_call(
        _route_kernel,
        grid=(T // tm, PEER_HEADS),
        in_specs=[pl.BlockSpec((tm, 2 * KEY_DIM), lambda t, h: (t, h)),
                  pl.BlockSpec((2, N_KEYS, KEY_DIM), lambda t, h: (h, 0, 0))],
        out_specs=[out, out, out, out],
        out_shape=[shp(F32), shp(BF16), shp(BF16), shp(F32)],
        scratch_shapes=[pltpu.VMEM((N_KEYS, tm), F32), pltpu.VMEM((N_KEYS, tm), F32)],
        compiler_params=_params(("arbitrary", "arbitrary")),
        name="peer_route",
    )(qp, keys_b)


def _bcast_rows_bf16(row, n):
    t = row.shape[1]
    tile = jnp.broadcast_to(row, (BF16_ROWS, t)).astype(BF16)
    return jnp.broadcast_to(tile[None], (n // BF16_ROWS, BF16_ROWS, t)).reshape(n, t)


def _peer_kernel(h2_ref, u_ref, vt_ref, e1_ref, e2_ref, r2_ref, c_ref, x1_ref, g2_ref, fn_ref,
                 y_ref, acc_sc, *, ni):
    e = pl.program_id(1)

    @pl.when(e == 0)
    def _():
        acc_sc[...] = jnp.zeros(acc_sc.shape, F32)

    h2 = h2_ref[...]
    pair = PEER_SUB * N_KEYS
    a_pairs = [_dot(u_ref[p * pair:(p + 1) * pair, :], h2) for p in range(ni // PEER_SUB)]
    for p in range(ni // PEER_SUB):
        gs = [jnp.zeros((N_KEYS, h2.shape[1]), BF16) for _ in range(PEER_SUB)]
        for hh in range(PEER_HEADS):
            r2h = r2_ref[hh]
            e2h = e2_ref[hh]
            for jj in range(PEER_SUB):
                i = PEER_SUB * p + jj
                c_row = _bcast_rows_bf16(c_ref[hh, pl.ds(i, 1), :], N_KEYS)
                e1_row = _bcast_rows_bf16(e1_ref[hh, pl.ds(i, 1), :], N_KEYS)
                gs[jj] = gs[jj] + jnp.where(r2h < c_row, e2h * e1_row, jnp.zeros_like(e2h))
        ws = []
        for jj in range(PEER_SUB):
            a = a_pairs[p][jj * N_KEYS:(jj + 1) * N_KEYS]
            act = 0.5 * a * (1.0 + lax.erf(a * (2.0 ** -0.5)))
            ws.append(gs[jj] * act.astype(BF16))
        w = jnp.concatenate(ws, axis=0)
        acc_sc[...] += _dot(vt_ref[:, p * pair:(p + 1) * pair], w)

    @pl.when(e == pl.num_programs(1) - 1)
    def _():
        x2 = x1_ref[...] + g2_ref[...] * acc_sc[...].T
        y_ref[...] = _rms_rows(x2) * fn_ref[...]


def peer_experts(h2t, u_b, vt_b, e1, e2, r2, cc, x1, g2, final_norm, tm, ni=EXPERT_ROWS):
    D, T = h2t.shape
    nb = ni * N_KEYS
    tok = pl.BlockSpec((tm, D), lambda t, e: (t, 0))
    gate = pl.BlockSpec((PEER_HEADS, N_KEYS, tm), lambda t, e: (0, 0, t))
    rows = pl.BlockSpec((PEER_HEADS, ni, tm), lambda t, e: (0, e, t))
    L = T // g2.shape[0]
    if L % tm == 0:
        g2_arg = g2.reshape(g2.shape[0], 1, D)
        g2_spec = pl.BlockSpec((None, 1, D), lambda t, e: (t // (L // tm), 0, 0))
    else:
        g2_arg = jnp.repeat(g2, L, axis=0)
        g2_spec = tok
    return pl.pallas_call(
        functools.partial(_peer_kernel, ni=ni),
        grid=(T // tm, u_b.shape[0] // nb),
        in_specs=[pl.BlockSpec((D, tm), lambda t, e: (0, t)),
                  pl.BlockSpec((nb, D), lambda t, e: (e, 0)),
                  pl.BlockSpec((D, nb), lambda t, e: (0, e)),
                  rows, gate, gate, rows, tok, g2_spec,
                  pl.BlockSpec((1, D), lambda t, e: (0, 0))],
        out_specs=tok,
        out_shape=jax.ShapeDtypeStruct((T, D), F32),
        scratch_shapes=[pltpu.VMEM((D, tm), F32)],
        compiler_params=_params(("arbitrary", "arbitrary")),
        name="peer_experts",
    )(h2t, u_b, vt_b, e1, e2, r2, cc, x1, g2_arg, final_norm.reshape(1, D))


def _rot_tables(pos):
    inv = 1.0 / (10000.0 ** jnp.linspace(0.0, 1.0, DKR // 2, dtype=F32))
    ang = pos[:, None].astype(F32) * inv[None, :]
    cos, sin = jnp.cos(ang), jnp.sin(ang)
    return jnp.concatenate([cos, cos], axis=-1), jnp.concatenate([-sin, sin], axis=-1)


def _pick_tile(n, pref):
    t = min(n, pref)
    assert n % t == 0, (n, t)
    return t


def _trunk(x, mods, pos, lam_init, w, attend, values_transposed, state0, final_norm):
    (norm1, norm2, w_in_b, subln_a, subln_r, w_ba_b, w_br_b, w_o_b, w_pq_b, keys_b, u_b, vt_b) = w
    sh1, sc1, g1, sh2, sc2, g2 = mods
    B, L, D = x.shape
    T = B * L
    cos2, sin2 = _rot_tables(pos)
    tm = _pick_tile(L, TOKEN_TILE)
    ka, va, qab, kab, vab, qrb, kr, vrb, gr, ga, gb, *vt = in_proj(
        x, sh1, sc1, norm1, w_in_b, cos2, sin2, tm, emit_vt=values_transposed)
    oa = attend(qab, kab, vt[0] if values_transposed else vab)
    orr, st = retention(qrb, kr, vrb, state0, _pick_tile(L, RET_CHUNK))
    x1, h2t, qp = out_mix(x, oa, orr, gr, ga, gb, g1, sh2, sc2, norm2, subln_a, subln_r,
                          w_ba_b, w_br_b, w_o_b, w_pq_b, lam_init, tm)
    e1, e2, r2, cc = peer_route(qp.reshape(T, -1), keys_b, _pick_tile(T, ROUTE_TILE))
    y = peer_experts(h2t, u_b, vt_b, e1, e2, r2, cc, x1.reshape(T, D), g2,
                     final_norm, _pick_tile(T, EXPERT_TILE))
    return y.reshape(B, L, D), ka, va, st


def kernel(x_prompt, x_sample, cache_k, cache_v, state_ret, c_prompt, c_sample, w_ada, b_ada, norm1,
           norm2, w_in, lam_q1, lam_k1, lam_q2, lam_k2, subln_a, subln_r, w_ba, w_br, w_o, rel_bias,
           w_pq, peer_keys, peer_u, peer_v, final_norm):
    depth = w_ada.shape[0]
    assert depth == 1, "the fused final norm assumes a single layer"
    Bp, Lp, D = x_prompt.shape
    Bs, Ls, _ = x_sample.shape
    past = cache_k.shape[2]
    pos_p = jnp.arange(Lp, dtype=jnp.int32)
    pos_s = past + jnp.arange(Ls, dtype=jnp.int32)
    lk = past + Ls
    lk_pad = -(-lk // LANES) * LANES
    k_pos_s = jnp.arange(lk_pad, dtype=jnp.int32)
    k_valid_s = k_pos_s < lk

    l = 0
    lam_init = 0.8 - 0.6 * math.exp(-0.3 * l)
    lamv = jnp.stack([lam_q1[l], lam_k1[l], lam_q2[l], lam_k2[l]]).astype(F32)
    mod = ada_mod(jnp.concatenate([c_prompt, c_sample], axis=0), w_ada[l], b_ada[l])
    mods = jnp.split(mod, 6, axis=-1)
    mods_p = [m[:Bp] for m in mods]
    mods_s = [m[Bp:] for m in mods]
    w = (norm1[l], norm2[l], w_in[l].astype(BF16), subln_a[l], subln_r[l], w_ba[l].astype(BF16),
         w_br[l].astype(BF16), w_o[l].astype(BF16), w_pq[l].astype(BF16),
         peer_keys[l].reshape(PEER_HEADS * 2, N_KEYS, KEY_DIM).astype(BF16),
         peer_u[l].astype(BF16), peer_v[l].T.astype(BF16))

    attend_p = lambda q, k, v: attn_prompt(q, k, v, rel_bias, lamv, lam_init)
    zero_state = jnp.zeros((Bp, HR, DKR, DVR), F32)
    yp, kp, vp, sp = _trunk(x_prompt, mods_p, pos_p, lam_init, w, attend_p, True, zero_state,
                            final_norm)

    def attend_s(q, k, v):
        padk = jnp.zeros((Bs, lk_pad - lk, WA), BF16)
        kc = cache_k[l].reshape(Bs, past, HA * 2 * DA).astype(BF16)
        vc = cache_v[l].reshape(Bs, past, WA).astype(BF16)
        k_all = jnp.concatenate([kc, k, padk], axis=1)
        v_all = jnp.concatenate([vc, v, padk], axis=1)
        return attn_small(q, k_all, v_all, pos_s, k_pos_s, k_valid_s, rel_bias, lamv, lam_init)

    ys, ks, vs, ss = _trunk(x_sample, mods_s, pos_s, lam_init, w, attend_s, False,
                            state_ret[l].astype(F32), final_norm)
    return (yp, ys, kp[None], vp[None], sp[None], ks[None], vs[None], ss[None])
```

```python
import functools
import math

import jax
import jax.numpy as jnp
from jax import lax
from jax.experimental import pallas as pl
from jax.experimental.pallas import tpu as pltpu

F32 = jnp.float32
BF16 = jnp.bfloat16

CHUNK = 64
HA = 4
DA = 64
DVA = 2 * DA
HR = 4
DKR = 128
DVR = 128
N_BUCKETS = 32
MAX_DIST = 128
PEER_HEADS = 8
N_KEYS = 128
KEY_DIM = 128
PEER_TOPK = 16
EPS = 1e-6
WA = HA * DVA
WR = HR * DVR
NEG = -1e30
LOG2E = math.log2(math.e)
LANES = 128
SUBLANES = 8
BF16_ROWS = 2 * SUBLANES
VMEM_LIMIT = 60 * 1024 * 1024

TOKEN_TILE = 512
RET_CHUNK = 256
ATTN_TQ = 256
ATTN_GROUP = 32
ATTN_TAIL = 8
ROUTE_TILE = 1024
ROUTE_CHUNKS = 8
EXPERT_TILE = 512
EXPERT_ROWS = 32
PEER_SUB = 2


def _params(sem, vmem=VMEM_LIMIT):
    return pltpu.CompilerParams(dimension_semantics=sem, vmem_limit_bytes=vmem)


def _dot(a, b):
    return jnp.dot(a, b, preferred_element_type=F32)


def _dot_nt(a, b):
    return lax.dot_general(a, b, (((1,), (1,)), ((), ())), preferred_element_type=F32)


def _dot_tn(a, b):
    return lax.dot_general(a, b, (((0,), (0,)), ((), ())), preferred_element_type=F32)


def _rms_rows(x):
    return x * lax.rsqrt(jnp.mean(x * x, axis=-1, keepdims=True) + EPS)


def _ada_kernel(c_ref, w_ref, b_ref, o_ref):
    c = c_ref[...]
    s = (c * jax.nn.sigmoid(c)).astype(BF16)
    o_ref[...] = _dot(s, w_ref[...].astype(BF16)) + b_ref[...]


def ada_mod(c, w_ada, b_ada, tn=1024):
    nb, d = c.shape
    n = w_ada.shape[1]
    return pl.pallas_call(
        _ada_kernel,
        grid=(n // tn,),
        in_specs=[pl.BlockSpec((nb, d), lambda j: (0, 0)),
                  pl.BlockSpec((d, tn), lambda j: (0, j)),
                  pl.BlockSpec((1, tn), lambda j: (0, j))],
        out_specs=pl.BlockSpec((nb, tn), lambda j: (0, j)),
        out_shape=jax.ShapeDtypeStruct((nb, n), F32),
        compiler_params=_params(("arbitrary",)),
        name="ada_mod",
    )(c, w_ada, b_ada.reshape(1, n))


def _inproj_kernel(x_ref, sh_ref, sc_ref, n1_ref, w_ref, cos_ref, sin_ref,
                   ka_ref, va_ref, qab_ref, kab_ref, vab_ref, qrb_ref, kr_ref, vrb_ref,
                   gr_ref, ga_ref, gb_ref, *, values_transposed):
    x = x_ref[...]
    h = _rms_rows(x) * n1_ref[...] * (1.0 + sc_ref[...]) + sh_ref[...]
    hb = h.astype(BF16)

    d_model = x.shape[1]
    sizes = (WA, WA, WA, WR, WR, WR, WR, d_model, d_model)
    starts = [sum(sizes[:g]) for g in range(len(sizes))]

    def proj(g):
        return _dot(hb, w_ref[:, starts[g]:starts[g] + sizes[g]])

    qa = proj(0)
    qab_ref[...] = (qa * (DA ** -0.5 * LOG2E)).astype(BF16)
    ka = proj(1)
    va = proj(2)
    for hh in range(HA):
        ka_ref[:, hh, :] = ka[:, hh * DVA:(hh + 1) * DVA]
        va_ref[:, hh, :] = va[:, hh * DVA:(hh + 1) * DVA]
    kab_ref[...] = ka.astype(BF16)
    vab_ref[...] = va.T.astype(BF16) if values_transposed else va.astype(BF16)
    cos2 = cos_ref[...]
    sin2 = sin_ref[...]

    def rot(z):
        parts = []
        for hh in range(HR):
            zh = z[:, hh * DKR:(hh + 1) * DKR]
            parts.append(zh * cos2 + pltpu.roll(zh, DKR // 2, 1) * sin2)
        return jnp.concatenate(parts, axis=-1)

    qrb_ref[...] = rot(proj(3)).astype(BF16)
    kr_ref[...] = rot(proj(4)) * (DKR ** -0.5)
    vrb_ref[...] = proj(5).astype(BF16)
    gr_ref[...] = proj(6)
    ga_ref[...] = proj(7)
    gb_ref[...] = proj(8)


def in_proj(x, sh1, sc1, norm1, w_in_b, cos2, sin2, tm, emit_vt):
    B, L, D = x.shape
    d_in = w_in_b.shape[1]
    row = lambda n: pl.BlockSpec((None, tm, n), lambda b, i: (b, i, 0))
    mod = pl.BlockSpec((None, 1, D), lambda b, i: (b, 0, 0))
    f = lambda n, dt: jax.ShapeDtypeStruct((B, L, n), dt)
    cache = pl.BlockSpec((None, tm, HA, DVA), lambda b, i: (b, i, 0, 0))
    cache_shape = jax.ShapeDtypeStruct((B, L, HA, DVA), F32)
    if emit_vt:
        v_spec = pl.BlockSpec((None, WA, tm), lambda b, i: (b, 0, i))
        v_shape = jax.ShapeDtypeStruct((B, WA, L), BF16)
    else:
        v_spec, v_shape = row(WA), f(WA, BF16)
    return pl.pallas_call(
        functools.partial(_inproj_kernel, values_transposed=emit_vt),
        grid=(B, L // tm),
        in_specs=[row(D), mod, mod,
                  pl.BlockSpec((1, D), lambda b, i: (0, 0)),
                  pl.BlockSpec((D, d_in), lambda b, i: (0, 0), pipeline_mode=pl.Buffered(1)),
                  pl.BlockSpec((tm, DKR), lambda b, i: (i, 0)),
                  pl.BlockSpec((tm, DKR), lambda b, i: (i, 0))],
        out_specs=[cache, cache, row(WA), row(WA), v_spec, row(WR), row(WR), row(WR),
                   row(WR), row(D), row(D)],
        out_shape=[cache_shape, cache_shape, f(WA, BF16), f(WA, BF16), v_shape,
                   f(WR, BF16), f(WR, F32), f(WR, BF16), f(WR, F32), f(D, F32), f(D, F32)],
        compiler_params=_params(("arbitrary", "arbitrary")),
        name="in_proj",
    )(x, sh1.reshape(B, 1, D), sc1.reshape(B, 1, D), norm1.reshape(1, D), w_in_b, cos2, sin2)


def _t5_bucket(rel):
    nb = N_BUCKETS // 2
    ret = jnp.where(rel > 0, nb, 0)
    n = jnp.abs(rel)
    max_exact = nb // 2
    nf = jnp.maximum(n, max_exact).astype(F32)
    large = max_exact + (jnp.log(nf / max_exact) / math.log(MAX_DIST / max_exact)
                         * (nb - max_exact)).astype(jnp.int32)
    large = jnp.minimum(large, nb - 1)
    return ret + jnp.where(n < max_exact, n, large)


def _bias_from_buckets(bkt, rb_ref, h, shift):
    val = jnp.where(bkt < 0, NEG, 0.0).astype(F32)
    for n in range(N_BUCKETS):
        val = jnp.where(bkt == n, (rb_ref[n, h] - shift) * LOG2E, val)
    return val


def _lam_value(lv_ref, lam_init):
    lv = lv_ref[...]
    a = jnp.sum(lv[0:1] * lv[1:2], axis=-1, keepdims=True)
    b = jnp.sum(lv[2:3] * lv[3:4], axis=-1, keepdims=True)
    return jnp.exp(a) - jnp.exp(b) + lam_init


def _block_diag_q(q):
    lane = lax.broadcasted_iota(jnp.int32, q.shape, 1)
    zero = jnp.zeros_like(q)
    return jnp.concatenate([jnp.where(lane < DA, q, zero), jnp.where(lane >= DA, q, zero)], axis=0)


def _attn_kernel(rb_ref, lv_ref, q_ref, k_ref, vt_ref, bkt_ref, o_ref,
                 bias_sc, m_sc, acc_sc, va_sc, *, tq, nsub, lam_init):
    h = pl.program_id(1)
    i = pl.program_id(2)

    @pl.when(i == 0)
    def _():
        far = rb_ref[N_BUCKETS // 2 - 1, h]
        for t in range(2):
            bias_sc[t] = _bias_from_buckets(bkt_ref[t], rb_ref, h, far)
        va_sc[0:DVA, :] = vt_ref[...]
        va_sc[DVA:, :] = jnp.ones((BF16_ROWS, va_sc.shape[1]), BF16)

    qt = q_ref[...].astype(F32).T
    drow = lax.broadcasted_iota(jnp.int32, qt.shape, 0)
    qbd_t = jnp.concatenate([jnp.where(drow < DA, qt, 0.0), jnp.where(drow >= DA, qt, 0.0)],
                            axis=1).astype(BF16)
    m_sc[...] = jnp.full(m_sc.shape, NEG, F32)
    acc_sc[...] = jnp.zeros(acc_sc.shape, F32)

    def scores(off, tk, bias=None):
        s = _dot(k_ref[pl.ds(off, tk), :], qbd_t)
        if bias is not None:
            s = s + jnp.concatenate([bias, bias], axis=1)
        return s

    def absorb(s, off, tk):
        m_prev = m_sc[...]
        m_new = jnp.maximum(m_prev, jnp.max(s, axis=0, keepdims=True))
        alpha = jnp.exp2(m_prev - m_new)
        p = jnp.exp2(s - m_new).astype(BF16)
        acc_sc[...] = alpha * acc_sc[...] + _dot(va_sc[:, pl.ds(off, tk)], p)
        m_sc[...] = m_new

    def tiles(specs):
        ss = [scores(*sp) for sp in specs]
        for s, sp in zip(ss, specs):
            absorb(s, sp[0], sp[1])

    ntile = jnp.maximum(i - 1, 0)
    nfull = ntile // nsub

    def far(first_tile, n):
        return [(pl.multiple_of((first_tile + j) * tq, tq), tq) for j in range(n)]

    def far_body(t, carry):
        tiles(far(t * nsub, nsub))
        return carry

    lax.fori_loop(0, nfull, far_body, 0)
    done = nfull * nsub
    n = nsub // 2
    while n >= ATTN_TAIL:
        take = ((ntile - done) // n) > 0

        @pl.when(take)
        def _(done=done, n=n):
            tiles(far(done, n))

        done = done + jnp.where(take, n, 0)
        n //= 2

    rest = ntile - done
    for r in range(ATTN_TAIL):
        @pl.when((i >= 1) & (rest == r))
        def _(r=r):
            off = pl.multiple_of((i - 1) * tq, tq)
            tiles(far(done, r) + [(off, tq, bias_sc[1]),
                                  (pl.multiple_of(off + tq, tq), tq, bias_sc[0])])

    @pl.when(i == 0)
    def _():
        tiles([(0, tq, bias_sc[0])])

    lam = _lam_value(lv_ref, lam_init)
    o = acc_sc[0:DVA, :] / acc_sc[DVA:DVA + 1, :]
    o_ref[...] = (o[:, 0:tq] - lam * o[:, tq:2 * tq]).T


def attn_prompt(qab, kab, vt, rel_bias, lamv, lam_init, tq=ATTN_TQ, nsub=ATTN_GROUP):
    B, S, _ = qab.shape
    assert S % tq == 0
    c = jnp.arange(tq, dtype=jnp.int32)[:, None]
    r = jnp.arange(tq, dtype=jnp.int32)[None, :]
    diag = jnp.where((c // CHUNK) <= (r // CHUNK), _t5_bucket(c - r), -1)
    prev = _t5_bucket(c - r - tq)
    bkt = jnp.stack([diag, prev]).astype(jnp.int32)
    smem = pl.BlockSpec(memory_space=pltpu.SMEM)
    return pl.pallas_call(
        functools.partial(_attn_kernel, tq=tq, nsub=nsub, lam_init=lam_init),
        grid=(B, HA, S // tq),
        in_specs=[smem,
                  pl.BlockSpec((4, DA), lambda b, h, i: (0, 0)),
                  pl.BlockSpec((None, tq, DVA), lambda b, h, i: (b, i, h)),
                  pl.BlockSpec((None, S, DVA), lambda b, h, i: (b, 0, h)),
                  pl.BlockSpec((None, DVA, S), lambda b, h, i: (b, h, 0)),
                  pl.BlockSpec((2, tq, tq), lambda b, h, i: (0, 0, 0))],
        out_specs=pl.BlockSpec((None, tq, DVA), lambda b, h, i: (b, i, h)),
        out_shape=jax.ShapeDtypeStruct((B, S, WA), F32),
        scratch_shapes=[pltpu.VMEM((2, tq, tq), F32),
                        pltpu.VMEM((1, 2 * tq), F32),
                        pltpu.VMEM((DVA + BF16_ROWS, 2 * tq), F32),
                        pltpu.VMEM((DVA + BF16_ROWS, S), BF16)],
        compiler_params=_params(("arbitrary", "arbitrary", "arbitrary")),
        name="attn_prompt",
    )(rel_bias, lamv, qab, kab, vt, bkt)


def _attn_small_kernel(rb_ref, lv_ref, q_ref, k_ref, v_ref, bkt_ref, o_ref, *, lq, lam_init):
    h = pl.program_id(1)
    bias = _bias_from_buckets(bkt_ref[...], rb_ref, h, 0.0)
    qbd = _block_diag_q(q_ref[...])
    s = _dot_nt(qbd, k_ref[...]) + jnp.concatenate([bias, bias], axis=0)
    m = jnp.max(s, axis=-1, keepdims=True)
    p = jnp.exp2(s - m)
    l = jnp.sum(p, axis=-1, keepdims=True)
    o = _dot(p.astype(BF16), v_ref[...]) / l
    lam = _lam_value(lv_ref, lam_init)
    o_ref[...] = o[0:lq] - lam * o[lq:2 * lq]


def attn_small(qab, k_all, v_all, q_pos, k_pos, k_valid, rel_bias, lamv, lam_init):
    B, Lq, _ = qab.shape
    Lk = k_all.shape[1]
    visible = ((k_pos[None, :] // CHUNK) <= (q_pos[:, None] // CHUNK)) & k_valid[None, :]
    bkt = jnp.where(visible, _t5_bucket(k_pos[None, :] - q_pos[:, None]), -1).astype(jnp.int32)
    smem = pl.BlockSpec(memory_space=pltpu.SMEM)
    return pl.pallas_call(
        functools.partial(_attn_small_kernel, lq=Lq, lam_init=lam_init),
        grid=(B, HA),
        in_specs=[smem,
                  pl.BlockSpec((4, DA), lambda b, h: (0, 0)),
                  pl.BlockSpec((None, Lq, DVA), lambda b, h: (b, 0, h)),
                  pl.BlockSpec((None, Lk, DVA), lambda b, h: (b, 0, h)),
                  pl.BlockSpec((None, Lk, DVA), lambda b, h: (b, 0, h)),
                  pl.BlockSpec((Lq, Lk), lambda b, h: (0, 0))],
        out_specs=pl.BlockSpec((None, Lq, DVA), lambda b, h: (b, 0, h)),
        out_shape=jax.ShapeDtypeStruct((B, Lq, WA), F32),
        compiler_params=_params(("arbitrary", "arbitrary")),
        name="attn_sample",
    )(rel_bias, lamv, qab, k_all, v_all, bkt)


def _ret_kernel(lg_ref, q_ref, k_ref, v_ref, s0_ref, o_ref, so_ref, state_sc, decay_sc, *, C):
    c = pl.program_id(1)

    @pl.when(c == 0)
    def _():
        state_sc[...] = s0_ref[...]
        r = lax.broadcasted_iota(jnp.int32, (C, C), 0)
        cc = lax.broadcasted_iota(jnp.int32, (C, C), 1)
        diff = (r - cc).astype(F32)
        for h in range(HR):
            decay_sc[h] = jnp.where(diff >= 0, jnp.exp(jnp.maximum(diff, 0.0) * lg_ref[h]), 0.0)

    n = lax.broadcasted_iota(jnp.int32, (C, 1), 0).astype(F32)
    for h in range(HR):
        lg = lg_ref[h]
        cols = slice(h * DKR, (h + 1) * DKR)
        xi = jnp.exp((n + 1.0) * lg)
        zeta = jnp.exp((C - 1.0 - n) * lg)
        q = q_ref[:, cols]
        k = k_ref[:, cols]
        v = v_ref[:, cols]
        state = state_sc[h]
        scores = _dot_nt(q, k.astype(BF16)) * decay_sc[h]
        intra = _dot(scores.astype(BF16), v)
        cross = _dot(q, state.astype(BF16)) * xi
        o_ref[:, cols] = intra + cross
        kz = (k * zeta).astype(BF16)
        state_sc[h] = jnp.exp(C * lg) * state + _dot_tn(kz, v)

    @pl.when(c == pl.num_programs(1) - 1)
    def _():
        so_ref[...] = state_sc[...]


def retention(qrb, kr, vrb, state0, C):
    B, L, _ = qrb.shape
    lg = jnp.log(1.0 - 2.0 ** (-5.0 - jnp.arange(HR, dtype=F32)))
    blk = pl.BlockSpec((None, C, WR), lambda b, c: (b, c, 0))
    st = pl.BlockSpec((None, HR, DKR, DVR), lambda b, c: (b, 0, 0, 0))
    return pl.pallas_call(
        functools.partial(_ret_kernel, C=C),
        grid=(B, L // C),
        in_specs=[pl.BlockSpec(memory_space=pltpu.SMEM), blk, blk, blk, st],
        out_specs=[blk, st],
        out_shape=[jax.ShapeDtypeStruct((B, L, WR), F32),
                   jax.ShapeDtypeStruct((B, HR, DKR, DVR), F32)],
        scratch_shapes=[pltpu.VMEM((HR, DKR, DVR), F32), pltpu.VMEM((HR, C, C), F32)],
        compiler_params=_params(("arbitrary", "arbitrary")),
        name="retention",
    )(lg, qrb, kr, vrb, state0)


def _outmix_kernel(x_ref, oa_ref, or_ref, gr_ref, ga_ref, gb_ref, g1_ref, sh2_ref, sc2_ref, n2_ref,
                   sa_ref, sr_ref, wba_ref, wbr_ref, wo_ref, wpq_ref,
                   x1_ref, h2_ref, qp_ref, *, lam_init, h2_transposed):
    sa = sa_ref[...]
    sr = sr_ref[...]
    gr = gr_ref[...]
    silu_gr = gr * jax.nn.sigmoid(gr)
    ya_parts, yr_parts = [], []
    for hh in range(HA):
        sl = slice(hh * DVA, (hh + 1) * DVA)
        ya_parts.append(_rms_rows(oa_ref[:, sl]) * sa * (1.0 - lam_init))
        yr_parts.append(silu_gr[:, sl] * (_rms_rows(or_ref[:, sl]) * sr))
    ya = jnp.concatenate(ya_parts, axis=-1).astype(BF16)
    yr = jnp.concatenate(yr_parts, axis=-1).astype(BF16)
    y = (jax.nn.sigmoid(ga_ref[...]) * _dot(ya, wba_ref[...])
         + jax.nn.sigmoid(gb_ref[...]) * _dot(yr, wbr_ref[...]))
    out = _dot(y.astype(BF16), wo_ref[...])
    x1 = x_ref[...] + g1_ref[...] * out
    x1_ref[...] = x1
    h2f = _rms_rows(x1) * n2_ref[...] * (1.0 + sc2_ref[...]) + sh2_ref[...]
    h2 = h2f.astype(BF16)
    h2_ref[...] = h2f.T.astype(BF16) if h2_transposed else h2
    qp_ref[...] = _dot(h2, wpq_ref[...]).astype(BF16)


def out_mix(x, oa, orr, gr, ga, gb, g1, sh2, sc2, norm2, subln_a, subln_r,
            w_ba_b, w_br_b, w_o_b, w_pq_b, lam_init, tm):
    B, L, D = x.shape
    nq = w_pq_b.shape[1]
    row = lambda n: pl.BlockSpec((None, tm, n), lambda b, i: (b, i, 0))
    mod = pl.BlockSpec((None, 1, D), lambda b, i: (b, 0, 0))
    full = lambda a: pl.BlockSpec(a.shape, lambda b, i: (0,) * a.ndim, pipeline_mode=pl.Buffered(1))
    n2 = norm2.reshape(1, D)
    sa = subln_a.reshape(1, DVA)
    sr = subln_r.reshape(1, DVR)
    h2_transposed = tm % LANES == 0
    nt = L // tm
    if h2_transposed:
        h2_spec = pl.BlockSpec((D, tm), lambda b, i: (0, b * nt + i))
        h2_shape = jax.ShapeDtypeStruct((D, B * L), BF16)
    else:
        h2_spec = row(D)
        h2_shape = jax.ShapeDtypeStruct((B, L, D), BF16)
    x1, h2, qp = pl.pallas_call(
        functools.partial(_outmix_kernel, lam_init=lam_init, h2_transposed=h2_transposed),
        grid=(B, nt),
        in_specs=[row(D), row(WA), row(WR), row(WR), row(D), row(D), mod, mod, mod,
                  full(n2), full(sa), full(sr), full(w_ba_b), full(w_br_b), full(w_o_b), full(w_pq_b)],
        out_specs=[row(D), h2_spec, row(nq)],
        out_shape=[jax.ShapeDtypeStruct((B, L, D), F32), h2_shape,
                   jax.ShapeDtypeStruct((B, L, nq), BF16)],
        compiler_params=_params(("arbitrary", "arbitrary")),
        name="out_mix",
    )(x, oa, orr, gr, ga, gb, g1.reshape(B, 1, D), sh2.reshape(B, 1, D), sc2.reshape(B, 1, D),
      n2, sa, sr, w_ba_b, w_br_b, w_o_b, w_pq_b)
    h2t = h2 if h2_transposed else h2.reshape(B * L, D).T
    return x1, h2t, qp


UNRANKED = float(PEER_TOPK * PEER_TOPK)


def _topk_rows(s, k, break_ties, want_rank=True):
    n = s.shape[0]
    iota = lax.broadcasted_iota(jnp.int32, s.shape, 0).astype(F32)
    work = s
    rank = jnp.full(s.shape, UNRANKED, F32) if want_rank else None
    vals = []
    for r in range(k):
        m = jnp.max(work, axis=0, keepdims=True)
        sel = work == m
        if break_ties:
            idx = jnp.min(jnp.where(sel, iota, float(n)), axis=0, keepdims=True)
            sel = iota == idx
        if want_rank:
            rank = jnp.where(sel, float(r), rank)
        work = jnp.where(sel, -jnp.inf, work)
        vals.append(m)
    return vals, rank


def _count_rows(mask):
    return jnp.sum(mask.astype(F32), axis=0, keepdims=True)


_HEAD_A = SUBLANES
_CAND_NB = [PEER_TOPK] + [SUBLANES] * (_HEAD_A - 1)


def _route_chunk(s1, s2, break_ties):
    k = PEER_TOPK
    v1, rank1 = _topk_rows(s1, k, break_ties, want_rank=break_ties)
    v2, rank2 = _topk_rows(s2, k, break_ties)
    v2m = jnp.concatenate(v2, axis=0)
    v1t = jnp.concatenate(v1[_HEAD_A:], axis=0)
    blocks = [v1[a] + v2m[0:nb] for a, nb in enumerate(_CAND_NB)] + [v1t + v2[0]]
    cand = jnp.concatenate(blocks, axis=0)
    vc, crank = _topk_rows(cand, k, break_ties, want_rank=break_ties)
    if break_ties:
        sel = crank < UNRANKED
        row_is = lambda a: rank1 == float(a)
        nbad = jnp.zeros((), jnp.int32)
    else:
        sel = cand >= vc[k - 1]
        row_is = lambda a: s1 == v1[a]
        bad = ((_count_rows(s1 >= v1[k - 1]) != float(k))
               | (_count_rows(rank2 < UNRANKED) != float(k)) | (_count_rows(sel) != float(k)))
        nbad = jnp.sum(bad.astype(jnp.int32))
    cmax = v1[0] + v2[0]
    z = jnp.sum(jnp.where(sel, jnp.exp(cand - cmax), 0.0), axis=0, keepdims=True)
    self32 = sel.astype(F32)
    cidx = jnp.zeros(s1.shape, F32)
    lo = 0
    for a, nb in enumerate(_CAND_NB):
        cnt = jnp.sum(self32[lo:lo + nb], axis=0, keepdims=True)
        cidx = jnp.where(row_is(a), cnt, cidx)
        lo += nb
    for a in range(_HEAD_A, k):
        cidx = jnp.where(row_is(a), self32[lo + a - _HEAD_A:lo + a - _HEAD_A + 1], cidx)
    e1 = jnp.exp(s1 - v1[0]) / z
    e2 = jnp.exp(s2 - v2[0])
    return (e1, e2, rank2, cidx), nbad


def _route_kernel(q_ref, keys_ref, e1_ref, e2_ref, r2_ref, c_ref, s1_sc, s2_sc):
    q = q_ref[...]
    s1_sc[...] = _dot_nt(keys_ref[0], q[:, 0:KEY_DIM])
    s2_sc[...] = _dot_nt(keys_ref[1], q[:, KEY_DIM:2 * KEY_DIM])

    def store(sl, res):
        e1, e2, rank2, cidx = res
        e1_ref[:, sl] = e1
        e2_ref[:, sl] = e2.astype(BF16)
        r2_ref[:, sl] = rank2.astype(BF16)
        c_ref[:, sl] = cidx

    group = math.gcd(ROUTE_CHUNKS, s1_sc.shape[1] // LANES)

    def chunks(ci, carry):
        sls = [pl.ds(pl.multiple_of((ci * group + g) * LANES, LANES), LANES) for g in range(group)]
        fast = [_route_chunk(s1_sc[:, sl], s2_sc[:, sl], break_ties=False) for sl in sls]
        for sl, (res, _) in zip(sls, fast):
            store(sl, res)
        for sl, (_, nbad) in zip(sls, fast):
            @pl.when(nbad > 0)
            def _():
                store(sl, _route_chunk(s1_sc[:, sl], s2_sc[:, sl], break_ties=True)[0])
        return carry

    lax.fori_loop(0, s1_sc.shape[1] // (group * LANES), chunks, 0)


def peer_route(qp, keys_b, tm):
    T = qp.shape[0]
    out = pl.BlockSpec((None, N_KEYS, tm), lambda t, h: (h, 0, t))
    shp = lambda dt: jax.ShapeDtypeStruct((PEER_HEADS, N_KEYS, T), dt)
    return pl.pallas_call(
        _route_kernel,
        grid=(T // tm, PEER_HEADS),
        in_specs=[pl.BlockSpec((tm, 2 * KEY_DIM), lambda t, h: (t, h)),
                  pl.BlockSpec((2, N_KEYS, KEY_DIM), lambda t, h: (h, 0, 0))],
        out_specs=[out, out, out, out],
        out_shape=[shp(F32), shp(BF16), shp(BF16), shp(F32)],
        scratch_shapes=[pltpu.VMEM((N_KEYS, tm), F32), pltpu.VMEM((N_KEYS, tm), F32)],
        compiler_params=_params(("arbitrary", "arbitrary")),
        name="peer_route",
    )(qp, keys_b)


def _bcast_rows_bf16(row, n):
    t = row.shape[1]
    tile = jnp.broadcast_to(row, (BF16_ROWS, t)).astype(BF16)
    return jnp.broadcast_to(tile[None], (n // BF16_ROWS, BF16_ROWS, t)).reshape(n, t)


def _peer_kernel(h2_ref, u_ref, vt_ref, e1_ref, e2_ref, r2_ref, c_ref, x1_ref, g2_ref, fn_ref,
                 y_ref, acc_sc, *, ni):
    e = pl.program_id(1)

    @pl.when(e == 0)
    def _():
        acc_sc[...] = jnp.zeros(acc_sc.shape, F32)

    h2 = h2_ref[...]
    pair = PEER_SUB * N_KEYS
    a_pairs = [_dot(u_ref[p * pair:(p + 1) * pair, :], h2) for p in range(ni // PEER_SUB)]
    for p in range(ni // PEER_SUB):
        gs = [jnp.zeros((N_KEYS, h2.shape[1]), BF16) for _ in range(PEER_SUB)]
        for hh in range(PEER_HEADS):
            r2h = r2_ref[hh]
            e2h = e2_ref[hh]
            for jj in range(PEER_SUB):
                i = PEER_SUB * p + jj
                c_row = _bcast_rows_bf16(c_ref[hh, pl.ds(i, 1), :], N_KEYS)
                e1_row = _bcast_rows_bf16(e1_ref[hh, pl.ds(i, 1), :], N_KEYS)
                gs[jj] = gs[jj] + jnp.where(r2h < c_row, e2h * e1_row, jnp.zeros_like(e2h))
        ws = []
        for jj in range(PEER_SUB):
            a = a_pairs[p][jj * N_KEYS:(jj + 1) * N_KEYS]
            act = 0.5 * a * (1.0 + lax.erf(a * (2.0 ** -0.5)))
            ws.append(gs[jj] * act.astype(BF16))
        w = jnp.concatenate(ws, axis=0)
        acc_sc[...] += _dot(vt_ref[:, p * pair:(p + 1) * pair], w)

    @pl.when(e == pl.num_programs(1) - 1)
    def _():
        x2 = x1_ref[...] + g2_ref[...] * acc_sc[...].T
        y_ref[...] = _rms_rows(x2) * fn_ref[...]


def peer_experts(h2t, u_b, vt_b, e1, e2, r2, cc, x1, g2, final_norm, tm, ni=EXPERT_ROWS):
    D, T = h2t.shape
    nb = ni * N_KEYS
    tok = pl.BlockSpec((tm, D), lambda t, e: (t, 0))
    gate = pl.BlockSpec((PEER_HEADS, N_KEYS, tm), lambda t, e: (0, 0, t))
    rows = pl.BlockSpec((PEER_HEADS, ni, tm), lambda t, e: (0, e, t))
    L = T // g2.shape[0]
    if L % tm == 0:
        g2_arg = g2.reshape(g2.shape[0], 1, D)
        g2_spec = pl.BlockSpec((None, 1, D), lambda t, e: (t // (L // tm), 0, 0))
    else:
        g2_arg = jnp.repeat(g2, L, axis=0)
        g2_spec = tok
    return pl.pallas_call(
        functools.partial(_peer_kernel, ni=ni),
        grid=(T // tm, u_b.shape[0] // nb),
        in_specs=[pl.BlockSpec((D, tm), lambda t, e: (0, t)),
                  pl.BlockSpec((nb, D), lambda t, e: (e, 0)),
                  pl.BlockSpec((D, nb), lambda t, e: (0, e)),
                  rows, gate, gate, rows, tok, g2_spec,
                  pl.BlockSpec((1, D), lambda t, e: (0, 0))],
        out_specs=tok,
        out_shape=jax.ShapeDtypeStruct((T, D), F32),
        scratch_shapes=[pltpu.VMEM((D, tm), F32)],
        compiler_params=_params(("arbitrary", "arbitrary")),
        name="peer_experts",
    )(h2t, u_b, vt_b, e1, e2, r2, cc, x1, g2_arg, final_norm.reshape(1, D))


def _rot_tables(pos):
    inv = 1.0 / (10000.0 ** jnp.linspace(0.0, 1.0, DKR // 2, dtype=F32))
    ang = pos[:, None].astype(F32) * inv[None, :]
    cos, sin = jnp.cos(ang), jnp.sin(ang)
    return jnp.concatenate([cos, cos], axis=-1), jnp.concatenate([-sin, sin], axis=-1)


def _pick_tile(n, pref):
    t = min(n, pref)
    assert n % t == 0, (n, t)
    return t


def _trunk(x, mods, pos, lam_init, w, attend, values_transposed, state0, final_norm):
    (norm1, norm2, w_in_b, subln_a, subln_r, w_ba_b, w_br_b, w_o_b, w_pq_b, keys_b, u_b, vt_b) = w
    sh1, sc1, g1, sh2, sc2, g2 = mods
    B, L, D = x.shape
    T = B * L
    cos2, sin2 = _rot_tables(pos)
    tm = _pick_tile(L, TOKEN_TILE)
    ka, va, qab, kab, vab, qrb, kr, vrb, gr, ga, gb = in_proj(
        x, sh1, sc1, norm1, w_in_b, cos2, sin2, tm, emit_vt=values_transposed)
    oa = attend(qab, kab, vab)
    orr, st = retention(qrb, kr, vrb, state0, _pick_tile(L, RET_CHUNK))
    x1, h2t, qp = out_mix(x, oa, orr, gr, ga, gb, g1, sh2, sc2, norm2, subln_a, subln_r,
                          w_ba_b, w_br_b, w_o_b, w_pq_b, lam_init, tm)
    e1, e2, r2, cc = peer_route(qp.reshape(T, -1), keys_b, _pick_tile(T, ROUTE_TILE))
    y = peer_experts(h2t, u_b, vt_b, e1, e2, r2, cc, x1.reshape(T, D), g2,
                     final_norm, _pick_tile(T, EXPERT_TILE))
    return y.reshape(B, L, D), ka, va, st


def kernel(x_prompt, x_sample, cache_k, cache_v, state_ret, c_prompt, c_sample, w_ada, b_ada, norm1,
           norm2, w_in, lam_q1, lam_k1, lam_q2, lam_k2, subln_a, subln_r, w_ba, w_br, w_o, rel_bias,
           w_pq, peer_keys, peer_u, peer_v, final_norm):
    depth = w_ada.shape[0]
    assert depth == 1, "the fused final norm assumes a single layer"
    Bp, Lp, D = x_prompt.shape
    Bs, Ls, _ = x_sample.shape
    past = cache_k.shape[2]
    pos_p = jnp.arange(Lp, dtype=jnp.int32)
    pos_s = past + jnp.arange(Ls, dtype=jnp.int32)
    lk = past + Ls
    lk_pad = -(-lk // LANES) * LANES
    k_pos_s = jnp.arange(lk_pad, dtype=jnp.int32)
    k_valid_s = k_pos_s < lk

    l = 0
    lam_init = 0.8 - 0.6 * math.exp(-0.3 * l)
    lamv = jnp.stack([lam_q1[l], lam_k1[l], lam_q2[l], lam_k2[l]]).astype(F32)
    mod = ada_mod(jnp.concatenate([c_prompt, c_sample], axis=0), w_ada[l], b_ada[l])
    mods = jnp.split(mod, 6, axis=-1)
    mods_p = [m[:Bp] for m in mods]
    mods_s = [m[Bp:] for m in mods]
    w = (norm1[l], norm2[l], w_in[l].astype(BF16), subln_a[l], subln_r[l], w_ba[l].astype(BF16),
         w_br[l].astype(BF16), w_o[l].astype(BF16), w_pq[l].astype(BF16),
         peer_keys[l].reshape(PEER_HEADS * 2, N_KEYS, KEY_DIM).astype(BF16),
         peer_u[l].astype(BF16), peer_v[l].T.astype(BF16))

    attend_p = lambda q, k, v: attn_prompt(q, k, v, rel_bias, lamv, lam_init)
    zero_state = jnp.zeros((Bp, HR, DKR, DVR), F32)
    yp, kp, vp, sp = _trunk(x_prompt, mods_p, pos_p, lam_init, w, attend_p, True, zero_state,
                            final_norm)

    def attend_s(q, k, v):
        padk = jnp.zeros((Bs, lk_pad - lk, WA), BF16)
        kc = cache_k[l].reshape(Bs, past, HA * 2 * DA).astype(BF16)
        vc = cache_v[l].reshape(Bs, past, WA).astype(BF16)
        k_all = jnp.concatenate([kc, k, padk], axis=1)
        v_all = jnp.concatenate([vc, v, padk], axis=1)
        return attn_small(q, k_all, v_all, pos_s, k_pos_s, k_valid_s, rel_bias, lamv, lam_init)

    ys, ks, vs, ss = _trunk(x_sample, mods_s, pos_s, lam_init, w, attend_s, False,
                            state_ret[l].astype(F32), final_norm)
    return (yp, ys, kp[None], vp[None], sp[None], ks[None], vs[None], ss[None])
```
